```python
import jax
import jax.numpy as jnp
from jax import lax
import numpy as np

D_MODEL = 1024
BATCH = 4
SEQ = 8192
DEPTH = 4

GRID_W = 64
CTX_LEN = 256
N_MIXERS = 2
N_MOD = 6
EPS = 1e-6
FNET_GROUPS = 4
FNET_GROUP_DIM = D_MODEL // FNET_GROUPS
HEAD_DIM = 64
N_HEADS = D_MODEL // HEAD_DIM
N_KV_HEADS = 4
GROUP = N_HEADS // N_KV_HEADS
QKV_DIM = (N_HEADS + 2 * N_KV_HEADS) * HEAD_DIM
WINDOW = 128
BLOCK = 128
ROPE_THETA = 10000.0
ROT_FREQS = HEAD_DIM // 4
D_FF = D_MODEL * 7 // 2
N_EXPERTS = 8
TOP_K = 2
MOE_BLOCK = 256

kernel_name = "hybrid_fnet_swa_moe_dit"


def rms_norm(x, g):
    xf = x.astype(jnp.float32)
    return (xf * lax.rsqrt(jnp.mean(xf * xf, axis=-1, keepdims=True) + EPS)).astype(x.dtype) * g


def modulate(h, shift, scale):
    return h * (1 + scale) + shift


def swiglu(h, w_gate, w_up, w_down):
    return (jax.nn.silu(h @ w_gate) * (h @ w_up)) @ w_down


def moe_swiglu(h, w_router, w_gate, w_up, w_down):
    T, D = h.shape
    logits = (h @ w_router).astype(jnp.float32)
    top_logit, top_idx = lax.top_k(logits, TOP_K)
    gates = jax.nn.softmax(top_logit, axis=-1)
    flat_e = top_idx.reshape(-1)
    flat_g = gates.reshape(-1)
    n_pairs = T * TOP_K
    order = jnp.argsort(flat_e)
    sorted_e = flat_e[order]
    sorted_tok = (order // TOP_K).astype(jnp.int32)
    counts = jnp.bincount(flat_e, length=N_EXPERTS)
    padded = (counts + MOE_BLOCK - 1) // MOE_BLOCK * MOE_BLOCK
    pad_end = jnp.cumsum(padded)
    pad_start = pad_end - padded
    start = jnp.cumsum(counts) - counts
    dest = pad_start[sorted_e] + jnp.arange(n_pairs) - start[sorted_e]
    n_blocks = -(-n_pairs // MOE_BLOCK) + N_EXPERTS
    cap = n_blocks * MOE_BLOCK
    slot_tok = jnp.zeros((cap,), jnp.int32).at[dest].set(sorted_tok)
    slot_w = jnp.zeros((cap,), jnp.float32).at[dest].set(flat_g[order])
    block_expert = jnp.minimum(
        jnp.searchsorted(pad_end, jnp.arange(n_blocks) * MOE_BLOCK, side='right'), N_EXPERTS - 1)
    xs = h[slot_tok].reshape(n_blocks, MOE_BLOCK, D)

    def expert_group(args):
        xb, e = args
        return swiglu(xb, w_gate[e], w_up[e], w_down[e])

    ys = lax.map(expert_group, (xs, block_expert)).reshape(cap, D)
    return jnp.zeros_like(h).at[slot_tok].add(ys * slot_w[:, None].astype(h.dtype))


def fourier_mix(h, w_out):
    B, N, _ = h.shape
    hg = h.astype(jnp.float32).reshape(B, N, FNET_GROUPS, FNET_GROUP_DIM)
    mixed = jnp.fft.fft2(hg, axes=(1, 3), norm='ortho').real
    return mixed.reshape(B, N, D_MODEL).astype(h.dtype) @ w_out


def axial_rope_tables(n_rows):
    row = jnp.repeat(jnp.arange(n_rows, dtype=jnp.float32), GRID_W)
    col = jnp.tile(jnp.arange(GRID_W, dtype=jnp.float32), n_rows)
    inv_freq = ROPE_THETA ** (-jnp.arange(ROT_FREQS, dtype=jnp.float32) / ROT_FREQS)
    ang = jnp.stack([row[:, None] * inv_freq, col[:, None] * inv_freq], axis=1)
    return jnp.cos(ang), jnp.sin(ang)


def apply_rope(x, cos, sin):
    xr = x.reshape(x.shape[:3] + (2, 2, ROT_FREQS))
    x1 = xr[..., 0, :]
    x2 = xr[..., 1, :]
    c = cos[None, :, None].astype(x.dtype)
    s = sin[None, :, None].astype(x.dtype)
    return jnp.stack([x1 * c - x2 * s, x2 * c + x1 * s], axis=-2).reshape(x.shape)


def window_gqa_attention(h_lat, h_ctx, w_qkv, w_o, sinks, cos, sin, ctx_out):
    B, S, _ = h_lat.shape
    L = h_ctx.shape[1]
    nb = S // BLOCK
    scale = HEAD_DIM ** -0.5

    def project(h):
        n = h.shape[1]
        q, k, v = jnp.split(h @ w_qkv, [N_HEADS * HEAD_DIM, (N_HEADS + N_KV_HEADS) * HEAD_DIM], axis=-1)
        return (q.reshape(B, n, N_HEADS, HEAD_DIM), k.reshape(B, n, N_KV_HEADS, HEAD_DIM),
                v.reshape(B, n, N_KV_HEADS, HEAD_DIM))

    q_l, k_l, v_l = project(h_lat)
    q_l = apply_rope(q_l, cos, sin).reshape(B, S, N_KV_HEADS, GROUP, HEAD_DIM)
    k_l = apply_rope(k_l, cos, sin)
    q_c, k_c, v_c = project(h_ctx)
    sink = sinks.astype(jnp.float32).reshape(N_KV_HEADS, GROUP)[None, :, :, None, None]

    def sink_softmax(logits):
        b, kv, g, nq, _ = logits.shape
        sink_col = jnp.broadcast_to(sink, (b, kv, g, nq, 1))
        return jax.nn.softmax(jnp.concatenate([logits, sink_col], axis=-1), axis=-1)[..., :-1]

    def band(t):
        tp = jnp.pad(t, ((0, 0), (BLOCK, BLOCK), (0, 0), (0, 0))).reshape(B, nb + 2, BLOCK, N_KV_HEADS, HEAD_DIM)
        w = jnp.concatenate([tp[:, :-2], tp[:, 1:-1], tp[:, 2:]], axis=2)
        return w.transpose(1, 0, 2, 3, 4)

    q_idx = jnp.arange(BLOCK)[:, None]
    k_idx = jnp.arange(3 * BLOCK)[None, :]
    in_window = jnp.abs(k_idx - BLOCK - q_idx) <= WINDOW
    key_pos = jnp.arange(nb)[:, None] * BLOCK - BLOCK + jnp.arange(3 * BLOCK)[None, :]
    in_range = (key_pos >= 0) & (key_pos < S)
    mask = jnp.concatenate([in_window[None] & in_range[:, None, :],
                            jnp.ones((nb, BLOCK, L), dtype=bool)], axis=-1)

    def latent_block(args):
        qb, kb, vb, mb = args
        k_all = jnp.concatenate([kb, k_c], axis=1)
        v_all = jnp.concatenate([vb, v_c], axis=1)
        logits = jnp.einsum('bqkgd,bjkd->bkgqj', qb, k_all).astype(jnp.float32) * scale
        logits = jnp.where(mb, logits, -jnp.inf)
        p = sink_softmax(logits).astype(vb.dtype)
        return jnp.einsum('bkgqj,bjkd->bqkgd', p, v_all)

    q_blocks = q_l.reshape(B, nb, BLOCK, N_KV_HEADS, GROUP, HEAD_DIM).transpose(1, 0, 2, 3, 4, 5)
    o_l = lax.map(latent_block, (q_blocks, band(k_l), band(v_l), mask))
    o_l = o_l.transpose(1, 0, 2, 3, 4, 5).reshape(B, S, N_HEADS * HEAD_DIM) @ w_o
    if not ctx_out:
        return o_l, None
    q_c = q_c.reshape(B, L, N_KV_HEADS, GROUP, HEAD_DIM)
    logits_c = jnp.einsum('bqkgd,bjkd->bkgqj', q_c, k_c).astype(jnp.float32) * scale
    p_c = sink_softmax(logits_c).astype(v_c.dtype)
    o_c = jnp.einsum('bkgqj,bjkd->bqkgd', p_c, v_c).reshape(B, L, N_HEADS * HEAD_DIM) @ w_o
    return o_l, o_c


def setup_inputs(seed: int = 0) -> dict:
    key = jax.random.key(seed)
    ks = jax.random.split(key, 20)
    n_even = (DEPTH + 1) // 2
    n_odd = DEPTH // 2
    f32 = jnp.float32

    def normal(k, shape):
        return jax.random.normal(k, shape, f32)

    def dense(k, shape, fan_in):
        return normal(k, shape) * fan_in ** -0.5

    return {
        'x': normal(ks[0], (BATCH, SEQ, D_MODEL)),
        'c': normal(ks[1], (BATCH, D_MODEL)),
        'ctx': normal(ks[2], (BATCH, CTX_LEN, D_MODEL)),
        'c_ctx': normal(ks[3], (D_MODEL,)),
        'ada_w': dense(ks[4], (DEPTH, D_MODEL, N_MOD * D_MODEL), D_MODEL) * 0.5,
        'ada_b': 0.02 * normal(ks[5], (DEPTH, N_MOD * D_MODEL)),
        'norm_mix_g': 1.0 + 0.05 * normal(ks[6], (DEPTH, D_MODEL)),
        'norm_ffn_g': 1.0 + 0.05 * normal(ks[7], (DEPTH, D_MODEL)),
        'fnet_w_out': dense(ks[8], (n_even, D_MODEL, D_MODEL), D_MODEL),
        'attn_w_qkv': dense(ks[9], (n_odd, D_MODEL, QKV_DIM), D_MODEL),
        'attn_w_o': dense(ks[10], (n_odd, N_HEADS * HEAD_DIM, D_MODEL), N_HEADS * HEAD_DIM),
        'attn_sinks': normal(ks[11], (n_odd, N_HEADS)),
        'ffn_w_gate': dense(ks[12], (n_even, D_MODEL, D_FF), D_MODEL),
        'ffn_w_up': dense(ks[13], (n_even, D_MODEL, D_FF), D_MODEL),
        'ffn_w_down': dense(ks[14], (n_even, D_FF, D_MODEL), D_FF),
        'moe_w_router': dense(ks[15], (n_odd, D_MODEL, N_EXPERTS), D_MODEL),
        'moe_w_gate': dense(ks[16], (n_odd, N_EXPERTS, D_MODEL, D_FF), D_MODEL),
        'moe_w_up': dense(ks[17], (n_odd, N_EXPERTS, D_MODEL, D_FF), D_MODEL),
        'moe_w_down': dense(ks[18], (n_odd, N_EXPERTS, D_FF, D_MODEL), D_FF),
        'final_norm_g': 1.0 + 0.05 * normal(ks[19], (D_MODEL,)),
    }


def reference(x, c, ctx, c_ctx, ada_w, ada_b, norm_mix_g, norm_ffn_g, fnet_w_out, attn_w_qkv, attn_w_o,
              attn_sinks, ffn_w_gate, ffn_w_up, ffn_w_down, moe_w_router, moe_w_gate, moe_w_up, moe_w_down,
              final_norm_g):
    B, S, D = x.shape
    L = ctx.shape[1]
    rows = S // GRID_W
    cos, sin = axial_rope_tables(rows)
    silu_c = jax.nn.silu(c)
    silu_cc = jax.nn.silu(c_ctx)
    x_lat, x_ctx = x, ctx
    for i in range(DEPTH):
        j = i // 2
        last = i == DEPTH - 1
        is_attn = i % N_MIXERS == 1
        sh_l, sc_l, g_l, shf_l, scf_l, gf_l = jnp.split(silu_c @ ada_w[i] + ada_b[i], N_MOD, axis=-1)
        sh_c, sc_c, g_c, shf_c, scf_c, gf_c = jnp.split(silu_cc @ ada_w[i] + ada_b[i], N_MOD, axis=-1)
        h_lat = modulate(rms_norm(x_lat, norm_mix_g[i]), sh_l[:, None], sc_l[:, None])
        if is_attn:
            h_ctx = modulate(rms_norm(x_ctx, norm_mix_g[i]), sh_c, sc_c)
            y_lat, y_ctx = window_gqa_attention(h_lat, h_ctx, attn_w_qkv[j], attn_w_o[j], attn_sinks[j],
                                                cos, sin, not last)
        else:
            y_lat = fourier_mix(h_lat, fnet_w_out[j])
            if not last:
                y_ctx = fourier_mix(modulate(rms_norm(x_ctx, norm_mix_g[i]), sh_c, sc_c), fnet_w_out[j])
        x_lat = x_lat + g_l[:, None] * y_lat
        if not last:
            x_ctx = x_ctx + g_c * y_ctx
        tokens = modulate(rms_norm(x_lat, norm_ffn_g[i]), shf_l[:, None], scf_l[:, None]).reshape(B * S, D)
        if not last:
            hf_ctx = modulate(rms_norm(x_ctx, norm_ffn_g[i]), shf_c, scf_c).reshape(B * L, D)
            tokens = jnp.concatenate([tokens, hf_ctx], axis=0)
        if i % 2 == 0:
            y = swiglu(tokens, ffn_w_gate[j], ffn_w_up[j], ffn_w_down[j])
        else:
            y = moe_swiglu(tokens, moe_w_router[j], moe_w_gate[j], moe_w_up[j], moe_w_down[j])
        x_lat = x_lat + gf_l[:, None] * y[:B * S].reshape(B, S, D)
        if not last:
            x_ctx = x_ctx + gf_c * y[B * S:].reshape(B, L, D)
    return rms_norm(x_lat, final_norm_g)
```

```python
import functools

import numpy as np
import jax
import jax.numpy as jnp
from jax import lax
from jax.experimental import pallas as pl
from jax.experimental.pallas import tpu as pltpu

F32 = jnp.float32
BF16 = jnp.bfloat16

GRID_W = 64
N_MOD = 6
EPS = 1e-6
FNET_GROUPS = 4
HEAD_DIM = 64
N_KV_HEADS = 4
WINDOW = 128
ROPE_THETA = 10000.0
ROT_FREQS = HEAD_DIM // 4
N_EXPERTS = 8
MOD_ROWS = 8
CTX_ROW = MOD_ROWS - 1
LANES = 128
NEG = -1e30
VMEM_LIMIT = 56 * 1024 * 1024

FFN_TM = 1024
FFN_TF = 512
MOE_TM = 512
ROUTE_TM = 512
ROW_TM = 512
PROJ_TM = 512
DFT_P = 64
DFT_R = 128
FNET_RB = 8
FNET_KB = 4


def _cparams(sem):
    return pltpu.CompilerParams(dimension_semantics=sem, vmem_limit_bytes=VMEM_LIMIT)


def _sigmoid(a):
    return 1.0 / (1.0 + jnp.exp(-a))


def _norm_mod(x, g, shift, scale):
    xn = x * lax.rsqrt(jnp.mean(x * x, axis=-1, keepdims=True) + EPS)
    return (xn * g) * (1.0 + scale) + shift


def _mod_spec(d, idx_fn):
    return pl.BlockSpec((None, 1, d), lambda *ids: (idx_fn(*ids), 0, 0))


def _mod_index(layer, j, tiles_per_batch, ctx):
    def fn(i, *_):
        b = CTX_ROW if ctx else i // tiles_per_batch
        return (layer * MOD_ROWS + b) * N_MOD + j
    return fn


def _ada_kernel(c_ref, w_ref, b_ref, o_ref):
    cc = c_ref[...]
    s = cc * _sigmoid(cc)
    o_ref[...] = jnp.dot(s, w_ref[...], preferred_element_type=F32,
                         precision=lax.Precision.HIGHEST) + b_ref[...]


def _ada(c, c_ctx, ada_w, ada_b):
    depth, d, _ = ada_w.shape
    nb = c.shape[0]
    assert nb < MOD_ROWS
    cc = jnp.concatenate([c, jnp.zeros((CTX_ROW - nb, d), F32), c_ctx[None]], axis=0)
    out = pl.pallas_call(
        _ada_kernel,
        grid=(depth, N_MOD),
        in_specs=[pl.BlockSpec((MOD_ROWS, d), lambda l, j: (0, 0)),
                  pl.BlockSpec((None, d, d), lambda l, j: (l, 0, j)),
                  pl.BlockSpec((None, 1, d), lambda l, j: (l * N_MOD + j, 0, 0))],
        out_specs=pl.BlockSpec((None, MOD_ROWS, d), lambda l, j: (l, 0, j)),
        out_shape=jax.ShapeDtypeStruct((depth, MOD_ROWS, N_MOD * d), F32),
        compiler_params=_cparams(("parallel", "parallel")),
        name="ada",
    )(cc, ada_w, ada_b.reshape(depth * N_MOD, 1, d))
    return out.reshape(depth * MOD_ROWS * N_MOD, 1, d)


def _ffn_kernel(x_ref, g_ref, sh_ref, sc_ref, gt_ref, wg_ref, wu_ref, wd_ref, o_ref, h_scr, acc_scr):
    f = pl.program_id(1)

    @pl.when(f == 0)
    def _():
        h_scr[...] = _norm_mod(x_ref[...], g_ref[...], sh_ref[...], sc_ref[...]).astype(BF16)
        acc_scr[...] = jnp.zeros_like(acc_scr)

    h = h_scr[...]
    a = jnp.dot(h, wg_ref[...], preferred_element_type=F32)
    u = jnp.dot(h, wu_ref[...], preferred_element_type=F32)
    t = (a * _sigmoid(a)) * u
    acc_scr[...] += jnp.dot(t.astype(BF16), wd_ref[...], preferred_element_type=F32)

    @pl.when(f == pl.num_programs(1) - 1)
    def _():
        o_ref[...] = x_ref[...] + gt_ref[...] * acc_scr[...]


def _ffn_dense(x, mods, g_all, wg, wu, wd, layer, tiles_per_batch, ctx, tm=FFN_TM, tf=FFN_TF):
    t, d = x.shape
    ff = wg.shape[1]
    tm = min(tm, t)
    tf = min(tf, ff)
    mi = functools.partial(_mod_index, layer, tiles_per_batch=tiles_per_batch // tm if not ctx else 1, ctx=ctx)
    return pl.pallas_call(
        _ffn_kernel,
        grid=(t // tm, ff // tf),
        in_specs=[pl.BlockSpec((tm, d), lambda i, f: (i, 0)),
                  pl.BlockSpec((None, 1, d), lambda i, f: (layer, 0, 0)),
                  _mod_spec(d, mi(3)), _mod_spec(d, mi(4)), _mod_spec(d, mi(5)),
                  pl.BlockSpec((d, tf), lambda i, f: (0, f)),
                  pl.BlockSpec((d, tf), lambda i, f: (0, f)),
                  pl.BlockSpec((tf, d), lambda i, f: (f, 0))],
        out_specs=pl.BlockSpec((tm, d), lambda i, f: (i, 0)),
        out_shape=jax.ShapeDtypeStruct((t, d), F32),
        scratch_shapes=[pltpu.VMEM((tm, d), BF16), pltpu.VMEM((tm, d), F32)],
        compiler_params=_cparams(("parallel", "arbitrary")),
        name="ffn_dense",
    )(x, g_all, mods, mods, mods, wg, wu, wd)


def _proj_res_kernel(a_ref, w_ref, x_ref, gt_ref, o_ref):
    y = jnp.dot(a_ref[...], w_ref[...], preferred_element_type=F32)
    o_ref[...] = x_ref[...] + gt_ref[...] * y


def _proj_res(a, w, x, mods, layer, j, tiles_per_batch, ctx, tm=PROJ_TM):
    t, d = x.shape
    k = a.shape[1]
    tm = min(tm, t)
    mi = _mod_index(layer, j, tiles_per_batch // tm if not ctx else 1, ctx)
    return pl.pallas_call(
        _proj_res_kernel,
        grid=(t // tm,),
        in_specs=[pl.BlockSpec((tm, k), lambda i: (i, 0)),
                  pl.BlockSpec((k, d), lambda i: (0, 0)),
                  pl.BlockSpec((tm, d), lambda i: (i, 0)),
                  _mod_spec(d, mi)],
        out_specs=pl.BlockSpec((tm, d), lambda i: (i, 0)),
        out_shape=jax.ShapeDtypeStruct((t, d), F32),
        compiler_params=_cparams(("parallel",)),
        name="proj_res",
    )(a, w, x, mods)


def _dft_angles(n):
    a = np.arange(n)
    return 2.0 * np.pi * ((a[:, None] * a[None, :]) % n) / n


def _seq_dft_tables(n):
    kp = np.arange(DFT_P)[None, :, None]
    p = np.arange(DFT_P)[None, None, :]
    r = np.arange(DFT_R)[:, None, None]
    th = 2.0 * np.pi * ((kp * (DFT_R * p + r)) % n) / n
    tab1 = np.concatenate([np.cos(th), -np.sin(th)], axis=1)
    th2 = _dft_angles(DFT_R)
    c2, s2 = np.cos(th2), np.sin(th2)
    tab2 = np.block([[c2, s2], [-s2, c2]])
    return jnp.asarray(tab1, BF16), jnp.asarray(tab2, BF16)


def _chan_dft_tables(group_dim):
    th = _dft_angles(group_dim)
    return jnp.asarray(np.cos(th), BF16), jnp.asarray(np.sin(th), BF16)


def _ctx_dft_table(n_ctx):
    th = _dft_angles(n_ctx)
    return jnp.asarray(np.concatenate([np.cos(th), -np.sin(th)], axis=0), BF16)


def _fnet_stage1_kernel(x_ref, g_ref, sh_ref, sc_ref, tab_ref, zr_ref, zi_ref):
    d = g_ref.shape[-1]
    for j in range(FNET_RB):
        h = _norm_mod(x_ref[:, j * d:(j + 1) * d], g_ref[...], sh_ref[...], sc_ref[...]).astype(BF16)
        z = jnp.dot(tab_ref[j], h, preferred_element_type=F32)
        zr_ref[:, j * d:(j + 1) * d] = z[:DFT_P].astype(BF16)
        zi_ref[:, j * d:(j + 1) * d] = z[DFT_P:].astype(BF16)


def _mix_tail(pr, pi, cc_ref, sc_ref, wout_ref, scale):
    gd = cc_ref.shape[0]
    ys = []
    for g in range(pr.shape[1] // gd):
        ys.append(jnp.dot(pr[:, g * gd:(g + 1) * gd], cc_ref[...], preferred_element_type=F32)
                  + jnp.dot(pi[:, g * gd:(g + 1) * gd], sc_ref[...], preferred_element_type=F32))
    mixed = (jnp.concatenate(ys, axis=-1) * scale).astype(BF16)
    return jnp.dot(mixed, wout_ref[...], preferred_element_type=F32)


def _fnet_stage2_kernel(zr_ref, zi_ref, tab2_ref, cc_ref, sc_ref, wout_ref, x_ref, gt_ref, o_ref, p_scr, *, scale):
    d = gt_ref.shape[-1]
    for j in range(FNET_KB):
        z = jnp.concatenate([zr_ref[j * DFT_R:(j + 1) * DFT_R, :], zi_ref[j * DFT_R:(j + 1) * DFT_R, :]], axis=0)
        p = jnp.dot(tab2_ref[...], z, preferred_element_type=F32)
        p_scr[j * DFT_R:(j + 1) * DFT_R, :d] = p[:DFT_R].astype(BF16)
        p_scr[j * DFT_R:(j + 1) * DFT_R, d:] = p[DFT_R:].astype(BF16)
    y = _mix_tail(p_scr[:, :d], p_scr[:, d:], cc_ref, sc_ref, wout_ref, scale)
    for j in range(FNET_KB):
        o_ref[:, j * d:(j + 1) * d] = (x_ref[:, j * d:(j + 1) * d]
                                       + gt_ref[...] * y[j * DFT_R:(j + 1) * DFT_R, :])


def _fnet_lat(x, mods, g_all, w_out, layer, n_batch):
    t, d = x.shape
    s = t // n_batch
    assert s == DFT_P * DFT_R
    tab1, tab2 = _seq_dft_tables(s)
    ccos, csin = _chan_dft_tables(d // FNET_GROUPS)
    mi = lambda j: (lambda b, *_: (layer * MOD_ROWS + b) * N_MOD + j)
    xv = x.reshape(n_batch, DFT_P, DFT_R * d)
    wide = FNET_RB * d
    zr, zi = pl.pallas_call(
        _fnet_stage1_kernel,
        grid=(n_batch, DFT_R // FNET_RB),
        in_specs=[pl.BlockSpec((None, DFT_P, wide), lambda b, r: (b, 0, r)),
                  pl.BlockSpec((None, 1, d), lambda b, r: (layer, 0, 0)),
                  _mod_spec(d, mi(0)), _mod_spec(d, mi(1)),
                  pl.BlockSpec((FNET_RB, 2 * DFT_P, DFT_P), lambda b, r: (r, 0, 0))],
        out_specs=[pl.BlockSpec((None, DFT_P, wide), lambda b, r: (b, 0, r))] * 2,
        out_shape=[jax.ShapeDtypeStruct((n_batch, DFT_P, DFT_R * d), BF16)] * 2,
        compiler_params=_cparams(("parallel", "parallel")),
        name="fnet_stage1",
    )(xv, g_all, mods, mods, tab1)
    zr = zr.reshape(n_batch, DFT_P * DFT_R, d)
    zi = zi.reshape(n_batch, DFT_P * DFT_R, d)
    xo = x.reshape(n_batch, DFT_R, DFT_P * d)
    rows = FNET_KB * DFT_R
    scale = float(1.0 / np.sqrt(float(s) * (d // FNET_GROUPS)))
    out = pl.pallas_call(
        functools.partial(_fnet_stage2_kernel, scale=scale),
        grid=(n_batch, DFT_P // FNET_KB),
        in_specs=[pl.BlockSpec((None, rows, d), lambda b, k: (b, k, 0)),
                  pl.BlockSpec((None, rows, d), lambda b, k: (b, k, 0)),
                  pl.BlockSpec((2 * DFT_R, 2 * DFT_R), lambda b, k: (0, 0)),
                  pl.BlockSpec(ccos.shape, lambda b, k: (0, 0)),
                  pl.BlockSpec(csin.shape, lambda b, k: (0, 0)),
                  pl.BlockSpec((d, d), lambda b, k: (0, 0)),
                  pl.BlockSpec((None, DFT_R, FNET_KB * d), lambda b, k: (b, 0, k)),
                  _mod_spec(d, mi(2))],
        out_specs=pl.BlockSpec((None, DFT_R, FNET_KB * d), lambda b, k: (b, 0, k)),
        out_shape=jax.ShapeDtypeStruct((n_batch, DFT_R, DFT_P * d), F32),
        scratch_shapes=[pltpu.VMEM((rows, 2 * d), BF16)],
        compiler_params=_cparams(("parallel", "parallel")),
        name="fnet_stage2",
    )(zr, zi, tab2, ccos, csin, w_out, xo, mods)
    return out.reshape(t, d)


def _fnet_ctx_kernel(x_ref, g_ref, sh_ref, sc_ref, gt_ref, tab_ref, cc_ref, sc2_ref, wout_ref, o_ref, *, scale):
    n = x_ref.shape[0]
    h = _norm_mod(x_ref[...], g_ref[...], sh_ref[...], sc_ref[...]).astype(BF16)
    p = jnp.dot(tab_ref[...], h, preferred_element_type=F32)
    y = _mix_tail(p[:n].astype(BF16), p[n:].astype(BF16), cc_ref, sc2_ref, wout_ref, scale)
    o_ref[...] = x_ref[...] + gt_ref[...] * y


def _fnet_ctx(x, mods, g_all, w_out, layer, n_batch):
    t, d = x.shape
    n = t // n_batch
    gd = d // FNET_GROUPS
    ccos, csin = _chan_dft_tables(gd)
    tab = _ctx_dft_table(n)
    mi = lambda j: (lambda b: (layer * MOD_ROWS + CTX_ROW) * N_MOD + j)
    scale = float(1.0 / np.sqrt(float(n) * gd))
    return pl.pallas_call(
        functools.partial(_fnet_ctx_kernel, scale=scale),
        grid=(n_batch,),
        in_specs=[pl.BlockSpec((n, d), lambda b: (b, 0)),
                  pl.BlockSpec((None, 1, d), lambda b: (layer, 0, 0)),
                  _mod_spec(d, mi(0)), _mod_spec(d, mi(1)), _mod_spec(d, mi(2)),
                  pl.BlockSpec(tab.shape, lambda b: (0, 0)),
                  pl.BlockSpec(ccos.shape, lambda b: (0, 0)),
                  pl.BlockSpec(csin.shape, lambda b: (0, 0)),
                  pl.BlockSpec((d, d), lambda b: (0, 0))],
        out_specs=pl.BlockSpec((n, d), lambda b: (b, 0)),
        out_shape=jax.ShapeDtypeStruct((t, d), F32),
        compiler_params=_cparams(("parallel",)),
        name="fnet_ctx",
    )(x, g_all, mods, mods, mods, tab, ccos, csin, w_out)


def _rope_tables(n_seq):
    rows = n_seq // GRID_W
    row = jnp.repeat(jnp.arange(rows, dtype=F32), GRID_W)
    col = jnp.tile(jnp.arange(GRID_W, dtype=F32), rows)
    inv_freq = ROPE_THETA ** (-jnp.arange(ROT_FREQS, dtype=F32) / ROT_FREQS)
    ang = jnp.stack([row[:, None] * inv_freq, col[:, None] * inv_freq], axis=1)
    cos, sin = jnp.cos(ang), jnp.sin(ang)
    zero = jnp.zeros_like(sin)
    cos_h = jnp.stack([cos, cos], axis=2).reshape(n_seq, HEAD_DIM)
    sin_lo = jnp.stack([-sin, zero], axis=2).reshape(n_seq, HEAD_DIM)
    sin_hi = jnp.stack([zero, sin], axis=2).reshape(n_seq, HEAD_DIM)
    rep = LANES // HEAD_DIM
    return jnp.tile(cos_h, (1, rep)), jnp.tile(sin_lo, (1, rep)), jnp.tile(sin_hi, (1, rep))


def _qkv_kernel(x_ref, g_ref, sh_ref, sc_ref, w_ref, *rest, rope, q_dim, kv_dim, q_scale):
    if rope:
        cos_ref, slo_ref, shi_ref, q_ref, kt_ref, v_ref = rest
    else:
        q_ref, kt_ref, v_ref = rest
    h = _norm_mod(x_ref[...], g_ref[...], sh_ref[...], sc_ref[...]).astype(BF16)
    qkv = jnp.dot(h, w_ref[...], preferred_element_type=F32)

    def rot(xs):
        if not rope:
            return xs
        return (xs * cos_ref[...] + pltpu.roll(xs, LANES - ROT_FREQS, axis=1) * slo_ref[...]
                + pltpu.roll(xs, ROT_FREQS, axis=1) * shi_ref[...])

    for j in range(q_dim // LANES):
        q_ref[:, j * LANES:(j + 1) * LANES] = (rot(qkv[:, j * LANES:(j + 1) * LANES]) * q_scale).astype(BF16)
    ks = [rot(qkv[:, q_dim + j * LANES:q_dim + (j + 1) * LANES]) for j in range(kv_dim // LANES)]
    kt_ref[...] = jnp.concatenate(ks, axis=-1).T.astype(BF16)
    v_ref[...] = qkv[:, q_dim + kv_dim:].astype(BF16)


def _qkv(x, mods, g_all, w_qkv, layer, n_batch, ctx, rope_tabs=None, tm=512):
    t, d = x.shape
    n = t // n_batch
    tm = min(tm, n)
    tpb = n // tm
    kv_dim = N_KV_HEADS * HEAD_DIM
    q_dim = w_qkv.shape[1] - 2 * kv_dim
    mi = lambda j: _mod_index(layer, j, tpb, ctx)
    in_specs = [pl.BlockSpec((tm, d), lambda i: (i, 0)),
                pl.BlockSpec((None, 1, d), lambda i: (layer, 0, 0)),
                _mod_spec(d, mi(0)), _mod_spec(d, mi(1)),
                pl.BlockSpec(w_qkv.shape, lambda i: (0, 0))]
    args = [x, g_all, mods, mods, w_qkv]
    rope = rope_tabs is not None
    if rope:
        in_specs += [pl.BlockSpec((tm, LANES), lambda i: (i % tpb, 0))] * 3
        args += list(rope_tabs)
    return pl.pallas_call(
        functools.partial(_qkv_kernel, rope=rope, q_dim=q_dim, kv_dim=kv_dim, q_scale=HEAD_DIM ** -0.5),
        grid=(t // tm,),
        in_specs=in_specs,
        out_specs=[pl.BlockSpec((tm, q_dim), lambda i: (i, 0)),
                   pl.BlockSpec((None, kv_dim, tm), lambda i: (i // tpb, 0, i % tpb)),
                   pl.BlockSpec((tm, kv_dim), lambda i: (i, 0))],
        out_shape=[jax.ShapeDtypeStruct((t, q_dim), BF16),
                   jax.ShapeDtypeStruct((n_batch, kv_dim, n), BF16),
                   jax.ShapeDtypeStruct((t, kv_dim), BF16)],
        compiler_params=_cparams(("parallel",)),
        name="qkv_ctx" if ctx else "qkv_lat",
    )(*args)


def _heads_attend(q_ref, kt, v, bias, sink_ref, o_ref):
    n_heads = q_ref.shape[1] // HEAD_DIM
    group = n_heads // N_KV_HEADS
    outs = []
    for hd in range(n_heads):
        g = hd // group
        s = jnp.dot(q_ref[:, hd * HEAD_DIM:(hd + 1) * HEAD_DIM], kt[g * HEAD_DIM:(g + 1) * HEAD_DIM, :],
                    preferred_element_type=F32)
        if bias is not None:
            s = s + bias
        sink = sink_ref[hd]
        m = jnp.maximum(jnp.max(s, axis=-1, keepdims=True), sink)
        p = jnp.exp(s - m)
        den = jnp.sum(p, axis=-1, keepdims=True) + jnp.exp(sink - m)
        pv = jnp.dot(p.astype(BF16), v, preferred_element_type=F32)
        outs.append(pv[:, g * HEAD_DIM:(g + 1) * HEAD_DIM] / den)
    o_ref[...] = jnp.concatenate(outs, axis=-1).astype(BF16)


def _attn_lat_kernel(sink_ref, q_ref, ktp_ref, ktc_ref, ktn_ref, vp_ref, vc_ref, vn_ref, ktx_ref, vx_ref, bias_ref,
                     o_ref, kt_scr, v_scr):
    w = ktc_ref.shape[1]
    n_ctx = vx_ref.shape[0]
    for c, (kr, vr) in enumerate(((ktp_ref, vp_ref), (ktc_ref, vc_ref), (ktn_ref, vn_ref))):
        kt_scr[:, c * w:(c + 1) * w] = kr[...]
        v_scr[c * w:(c + 1) * w, :] = vr[...]
    kt_scr[:, 3 * w:3 * w + n_ctx] = ktx_ref[...]
    v_scr[3 * w:3 * w + n_ctx, :] = vx_ref[...]
    _heads_attend(q_ref, kt_scr[...], v_scr[...], bias_ref[...], sink_ref, o_ref)


def _attn_lat(q, kt, v, ktx, vx, sinks, n_batch):
    t, qd = q.shape
    s = t // n_batch
    n_ctx = vx.shape[0] // n_batch
    w = WINDOW
    nb = s // w
    kvd = v.shape[1]
    j_all = 3 * w + n_ctx
    qi = np.arange(w)[:, None]
    ki = np.arange(3 * w)[None, :]
    band = np.abs(ki - w - qi) <= WINDOW
    bias = np.zeros((3, w, j_all), np.float32)
    for var in range(3):
        ok = band.copy()
        if var == 0:
            ok[:, :w] = False
        if var == 2:
            ok[:, 2 * w:] = False
        bias[var, :, :3 * w] = np.where(ok, 0.0, NEG)
    if nb == 1:
        bias[:, :, :w] = NEG
        bias[:, :, 2 * w:3 * w] = NEG
    bias = jnp.asarray(bias)
    prev = lambda b, i: (b, 0, jnp.maximum(i - 1, 0))
    nxt = lambda b, i: (b, 0, jnp.minimum(i + 1, nb - 1))
    vprev = lambda b, i: (b * nb + jnp.maximum(i - 1, 0), 0)
    vnxt = lambda b, i: (b * nb + jnp.minimum(i + 1, nb - 1), 0)
    var_idx = lambda b, i: (jnp.where(i == 0, 0, jnp.where(i == nb - 1, 2, 1)), 0, 0)
    return pl.pallas_call(
        _attn_lat_kernel,
        grid=(n_batch, nb),
        in_specs=[pl.BlockSpec(memory_space=pltpu.SMEM),
                  pl.BlockSpec((w, qd), lambda b, i: (b * nb + i, 0)),
                  pl.BlockSpec((None, kvd, w), prev),
                  pl.BlockSpec((None, kvd, w), lambda b, i: (b, 0, i)),
                  pl.BlockSpec((None, kvd, w), nxt),
                  pl.BlockSpec((w, kvd), vprev),
                  pl.BlockSpec((w, kvd), lambda b, i: (b * nb + i, 0)),
                  pl.BlockSpec((w, kvd), vnxt),
                  pl.BlockSpec((None, kvd, n_ctx), lambda b, i: (b, 0, 0)),
                  pl.BlockSpec((n_ctx, kvd), lambda b, i: (b, 0)),
                  pl.BlockSpec((None, w, j_all), var_idx)],
        out_specs=pl.BlockSpec((w, qd), lambda b, i: (b * nb + i, 0)),
        out_shape=jax.ShapeDtypeStruct((t, qd), BF16),
        scratch_shapes=[pltpu.VMEM((kvd, j_all), BF16), pltpu.VMEM((j_all, kvd), BF16)],
        compiler_params=_cparams(("parallel", "parallel")),
        name="attn_lat",
    )(sinks, q, kt, kt, kt, v, v, v, ktx, vx, bias)


def _attn_ctx_kernel(sink_ref, q_ref, kt_ref, v_ref, o_ref):
    _heads_attend(q_ref, kt_ref[...], v_ref[...], None, sink_ref, o_ref)


def _attn_ctx(q, kt, v, sinks, n_batch):
    t, qd = q.shape
    n = t // n_batch
    kvd = v.shape[1]
    return pl.pallas_call(
        _attn_ctx_kernel,
        grid=(n_batch,),
        in_specs=[pl.BlockSpec(memory_space=pltpu.SMEM),
                  pl.BlockSpec((n, qd), lambda b: (b, 0)),
                  pl.BlockSpec((None, kvd, n), lambda b: (b, 0, 0)),
                  pl.BlockSpec((n, kvd), lambda b: (b, 0))],
        out_specs=pl.BlockSpec((n, qd), lambda b: (b, 0)),
        out_shape=jax.ShapeDtypeStruct((t, qd), BF16),
        compiler_params=_cparams(("parallel",)),
        name="attn_ctx",
    )(sinks, q, kt, v)


ROUTE_I1, ROUTE_I2, ROUTE_R1, ROUTE_R2, ROUTE_G1, ROUTE_G2 = range(6)


def _router_kernel(x_ref, g_ref, sh_ref, sc_ref, wr_ref, tri_ref, h_ref, route_ref, cnt_ref, carry_scr):
    @pl.when(pl.program_id(0) == 0)
    def _():
        carry_scr[...] = jnp.zeros_like(carry_scr)

    h = _norm_mod(x_ref[...], g_ref[...], sh_ref[...], sc_ref[...])
    h_ref[...] = h
    logits = jnp.dot(h, wr_ref[...], preferred_element_type=F32, precision=lax.Precision.HIGHEST)
    lane = lax.broadcasted_iota(jnp.int32, logits.shape, 1)
    lane_f = lane.astype(F32)
    logits = jnp.where(lane < N_EXPERTS, logits, -jnp.inf)
    m1 = jnp.max(logits, axis=-1, keepdims=True)
    i1 = jnp.min(jnp.where(logits == m1, lane_f, float(LANES)), axis=-1, keepdims=True)
    oh1 = lane_f == i1
    rest = jnp.where(oh1, -jnp.inf, logits)
    m2 = jnp.max(rest, axis=-1, keepdims=True)
    i2 = jnp.min(jnp.where(rest == m2, lane_f, float(LANES)), axis=-1, keepdims=True)
    oh2 = lane_f == i2
    e2 = jnp.exp(m2 - m1)
    g1 = 1.0 / (1.0 + e2)
    g2 = e2 / (1.0 + e2)
    sel = jnp.where(oh1, 1.0, 0.0) + jnp.where(oh2, 1.0, 0.0)
    before = jnp.dot(tri_ref[...], sel.astype(BF16), preferred_element_type=F32) + carry_scr[0:1, :]
    r1 = jnp.sum(jnp.where(oh1, before, 0.0), axis=-1, keepdims=True)
    r2 = jnp.sum(jnp.where(oh2, before, 0.0), axis=-1, keepdims=True)
    total = carry_scr[0:1, :] + jnp.sum(sel, axis=0, keepdims=True)
    carry_scr[...] = jnp.broadcast_to(total, carry_scr.shape)
    cnt_ref[...] = jnp.broadcast_to(total, cnt_ref.shape)
    rec = jnp.zeros_like(logits)
    for ln, val in ((ROUTE_I1, i1), (ROUTE_I2, i2), (ROUTE_R1, r1), (ROUTE_R2, r2), (ROUTE_G1, g1), (ROUTE_G2, g2)):
        rec = jnp.where(lane == ln, val, rec)
    route_ref[...] = rec


def _router(x, mods, g_all, w_router, layer, tiles_per_batch, ctx, tm=ROUTE_TM):
    t, d = x.shape
    tm = min(tm, t)
    mi = lambda j: _mod_index(layer, j, tiles_per_batch // tm if not ctx else 1, ctx)
    wr = jnp.zeros((d, LANES), F32).at[:, :N_EXPERTS].set(w_router)
    tri = jnp.asarray(np.tril(np.ones((tm, tm), np.float32), -1), BF16)
    return pl.pallas_call(
        _router_kernel,
        grid=(t // tm,),
        in_specs=[pl.BlockSpec((tm, d), lambda i: (i, 0)),
                  pl.BlockSpec((None, 1, d), lambda i: (layer, 0, 0)),
                  _mod_spec(d, mi(3)), _mod_spec(d, mi(4)),
                  pl.BlockSpec((d, LANES), lambda i: (0, 0)),
                  pl.BlockSpec((tm, tm), lambda i: (0, 0))],
        out_specs=[pl.BlockSpec((tm, d), lambda i: (i, 0)),
                   pl.BlockSpec((tm, LANES), lambda i: (i, 0)),
                   pl.BlockSpec((8, LANES), lambda i: (0, 0))],
        out_shape=[jax.ShapeDtypeStruct((t, d), F32),
                   jax.ShapeDtypeStruct((t, LANES), F32),
                   jax.ShapeDtypeStruct((8, LANES), F32)],
        scratch_shapes=[pltpu.VMEM((8, LANES), F32)],
        compiler_params=_cparams(("arbitrary",)),
        name="router",
    )(x, g_all, mods, mods, wr, tri)


def _row_copy(src, dst, src_row, dst_row, sem):
    return pltpu.make_async_copy(src.at[pl.ds(src_row, 1)], dst.at[pl.ds(dst_row, 1)], sem)


def _dispatch_kernel(d1_ref, d2_ref, h_ref, xs_in_ref, xs_ref, sems):
    del xs_in_ref
    tm = d1_ref.shape[-1]
    base = pl.program_id(0) * tm

    def issue(r, _):
        _row_copy(h_ref, xs_ref, base + r, d1_ref[0, 0, r], sems.at[0]).start()
        _row_copy(h_ref, xs_ref, base + r, d2_ref[0, 0, r], sems.at[1]).start()
        return 0

    lax.fori_loop(0, tm, issue, 0)

    def drain(r, _):
        _row_copy(h_ref, xs_ref, 0, 0, sems.at[0]).wait()
        _row_copy(h_ref, xs_ref, 0, 0, sems.at[1]).wait()
        return 0

    lax.fori_loop(0, tm, drain, 0)


def _dispatch(h, d1, d2, cap, tm):
    t, d = h.shape
    idx_spec = pl.BlockSpec((1, 1, tm), lambda i: (i, 0, 0), memory_space=pltpu.SMEM)
    return pl.pallas_call(
        _dispatch_kernel,
        grid=(t // tm,),
        in_specs=[idx_spec, idx_spec,
                  pl.BlockSpec(memory_space=pl.ANY), pl.BlockSpec(memory_space=pl.ANY)],
        out_specs=pl.BlockSpec(memory_space=pl.ANY),
        out_shape=jax.ShapeDtypeStruct((cap, d), F32),
        scratch_shapes=[pltpu.SemaphoreType.DMA((2,))],
        input_output_aliases={3: 0},
        compiler_params=_cparams(("arbitrary",)),
        name="moe_dispatch",
    )(d1.reshape(t // tm, 1, tm), d2.reshape(t // tm, 1, tm), h, jnp.zeros((cap, d), F32))


def _moe_ffn_kernel(be_ref, bv_ref, xs_ref, wg_ref, wu_ref, wd_ref, ys_ref, h_scr, acc_scr):
    i = pl.program_id(0)
    f = pl.program_id(1)
    last = f == pl.num_programs(1) - 1
    valid = bv_ref[i] > 0

    @pl.when(jnp.logical_and(valid, f == 0))
    def _():
        h_scr[...] = xs_ref[...].astype(BF16)
        acc_scr[...] = jnp.zeros_like(acc_scr)

    @pl.when(valid)
    def _():
        h = h_scr[...]
        a = jnp.dot(h, wg_ref[...], preferred_element_type=F32)
        u = jnp.dot(h, wu_ref[...], preferred_element_type=F32)
        t = (a * _sigmoid(a)) * u
        acc_scr[...] += jnp.dot(t.astype(BF16), wd_ref[...], preferred_element_type=F32)

    @pl.when(jnp.logical_and(valid, last))
    def _():
        ys_ref[...] = acc_scr[...]

    @pl.when(jnp.logical_and(jnp.logical_not(valid), last))
    def _():
        ys_ref[...] = jnp.zeros_like(ys_ref)


def _moe_ffn(xs, block_expert, block_valid, wg, wu, wd, tm, tf=FFN_TF):
    cap, d = xs.shape
    ff = wg.shape[2]
    tf = min(tf, ff)
    grid_spec = pltpu.PrefetchScalarGridSpec(
        num_scalar_prefetch=2,
        grid=(cap // tm, ff // tf),
        in_specs=[pl.BlockSpec((tm, d), lambda i, f, be, bv: (i, 0)),
                  pl.BlockSpec((None, d, tf), lambda i, f, be, bv: (be[i], 0, f)),
                  pl.BlockSpec((None, d, tf), lambda i, f, be, bv: (be[i], 0, f)),
                  pl.BlockSpec((None, tf, d), lambda i, f, be, bv: (be[i], f, 0))],
        out_specs=pl.BlockSpec((tm, d), lambda i, f, be, bv: (i, 0)),
        scratch_shapes=[pltpu.VMEM((tm, d), BF16), pltpu.VMEM((tm, d), F32)],
    )
    return pl.pallas_call(
        _moe_ffn_kernel,
        grid_spec=grid_spec,
        out_shape=jax.ShapeDtypeStruct((cap, d), F32),
        compiler_params=_cparams(("parallel", "arbitrary")),
        name="moe_ffn",
    )(block_expert, block_valid, xs, wg, wu, wd)


def _combine_kernel(d1_ref, d2_ref, route_ref, x_ref, gt_ref, fg_ref, ys_ref, o_ref, buf1, buf2, sems, *, final_norm):
    tm = x_ref.shape[0]

    def issue(r, _):
        _row_copy(ys_ref, buf1, d1_ref[0, 0, r], r, sems.at[0]).start()
        _row_copy(ys_ref, buf2, d2_ref[0, 0, r], r, sems.at[1]).start()
        return 0

    lax.fori_loop(0, tm, issue, 0)

    def drain(r, _):
        _row_copy(ys_ref, buf1, 0, 0, sems.at[0]).wait()
        _row_copy(ys_ref, buf2, 0, 0, sems.at[1]).wait()
        return 0

    lax.fori_loop(0, tm, drain, 0)
    rec = route_ref[...]
    g1 = rec[:, ROUTE_G1:ROUTE_G1 + 1]
    g2 = rec[:, ROUTE_G2:ROUTE_G2 + 1]
    y = buf1[...] * g1 + buf2[...] * g2
    out = x_ref[...] + gt_ref[...] * y
    if final_norm:
        out = out * lax.rsqrt(jnp.mean(out * out, axis=-1, keepdims=True) + EPS) * fg_ref[...]
    o_ref[...] = out


def _combine(ys, d1, d2, route, x, mods, final_g, layer, tiles_per_batch, ctx, final_norm, tm):
    t, d = x.shape
    mi = _mod_index(layer, 5, tiles_per_batch // tm if not ctx else 1, ctx)
    idx_spec = pl.BlockSpec((1, 1, tm), lambda i: (i, 0, 0), memory_space=pltpu.SMEM)
    return pl.pallas_call(
        functools.partial(_combine_kernel, final_norm=final_norm),
        grid=(t // tm,),
        in_specs=[idx_spec, idx_spec,
                  pl.BlockSpec((tm, LANES), lambda i: (i, 0)),
                  pl.BlockSpec((tm, d), lambda i: (i, 0)),
                  _mod_spec(d, mi),
                  pl.BlockSpec((1, d), lambda i: (0, 0)),
                  pl.BlockSpec(memory_space=pl.ANY)],
        out_specs=pl.BlockSpec((tm, d), lambda i: (i, 0)),
        out_shape=jax.ShapeDtypeStruct((t, d), F32),
        scratch_shapes=[pltpu.VMEM((tm, d), F32), pltpu.VMEM((tm, d), F32), pltpu.SemaphoreType.DMA((2,))],
        compiler_params=_cparams(("arbitrary",)),
        name="moe_combine",
    )(d1.reshape(t // tm, 1, tm), d2.reshape(t // tm, 1, tm), route, x, mods, final_g.reshape(1, d), ys)


def _moe(x, mods, g_all, w_router, wg, wu, wd, final_g, layer, tiles_per_batch, ctx, final_norm):
    t, d = x.shape
    tm = min(MOE_TM, t)
    row_tm = min(ROW_TM, t)
    h, route, counts = _router(x, mods, g_all, w_router, layer, tiles_per_batch, ctx)
    counts = counts[0, :N_EXPERTS].astype(jnp.int32)
    padded = (counts + tm - 1) // tm * tm
    pad_end = jnp.cumsum(padded)
    pad_start = pad_end - padded
    n_blocks = (2 * t + tm - 1) // tm + N_EXPERTS
    cap = n_blocks * tm
    i1 = route[:, ROUTE_I1].astype(jnp.int32)
    i2 = route[:, ROUTE_I2].astype(jnp.int32)
    d1 = pad_start[i1] + route[:, ROUTE_R1].astype(jnp.int32)
    d2 = pad_start[i2] + route[:, ROUTE_R2].astype(jnp.int32)
    starts = jnp.arange(n_blocks, dtype=jnp.int32) * tm
    block_expert = jnp.minimum(jnp.searchsorted(pad_end, starts, side='right'), N_EXPERTS - 1).astype(jnp.int32)
    block_valid = (starts < pad_end[-1]).astype(jnp.int32)
    xs = _dispatch(h, d1, d2, cap, row_tm)
    ys = _moe_ffn(xs, block_expert, block_valid, wg, wu, wd, tm)
    return _combine(ys, d1, d2, route, x, mods, final_g, layer, tiles_per_batch, ctx, final_norm, row_tm)


def kernel(x, c, ctx, c_ctx, ada_w, ada_b, norm_mix_g, norm_ffn_g, fnet_w_out, attn_w_qkv, attn_w_o, attn_sinks,
           ffn_w_gate, ffn_w_up, ffn_w_down, moe_w_router, moe_w_gate, moe_w_up, moe_w_down, final_norm_g):
    n_batch, s, d = x.shape
    n_ctx = ctx.shape[1]
    depth = ada_w.shape[0]
    mods = _ada(c, c_ctx, ada_w, ada_b)
    g_mix = norm_mix_g.reshape(depth, 1, d)
    g_ffn = norm_ffn_g.reshape(depth, 1, d)
    rope_tabs = _rope_tables(s)
    x_lat = x.reshape(n_batch * s, d)
    x_ctx = ctx.reshape(n_batch * n_ctx, d)
    bf = lambda a: a.astype(BF16)
    for i in range(depth):
        j = i // 2
        last = i == depth - 1
        if i % 2 == 0:
            w_out = bf(fnet_w_out[j])
            x_lat = _fnet_lat(x_lat, mods, g_mix, w_out, i, n_batch)
            if not last:
                x_ctx = _fnet_ctx(x_ctx, mods, g_mix, w_out, i, n_batch)
            wg, wu, wd = bf(ffn_w_gate[j]), bf(ffn_w_up[j]), bf(ffn_w_down[j])
            x_lat = _ffn_dense(x_lat, mods, g_ffn, wg, wu, wd, i, s, False)
            if not last:
                x_ctx = _ffn_dense(x_ctx, mods, g_ffn, wg, wu, wd, i, n_ctx, True)
        else:
            w_qkv, w_o = bf(attn_w_qkv[j]), bf(attn_w_o[j])
            q, kt, v = _qkv(x_lat, mods, g_mix, w_qkv, i, n_batch, False, rope_tabs)
            qx, ktx, vx = _qkv(x_ctx, mods, g_mix, w_qkv, i, n_batch, True)
            o = _attn_lat(q, kt, v, ktx, vx, attn_sinks[j], n_batch)
            x_lat = _proj_res(o, w_o, x_lat, mods, i, 2, s, False)
            if not last:
                ox = _attn_ctx(qx, ktx, vx, attn_sinks[j], n_batch)
                x_ctx = _proj_res(ox, w_o, x_ctx, mods, i, 2, n_ctx, True)
            wg, wu, wd = bf(moe_w_gate[j]), bf(moe_w_up[j]), bf(moe_w_down[j])
            x_lat = _moe(x_lat, mods, g_ffn, moe_w_router[j], wg, wu, wd, final_norm_g, i, s, False, last)
            if not last:
                x_ctx = _moe(x_ctx, mods, g_ffn, moe_w_router[j], wg, wu, wd, final_norm_g, i, n_ctx, True, False)
    return x_lat.reshape(n_batch, s, d)
```

```python
import functools

import numpy as np
import jax
import jax.numpy as jnp
from jax import lax
from jax.experimental import pallas as pl
from jax.experimental.pallas import tpu as pltpu

F32 = jnp.float32
BF16 = jnp.bfloat16

GRID_W = 64
N_MOD = 6
EPS = 1e-6
FNET_GROUPS = 4
HEAD_DIM = 64
N_KV_HEADS = 4
WINDOW = 128
ROPE_THETA = 10000.0
ROT_FREQS = HEAD_DIM // 4
N_EXPERTS = 8
MOD_ROWS = 8
CTX_ROW = MOD_ROWS - 1
LANES = 128
NEG = -1e30
VMEM_LIMIT = 56 * 1024 * 1024

FFN_TM = 1024
FFN_TF = 512
MOE_TM = 512
ROW_TM = 512
PROJ_TM = 512
DFT_P = 64
DFT_R = 128
FNET_RB = 16
FNET_KB = 8


def _cparams(sem):
    return pltpu.CompilerParams(dimension_semantics=sem, vmem_limit_bytes=VMEM_LIMIT)


def _sigmoid(a):
    return 1.0 / (1.0 + jnp.exp(-a))


def _norm_mod(x, g, shift, scale):
    xn = x * lax.rsqrt(jnp.mean(x * x, axis=-1, keepdims=True) + EPS)
    return (xn * g) * (1.0 + scale) + shift


def _mod_spec(d, idx_fn):
    return pl.BlockSpec((None, 1, d), lambda *ids: (idx_fn(*ids), 0, 0))


def _mod_index(layer, j, tiles_per_batch, ctx):
    def fn(i, *_):
        b = CTX_ROW if ctx else i // tiles_per_batch
        return (layer * MOD_ROWS + b) * N_MOD + j
    return fn


def _ada_kernel(c_ref, w_ref, b_ref, o_ref):
    cc = c_ref[...]
    s = cc * _sigmoid(cc)
    o_ref[...] = jnp.dot(s, w_ref[...], preferred_element_type=F32,
                         precision=lax.Precision.HIGHEST) + b_ref[...]


def _ada(c, c_ctx, ada_w, ada_b):
    depth, d, _ = ada_w.shape
    nb = c.shape[0]
    assert nb < MOD_ROWS
    cc = jnp.concatenate([c, jnp.zeros((CTX_ROW - nb, d), F32), c_ctx[None]], axis=0)
    out = pl.pallas_call(
        _ada_kernel,
        grid=(depth, N_MOD),
        in_specs=[pl.BlockSpec((MOD_ROWS, d), lambda l, j: (0, 0)),
                  pl.BlockSpec((None, d, d), lambda l, j: (l, 0, j)),
                  pl.BlockSpec((None, 1, d), lambda l, j: (l * N_MOD + j, 0, 0))],
        out_specs=pl.BlockSpec((None, MOD_ROWS, d), lambda l, j: (l, 0, j)),
        out_shape=jax.ShapeDtypeStruct((depth, MOD_ROWS, N_MOD * d), F32),
        compiler_params=_cparams(("parallel", "parallel")),
        name="ada",
    )(cc, ada_w, ada_b.reshape(depth * N_MOD, 1, d))
    return out.reshape(depth * MOD_ROWS * N_MOD, 1, d)


def _ffn_kernel(x_ref, g_ref, sh_ref, sc_ref, gt_ref, wg_ref, wu_ref, wd_ref, o_ref, h_scr, acc_scr):
    f = pl.program_id(1)

    @pl.when(f == 0)
    def _():
        h_scr[...] = _norm_mod(x_ref[...], g_ref[...], sh_ref[...], sc_ref[...]).astype(BF16)
        acc_scr[...] = jnp.zeros_like(acc_scr)

    h = h_scr[...]
    a = jnp.dot(h, wg_ref[...], preferred_element_type=F32)
    u = jnp.dot(h, wu_ref[...], preferred_element_type=F32)
    t = (a * _sigmoid(a)) * u
    acc_scr[...] += jnp.dot(t.astype(BF16), wd_ref[...], preferred_element_type=F32)

    @pl.when(f == pl.num_programs(1) - 1)
    def _():
        o_ref[...] = x_ref[...] + gt_ref[...] * acc_scr[...]


def _ffn_dense(x, mods, g_all, wg, wu, wd, layer, tiles_per_batch, ctx, tm=FFN_TM, tf=FFN_TF):
    t, d = x.shape
    ff = wg.shape[1]
    tm = min(tm, t)
    tf = min(tf, ff)
    mi = functools.partial(_mod_index, layer, tiles_per_batch=tiles_per_batch // tm if not ctx else 1, ctx=ctx)
    return pl.pallas_call(
        _ffn_kernel,
        grid=(t // tm, ff // tf),
        in_specs=[pl.BlockSpec((tm, d), lambda i, f: (i, 0)),
                  pl.BlockSpec((None, 1, d), lambda i, f: (layer, 0, 0)),
                  _mod_spec(d, mi(3)), _mod_spec(d, mi(4)), _mod_spec(d, mi(5)),
                  pl.BlockSpec((d, tf), lambda i, f: (0, f)),
                  pl.BlockSpec((d, tf), lambda i, f: (0, f)),
                  pl.BlockSpec((tf, d), lambda i, f: (f, 0))],
        out_specs=pl.BlockSpec((tm, d), lambda i, f: (i, 0)),
        out_shape=jax.ShapeDtypeStruct((t, d), F32),
        scratch_shapes=[pltpu.VMEM((tm, d), BF16), pltpu.VMEM((tm, d), F32)],
        compiler_params=_cparams(("parallel", "arbitrary")),
        name="ffn_dense",
    )(x, g_all, mods, mods, mods, wg, wu, wd)


def _proj_res_kernel(a_ref, w_ref, x_ref, gt_ref, o_ref):
    y = jnp.dot(a_ref[...], w_ref[...], preferred_element_type=F32)
    o_ref[...] = x_ref[...] + gt_ref[...] * y


def _proj_res(a, w, x, mods, layer, j, tiles_per_batch, ctx, tm=PROJ_TM):
    t, d = x.shape
    k = a.shape[1]
    tm = min(tm, t)
    mi = _mod_index(layer, j, tiles_per_batch // tm if not ctx else 1, ctx)
    return pl.pallas_call(
        _proj_res_kernel,
        grid=(t // tm,),
        in_specs=[pl.BlockSpec((tm, k), lambda i: (i, 0)),
                  pl.BlockSpec((k, d), lambda i: (0, 0)),
                  pl.BlockSpec((tm, d), lambda i: (i, 0)),
                  _mod_spec(d, mi)],
        out_specs=pl.BlockSpec((tm, d), lambda i: (i, 0)),
        out_shape=jax.ShapeDtypeStruct((t, d), F32),
        compiler_params=_cparams(("parallel",)),
        name="proj_res",
    )(a, w, x, mods)


def _dft_angles(n):
    a = np.arange(n)
    return 2.0 * np.pi * ((a[:, None] * a[None, :]) % n) / n


def _seq_dft_tables(n):
    kp = np.arange(DFT_P)[None, :, None]
    p = np.arange(DFT_P)[None, None, :]
    r = np.arange(DFT_R)[:, None, None]
    th = 2.0 * np.pi * ((kp * (DFT_R * p + r)) % n) / n
    tab1 = np.concatenate([np.cos(th), -np.sin(th)], axis=1)
    th2 = _dft_angles(DFT_R)
    c2, s2 = np.cos(th2), np.sin(th2)
    tab2 = np.block([[c2, s2], [-s2, c2]])
    return jnp.asarray(tab1, BF16), jnp.asarray(tab2, BF16)


def _chan_dft_tables(group_dim):
    th = _dft_angles(group_dim)
    return jnp.asarray(np.cos(th), BF16), jnp.asarray(np.sin(th), BF16)


def _ctx_dft_table(n_ctx):
    th = _dft_angles(n_ctx)
    return jnp.asarray(np.concatenate([np.cos(th), -np.sin(th)], axis=0), BF16)


def _cols_store(scr, val):
    for c in range(scr.shape[0]):
        scr[c] = val[:, c * LANES:(c + 1) * LANES]


def _cols_load(scr):
    return jnp.concatenate([scr[c] for c in range(scr.shape[0])], axis=-1)


def _cols_store_rows(scr, sel, val):
    for c in range(scr.shape[0]):
        scr[c, sel, :] = val[:, c * LANES:(c + 1) * LANES]


def _cols_load_rows(scr, sel):
    return jnp.concatenate([scr[c, sel, :] for c in range(scr.shape[0])], axis=-1)


def _fnet_stage1_kernel(x_ref, g_ref, sh_ref, sc_ref, tab_ref, zr_ref, zi_ref, h_scr, zr_scr, zi_scr):
    d = g_ref.shape[-1]
    rows = DFT_P * FNET_RB
    _cols_store(h_scr, _norm_mod(x_ref[...].reshape(rows, d), g_ref[...], sh_ref[...], sc_ref[...]))
    for j in range(FNET_RB):
        sel = pl.ds(j, DFT_P, stride=FNET_RB)
        z = jnp.dot(tab_ref[j], _cols_load_rows(h_scr, sel).astype(BF16), preferred_element_type=F32)
        _cols_store_rows(zr_scr, sel, z[:DFT_P])
        _cols_store_rows(zi_scr, sel, z[DFT_P:])
    zr_ref[...] = _cols_load(zr_scr).astype(BF16).reshape(DFT_P, FNET_RB, d)
    zi_ref[...] = _cols_load(zi_scr).astype(BF16).reshape(DFT_P, FNET_RB, d)


def _mix_tail(pr, pi, cc_ref, sc_ref, wout_ref, scale):
    gd = cc_ref.shape[0]
    ys = []
    for g in range(pr.shape[1] // gd):
        ys.append(jnp.dot(pr[:, g * gd:(g + 1) * gd], cc_ref[...], preferred_element_type=F32)
                  + jnp.dot(pi[:, g * gd:(g + 1) * gd], sc_ref[...], preferred_element_type=F32))
    mixed = (jnp.concatenate(ys, axis=-1) * scale).astype(BF16)
    return jnp.dot(mixed, wout_ref[...], preferred_element_type=F32)


def _fnet_stage2_kernel(zr_ref, zi_ref, tab2_ref, cc_ref, sc_ref, wout_ref, x_ref, gt_ref, o_ref,
                        p_scr, x_scr, o_scr, *, scale):
    d = gt_ref.shape[-1]
    rows = DFT_R * FNET_KB
    for j in range(FNET_KB):
        z = jnp.concatenate([zr_ref[j * DFT_R:(j + 1) * DFT_R, :], zi_ref[j * DFT_R:(j + 1) * DFT_R, :]], axis=0)
        p = jnp.dot(tab2_ref[...], z, preferred_element_type=F32)
        p_scr[j * DFT_R:(j + 1) * DFT_R, :d] = p[:DFT_R].astype(BF16)
        p_scr[j * DFT_R:(j + 1) * DFT_R, d:] = p[DFT_R:].astype(BF16)
    y = _mix_tail(p_scr[:, :d], p_scr[:, d:], cc_ref, sc_ref, wout_ref, scale)
    _cols_store(x_scr, x_ref[...].reshape(rows, d))
    for j in range(FNET_KB):
        sel = pl.ds(j, DFT_R, stride=FNET_KB)
        _cols_store_rows(o_scr, sel, _cols_load_rows(x_scr, sel) + gt_ref[...] * y[j * DFT_R:(j + 1) * DFT_R, :])
    o_ref[...] = _cols_load(o_scr).reshape(DFT_R, FNET_KB, d)


def _fnet_lat(x, mods, g_all, w_out, layer, n_batch):
    t, d = x.shape
    s = t // n_batch
    assert s == DFT_P * DFT_R
    tab1, tab2 = _seq_dft_tables(s)
    ccos, csin = _chan_dft_tables(d // FNET_GROUPS)
    mi = lambda j: (lambda b, *_: (layer * MOD_ROWS + b) * N_MOD + j)
    xv = x.reshape(n_batch, DFT_P, DFT_R, d)
    blk1 = (None, DFT_P, FNET_RB, d)
    rows1 = DFT_P * FNET_RB
    zr, zi = pl.pallas_call(
        _fnet_stage1_kernel,
        grid=(n_batch, DFT_R // FNET_RB),
        in_specs=[pl.BlockSpec(blk1, lambda b, r: (b, 0, r, 0)),
                  pl.BlockSpec((None, 1, d), lambda b, r: (layer, 0, 0)),
                  _mod_spec(d, mi(0)), _mod_spec(d, mi(1)),
                  pl.BlockSpec((FNET_RB, 2 * DFT_P, DFT_P), lambda b, r: (r, 0, 0))],
        out_specs=[pl.BlockSpec(blk1, lambda b, r: (b, 0, r, 0))] * 2,
        out_shape=[jax.ShapeDtypeStruct((n_batch, DFT_P, DFT_R, d), BF16)] * 2,
        scratch_shapes=[pltpu.VMEM((d // LANES, rows1, LANES), F32)] * 3,
        compiler_params=_cparams(("parallel", "parallel")),
        name="fnet_stage1",
    )(xv, g_all, mods, mods, tab1)
    zr = zr.reshape(n_batch, DFT_P * DFT_R, d)
    zi = zi.reshape(n_batch, DFT_P * DFT_R, d)
    xo = x.reshape(n_batch, DFT_R, DFT_P, d)
    blk2 = (None, DFT_R, FNET_KB, d)
    rows = FNET_KB * DFT_R
    scale = float(1.0 / np.sqrt(float(s) * (d // FNET_GROUPS)))
    out = pl.pallas_call(
        functools.partial(_fnet_stage2_kernel, scale=scale),
        grid=(n_batch, DFT_P // FNET_KB),
        in_specs=[pl.BlockSpec((None, rows, d), lambda b, k: (b, k, 0)),
                  pl.BlockSpec((None, rows, d), lambda b, k: (b, k, 0)),
                  pl.BlockSpec((2 * DFT_R, 2 * DFT_R), lambda b, k: (0, 0)),
                  pl.BlockSpec(ccos.shape, lambda b, k: (0, 0)),
                  pl.BlockSpec(csin.shape, lambda b, k: (0, 0)),
                  pl.BlockSpec((d, d), lambda b, k: (0, 0)),
                  pl.BlockSpec(blk2, lambda b, k: (b, 0, k, 0)),
                  _mod_spec(d, mi(2))],
        out_specs=pl.BlockSpec(blk2, lambda b, k: (b, 0, k, 0)),
        out_shape=jax.ShapeDtypeStruct((n_batch, DFT_R, DFT_P, d), F32),
        scratch_shapes=[pltpu.VMEM((rows, 2 * d), BF16)] + [pltpu.VMEM((d // LANES, rows, LANES), F32)] * 2,
        compiler_params=_cparams(("parallel", "parallel")),
        name="fnet_stage2",
    )(zr, zi, tab2, ccos, csin, w_out, xo, mods)
    return out.reshape(t, d)


def _fnet_ctx_kernel(x_ref, g_ref, sh_ref, sc_ref, gt_ref, tab_ref, cc_ref, sc2_ref, wout_ref, o_ref, *, scale):
    n = x_ref.shape[0]
    h = _norm_mod(x_ref[...], g_ref[...], sh_ref[...], sc_ref[...]).astype(BF16)
    p = jnp.dot(tab_ref[...], h, preferred_element_type=F32)
    y = _mix_tail(p[:n].astype(BF16), p[n:].astype(BF16), cc_ref, sc2_ref, wout_ref, scale)
    o_ref[...] = x_ref[...] + gt_ref[...] * y


def _fnet_ctx(x, mods, g_all, w_out, layer, n_batch):
    t, d = x.shape
    n = t // n_batch
    gd = d // FNET_GROUPS
    ccos, csin = _chan_dft_tables(gd)
    tab = _ctx_dft_table(n)
    mi = lambda j: (lambda b: (layer * MOD_ROWS + CTX_ROW) * N_MOD + j)
    scale = float(1.0 / np.sqrt(float(n) * gd))
    return pl.pallas_call(
        functools.partial(_fnet_ctx_kernel, scale=scale),
        grid=(n_batch,),
        in_specs=[pl.BlockSpec((n, d), lambda b: (b, 0)),
                  pl.BlockSpec((None, 1, d), lambda b: (layer, 0, 0)),
                  _mod_spec(d, mi(0)), _mod_spec(d, mi(1)), _mod_spec(d, mi(2)),
                  pl.BlockSpec(tab.shape, lambda b: (0, 0)),
                  pl.BlockSpec(ccos.shape, lambda b: (0, 0)),
                  pl.BlockSpec(csin.shape, lambda b: (0, 0)),
                  pl.BlockSpec((d, d), lambda b: (0, 0))],
        out_specs=pl.BlockSpec((n, d), lambda b: (b, 0)),
        out_shape=jax.ShapeDtypeStruct((t, d), F32),
        compiler_params=_cparams(("parallel",)),
        name="fnet_ctx",
    )(x, g_all, mods, mods, mods, tab, ccos, csin, w_out)


def _rope_tables(n_seq):
    rows = n_seq // GRID_W
    row = jnp.repeat(jnp.arange(rows, dtype=F32), GRID_W)
    col = jnp.tile(jnp.arange(GRID_W, dtype=F32), rows)
    inv_freq = ROPE_THETA ** (-jnp.arange(ROT_FREQS, dtype=F32) / ROT_FREQS)
    ang = jnp.stack([row[:, None] * inv_freq, col[:, None] * inv_freq], axis=1)
    cos, sin = jnp.cos(ang), jnp.sin(ang)
    zero = jnp.zeros_like(sin)
    cos_h = jnp.stack([cos, cos], axis=2).reshape(n_seq, HEAD_DIM)
    sin_lo = jnp.stack([-sin, zero], axis=2).reshape(n_seq, HEAD_DIM)
    sin_hi = jnp.stack([zero, sin], axis=2).reshape(n_seq, HEAD_DIM)
    rep = LANES // HEAD_DIM
    return jnp.tile(cos_h, (1, rep)), jnp.tile(sin_lo, (1, rep)), jnp.tile(sin_hi, (1, rep))


def _qkv_kernel(x_ref, g_ref, sh_ref, sc_ref, w_ref, *rest, rope, q_dim, kv_dim, q_scale):
    if rope:
        cos_ref, slo_ref, shi_ref, q_ref, kt_ref, v_ref = rest
    else:
        q_ref, kt_ref, v_ref = rest
    h = _norm_mod(x_ref[...], g_ref[...], sh_ref[...], sc_ref[...]).astype(BF16)
    qkv = jnp.dot(h, w_ref[...], preferred_element_type=F32)

    def rot(xs):
        if not rope:
            return xs
        return (xs * cos_ref[...] + pltpu.roll(xs, LANES - ROT_FREQS, axis=1) * slo_ref[...]
                + pltpu.roll(xs, ROT_FREQS, axis=1) * shi_ref[...])

    for j in range(q_dim // LANES):
        q_ref[:, j * LANES:(j + 1) * LANES] = (rot(qkv[:, j * LANES:(j + 1) * LANES]) * q_scale).astype(BF16)
    ks = [rot(qkv[:, q_dim + j * LANES:q_dim + (j + 1) * LANES]) for j in range(kv_dim // LANES)]
    kt_ref[...] = jnp.concatenate(ks, axis=-1).T.astype(BF16)
    v_ref[...] = qkv[:, q_dim + kv_dim:].astype(BF16)


def _qkv(x, mods, g_all, w_qkv, layer, n_batch, ctx, rope_tabs=None, tm=512):
    t, d = x.shape
    n = t // n_batch
    tm = min(tm, n)
    tpb = n // tm
    kv_dim = N_KV_HEADS * HEAD_DIM
    q_dim = w_qkv.shape[1] - 2 * kv_dim
    mi = lambda j: _mod_index(layer, j, tpb, ctx)
    in_specs = [pl.BlockSpec((tm, d), lambda i: (i, 0)),
                pl.BlockSpec((None, 1, d), lambda i: (layer, 0, 0)),
                _mod_spec(d, mi(0)), _mod_spec(d, mi(1)),
                pl.BlockSpec(w_qkv.shape, lambda i: (0, 0))]
    args = [x, g_all, mods, mods, w_qkv]
    rope = rope_tabs is not None
    if rope:
        in_specs += [pl.BlockSpec((tm, LANES), lambda i: (i % tpb, 0))] * 3
        args += list(rope_tabs)
    return pl.pallas_call(
        functools.partial(_qkv_kernel, rope=rope, q_dim=q_dim, kv_dim=kv_dim, q_scale=HEAD_DIM ** -0.5),
        grid=(t // tm,),
        in_specs=in_specs,
        out_specs=[pl.BlockSpec((tm, q_dim), lambda i: (i, 0)),
                   pl.BlockSpec((None, kv_dim, tm), lambda i: (i // tpb, 0, i % tpb)),
                   pl.BlockSpec((tm, kv_dim), lambda i: (i, 0))],
        out_shape=[jax.ShapeDtypeStruct((t, q_dim), BF16),
                   jax.ShapeDtypeStruct((n_batch, kv_dim, n), BF16),
                   jax.ShapeDtypeStruct((t, kv_dim), BF16)],
        compiler_params=_cparams(("parallel",)),
        name="qkv_ctx" if ctx else "qkv_lat",
    )(*args)


def _heads_attend(q_ref, kt, v, bias, sink_ref, o_ref):
    n_heads = q_ref.shape[1] // HEAD_DIM
    group = n_heads // N_KV_HEADS
    outs = []
    for hd in range(n_heads):
        g = hd // group
        s = jnp.dot(q_ref[:, hd * HEAD_DIM:(hd + 1) * HEAD_DIM], kt[g * HEAD_DIM:(g + 1) * HEAD_DIM, :],
                    preferred_element_type=F32)
        if bias is not None:
            s = s + bias
        sink = sink_ref[hd]
        m = jnp.maximum(jnp.max(s, axis=-1, keepdims=True), sink)
        p = jnp.exp(s - m)
        den = jnp.sum(p, axis=-1, keepdims=True) + jnp.exp(sink - m)
        pv = jnp.dot(p.astype(BF16), v, preferred_element_type=F32)
        outs.append(pv[:, g * HEAD_DIM:(g + 1) * HEAD_DIM] / den)
    o_ref[...] = jnp.concatenate(outs, axis=-1).astype(BF16)


def _attn_lat_kernel(sink_ref, q_ref, ktp_ref, ktc_ref, ktn_ref, vp_ref, vc_ref, vn_ref, ktx_ref, vx_ref, bias_ref,
                     o_ref, kt_scr, v_scr):
    w = ktc_ref.shape[1]
    n_ctx = vx_ref.shape[0]
    for c, (kr, vr) in enumerate(((ktp_ref, vp_ref), (ktc_ref, vc_ref), (ktn_ref, vn_ref))):
        kt_scr[:, c * w:(c + 1) * w] = kr[...]
        v_scr[c * w:(c + 1) * w, :] = vr[...]
    kt_scr[:, 3 * w:3 * w + n_ctx] = ktx_ref[...]
    v_scr[3 * w:3 * w + n_ctx, :] = vx_ref[...]
    _heads_attend(q_ref, kt_scr[...], v_scr[...], bias_ref[...], sink_ref, o_ref)


def _attn_lat(q, kt, v, ktx, vx, sinks, n_batch):
    t, qd = q.shape
    s = t // n_batch
    n_ctx = vx.shape[0] // n_batch
    w = WINDOW
    nb = s // w
    kvd = v.shape[1]
    j_all = 3 * w + n_ctx
    qi = np.arange(w)[:, None]
    ki = np.arange(3 * w)[None, :]
    band = np.abs(ki - w - qi) <= WINDOW
    bias = np.zeros((3, w, j_all), np.float32)
    for var in range(3):
        ok = band.copy()
        if var == 0:
            ok[:, :w] = False
        if var == 2:
            ok[:, 2 * w:] = False
        bias[var, :, :3 * w] = np.where(ok, 0.0, NEG)
    if nb == 1:
        bias[:, :, :w] = NEG
        bias[:, :, 2 * w:3 * w] = NEG
    bias = jnp.asarray(bias)
    prev = lambda b, i: (b, 0, jnp.maximum(i - 1, 0))
    nxt = lambda b, i: (b, 0, jnp.minimum(i + 1, nb - 1))
    vprev = lambda b, i: (b * nb + jnp.maximum(i - 1, 0), 0)
    vnxt = lambda b, i: (b * nb + jnp.minimum(i + 1, nb - 1), 0)
    var_idx = lambda b, i: (jnp.where(i == 0, 0, jnp.where(i == nb - 1, 2, 1)), 0, 0)
    return pl.pallas_call(
        _attn_lat_kernel,
        grid=(n_batch, nb),
        in_specs=[pl.BlockSpec(memory_space=pltpu.SMEM),
                  pl.BlockSpec((w, qd), lambda b, i: (b * nb + i, 0)),
                  pl.BlockSpec((None, kvd, w), prev),
                  pl.BlockSpec((None, kvd, w), lambda b, i: (b, 0, i)),
                  pl.BlockSpec((None, kvd, w), nxt),
                  pl.BlockSpec((w, kvd), vprev),
                  pl.BlockSpec((w, kvd), lambda b, i: (b * nb + i, 0)),
                  pl.BlockSpec((w, kvd), vnxt),
                  pl.BlockSpec((None, kvd, n_ctx), lambda b, i: (b, 0, 0)),
                  pl.BlockSpec((n_ctx, kvd), lambda b, i: (b, 0)),
                  pl.BlockSpec((None, w, j_all), var_idx)],
        out_specs=pl.BlockSpec((w, qd), lambda b, i: (b * nb + i, 0)),
        out_shape=jax.ShapeDtypeStruct((t, qd), BF16),
        scratch_shapes=[pltpu.VMEM((kvd, j_all), BF16), pltpu.VMEM((j_all, kvd), BF16)],
        compiler_params=_cparams(("parallel", "parallel")),
        name="attn_lat",
    )(sinks, q, kt, kt, kt, v, v, v, ktx, vx, bias)


def _attn_ctx_kernel(sink_ref, q_ref, kt_ref, v_ref, o_ref):
    _heads_attend(q_ref, kt_ref[...], v_ref[...], None, sink_ref, o_ref)


def _attn_ctx(q, kt, v, sinks, n_batch):
    t, qd = q.shape
    n = t // n_batch
    kvd = v.shape[1]
    return pl.pallas_call(
        _attn_ctx_kernel,
        grid=(n_batch,),
        in_specs=[pl.BlockSpec(memory_space=pltpu.SMEM),
                  pl.BlockSpec((n, qd), lambda b: (b, 0)),
                  pl.BlockSpec((None, kvd, n), lambda b: (b, 0, 0)),
                  pl.BlockSpec((n, kvd), lambda b: (b, 0))],
        out_specs=pl.BlockSpec((n, qd), lambda b: (b, 0)),
        out_shape=jax.ShapeDtypeStruct((t, qd), BF16),
        compiler_params=_cparams(("parallel",)),
        name="attn_ctx",
    )(sinks, q, kt, v)


ROUTE_I1, ROUTE_I2, ROUTE_R1, ROUTE_R2, ROUTE_G1, ROUTE_G2 = range(6)
ROUTE_ROWS = 8
TAB_START, TAB_COUNT, TAB_USED, TAB_SIZE = 0, N_EXPERTS, 2 * N_EXPERTS, 2 * N_EXPERTS + 8
ROW_UNROLL = 8


def _router_kernel(x_ref, g_ref, sh_ref, sc_ref, wr_ref, tri_ref, route_ref, idx_ref, cnt_ref, carry_scr):
    @pl.when(pl.program_id(0) == 0)
    def _():
        carry_scr[...] = jnp.zeros_like(carry_scr)

    h = _norm_mod(x_ref[...], g_ref[...], sh_ref[...], sc_ref[...])
    logits = jnp.dot(h, wr_ref[...], preferred_element_type=F32, precision=lax.Precision.HIGHEST)
    lane = lax.broadcasted_iota(jnp.int32, logits.shape, 1)
    lane_f = lane.astype(F32)
    logits = jnp.where(lane < N_EXPERTS, logits, -jnp.inf)
    m1 = jnp.max(logits, axis=-1, keepdims=True)
    i1 = jnp.min(jnp.where(logits == m1, lane_f, float(LANES)), axis=-1, keepdims=True)
    oh1 = lane_f == i1
    rest = jnp.where(oh1, -jnp.inf, logits)
    m2 = jnp.max(rest, axis=-1, keepdims=True)
    i2 = jnp.min(jnp.where(rest == m2, lane_f, float(LANES)), axis=-1, keepdims=True)
    oh2 = lane_f == i2
    e2 = jnp.exp(m2 - m1)
    g1 = 1.0 / (1.0 + e2)
    g2 = e2 / (1.0 + e2)
    sel = jnp.where(oh1, 1.0, 0.0) + jnp.where(oh2, 1.0, 0.0)
    before = jnp.dot(tri_ref[...], sel.astype(BF16), preferred_element_type=F32) + carry_scr[0:1, :]
    r1 = jnp.sum(jnp.where(oh1, before, 0.0), axis=-1, keepdims=True)
    r2 = jnp.sum(jnp.where(oh2, before, 0.0), axis=-1, keepdims=True)
    total = carry_scr[0:1, :] + jnp.sum(sel, axis=0, keepdims=True)
    carry_scr[...] = jnp.broadcast_to(total, carry_scr.shape)
    cnt_ref[...] = jnp.broadcast_to(total, cnt_ref.shape)
    rec = jnp.zeros_like(logits)
    for ln, val in ((ROUTE_I1, i1), (ROUTE_I2, i2), (ROUTE_R1, r1), (ROUTE_R2, r2), (ROUTE_G1, g1), (ROUTE_G2, g2)):
        rec = jnp.where(lane == ln, val, rec)
    route_ref[...] = rec
    idx_ref[...] = rec.T[:ROUTE_ROWS, :].astype(jnp.int32)


def _router(x, mods, g_all, w_router, layer, tiles_per_batch, ctx, tm):
    t, d = x.shape
    mi = lambda j: _mod_index(layer, j, tiles_per_batch // tm if not ctx else 1, ctx)
    wr = jnp.zeros((d, LANES), F32).at[:, :N_EXPERTS].set(w_router)
    tri = jnp.asarray(np.tril(np.ones((tm, tm), np.float32), -1), BF16)
    return pl.pallas_call(
        _router_kernel,
        grid=(t // tm,),
        in_specs=[pl.BlockSpec((tm, d), lambda i: (i, 0)),
                  pl.BlockSpec((None, 1, d), lambda i: (layer, 0, 0)),
                  _mod_spec(d, mi(3)), _mod_spec(d, mi(4)),
                  pl.BlockSpec((d, LANES), lambda i: (0, 0)),
                  pl.BlockSpec((tm, tm), lambda i: (0, 0))],
        out_specs=[pl.BlockSpec((tm, LANES), lambda i: (i, 0)),
                   pl.BlockSpec((None, ROUTE_ROWS, tm), lambda i: (i, 0, 0)),
                   pl.BlockSpec((8, LANES), lambda i: (0, 0))],
        out_shape=[jax.ShapeDtypeStruct((t, LANES), F32),
                   jax.ShapeDtypeStruct((t // tm, ROUTE_ROWS, tm), jnp.int32),
                   jax.ShapeDtypeStruct((8, LANES), F32)],
        scratch_shapes=[pltpu.VMEM((8, LANES), F32)],
        compiler_params=_cparams(("arbitrary",)),
        name="router",
    )(x, g_all, mods, mods, wr, tri)


def _row_copy(src, dst, src_row, dst_row, sem):
    return pltpu.make_async_copy(src.at[pl.ds(src_row, 1)], dst.at[pl.ds(dst_row, 1)], sem)


def _slot_rows(tab_ref, idx_ref, r):
    s1 = tab_ref[TAB_START + idx_ref[0, ROUTE_I1, r]] + idx_ref[0, ROUTE_R1, r]
    s2 = tab_ref[TAB_START + idx_ref[0, ROUTE_I2, r]] + idx_ref[0, ROUTE_R2, r]
    return s1, s2


def _dispatch_kernel(tab_ref, idx_ref, x_ref, g_ref, sh_ref, sc_ref, xs_ref, h_scr, zero_scr, sems, *, group):
    tm = x_ref.shape[0]
    n_groups = xs_ref.shape[0] // group
    h_scr[...] = _norm_mod(x_ref[...], g_ref[...], sh_ref[...], sc_ref[...])

    def issue(r, _):
        s1, s2 = _slot_rows(tab_ref, idx_ref, r)
        _row_copy(h_scr, xs_ref, r, s1, sems.at[0]).start()
        _row_copy(h_scr, xs_ref, r, s2, sems.at[1]).start()
        return 0

    lax.fori_loop(0, tm, issue, 0, unroll=ROW_UNROLL)
    for k in range(2):
        pltpu.make_async_copy(h_scr, xs_ref.at[pl.ds(0, tm)], sems.at[k]).wait()

    @pl.when(pl.program_id(0) == pl.num_programs(0) - 1)
    def _():
        zero_scr[...] = jnp.zeros_like(zero_scr)
        for e in range(N_EXPERTS):
            n = tab_ref[TAB_COUNT + e]
            n_pad = lax.rem(group - lax.rem(n, group), group)
            first = tab_ref[TAB_START + e] + n

            def fill(k, _, first=first):
                _row_copy(zero_scr, xs_ref, 0, first + k, sems.at[2]).start()
                return 0

            def fill_done(k, _):
                _row_copy(zero_scr, xs_ref, 0, 0, sems.at[2]).wait()
                return 0

            lax.fori_loop(0, n_pad, fill, 0)
            lax.fori_loop(0, n_pad, fill_done, 0)

        def clear(j, _):
            row = pl.multiple_of(j * group, group)
            cp = pltpu.make_async_copy(zero_scr, xs_ref.at[pl.ds(row, group)], sems.at[2])
            cp.start()
            cp.wait()
            return 0

        lax.fori_loop(tab_ref[TAB_USED], n_groups, clear, 0)


def _dispatch(x, tab, idx, mods, g_all, layer, tiles_per_batch, ctx, n_groups, group, tm):
    t, d = x.shape
    mi = lambda j: _mod_index(layer, j, tiles_per_batch // tm if not ctx else 1, ctx)
    return pl.pallas_call(
        functools.partial(_dispatch_kernel, group=group),
        grid=(t // tm,),
        in_specs=[pl.BlockSpec(memory_space=pltpu.SMEM),
                  pl.BlockSpec((1, ROUTE_ROWS, tm), lambda i: (i, 0, 0), memory_space=pltpu.SMEM),
                  pl.BlockSpec((tm, d), lambda i: (i, 0)),
                  pl.BlockSpec((None, 1, d), lambda i: (layer, 0, 0)),
                  _mod_spec(d, mi(3)), _mod_spec(d, mi(4))],
        out_specs=pl.BlockSpec(memory_space=pl.ANY),
        out_shape=jax.ShapeDtypeStruct((n_groups * group, d), F32),
        scratch_shapes=[pltpu.VMEM((tm, d), F32), pltpu.VMEM((group, d), F32), pltpu.SemaphoreType.DMA((3,))],
        compiler_params=_cparams(("arbitrary",)),
        name="moe_dispatch",
    )(tab, idx, x, g_all, mods, mods)


def _moe_ffn_kernel(be_ref, bv_ref, xs_ref, wg_ref, wu_ref, wd_ref, ys_ref, h_scr, acc_scr):
    del be_ref
    i = pl.program_id(0)
    f = pl.program_id(1)
    last = f == pl.num_programs(1) - 1
    valid = bv_ref[i] > 0

    @pl.when(jnp.logical_and(valid, f == 0))
    def _():
        h_scr[...] = xs_ref[...].astype(BF16)
        acc_scr[...] = jnp.zeros_like(acc_scr)

    @pl.when(valid)
    def _():
        h = h_scr[...]
        a = jnp.dot(h, wg_ref[...], preferred_element_type=F32)
        u = jnp.dot(h, wu_ref[...], preferred_element_type=F32)
        t = (a * _sigmoid(a)) * u
        acc_scr[...] += jnp.dot(t.astype(BF16), wd_ref[...], preferred_element_type=F32)

    @pl.when(jnp.logical_and(valid, last))
    def _():
        ys_ref[...] = acc_scr[...]

    @pl.when(jnp.logical_and(jnp.logical_not(valid), last))
    def _():
        ys_ref[...] = jnp.zeros_like(ys_ref)


def _moe_ffn(xs, block_expert, block_valid, wg, wu, wd, tm, tf=FFN_TF):
    cap, d = xs.shape
    ff = wg.shape[2]
    tf = min(tf, ff)
    nf = ff // tf
    fsel = lambda i, f, bv: jnp.where(bv[i] > 0, f, nf - 1)
    grid_spec = pltpu.PrefetchScalarGridSpec(
        num_scalar_prefetch=2,
        grid=(cap // tm, nf),
        in_specs=[pl.BlockSpec((tm, d), lambda i, f, be, bv: (i, 0)),
                  pl.BlockSpec((None, d, tf), lambda i, f, be, bv: (be[i], 0, fsel(i, f, bv))),
                  pl.BlockSpec((None, d, tf), lambda i, f, be, bv: (be[i], 0, fsel(i, f, bv))),
                  pl.BlockSpec((None, tf, d), lambda i, f, be, bv: (be[i], fsel(i, f, bv), 0))],
        out_specs=pl.BlockSpec((tm, d), lambda i, f, be, bv: (i, 0)),
        scratch_shapes=[pltpu.VMEM((tm, d), BF16), pltpu.VMEM((tm, d), F32)],
    )
    return pl.pallas_call(
        _moe_ffn_kernel,
        grid_spec=grid_spec,
        out_shape=jax.ShapeDtypeStruct((cap, d), F32),
        compiler_params=_cparams(("parallel", "arbitrary")),
        name="moe_ffn",
    )(block_expert, block_valid, xs, wg, wu, wd)


def _combine_kernel(tab_ref, idx_ref, route_ref, x_ref, gt_ref, fg_ref, ys_ref, o_ref, buf1, buf2, sems, *, final_norm):
    tm = x_ref.shape[0]

    def issue(r, _):
        s1, s2 = _slot_rows(tab_ref, idx_ref, r)
        _row_copy(ys_ref, buf1, s1, r, sems.at[0]).start()
        _row_copy(ys_ref, buf2, s2, r, sems.at[1]).start()
        return 0

    lax.fori_loop(0, tm, issue, 0, unroll=ROW_UNROLL)
    for k, buf in enumerate((buf1, buf2)):
        pltpu.make_async_copy(ys_ref.at[pl.ds(0, tm)], buf, sems.at[k]).wait()
    rec = route_ref[...]
    g1 = rec[:, ROUTE_G1:ROUTE_G1 + 1]
    g2 = rec[:, ROUTE_G2:ROUTE_G2 + 1]
    y = buf1[...] * g1 + buf2[...] * g2
    out = x_ref[...] + gt_ref[...] * y
    if final_norm:
        out = out * lax.rsqrt(jnp.mean(out * out, axis=-1, keepdims=True) + EPS) * fg_ref[...]
    o_ref[...] = out


def _combine(ys, tab, idx, route, x, mods, final_g, layer, tiles_per_batch, ctx, final_norm, tm):
    t, d = x.shape
    mi = _mod_index(layer, 5, tiles_per_batch // tm if not ctx else 1, ctx)
    return pl.pallas_call(
        functools.partial(_combine_kernel, final_norm=final_norm),
        grid=(t // tm,),
        in_specs=[pl.BlockSpec(memory_space=pltpu.SMEM),
                  pl.BlockSpec((1, ROUTE_ROWS, tm), lambda i: (i, 0, 0), memory_space=pltpu.SMEM),
                  pl.BlockSpec((tm, LANES), lambda i: (i, 0)),
                  pl.BlockSpec((tm, d), lambda i: (i, 0)),
                  _mod_spec(d, mi),
                  pl.BlockSpec((1, d), lambda i: (0, 0)),
                  pl.BlockSpec(memory_space=pl.ANY)],
        out_specs=pl.BlockSpec((tm, d), lambda i: (i, 0)),
        out_shape=jax.ShapeDtypeStruct((t, d), F32),
        scratch_shapes=[pltpu.VMEM((tm, d), F32), pltpu.VMEM((tm, d), F32), pltpu.SemaphoreType.DMA((2,))],
        compiler_params=_cparams(("arbitrary",)),
        name="moe_combine",
    )(tab, idx, route, x, mods, final_g.reshape(1, d), ys)


def _moe(x, mods, g_all, w_router, wg, wu, wd, final_g, layer, tiles_per_batch, ctx, final_norm):
    t, d = x.shape
    group = min(MOE_TM, t)
    row_tm = min(ROW_TM, t)
    route, idx, counts = _router(x, mods, g_all, w_router, layer, tiles_per_batch, ctx, row_tm)
    counts = counts[0, :N_EXPERTS].astype(jnp.int32)
    groups = (counts + group - 1) // group
    ends = jnp.cumsum(groups)
    starts = (ends - groups) * group
    n_groups = (2 * t + group - 1) // group + N_EXPERTS
    tab = jnp.zeros((TAB_SIZE,), jnp.int32)
    tab = tab.at[TAB_START:TAB_START + N_EXPERTS].set(starts).at[TAB_COUNT:TAB_COUNT + N_EXPERTS].set(counts)
    tab = tab.at[TAB_USED].set(ends[-1])
    gi = jnp.arange(n_groups, dtype=jnp.int32)
    block_expert = jnp.minimum(jnp.searchsorted(ends, gi, side='right'), N_EXPERTS - 1).astype(jnp.int32)
    block_valid = (gi < ends[-1]).astype(jnp.int32)
    xs = _dispatch(x, tab, idx, mods, g_all, layer, tiles_per_batch, ctx, n_groups, group, row_tm)
    ys = _moe_ffn(xs, block_expert, block_valid, wg, wu, wd, group)
    return _combine(ys, tab, idx, route, x, mods, final_g, layer, tiles_per_batch, ctx, final_norm, row_tm)


def kernel(x, c, ctx, c_ctx, ada_w, ada_b, norm_mix_g, norm_ffn_g, fnet_w_out, attn_w_qkv, attn_w_o, attn_sinks,
           ffn_w_gate, ffn_w_up, ffn_w_down, moe_w_router, moe_w_gate, moe_w_up, moe_w_down, final_norm_g):
    n_batch, s, d = x.shape
    n_ctx = ctx.shape[1]
    depth = ada_w.shape[0]
    mods = _ada(c, c_ctx, ada_w, ada_b)
    g_mix = norm_mix_g.reshape(depth, 1, d)
    g_ffn = norm_ffn_g.reshape(depth, 1, d)
    rope_tabs = _rope_tables(s)
    x_lat = x.reshape(n_batch * s, d)
    x_ctx = ctx.reshape(n_batch * n_ctx, d)
    bf = lambda a: a.astype(BF16)
    for i in range(depth):
        j = i // 2
        last = i == depth - 1
        if i % 2 == 0:
            w_out = bf(fnet_w_out[j])
            x_lat = _fnet_lat(x_lat, mods, g_mix, w_out, i, n_batch)
            if not last:
                x_ctx = _fnet_ctx(x_ctx, mods, g_mix, w_out, i, n_batch)
            wg, wu, wd = bf(ffn_w_gate[j]), bf(ffn_w_up[j]), bf(ffn_w_down[j])
            x_lat = _ffn_dense(x_lat, mods, g_ffn, wg, wu, wd, i, s, False)
            if not last:
                x_ctx = _ffn_dense(x_ctx, mods, g_ffn, wg, wu, wd, i, n_ctx, True)
        else:
            w_qkv, w_o = bf(attn_w_qkv[j]), bf(attn_w_o[j])
            q, kt, v = _qkv(x_lat, mods, g_mix, w_qkv, i, n_batch, False, rope_tabs)
            qx, ktx, vx = _qkv(x_ctx, mods, g_mix, w_qkv, i, n_batch, True)
            o = _attn_lat(q, kt, v, ktx, vx, attn_sinks[j], n_batch)
            x_lat = _proj_res(o, w_o, x_lat, mods, i, 2, s, False)
            if not last:
                ox = _attn_ctx(qx, ktx, vx, attn_sinks[j], n_batch)
                x_ctx = _proj_res(ox, w_o, x_ctx, mods, i, 2, n_ctx, True)
            wg, wu, wd = bf(moe_w_gate[j]), bf(moe_w_up[j]), bf(moe_w_down[j])
            x_lat = _moe(x_lat, mods, g_ffn, moe_w_router[j], wg, wu, wd, final_norm_g, i, s, False, last)
            if not last:
                x_ctx = _moe(x_ctx, mods, g_ffn, moe_w_router[j], wg, wu, wd, final_norm_g, i, n_ctx, True, False)
    return x_lat.reshape(n_batch, s, d)
```

```python
import functools

import numpy as np
import jax
import jax.numpy as jnp
from jax import lax
from jax.experimental import pallas as pl
from jax.experimental.pallas import tpu as pltpu

F32 = jnp.float32
BF16 = jnp.bfloat16

GRID_W = 64
N_MOD = 6
EPS = 1e-6
FNET_GROUPS = 4
HEAD_DIM = 64
N_KV_HEADS = 4
WINDOW = 128
ROPE_THETA = 10000.0
ROT_FREQS = HEAD_DIM // 4
N_EXPERTS = 8
MOD_ROWS = 8
CTX_ROW = MOD_ROWS - 1
LANES = 128
NEG = -1e30
VMEM_LIMIT = 56 * 1024 * 1024

FFN_TM = 1024
FFN_TF = 896
MOE_TM = 512
MOE_TF = 1792
ROW_TM = 512
PROJ_TM = 512
DFT_P = 64
DFT_R = 128
FNET_RB = 16
FNET_KB = 8


def _cparams(sem):
    return pltpu.CompilerParams(dimension_semantics=sem, vmem_limit_bytes=VMEM_LIMIT)


def _sigmoid(a):
    return 1.0 / (1.0 + jnp.exp(-a))


def _norm_mod(x, g, shift, scale):
    xn = x * lax.rsqrt(jnp.mean(x * x, axis=-1, keepdims=True) + EPS)
    return (xn * g) * (1.0 + scale) + shift


def _mod_spec(d, idx_fn):
    return pl.BlockSpec((None, 1, d), lambda *ids: (idx_fn(*ids), 0, 0))


def _mod_index(layer, j, tiles_per_batch, ctx):
    def fn(i, *_):
        b = CTX_ROW if ctx else i // tiles_per_batch
        return (layer * MOD_ROWS + b) * N_MOD + j
    return fn


def _ada_kernel(c_ref, w_ref, b_ref, o_ref):
    cc = c_ref[...]
    s = cc * _sigmoid(cc)
    o_ref[...] = jnp.dot(s, w_ref[...], preferred_element_type=F32,
                         precision=lax.Precision.HIGHEST) + b_ref[...]


def _ada(c, c_ctx, ada_w, ada_b):
    depth, d, _ = ada_w.shape
    nb = c.shape[0]
    assert nb < MOD_ROWS
    cc = jnp.concatenate([c, jnp.zeros((CTX_ROW - nb, d), F32), c_ctx[None]], axis=0)
    out = pl.pallas_call(
        _ada_kernel,
        grid=(depth, N_MOD),
        in_specs=[pl.BlockSpec((MOD_ROWS, d), lambda l, j: (0, 0)),
                  pl.BlockSpec((None, d, d), lambda l, j: (l, 0, j)),
                  pl.BlockSpec((None, 1, d), lambda l, j: (l * N_MOD + j, 0, 0))],
        out_specs=pl.BlockSpec((None, MOD_ROWS, d), lambda l, j: (l, 0, j)),
        out_shape=jax.ShapeDtypeStruct((depth, MOD_ROWS, N_MOD * d), F32),
        compiler_params=_cparams(("parallel", "parallel")),
        name="ada",
    )(cc, ada_w, ada_b.reshape(depth * N_MOD, 1, d))
    return out.reshape(depth * MOD_ROWS * N_MOD, 1, d)


def _ffn_kernel(x_ref, g_ref, sh_ref, sc_ref, gt_ref, wg_ref, wu_ref, wd_ref, o_ref, h_scr, acc_scr):
    f = pl.program_id(1)

    @pl.when(f == 0)
    def _():
        h_scr[...] = _norm_mod(x_ref[...], g_ref[...], sh_ref[...], sc_ref[...]).astype(BF16)
        acc_scr[...] = jnp.zeros_like(acc_scr)

    h = h_scr[...]
    a = jnp.dot(h, wg_ref[...], preferred_element_type=F32)
    u = jnp.dot(h, wu_ref[...], preferred_element_type=F32)
    t = (a * _sigmoid(a)) * u
    acc_scr[...] += jnp.dot(t.astype(BF16), wd_ref[...], preferred_element_type=F32)

    @pl.when(f == pl.num_programs(1) - 1)
    def _():
        o_ref[...] = x_ref[...] + gt_ref[...] * acc_scr[...]


def _ffn_dense(x, mods, g_all, wg, wu, wd, layer, tiles_per_batch, ctx, tm=FFN_TM, tf=FFN_TF):
    t, d = x.shape
    ff = wg.shape[2]
    tm = min(tm, t)
    tf = min(tf, ff)
    sj = layer // 2
    mi = functools.partial(_mod_index, layer, tiles_per_batch=tiles_per_batch // tm if not ctx else 1, ctx=ctx)
    return pl.pallas_call(
        _ffn_kernel,
        grid=(t // tm, ff // tf),
        in_specs=[pl.BlockSpec((tm, d), lambda i, f: (i, 0)),
                  pl.BlockSpec((None, 1, d), lambda i, f: (layer, 0, 0)),
                  _mod_spec(d, mi(3)), _mod_spec(d, mi(4)), _mod_spec(d, mi(5)),
                  pl.BlockSpec((None, d, tf), lambda i, f: (sj, 0, f)),
                  pl.BlockSpec((None, d, tf), lambda i, f: (sj, 0, f)),
                  pl.BlockSpec((None, tf, d), lambda i, f: (sj, f, 0))],
        out_specs=pl.BlockSpec((tm, d), lambda i, f: (i, 0)),
        out_shape=jax.ShapeDtypeStruct((t, d), F32),
        scratch_shapes=[pltpu.VMEM((tm, d), BF16), pltpu.VMEM((tm, d), F32)],
        compiler_params=_cparams(("parallel", "arbitrary")),
        name="ffn_dense",
    )(x, g_all, mods, mods, mods, wg, wu, wd)


def _proj_res_kernel(a_ref, w_ref, x_ref, gt_ref, o_ref):
    y = jnp.dot(a_ref[...], w_ref[...], preferred_element_type=F32)
    o_ref[...] = x_ref[...] + gt_ref[...] * y


def _proj_res(a, w, x, mods, layer, j, tiles_per_batch, ctx, tm=PROJ_TM):
    t, d = x.shape
    k = a.shape[1]
    tm = min(tm, t)
    mi = _mod_index(layer, j, tiles_per_batch // tm if not ctx else 1, ctx)
    return pl.pallas_call(
        _proj_res_kernel,
        grid=(t // tm,),
        in_specs=[pl.BlockSpec((tm, k), lambda i: (i, 0)),
                  pl.BlockSpec((None, k, d), lambda i: (layer // 2, 0, 0)),
                  pl.BlockSpec((tm, d), lambda i: (i, 0)),
                  _mod_spec(d, mi)],
        out_specs=pl.BlockSpec((tm, d), lambda i: (i, 0)),
        out_shape=jax.ShapeDtypeStruct((t, d), F32),
        compiler_params=_cparams(("parallel",)),
        name="proj_res",
    )(a, w, x, mods)


def _dft_angles(n):
    a = np.arange(n)
    return 2.0 * np.pi * ((a[:, None] * a[None, :]) % n) / n


def _seq_dft_tables(n):
    kp = np.arange(DFT_P)[None, :, None]
    p = np.arange(DFT_P)[None, None, :]
    r = np.arange(DFT_R)[:, None, None]
    th = 2.0 * np.pi * ((kp * (DFT_R * p + r)) % n) / n
    tab1 = np.concatenate([np.cos(th), -np.sin(th)], axis=1)
    th2 = _dft_angles(DFT_R)
    c2, s2 = np.cos(th2), np.sin(th2)
    tab2 = np.block([[c2, s2], [-s2, c2]])
    return jnp.asarray(tab1, BF16), jnp.asarray(tab2, BF16)


def _chan_dft_tables(group_dim):
    th = _dft_angles(group_dim)
    return jnp.asarray(np.cos(th), BF16), jnp.asarray(np.sin(th), BF16)


def _ctx_dft_table(n_ctx):
    th = _dft_angles(n_ctx)
    return jnp.asarray(np.concatenate([np.cos(th), -np.sin(th)], axis=0), BF16)


def _cols_store(scr, val):
    for c in range(scr.shape[0]):
        scr[c] = val[:, c * LANES:(c + 1) * LANES]


def _cols_load(scr):
    return jnp.concatenate([scr[c] for c in range(scr.shape[0])], axis=-1)


def _cols_store_rows(scr, sel, val):
    for c in range(scr.shape[0]):
        scr[c, sel, :] = val[:, c * LANES:(c + 1) * LANES]


def _cols_load_rows(scr, sel):
    return jnp.concatenate([scr[c, sel, :] for c in range(scr.shape[0])], axis=-1)


def _fnet_stage1_kernel(x_ref, g_ref, sh_ref, sc_ref, tab_ref, zr_ref, zi_ref, h_scr, zr_scr, zi_scr):
    d = g_ref.shape[-1]
    rows = DFT_P * FNET_RB
    _cols_store(h_scr, _norm_mod(x_ref[...].reshape(rows, d), g_ref[...], sh_ref[...], sc_ref[...]))
    for j in range(FNET_RB):
        sel = pl.ds(j, DFT_P, stride=FNET_RB)
        z = jnp.dot(tab_ref[j], _cols_load_rows(h_scr, sel).astype(BF16), preferred_element_type=F32)
        _cols_store_rows(zr_scr, sel, z[:DFT_P])
        _cols_store_rows(zi_scr, sel, z[DFT_P:])
    zr_ref[...] = _cols_load(zr_scr).astype(BF16).reshape(DFT_P, FNET_RB, d)
    zi_ref[...] = _cols_load(zi_scr).astype(BF16).reshape(DFT_P, FNET_RB, d)


def _mix_tail(pr, pi, cc_ref, sc_ref, wout_ref, scale):
    gd = cc_ref.shape[0]
    ys = []
    for g in range(pr.shape[1] // gd):
        ys.append(jnp.dot(pr[:, g * gd:(g + 1) * gd], cc_ref[...], preferred_element_type=F32)
                  + jnp.dot(pi[:, g * gd:(g + 1) * gd], sc_ref[...], preferred_element_type=F32))
    mixed = (jnp.concatenate(ys, axis=-1) * scale).astype(BF16)
    return jnp.dot(mixed, wout_ref[...], preferred_element_type=F32)


def _fnet_stage2_kernel(zr_ref, zi_ref, tab2_ref, cc_ref, sc_ref, wout_ref, x_ref, gt_ref, o_ref,
                        p_scr, x_scr, o_scr, *, scale):
    d = gt_ref.shape[-1]
    rows = DFT_R * FNET_KB
    for j in range(FNET_KB):
        z = jnp.concatenate([zr_ref[j * DFT_R:(j + 1) * DFT_R, :], zi_ref[j * DFT_R:(j + 1) * DFT_R, :]], axis=0)
        p = jnp.dot(tab2_ref[...], z, preferred_element_type=F32)
        p_scr[j * DFT_R:(j + 1) * DFT_R, :d] = p[:DFT_R].astype(BF16)
        p_scr[j * DFT_R:(j + 1) * DFT_R, d:] = p[DFT_R:].astype(BF16)
    y = _mix_tail(p_scr[:, :d], p_scr[:, d:], cc_ref, sc_ref, wout_ref, scale)
    _cols_store(x_scr, x_ref[...].reshape(rows, d))
    for j in range(FNET_KB):
        sel = pl.ds(j, DFT_R, stride=FNET_KB)
        _cols_store_rows(o_scr, sel, _cols_load_rows(x_scr, sel) + gt_ref[...] * y[j * DFT_R:(j + 1) * DFT_R, :])
    o_ref[...] = _cols_load(o_scr).reshape(DFT_R, FNET_KB, d)


def _fnet_lat(x, mods, g_all, w_out, layer, n_batch):
    t, d = x.shape
    s = t // n_batch
    assert s == DFT_P * DFT_R
    tab1, tab2 = _seq_dft_tables(s)
    ccos, csin = _chan_dft_tables(d // FNET_GROUPS)
    mi = lambda j: (lambda b, *_: (layer * MOD_ROWS + b) * N_MOD + j)
    xv = x.reshape(n_batch, DFT_P, DFT_R, d)
    blk1 = (None, DFT_P, FNET_RB, d)
    rows1 = DFT_P * FNET_RB
    zr, zi = pl.pallas_call(
        _fnet_stage1_kernel,
        grid=(n_batch, DFT_R // FNET_RB),
        in_specs=[pl.BlockSpec(blk1, lambda b, r: (b, 0, r, 0)),
                  pl.BlockSpec((None, 1, d), lambda b, r: (layer, 0, 0)),
                  _mod_spec(d, mi(0)), _mod_spec(d, mi(1)),
                  pl.BlockSpec((FNET_RB, 2 * DFT_P, DFT_P), lambda b, r: (r, 0, 0))],
        out_specs=[pl.BlockSpec(blk1, lambda b, r: (b, 0, r, 0))] * 2,
        out_shape=[jax.ShapeDtypeStruct((n_batch, DFT_P, DFT_R, d), BF16)] * 2,
        scratch_shapes=[pltpu.VMEM((d // LANES, rows1, LANES), F32)] * 3,
        compiler_params=_cparams(("parallel", "parallel")),
        name="fnet_stage1",
    )(xv, g_all, mods, mods, tab1)
    zr = zr.reshape(n_batch, DFT_P * DFT_R, d)
    zi = zi.reshape(n_batch, DFT_P * DFT_R, d)
    xo = x.reshape(n_batch, DFT_R, DFT_P, d)
    blk2 = (None, DFT_R, FNET_KB, d)
    rows = FNET_KB * DFT_R
    scale = float(1.0 / np.sqrt(float(s) * (d // FNET_GROUPS)))
    out = pl.pallas_call(
        functools.partial(_fnet_stage2_kernel, scale=scale),
        grid=(n_batch, DFT_P // FNET_KB),
        in_specs=[pl.BlockSpec((None, rows, d), lambda b, k: (b, k, 0)),
                  pl.BlockSpec((None, rows, d), lambda b, k: (b, k, 0)),
                  pl.BlockSpec((2 * DFT_R, 2 * DFT_R), lambda b, k: (0, 0)),
                  pl.BlockSpec(ccos.shape, lambda b, k: (0, 0)),
                  pl.BlockSpec(csin.shape, lambda b, k: (0, 0)),
                  pl.BlockSpec((None, d, d), lambda b, k: (layer // 2, 0, 0)),
                  pl.BlockSpec(blk2, lambda b, k: (b, 0, k, 0)),
                  _mod_spec(d, mi(2))],
        out_specs=pl.BlockSpec(blk2, lambda b, k: (b, 0, k, 0)),
        out_shape=jax.ShapeDtypeStruct((n_batch, DFT_R, DFT_P, d), F32),
        scratch_shapes=[pltpu.VMEM((rows, 2 * d), BF16)] + [pltpu.VMEM((d // LANES, rows, LANES), F32)] * 2,
        compiler_params=_cparams(("parallel", "parallel")),
        name="fnet_stage2",
    )(zr, zi, tab2, ccos, csin, w_out, xo, mods)
    return out.reshape(t, d)


def _fnet_ctx_kernel(x_ref, g_ref, sh_ref, sc_ref, gt_ref, tab_ref, cc_ref, sc2_ref, wout_ref, o_ref, *, scale):
    n = x_ref.shape[0]
    h = _norm_mod(x_ref[...], g_ref[...], sh_ref[...], sc_ref[...]).astype(BF16)
    p = jnp.dot(tab_ref[...], h, preferred_element_type=F32)
    y = _mix_tail(p[:n].astype(BF16), p[n:].astype(BF16), cc_ref, sc2_ref, wout_ref, scale)
    o_ref[...] = x_ref[...] + gt_ref[...] * y


def _fnet_ctx(x, mods, g_all, w_out, layer, n_batch):
    t, d = x.shape
    n = t // n_batch
    gd = d // FNET_GROUPS
    ccos, csin = _chan_dft_tables(gd)
    tab = _ctx_dft_table(n)
    mi = lambda j: (lambda b: (layer * MOD_ROWS + CTX_ROW) * N_MOD + j)
    scale = float(1.0 / np.sqrt(float(n) * gd))
    return pl.pallas_call(
        functools.partial(_fnet_ctx_kernel, scale=scale),
        grid=(n_batch,),
        in_specs=[pl.BlockSpec((n, d), lambda b: (b, 0)),
                  pl.BlockSpec((None, 1, d), lambda b: (layer, 0, 0)),
                  _mod_spec(d, mi(0)), _mod_spec(d, mi(1)), _mod_spec(d, mi(2)),
                  pl.BlockSpec(tab.shape, lambda b: (0, 0)),
                  pl.BlockSpec(ccos.shape, lambda b: (0, 0)),
                  pl.BlockSpec(csin.shape, lambda b: (0, 0)),
                  pl.BlockSpec((None, d, d), lambda b: (layer // 2, 0, 0))],
        out_specs=pl.BlockSpec((n, d), lambda b: (b, 0)),
        out_shape=jax.ShapeDtypeStruct((t, d), F32),
        compiler_params=_cparams(("parallel",)),
        name="fnet_ctx",
    )(x, g_all, mods, mods, mods, tab, ccos, csin, w_out)


def _rope_tables(n_seq):
    rows = n_seq // GRID_W
    row = jnp.repeat(jnp.arange(rows, dtype=F32), GRID_W)
    col = jnp.tile(jnp.arange(GRID_W, dtype=F32), rows)
    inv_freq = ROPE_THETA ** (-jnp.arange(ROT_FREQS, dtype=F32) / ROT_FREQS)
    ang = jnp.stack([row[:, None] * inv_freq, col[:, None] * inv_freq], axis=1)
    cos, sin = jnp.cos(ang), jnp.sin(ang)
    zero = jnp.zeros_like(sin)
    cos_h = jnp.stack([cos, cos], axis=2).reshape(n_seq, HEAD_DIM)
    sin_lo = jnp.stack([-sin, zero], axis=2).reshape(n_seq, HEAD_DIM)
    sin_hi = jnp.stack([zero, sin], axis=2).reshape(n_seq, HEAD_DIM)
    rep = LANES // HEAD_DIM
    return jnp.tile(cos_h, (1, rep)), jnp.tile(sin_lo, (1, rep)), jnp.tile(sin_hi, (1, rep))


def _qkv_kernel(x_ref, g_ref, sh_ref, sc_ref, w_ref, *rest, rope, q_dim, kv_dim, q_scale):
    if rope:
        cos_ref, slo_ref, shi_ref, q_ref, kt_ref, v_ref = rest
    else:
        q_ref, kt_ref, v_ref = rest
    h = _norm_mod(x_ref[...], g_ref[...], sh_ref[...], sc_ref[...]).astype(BF16)
    qkv = jnp.dot(h, w_ref[...], preferred_element_type=F32)

    def rot(xs):
        if not rope:
            return xs
        return (xs * cos_ref[...] + pltpu.roll(xs, LANES - ROT_FREQS, axis=1) * slo_ref[...]
                + pltpu.roll(xs, ROT_FREQS, axis=1) * shi_ref[...])

    for j in range(q_dim // LANES):
        q_ref[:, j * LANES:(j + 1) * LANES] = (rot(qkv[:, j * LANES:(j + 1) * LANES]) * q_scale).astype(BF16)
    ks = [rot(qkv[:, q_dim + j * LANES:q_dim + (j + 1) * LANES]) for j in range(kv_dim // LANES)]
    kt_ref[...] = jnp.concatenate(ks, axis=-1).T.astype(BF16)
    v_ref[...] = qkv[:, q_dim + kv_dim:].astype(BF16)


def _qkv(x, mods, g_all, w_qkv, layer, n_batch, ctx, rope_tabs=None, tm=512):
    t, d = x.shape
    n = t // n_batch
    tm = min(tm, n)
    tpb = n // tm
    kv_dim = N_KV_HEADS * HEAD_DIM
    q_dim = w_qkv.shape[2] - 2 * kv_dim
    mi = lambda j: _mod_index(layer, j, tpb, ctx)
    in_specs = [pl.BlockSpec((tm, d), lambda i: (i, 0)),
                pl.BlockSpec((None, 1, d), lambda i: (layer, 0, 0)),
                _mod_spec(d, mi(0)), _mod_spec(d, mi(1)),
                pl.BlockSpec((None,) + w_qkv.shape[1:], lambda i: (layer // 2, 0, 0))]
    args = [x, g_all, mods, mods, w_qkv]
    rope = rope_tabs is not None
    if rope:
        in_specs += [pl.BlockSpec((tm, LANES), lambda i: (i % tpb, 0))] * 3
        args += list(rope_tabs)
    return pl.pallas_call(
        functools.partial(_qkv_kernel, rope=rope, q_dim=q_dim, kv_dim=kv_dim, q_scale=HEAD_DIM ** -0.5),
        grid=(t // tm,),
        in_specs=in_specs,
        out_specs=[pl.BlockSpec((tm, q_dim), lambda i: (i, 0)),
                   pl.BlockSpec((None, kv_dim, tm), lambda i: (i // tpb, 0, i % tpb)),
                   pl.BlockSpec((tm, kv_dim), lambda i: (i, 0))],
        out_shape=[jax.ShapeDtypeStruct((t, q_dim), BF16),
                   jax.ShapeDtypeStruct((n_batch, kv_dim, n), BF16),
                   jax.ShapeDtypeStruct((t, kv_dim), BF16)],
        compiler_params=_cparams(("parallel",)),
        name="qkv_ctx" if ctx else "qkv_lat",
    )(*args)


def _heads_attend(q_ref, kt, v, bias, sink_ref, o_ref):
    nq = q_ref.shape[0]
    n_heads = q_ref.shape[1] // HEAD_DIM
    group = n_heads // N_KV_HEADS
    outs = [None] * n_heads
    for g in range(N_KV_HEADS):
        heads = range(g * group, (g + 1) * group)
        qg = jnp.concatenate([q_ref[:, hd * HEAD_DIM:(hd + 1) * HEAD_DIM] for hd in heads], axis=0)
        s = jnp.dot(qg, kt[g * HEAD_DIM:(g + 1) * HEAD_DIM, :], preferred_element_type=F32)
        parts = [s[:, k * LANES:(k + 1) * LANES] for k in range(s.shape[1] // LANES)]
        if bias is not None:
            parts[0] = parts[0] + bias[0]
            parts[2] = parts[2] + bias[1]
        sink = jnp.concatenate([jnp.full((nq, 1), sink_ref[hd], F32) for hd in heads], axis=0)
        m = jnp.maximum(jnp.max(functools.reduce(jnp.maximum, parts), axis=-1, keepdims=True), sink)
        ps = [jnp.exp(part - m) for part in parts]
        den = jnp.sum(functools.reduce(jnp.add, ps), axis=-1, keepdims=True) + jnp.exp(sink - m)
        pv = jnp.dot(jnp.concatenate(ps, axis=-1).astype(BF16), v, preferred_element_type=F32)
        on = pv[:, g * HEAD_DIM:(g + 1) * HEAD_DIM] / den
        for k, hd in enumerate(heads):
            outs[hd] = on[k * nq:(k + 1) * nq, :]
    o_ref[...] = jnp.concatenate(outs, axis=-1).astype(BF16)


def _attn_lat_kernel(sink_ref, q_ref, ktp_ref, ktc_ref, ktn_ref, vp_ref, vc_ref, vn_ref, ktx_ref, vx_ref,
                     blo_ref, bhi_ref, o_ref, kt_scr, v_scr):
    w = ktc_ref.shape[1]
    n_ctx = vx_ref.shape[0]
    for c, (kr, vr) in enumerate(((ktp_ref, vp_ref), (ktc_ref, vc_ref), (ktn_ref, vn_ref))):
        kt_scr[:, c * w:(c + 1) * w] = kr[...]
        v_scr[c * w:(c + 1) * w, :] = vr[...]
    kt_scr[:, 3 * w:3 * w + n_ctx] = ktx_ref[...]
    v_scr[3 * w:3 * w + n_ctx, :] = vx_ref[...]
    _heads_attend(q_ref, kt_scr[...], v_scr[...], (blo_ref[...], bhi_ref[...]), sink_ref, o_ref)


def _attn_lat(q, kt, v, ktx, vx, sinks, n_batch):
    t, qd = q.shape
    s = t // n_batch
    n_ctx = vx.shape[0] // n_batch
    w = WINDOW
    nb = s // w
    kvd = v.shape[1]
    j_all = 3 * w + n_ctx
    group = qd // HEAD_DIM // N_KV_HEADS
    qi = np.arange(w)[:, None]
    ki = np.arange(w)[None, :]
    lo = np.where(ki >= qi, 0.0, NEG).astype(np.float32)
    hi = np.where(ki <= qi, 0.0, NEG).astype(np.float32)
    off = np.full((w, w), NEG, np.float32)
    blo = jnp.asarray(np.stack([np.tile(lo, (group, 1)), np.tile(off, (group, 1))]))
    bhi = jnp.asarray(np.stack([np.tile(hi, (group, 1)), np.tile(off, (group, 1))]))
    prev = lambda b, i: (b, 0, jnp.maximum(i - 1, 0))
    nxt = lambda b, i: (b, 0, jnp.minimum(i + 1, nb - 1))
    vprev = lambda b, i: (b * nb + jnp.maximum(i - 1, 0), 0)
    vnxt = lambda b, i: (b * nb + jnp.minimum(i + 1, nb - 1), 0)
    return pl.pallas_call(
        _attn_lat_kernel,
        grid=(n_batch, nb),
        in_specs=[pl.BlockSpec(memory_space=pltpu.SMEM),
                  pl.BlockSpec((w, qd), lambda b, i: (b * nb + i, 0)),
                  pl.BlockSpec((None, kvd, w), prev),
                  pl.BlockSpec((None, kvd, w), lambda b, i: (b, 0, i)),
                  pl.BlockSpec((None, kvd, w), nxt),
                  pl.BlockSpec((w, kvd), vprev),
                  pl.BlockSpec((w, kvd), lambda b, i: (b * nb + i, 0)),
                  pl.BlockSpec((w, kvd), vnxt),
                  pl.BlockSpec((None, kvd, n_ctx), lambda b, i: (b, 0, 0)),
                  pl.BlockSpec((n_ctx, kvd), lambda b, i: (b, 0)),
                  pl.BlockSpec((None, group * w, w), lambda b, i: (jnp.where(i == 0, 1, 0), 0, 0)),
                  pl.BlockSpec((None, group * w, w), lambda b, i: (jnp.where(i == nb - 1, 1, 0), 0, 0))],
        out_specs=pl.BlockSpec((w, qd), lambda b, i: (b * nb + i, 0)),
        out_shape=jax.ShapeDtypeStruct((t, qd), BF16),
        scratch_shapes=[pltpu.VMEM((kvd, j_all), BF16), pltpu.VMEM((j_all, kvd), BF16)],
        compiler_params=_cparams(("parallel", "parallel")),
        name="attn_lat",
    )(sinks, q, kt, kt, kt, v, v, v, ktx, vx, blo, bhi)


def _attn_ctx_kernel(sink_ref, q_ref, kt_ref, v_ref, o_ref):
    _heads_attend(q_ref, kt_ref[...], v_ref[...], None, sink_ref, o_ref)


def _attn_ctx(q, kt, v, sinks, n_batch):
    t, qd = q.shape
    n = t // n_batch
    kvd = v.shape[1]
    return pl.pallas_call(
        _attn_ctx_kernel,
        grid=(n_batch,),
        in_specs=[pl.BlockSpec(memory_space=pltpu.SMEM),
                  pl.BlockSpec((n, qd), lambda b: (b, 0)),
                  pl.BlockSpec((None, kvd, n), lambda b: (b, 0, 0)),
                  pl.BlockSpec((n, kvd), lambda b: (b, 0))],
        out_specs=pl.BlockSpec((n, qd), lambda b: (b, 0)),
        out_shape=jax.ShapeDtypeStruct((t, qd), BF16),
        compiler_params=_cparams(("parallel",)),
        name="attn_ctx",
    )(sinks, q, kt, v)


ROUTE_I1, ROUTE_I2, ROUTE_R1, ROUTE_R2, ROUTE_G1, ROUTE_G2 = range(6)
ROUTE_ROWS = 8
TAB_START, TAB_COUNT, TAB_USED, TAB_SIZE = 0, N_EXPERTS, 2 * N_EXPERTS, 2 * N_EXPERTS + 8
ROW_UNROLL = 8


def _router_kernel(x_ref, g_ref, sh_ref, sc_ref, wr_ref, tri_ref, route_ref, idx_ref, cnt_ref, carry_scr):
    @pl.when(pl.program_id(0) == 0)
    def _():
        carry_scr[...] = jnp.zeros_like(carry_scr)

    h = _norm_mod(x_ref[...], g_ref[...], sh_ref[...], sc_ref[...])
    logits = jnp.dot(h, wr_ref[...], preferred_element_type=F32, precision=lax.Precision.HIGHEST)
    lane = lax.broadcasted_iota(jnp.int32, logits.shape, 1)
    lane_f = lane.astype(F32)
    logits = jnp.where(lane < N_EXPERTS, logits, -jnp.inf)
    m1 = jnp.max(logits, axis=-1, keepdims=True)
    i1 = jnp.min(jnp.where(logits == m1, lane_f, float(LANES)), axis=-1, keepdims=True)
    oh1 = lane_f == i1
    rest = jnp.where(oh1, -jnp.inf, logits)
    m2 = jnp.max(rest, axis=-1, keepdims=True)
    i2 = jnp.min(jnp.where(rest == m2, lane_f, float(LANES)), axis=-1, keepdims=True)
    oh2 = lane_f == i2
    e2 = jnp.exp(m2 - m1)
    g1 = 1.0 / (1.0 + e2)
    g2 = e2 / (1.0 + e2)
    sel = jnp.where(oh1, 1.0, 0.0) + jnp.where(oh2, 1.0, 0.0)
    before = jnp.dot(tri_ref[...], sel.astype(BF16), preferred_element_type=F32) + carry_scr[0:1, :]
    r1 = jnp.sum(jnp.where(oh1, before, 0.0), axis=-1, keepdims=True)
    r2 = jnp.sum(jnp.where(oh2, before, 0.0), axis=-1, keepdims=True)
    total = carry_scr[0:1, :] + jnp.sum(sel, axis=0, keepdims=True)
    carry_scr[...] = jnp.broadcast_to(total, carry_scr.shape)
    cnt_ref[...] = jnp.broadcast_to(total, cnt_ref.shape)
    rec = jnp.zeros_like(logits)
    for ln, val in ((ROUTE_I1, i1), (ROUTE_I2, i2), (ROUTE_R1, r1), (ROUTE_R2, r2), (ROUTE_G1, g1), (ROUTE_G2, g2)):
        rec = jnp.where(lane == ln, val, rec)
    route_ref[...] = rec
    idx_ref[...] = rec.T[:ROUTE_ROWS, :].astype(jnp.int32)


def _router(x, mods, g_all, w_router, layer, tiles_per_batch, ctx, tm):
    t, d = x.shape
    mi = lambda j: _mod_index(layer, j, tiles_per_batch // tm if not ctx else 1, ctx)
    wr = jnp.zeros((d, LANES), F32).at[:, :N_EXPERTS].set(w_router)
    tri = jnp.asarray(np.tril(np.ones((tm, tm), np.float32), -1), BF16)
    return pl.pallas_call(
        _router_kernel,
        grid=(t // tm,),
        in_specs=[pl.BlockSpec((tm, d), lambda i: (i, 0)),
                  pl.BlockSpec((None, 1, d), lambda i: (layer, 0, 0)),
                  _mod_spec(d, mi(3)), _mod_spec(d, mi(4)),
                  pl.BlockSpec((d, LANES), lambda i: (0, 0)),
                  pl.BlockSpec((tm, tm), lambda i: (0, 0))],
        out_specs=[pl.BlockSpec((tm, LANES), lambda i: (i, 0)),
                   pl.BlockSpec((None, ROUTE_ROWS, tm), lambda i: (i, 0, 0)),
                   pl.BlockSpec((8, LANES), lambda i: (0, 0))],
        out_shape=[jax.ShapeDtypeStruct((t, LANES), F32),
                   jax.ShapeDtypeStruct((t // tm, ROUTE_ROWS, tm), jnp.int32),
                   jax.ShapeDtypeStruct((8, LANES), F32)],
        scratch_shapes=[pltpu.VMEM((8, LANES), F32)],
        compiler_params=_cparams(("arbitrary",)),
        name="router",
    )(x, g_all, mods, mods, wr, tri)


def _row_copy(src, dst, src_row, dst_row, sem):
    return pltpu.make_async_copy(src.at[pl.ds(src_row, 1)], dst.at[pl.ds(dst_row, 1)], sem)


def _slot_rows(tab_ref, idx_ref, r):
    s1 = tab_ref[TAB_START + idx_ref[0, ROUTE_I1, r]] + idx_ref[0, ROUTE_R1, r]
    s2 = tab_ref[TAB_START + idx_ref[0, ROUTE_I2, r]] + idx_ref[0, ROUTE_R2, r]
    return s1, s2


def _dispatch_kernel(tab_ref, idx_ref, x_ref, g_ref, sh_ref, sc_ref, xs_ref, h_scr, zero_scr, sems, *, group):
    tm = x_ref.shape[0]
    n_groups = xs_ref.shape[0] // group
    h_scr[...] = _norm_mod(x_ref[...], g_ref[...], sh_ref[...], sc_ref[...])

    def issue(r, _):
        s1, s2 = _slot_rows(tab_ref, idx_ref, r)
        _row_copy(h_scr, xs_ref, r, s1, sems.at[0]).start()
        _row_copy(h_scr, xs_ref, r, s2, sems.at[1]).start()
        return 0

    lax.fori_loop(0, tm, issue, 0, unroll=ROW_UNROLL)
    for k in range(2):
        pltpu.make_async_copy(h_scr, xs_ref.at[pl.ds(0, tm)], sems.at[k]).wait()

    @pl.when(pl.program_id(0) == pl.num_programs(0) - 1)
    def _():
        zero_scr[...] = jnp.zeros_like(zero_scr)
        for e in range(N_EXPERTS):
            n = tab_ref[TAB_COUNT + e]
            n_pad = lax.rem(group - lax.rem(n, group), group)
            first = tab_ref[TAB_START + e] + n

            def fill(k, _, first=first):
                _row_copy(zero_scr, xs_ref, 0, first + k, sems.at[2]).start()
                return 0

            def fill_done(k, _):
                _row_copy(zero_scr, xs_ref, 0, 0, sems.at[2]).wait()
                return 0

            lax.fori_loop(0, n_pad, fill, 0)
            lax.fori_loop(0, n_pad, fill_done, 0)

        def clear(j, _):
            row = pl.multiple_of(j * group, group)
            cp = pltpu.make_async_copy(zero_scr, xs_ref.at[pl.ds(row, group)], sems.at[2])
            cp.start()
            cp.wait()
            return 0

        lax.fori_loop(tab_ref[TAB_USED], n_groups, clear, 0)


def _dispatch(x, tab, idx, mods, g_all, layer, tiles_per_batch, ctx, n_groups, group, tm):
    t, d = x.shape
    mi = lambda j: _mod_index(layer, j, tiles_per_batch // tm if not ctx else 1, ctx)
    return pl.pallas_call(
        functools.partial(_dispatch_kernel, group=group),
        grid=(t // tm,),
        in_specs=[pl.BlockSpec(memory_space=pltpu.SMEM),
                  pl.BlockSpec((1, ROUTE_ROWS, tm), lambda i: (i, 0, 0), memory_space=pltpu.SMEM),
                  pl.BlockSpec((tm, d), lambda i: (i, 0)),
                  pl.BlockSpec((None, 1, d), lambda i: (layer, 0, 0)),
                  _mod_spec(d, mi(3)), _mod_spec(d, mi(4))],
        out_specs=pl.BlockSpec(memory_space=pl.ANY),
        out_shape=jax.ShapeDtypeStruct((n_groups * group, d), F32),
        scratch_shapes=[pltpu.VMEM((tm, d), F32), pltpu.VMEM((group, d), F32), pltpu.SemaphoreType.DMA((3,))],
        compiler_params=_cparams(("arbitrary",)),
        name="moe_dispatch",
    )(tab, idx, x, g_all, mods, mods)


def _moe_ffn_kernel(be_ref, bv_ref, xs_ref, wg_ref, wu_ref, wd_ref, ys_ref, h_scr, acc_scr):
    del be_ref
    i = pl.program_id(0)
    f = pl.program_id(1)
    last = f == pl.num_programs(1) - 1
    valid = bv_ref[i] > 0

    @pl.when(jnp.logical_and(valid, f == 0))
    def _():
        h_scr[...] = xs_ref[...].astype(BF16)
        acc_scr[...] = jnp.zeros_like(acc_scr)

    @pl.when(valid)
    def _():
        h = h_scr[...]
        a = jnp.dot(h, wg_ref[...], preferred_element_type=F32)
        u = jnp.dot(h, wu_ref[...], preferred_element_type=F32)
        t = (a * _sigmoid(a)) * u
        acc_scr[...] += jnp.dot(t.astype(BF16), wd_ref[...], preferred_element_type=F32)

    @pl.when(jnp.logical_and(valid, last))
    def _():
        ys_ref[...] = acc_scr[...]

    @pl.when(jnp.logical_and(jnp.logical_not(valid), last))
    def _():
        ys_ref[...] = jnp.zeros_like(ys_ref)


def _moe_ffn(xs, block_expert, block_valid, wg, wu, wd, layer, tm, tf=MOE_TF):
    cap, d = xs.shape
    ff = wg.shape[3]
    tf = min(tf, ff)
    nf = ff // tf
    sj = layer // 2
    fsel = lambda i, f, bv: jnp.where(bv[i] > 0, f, nf - 1)
    grid_spec = pltpu.PrefetchScalarGridSpec(
        num_scalar_prefetch=2,
        grid=(cap // tm, nf),
        in_specs=[pl.BlockSpec((tm, d), lambda i, f, be, bv: (i, 0)),
                  pl.BlockSpec((None, None, d, tf), lambda i, f, be, bv: (sj, be[i], 0, fsel(i, f, bv))),
                  pl.BlockSpec((None, None, d, tf), lambda i, f, be, bv: (sj, be[i], 0, fsel(i, f, bv))),
                  pl.BlockSpec((None, None, tf, d), lambda i, f, be, bv: (sj, be[i], fsel(i, f, bv), 0))],
        out_specs=pl.BlockSpec((tm, d), lambda i, f, be, bv: (i, 0)),
        scratch_shapes=[pltpu.VMEM((tm, d), BF16), pltpu.VMEM((tm, d), F32)],
    )
    return pl.pallas_call(
        _moe_ffn_kernel,
        grid_spec=grid_spec,
        out_shape=jax.ShapeDtypeStruct((cap, d), F32),
        compiler_params=_cparams(("parallel", "arbitrary")),
        name="moe_ffn",
    )(block_expert, block_valid, xs, wg, wu, wd)


def _combine_kernel(tab_ref, idx_ref, route_ref, x_ref, gt_ref, fg_ref, ys_ref, o_ref, buf1, buf2, sems, *, final_norm):
    tm = x_ref.shape[0]

    def issue(r, _):
        s1, s2 = _slot_rows(tab_ref, idx_ref, r)
        _row_copy(ys_ref, buf1, s1, r, sems.at[0]).start()
        _row_copy(ys_ref, buf2, s2, r, sems.at[1]).start()
        return 0

    lax.fori_loop(0, tm, issue, 0, unroll=ROW_UNROLL)
    for k, buf in enumerate((buf1, buf2)):
        pltpu.make_async_copy(ys_ref.at[pl.ds(0, tm)], buf, sems.at[k]).wait()
    rec = route_ref[...]
    g1 = rec[:, ROUTE_G1:ROUTE_G1 + 1]
    g2 = rec[:, ROUTE_G2:ROUTE_G2 + 1]
    y = buf1[...] * g1 + buf2[...] * g2
    out = x_ref[...] + gt_ref[...] * y
    if final_norm:
        out = out * lax.rsqrt(jnp.mean(out * out, axis=-1, keepdims=True) + EPS) * fg_ref[...]
    o_ref[...] = out


def _combine(ys, tab, idx, route, x, mods, final_g, layer, tiles_per_batch, ctx, final_norm, tm):
    t, d = x.shape
    mi = _mod_index(layer, 5, tiles_per_batch // tm if not ctx else 1, ctx)
    return pl.pallas_call(
        functools.partial(_combine_kernel, final_norm=final_norm),
        grid=(t // tm,),
        in_specs=[pl.BlockSpec(memory_space=pltpu.SMEM),
                  pl.BlockSpec((1, ROUTE_ROWS, tm), lambda i: (i, 0, 0), memory_space=pltpu.SMEM),
                  pl.BlockSpec((tm, LANES), lambda i: (i, 0)),
                  pl.BlockSpec((tm, d), lambda i: (i, 0)),
                  _mod_spec(d, mi),
                  pl.BlockSpec((1, d), lambda i: (0, 0)),
                  pl.BlockSpec(memory_space=pl.ANY)],
        out_specs=pl.BlockSpec((tm, d), lambda i: (i, 0)),
        out_shape=jax.ShapeDtypeStruct((t, d), F32),
        scratch_shapes=[pltpu.VMEM((tm, d), F32), pltpu.VMEM((tm, d), F32), pltpu.SemaphoreType.DMA((2,))],
        compiler_params=_cparams(("arbitrary",)),
        name="moe_combine",
    )(tab, idx, route, x, mods, final_g.reshape(1, d), ys)


def _moe(x, mods, g_all, w_router, wg, wu, wd, final_g, layer, tiles_per_batch, ctx, final_norm):
    t, d = x.shape
    group = min(MOE_TM, t)
    row_tm = min(ROW_TM, t)
    route, idx, counts = _router(x, mods, g_all, w_router, layer, tiles_per_batch, ctx, row_tm)
    counts = counts[0, :N_EXPERTS].astype(jnp.int32)
    groups = (counts + group - 1) // group
    ends = jnp.cumsum(groups)
    starts = (ends - groups) * group
    n_groups = (2 * t + group - 1) // group + N_EXPERTS
    tab = jnp.zeros((TAB_SIZE,), jnp.int32)
    tab = tab.at[TAB_START:TAB_START + N_EXPERTS].set(starts).at[TAB_COUNT:TAB_COUNT + N_EXPERTS].set(counts)
    tab = tab.at[TAB_USED].set(ends[-1])
    gi = jnp.arange(n_groups, dtype=jnp.int32)
    block_expert = jnp.minimum(jnp.searchsorted(ends, gi, side='right'), N_EXPERTS - 1).astype(jnp.int32)
    block_valid = (gi < ends[-1]).astype(jnp.int32)
    xs = _dispatch(x, tab, idx, mods, g_all, layer, tiles_per_batch, ctx, n_groups, group, row_tm)
    ys = _moe_ffn(xs, block_expert, block_valid, wg, wu, wd, layer, group)
    return _combine(ys, tab, idx, route, x, mods, final_g, layer, tiles_per_batch, ctx, final_norm, row_tm)


def kernel(x, c, ctx, c_ctx, ada_w, ada_b, norm_mix_g, norm_ffn_g, fnet_w_out, attn_w_qkv, attn_w_o, attn_sinks,
           ffn_w_gate, ffn_w_up, ffn_w_down, moe_w_router, moe_w_gate, moe_w_up, moe_w_down, final_norm_g):
    n_batch, s, d = x.shape
    n_ctx = ctx.shape[1]
    depth = ada_w.shape[0]
    mods = _ada(c, c_ctx, ada_w, ada_b)
    g_mix = norm_mix_g.reshape(depth, 1, d)
    g_ffn = norm_ffn_g.reshape(depth, 1, d)
    rope_tabs = _rope_tables(s)
    x_lat = x.reshape(n_batch * s, d)
    x_ctx = ctx.reshape(n_batch * n_ctx, d)
    bf = lambda a: a.astype(BF16)
    w_out, w_qkv, w_o = bf(fnet_w_out), bf(attn_w_qkv), bf(attn_w_o)
    wg, wu, wd = bf(ffn_w_gate), bf(ffn_w_up), bf(ffn_w_down)
    eg, eu, ed = bf(moe_w_gate), bf(moe_w_up), bf(moe_w_down)
    for i in range(depth):
        j = i // 2
        last = i == depth - 1
        if i % 2 == 0:
            x_lat = _fnet_lat(x_lat, mods, g_mix, w_out, i, n_batch)
            if not last:
                x_ctx = _fnet_ctx(x_ctx, mods, g_mix, w_out, i, n_batch)
            x_lat = _ffn_dense(x_lat, mods, g_ffn, wg, wu, wd, i, s, False)
            if not last:
                x_ctx = _ffn_dense(x_ctx, mods, g_ffn, wg, wu, wd, i, n_ctx, True)
        else:
            q, kt, v = _qkv(x_lat, mods, g_mix, w_qkv, i, n_batch, False, rope_tabs)
            qx, ktx, vx = _qkv(x_ctx, mods, g_mix, w_qkv, i, n_batch, True)
            o = _attn_lat(q, kt, v, ktx, vx, attn_sinks[j], n_batch)
            x_lat = _proj_res(o, w_o, x_lat, mods, i, 2, s, False)
            if not last:
                ox = _attn_ctx(qx, ktx, vx, attn_sinks[j], n_batch)
                x_ctx = _proj_res(ox, w_o, x_ctx, mods, i, 2, n_ctx, True)
            x_lat = _moe(x_lat, mods, g_ffn, moe_w_router[j], eg, eu, ed, final_norm_g, i, s, False, last)
            if not last:
                x_ctx = _moe(x_ctx, mods, g_ffn, moe_w_router[j], eg, eu, ed, final_norm_g, i, n_ctx, True, False)
    return x_lat.reshape(n_batch, s, d)
```

```python
import functools

import numpy as np
import jax
import jax.numpy as jnp
from jax import lax
from jax.experimental import pallas as pl
from jax.experimental.pallas import tpu as pltpu

F32 = jnp.float32
BF16 = jnp.bfloat16

GRID_W = 64
N_MOD = 6
EPS = 1e-6
FNET_GROUPS = 4
HEAD_DIM = 64
N_KV_HEADS = 4
WINDOW = 128
ROPE_THETA = 10000.0
ROT_FREQS = HEAD_DIM // 4
N_EXPERTS = 8
MOD_ROWS = 8
CTX_ROW = MOD_ROWS - 1
LANES = 128
NEG = -1e30
VMEM_LIMIT = 56 * 1024 * 1024

FFN_TM = 512
FFN_TF = 1792
MOE_TM = 512
MOE_TF = 1792
ROW_TM = 512
PROJ_TM = 512
DFT_P = 64
DFT_R = 128
FNET_RB = 16
FNET_KB = 8


def _cparams(sem):
    return pltpu.CompilerParams(dimension_semantics=sem, vmem_limit_bytes=VMEM_LIMIT)


def _sigmoid(a):
    return 1.0 / (1.0 + jnp.exp(-a))


def _norm_mod(x, g, shift, scale):
    xn = x * lax.rsqrt(jnp.mean(x * x, axis=-1, keepdims=True) + EPS)
    return (xn * g) * (1.0 + scale) + shift


def _mod_spec(d, idx_fn):
    return pl.BlockSpec((None, 1, d), lambda *ids: (idx_fn(*ids), 0, 0))


def _mod_index(layer, j, tiles_per_batch, ctx):
    def fn(i, *_):
        b = CTX_ROW if ctx else i // tiles_per_batch
        return (layer * MOD_ROWS + b) * N_MOD + j
    return fn


def _ada_kernel(c_ref, w_ref, b_ref, o_ref):
    cc = c_ref[...]
    s = cc * _sigmoid(cc)
    o_ref[...] = jnp.dot(s, w_ref[...], preferred_element_type=F32,
                         precision=lax.Precision.HIGHEST) + b_ref[...]


def _ada(c, c_ctx, ada_w, ada_b):
    depth, d, _ = ada_w.shape
    nb = c.shape[0]
    assert nb < MOD_ROWS
    cc = jnp.concatenate([c, jnp.zeros((CTX_ROW - nb, d), F32), c_ctx[None]], axis=0)
    out = pl.pallas_call(
        _ada_kernel,
        grid=(depth, N_MOD),
        in_specs=[pl.BlockSpec((MOD_ROWS, d), lambda l, j: (0, 0)),
                  pl.BlockSpec((None, d, d), lambda l, j: (l, 0, j)),
                  pl.BlockSpec((None, 1, d), lambda l, j: (l * N_MOD + j, 0, 0))],
        out_specs=pl.BlockSpec((None, MOD_ROWS, d), lambda l, j: (l, 0, j)),
        out_shape=jax.ShapeDtypeStruct((depth, MOD_ROWS, N_MOD * d), F32),
        compiler_params=_cparams(("parallel", "parallel")),
        name="ada",
    )(cc, ada_w, ada_b.reshape(depth * N_MOD, 1, d))
    return out.reshape(depth * MOD_ROWS * N_MOD, 1, d)


def _ffn_kernel(x_ref, g_ref, sh_ref, sc_ref, gt_ref, wg_ref, wu_ref, wd_ref, o_ref, h_scr, acc_scr):
    f = pl.program_id(1)

    @pl.when(f == 0)
    def _():
        h_scr[...] = _norm_mod(x_ref[...], g_ref[...], sh_ref[...], sc_ref[...]).astype(BF16)
        acc_scr[...] = jnp.zeros_like(acc_scr)

    h = h_scr[...]
    a = jnp.dot(h, wg_ref[...], preferred_element_type=F32)
    u = jnp.dot(h, wu_ref[...], preferred_element_type=F32)
    t = (a * _sigmoid(a)) * u
    acc_scr[...] += jnp.dot(t.astype(BF16), wd_ref[...], preferred_element_type=F32)

    @pl.when(f == pl.num_programs(1) - 1)
    def _():
        o_ref[...] = x_ref[...] + gt_ref[...] * acc_scr[...]


def _ffn_dense(x, mods, g_all, wg, wu, wd, layer, tiles_per_batch, ctx, tm=FFN_TM, tf=FFN_TF):
    t, d = x.shape
    ff = wg.shape[2]
    tm = min(tm, t)
    tf = min(tf, ff)
    sj = layer // 2
    mi = functools.partial(_mod_index, layer, tiles_per_batch=tiles_per_batch // tm if not ctx else 1, ctx=ctx)
    return pl.pallas_call(
        _ffn_kernel,
        grid=(t // tm, ff // tf),
        in_specs=[pl.BlockSpec((tm, d), lambda i, f: (i, 0)),
                  pl.BlockSpec((None, 1, d), lambda i, f: (layer, 0, 0)),
                  _mod_spec(d, mi(3)), _mod_spec(d, mi(4)), _mod_spec(d, mi(5)),
                  pl.BlockSpec((None, d, tf), lambda i, f: (sj, 0, f)),
                  pl.BlockSpec((None, d, tf), lambda i, f: (sj, 0, f)),
                  pl.BlockSpec((None, tf, d), lambda i, f: (sj, f, 0))],
        out_specs=pl.BlockSpec((tm, d), lambda i, f: (i, 0)),
        out_shape=jax.ShapeDtypeStruct((t, d), F32),
        scratch_shapes=[pltpu.VMEM((tm, d), BF16), pltpu.VMEM((tm, d), F32)],
        compiler_params=_cparams(("parallel", "arbitrary")),
        name="ffn_dense",
    )(x, g_all, mods, mods, mods, wg, wu, wd)


def _proj_res_kernel(a_ref, w_ref, x_ref, gt_ref, o_ref):
    y = jnp.dot(a_ref[...], w_ref[...], preferred_element_type=F32)
    o_ref[...] = x_ref[...] + gt_ref[...] * y


def _proj_res(a, w, x, mods, layer, j, tiles_per_batch, ctx, tm=PROJ_TM):
    t, d = x.shape
    k = a.shape[1]
    tm = min(tm, t)
    mi = _mod_index(layer, j, tiles_per_batch // tm if not ctx else 1, ctx)
    return pl.pallas_call(
        _proj_res_kernel,
        grid=(t // tm,),
        in_specs=[pl.BlockSpec((tm, k), lambda i: (i, 0)),
                  pl.BlockSpec((None, k, d), lambda i: (layer // 2, 0, 0)),
                  pl.BlockSpec((tm, d), lambda i: (i, 0)),
                  _mod_spec(d, mi)],
        out_specs=pl.BlockSpec((tm, d), lambda i: (i, 0)),
        out_shape=jax.ShapeDtypeStruct((t, d), F32),
        compiler_params=_cparams(("parallel",)),
        name="proj_res",
    )(a, w, x, mods)


def _dft_angles(n):
    a = np.arange(n)
    return 2.0 * np.pi * ((a[:, None] * a[None, :]) % n) / n


def _seq_dft_tables(n):
    kp = np.arange(DFT_P)[None, :, None]
    p = np.arange(DFT_P)[None, None, :]
    r = np.arange(DFT_R)[:, None, None]
    th = 2.0 * np.pi * ((kp * (DFT_R * p + r)) % n) / n
    tab1 = np.concatenate([np.cos(th), -np.sin(th)], axis=1)
    th2 = _dft_angles(DFT_R)
    c2, s2 = np.cos(th2), np.sin(th2)
    tab2 = np.block([[c2, s2], [-s2, c2]])
    return jnp.asarray(tab1, BF16), jnp.asarray(tab2, BF16)


def _chan_dft_tables(group_dim):
    th = _dft_angles(group_dim)
    return jnp.asarray(np.cos(th), BF16), jnp.asarray(np.sin(th), BF16)


def _ctx_dft_table(n_ctx):
    th = _dft_angles(n_ctx)
    return jnp.asarray(np.concatenate([np.cos(th), -np.sin(th)], axis=0), BF16)


def _cols_store(scr, val):
    for c in range(scr.shape[0]):
        scr[c] = val[:, c * LANES:(c + 1) * LANES]


def _cols_load(scr):
    return jnp.concatenate([scr[c] for c in range(scr.shape[0])], axis=-1)


def _cols_store_rows(scr, sel, val):
    for c in range(scr.shape[0]):
        scr[c, sel, :] = val[:, c * LANES:(c + 1) * LANES]


def _cols_load_rows(scr, sel):
    return jnp.concatenate([scr[c, sel, :] for c in range(scr.shape[0])], axis=-1)


def _fnet_stage1_kernel(x_ref, g_ref, sh_ref, sc_ref, tab_ref, zr_ref, zi_ref, h_scr, zr_scr, zi_scr):
    d = g_ref.shape[-1]
    rows = DFT_P * FNET_RB
    _cols_store(h_scr, _norm_mod(x_ref[...].reshape(rows, d), g_ref[...], sh_ref[...], sc_ref[...]))
    for j in range(FNET_RB):
        sel = pl.ds(j, DFT_P, stride=FNET_RB)
        z = jnp.dot(tab_ref[j], _cols_load_rows(h_scr, sel).astype(BF16), preferred_element_type=F32)
        _cols_store_rows(zr_scr, sel, z[:DFT_P])
        _cols_store_rows(zi_scr, sel, z[DFT_P:])
    zr_ref[...] = _cols_load(zr_scr).astype(BF16).reshape(DFT_P, FNET_RB, d)
    zi_ref[...] = _cols_load(zi_scr).astype(BF16).reshape(DFT_P, FNET_RB, d)


def _mix_tail(pr, pi, cc_ref, sc_ref, wout_ref, scale):
    gd = cc_ref.shape[0]
    ys = []
    for g in range(pr.shape[1] // gd):
        ys.append(jnp.dot(pr[:, g * gd:(g + 1) * gd], cc_ref[...], preferred_element_type=F32)
                  + jnp.dot(pi[:, g * gd:(g + 1) * gd], sc_ref[...], preferred_element_type=F32))
    mixed = (jnp.concatenate(ys, axis=-1) * scale).astype(BF16)
    return jnp.dot(mixed, wout_ref[...], preferred_element_type=F32)


def _fnet_stage2_kernel(zr_ref, zi_ref, tab2_ref, cc_ref, sc_ref, wout_ref, x_ref, gt_ref, o_ref,
                        p_scr, x_scr, o_scr, *, scale):
    d = gt_ref.shape[-1]
    rows = DFT_R * FNET_KB
    for j in range(FNET_KB):
        z = jnp.concatenate([zr_ref[j * DFT_R:(j + 1) * DFT_R, :], zi_ref[j * DFT_R:(j + 1) * DFT_R, :]], axis=0)
        p = jnp.dot(tab2_ref[...], z, preferred_element_type=F32)
        p_scr[j * DFT_R:(j + 1) * DFT_R, :d] = p[:DFT_R].astype(BF16)
        p_scr[j * DFT_R:(j + 1) * DFT_R, d:] = p[DFT_R:].astype(BF16)
    y = _mix_tail(p_scr[:, :d], p_scr[:, d:], cc_ref, sc_ref, wout_ref, scale)
    _cols_store(x_scr, x_ref[...].reshape(rows, d))
    for j in range(FNET_KB):
        sel = pl.ds(j, DFT_R, stride=FNET_KB)
        _cols_store_rows(o_scr, sel, _cols_load_rows(x_scr, sel) + gt_ref[...] * y[j * DFT_R:(j + 1) * DFT_R, :])
    o_ref[...] = _cols_load(o_scr).reshape(DFT_R, FNET_KB, d)


def _fnet_lat(x, mods, g_all, w_out, layer, n_batch):
    t, d = x.shape
    s = t // n_batch
    assert s == DFT_P * DFT_R
    tab1, tab2 = _seq_dft_tables(s)
    ccos, csin = _chan_dft_tables(d // FNET_GROUPS)
    mi = lambda j: (lambda b, *_: (layer * MOD_ROWS + b) * N_MOD + j)
    xv = x.reshape(n_batch, DFT_P, DFT_R, d)
    blk1 = (None, DFT_P, FNET_RB, d)
    rows1 = DFT_P * FNET_RB
    zr, zi = pl.pallas_call(
        _fnet_stage1_kernel,
        grid=(n_batch, DFT_R // FNET_RB),
        in_specs=[pl.BlockSpec(blk1, lambda b, r: (b, 0, r, 0)),
                  pl.BlockSpec((None, 1, d), lambda b, r: (layer, 0, 0)),
                  _mod_spec(d, mi(0)), _mod_spec(d, mi(1)),
                  pl.BlockSpec((FNET_RB, 2 * DFT_P, DFT_P), lambda b, r: (r, 0, 0))],
        out_specs=[pl.BlockSpec(blk1, lambda b, r: (b, 0, r, 0))] * 2,
        out_shape=[jax.ShapeDtypeStruct((n_batch, DFT_P, DFT_R, d), BF16)] * 2,
        scratch_shapes=[pltpu.VMEM((d // LANES, rows1, LANES), F32)] * 3,
        compiler_params=_cparams(("parallel", "parallel")),
        name="fnet_stage1",
    )(xv, g_all, mods, mods, tab1)
    zr = zr.reshape(n_batch, DFT_P * DFT_R, d)
    zi = zi.reshape(n_batch, DFT_P * DFT_R, d)
    xo = x.reshape(n_batch, DFT_R, DFT_P, d)
    blk2 = (None, DFT_R, FNET_KB, d)
    rows = FNET_KB * DFT_R
    scale = float(1.0 / np.sqrt(float(s) * (d // FNET_GROUPS)))
    out = pl.pallas_call(
        functools.partial(_fnet_stage2_kernel, scale=scale),
        grid=(n_batch, DFT_P // FNET_KB),
        in_specs=[pl.BlockSpec((None, rows, d), lambda b, k: (b, k, 0)),
                  pl.BlockSpec((None, rows, d), lambda b, k: (b, k, 0)),
                  pl.BlockSpec((2 * DFT_R, 2 * DFT_R), lambda b, k: (0, 0)),
                  pl.BlockSpec(ccos.shape, lambda b, k: (0, 0)),
                  pl.BlockSpec(csin.shape, lambda b, k: (0, 0)),
                  pl.BlockSpec((None, d, d), lambda b, k: (layer // 2, 0, 0)),
                  pl.BlockSpec(blk2, lambda b, k: (b, 0, k, 0)),
                  _mod_spec(d, mi(2))],
        out_specs=pl.BlockSpec(blk2, lambda b, k: (b, 0, k, 0)),
        out_shape=jax.ShapeDtypeStruct((n_batch, DFT_R, DFT_P, d), F32),
        scratch_shapes=[pltpu.VMEM((rows, 2 * d), BF16)] + [pltpu.VMEM((d // LANES, rows, LANES), F32)] * 2,
        compiler_params=_cparams(("parallel", "parallel")),
        name="fnet_stage2",
    )(zr, zi, tab2, ccos, csin, w_out, xo, mods)
    return out.reshape(t, d)


def _fnet_ctx_kernel(x_ref, g_ref, sh_ref, sc_ref, gt_ref, tab_ref, cc_ref, sc2_ref, wout_ref, o_ref, *, scale):
    n = x_ref.shape[0]
    h = _norm_mod(x_ref[...], g_ref[...], sh_ref[...], sc_ref[...]).astype(BF16)
    p = jnp.dot(tab_ref[...], h, preferred_element_type=F32)
    y = _mix_tail(p[:n].astype(BF16), p[n:].astype(BF16), cc_ref, sc2_ref, wout_ref, scale)
    o_ref[...] = x_ref[...] + gt_ref[...] * y


def _fnet_ctx(x, mods, g_all, w_out, layer, n_batch):
    t, d = x.shape
    n = t // n_batch
    gd = d // FNET_GROUPS
    ccos, csin = _chan_dft_tables(gd)
    tab = _ctx_dft_table(n)
    mi = lambda j: (lambda b: (layer * MOD_ROWS + CTX_ROW) * N_MOD + j)
    scale = float(1.0 / np.sqrt(float(n) * gd))
    return pl.pallas_call(
        functools.partial(_fnet_ctx_kernel, scale=scale),
        grid=(n_batch,),
        in_specs=[pl.BlockSpec((n, d), lambda b: (b, 0)),
                  pl.BlockSpec((None, 1, d), lambda b: (layer, 0, 0)),
                  _mod_spec(d, mi(0)), _mod_spec(d, mi(1)), _mod_spec(d, mi(2)),
                  pl.BlockSpec(tab.shape, lambda b: (0, 0)),
                  pl.BlockSpec(ccos.shape, lambda b: (0, 0)),
                  pl.BlockSpec(csin.shape, lambda b: (0, 0)),
                  pl.BlockSpec((None, d, d), lambda b: (layer // 2, 0, 0))],
        out_specs=pl.BlockSpec((n, d), lambda b: (b, 0)),
        out_shape=jax.ShapeDtypeStruct((t, d), F32),
        compiler_params=_cparams(("parallel",)),
        name="fnet_ctx",
    )(x, g_all, mods, mods, mods, tab, ccos, csin, w_out)


def _rope_tables(n_seq):
    rows = n_seq // GRID_W
    row = jnp.repeat(jnp.arange(rows, dtype=F32), GRID_W)
    col = jnp.tile(jnp.arange(GRID_W, dtype=F32), rows)
    inv_freq = ROPE_THETA ** (-jnp.arange(ROT_FREQS, dtype=F32) / ROT_FREQS)
    ang = jnp.stack([row[:, None] * inv_freq, col[:, None] * inv_freq], axis=1)
    cos, sin = jnp.cos(ang), jnp.sin(ang)
    zero = jnp.zeros_like(sin)
    cos_h = jnp.stack([cos, cos], axis=2).reshape(n_seq, HEAD_DIM)
    sin_lo = jnp.stack([-sin, zero], axis=2).reshape(n_seq, HEAD_DIM)
    sin_hi = jnp.stack([zero, sin], axis=2).reshape(n_seq, HEAD_DIM)
    rep = LANES // HEAD_DIM
    return jnp.tile(cos_h, (1, rep)), jnp.tile(sin_lo, (1, rep)), jnp.tile(sin_hi, (1, rep))


def _qkv_kernel(x_ref, g_ref, sh_ref, sc_ref, w_ref, *rest, rope, q_dim, kv_dim, q_scale):
    if rope:
        cos_ref, slo_ref, shi_ref, q_ref, kt_ref, v_ref = rest
    else:
        q_ref, kt_ref, v_ref = rest
    h = _norm_mod(x_ref[...], g_ref[...], sh_ref[...], sc_ref[...]).astype(BF16)
    qkv = jnp.dot(h, w_ref[...], preferred_element_type=F32)

    def rot(xs):
        if not rope:
            return xs
        return (xs * cos_ref[...] + pltpu.roll(xs, LANES - ROT_FREQS, axis=1) * slo_ref[...]
                + pltpu.roll(xs, ROT_FREQS, axis=1) * shi_ref[...])

    for j in range(q_dim // LANES):
        q_ref[:, j * LANES:(j + 1) * LANES] = (rot(qkv[:, j * LANES:(j + 1) * LANES]) * q_scale).astype(BF16)
    ks = [rot(qkv[:, q_dim + j * LANES:q_dim + (j + 1) * LANES]) for j in range(kv_dim // LANES)]
    kt_ref[...] = jnp.concatenate(ks, axis=-1).T.astype(BF16)
    v_ref[...] = qkv[:, q_dim + kv_dim:].astype(BF16)


def _qkv(x, mods, g_all, w_qkv, layer, n_batch, ctx, rope_tabs=None, tm=512):
    t, d = x.shape
    n = t // n_batch
    tm = min(tm, n)
    tpb = n // tm
    kv_dim = N_KV_HEADS * HEAD_DIM
    q_dim = w_qkv.shape[2] - 2 * kv_dim
    mi = lambda j: _mod_index(layer, j, tpb, ctx)
    in_specs = [pl.BlockSpec((tm, d), lambda i: (i, 0)),
                pl.BlockSpec((None, 1, d), lambda i: (layer, 0, 0)),
                _mod_spec(d, mi(0)), _mod_spec(d, mi(1)),
                pl.BlockSpec((None,) + w_qkv.shape[1:], lambda i: (layer // 2, 0, 0))]
    args = [x, g_all, mods, mods, w_qkv]
    rope = rope_tabs is not None
    if rope:
        in_specs += [pl.BlockSpec((tm, LANES), lambda i: (i % tpb, 0))] * 3
        args += list(rope_tabs)
    return pl.pallas_call(
        functools.partial(_qkv_kernel, rope=rope, q_dim=q_dim, kv_dim=kv_dim, q_scale=HEAD_DIM ** -0.5),
        grid=(t // tm,),
        in_specs=in_specs,
        out_specs=[pl.BlockSpec((tm, q_dim), lambda i: (i, 0)),
                   pl.BlockSpec((None, kv_dim, tm), lambda i: (i // tpb, 0, i % tpb)),
                   pl.BlockSpec((tm, kv_dim), lambda i: (i, 0))],
        out_shape=[jax.ShapeDtypeStruct((t, q_dim), BF16),
                   jax.ShapeDtypeStruct((n_batch, kv_dim, n), BF16),
                   jax.ShapeDtypeStruct((t, kv_dim), BF16)],
        compiler_params=_cparams(("parallel",)),
        name="qkv_ctx" if ctx else "qkv_lat",
    )(*args)


def _heads_attend(q_ref, kt, v, bias, sink_ref, o_ref):
    nq = q_ref.shape[0]
    n_heads = q_ref.shape[1] // HEAD_DIM
    group = n_heads // N_KV_HEADS
    outs = [None] * n_heads
    for g in range(N_KV_HEADS):
        heads = range(g * group, (g + 1) * group)
        qg = jnp.concatenate([q_ref[:, hd * HEAD_DIM:(hd + 1) * HEAD_DIM] for hd in heads], axis=0)
        s = jnp.dot(qg, kt[g * HEAD_DIM:(g + 1) * HEAD_DIM, :], preferred_element_type=F32)
        parts = [s[:, k * LANES:(k + 1) * LANES] for k in range(s.shape[1] // LANES)]
        if bias is not None:
            parts[0] = parts[0] + bias[0]
            parts[2] = parts[2] + bias[1]
        sink = jnp.concatenate([jnp.full((nq, 1), sink_ref[hd], F32) for hd in heads], axis=0)
        m = jnp.maximum(jnp.max(functools.reduce(jnp.maximum, parts), axis=-1, keepdims=True), sink)
        ps = [jnp.exp(part - m) for part in parts]
        den = jnp.sum(functools.reduce(jnp.add, ps), axis=-1, keepdims=True) + jnp.exp(sink - m)
        pv = jnp.dot(jnp.concatenate(ps, axis=-1).astype(BF16), v, preferred_element_type=F32)
        on = pv[:, g * HEAD_DIM:(g + 1) * HEAD_DIM] / den
        for k, hd in enumerate(heads):
            outs[hd] = on[k * nq:(k + 1) * nq, :]
    o_ref[...] = jnp.concatenate(outs, axis=-1).astype(BF16)


def _attn_lat_kernel(sink_ref, q_ref, ktp_ref, ktc_ref, ktn_ref, vp_ref, vc_ref, vn_ref, ktx_ref, vx_ref,
                     blo_ref, bhi_ref, o_ref, kt_scr, v_scr):
    w = ktc_ref.shape[1]
    n_ctx = vx_ref.shape[0]
    for c, (kr, vr) in enumerate(((ktp_ref, vp_ref), (ktc_ref, vc_ref), (ktn_ref, vn_ref))):
        kt_scr[:, c * w:(c + 1) * w] = kr[...]
        v_scr[c * w:(c + 1) * w, :] = vr[...]
    kt_scr[:, 3 * w:3 * w + n_ctx] = ktx_ref[...]
    v_scr[3 * w:3 * w + n_ctx, :] = vx_ref[...]
    _heads_attend(q_ref, kt_scr[...], v_scr[...], (blo_ref[...], bhi_ref[...]), sink_ref, o_ref)


def _attn_lat(q, kt, v, ktx, vx, sinks, n_batch):
    t, qd = q.shape
    s = t // n_batch
    n_ctx = vx.shape[0] // n_batch
    w = WINDOW
    nb = s // w
    kvd = v.shape[1]
    j_all = 3 * w + n_ctx
    group = qd // HEAD_DIM // N_KV_HEADS
    qi = np.arange(w)[:, None]
    ki = np.arange(w)[None, :]
    lo = np.where(ki >= qi, 0.0, NEG).astype(np.float32)
    hi = np.where(ki <= qi, 0.0, NEG).astype(np.float32)
    off = np.full((w, w), NEG, np.float32)
    blo = jnp.asarray(np.stack([np.tile(lo, (group, 1)), np.tile(off, (group, 1))]))
    bhi = jnp.asarray(np.stack([np.tile(hi, (group, 1)), np.tile(off, (group, 1))]))
    prev = lambda b, i: (b, 0, jnp.maximum(i - 1, 0))
    nxt = lambda b, i: (b, 0, jnp.minimum(i + 1, nb - 1))
    vprev = lambda b, i: (b * nb + jnp.maximum(i - 1, 0), 0)
    vnxt = lambda b, i: (b * nb + jnp.minimum(i + 1, nb - 1), 0)
    return pl.pallas_call(
        _attn_lat_kernel,
        grid=(n_batch, nb),
        in_specs=[pl.BlockSpec(memory_space=pltpu.SMEM),
                  pl.BlockSpec((w, qd), lambda b, i: (b * nb + i, 0)),
                  pl.BlockSpec((None, kvd, w), prev),
                  pl.BlockSpec((None, kvd, w), lambda b, i: (b, 0, i)),
                  pl.BlockSpec((None, kvd, w), nxt),
                  pl.BlockSpec((w, kvd), vprev),
                  pl.BlockSpec((w, kvd), lambda b, i: (b * nb + i, 0)),
                  pl.BlockSpec((w, kvd), vnxt),
                  pl.BlockSpec((None, kvd, n_ctx), lambda b, i: (b, 0, 0)),
                  pl.BlockSpec((n_ctx, kvd), lambda b, i: (b, 0)),
                  pl.BlockSpec((None, group * w, w), lambda b, i: (jnp.where(i == 0, 1, 0), 0, 0)),
                  pl.BlockSpec((None, group * w, w), lambda b, i: (jnp.where(i == nb - 1, 1, 0), 0, 0))],
        out_specs=pl.BlockSpec((w, qd), lambda b, i: (b * nb + i, 0)),
        out_shape=jax.ShapeDtypeStruct((t, qd), BF16),
        scratch_shapes=[pltpu.VMEM((kvd, j_all), BF16), pltpu.VMEM((j_all, kvd), BF16)],
        compiler_params=_cparams(("parallel", "parallel")),
        name="attn_lat",
    )(sinks, q, kt, kt, kt, v, v, v, ktx, vx, blo, bhi)


def _attn_ctx_kernel(sink_ref, q_ref, kt_ref, v_ref, o_ref):
    _heads_attend(q_ref, kt_ref[...], v_ref[...], None, sink_ref, o_ref)


def _attn_ctx(q, kt, v, sinks, n_batch):
    t, qd = q.shape
    n = t // n_batch
    kvd = v.shape[1]
    return pl.pallas_call(
        _attn_ctx_kernel,
        grid=(n_batch,),
        in_specs=[pl.BlockSpec(memory_space=pltpu.SMEM),
                  pl.BlockSpec((n, qd), lambda b: (b, 0)),
                  pl.BlockSpec((None, kvd, n), lambda b: (b, 0, 0)),
                  pl.BlockSpec((n, kvd), lambda b: (b, 0))],
        out_specs=pl.BlockSpec((n, qd), lambda b: (b, 0)),
        out_shape=jax.ShapeDtypeStruct((t, qd), BF16),
        compiler_params=_cparams(("parallel",)),
        name="attn_ctx",
    )(sinks, q, kt, v)


ROUTE_I1, ROUTE_I2, ROUTE_R1, ROUTE_R2, ROUTE_G1, ROUTE_G2 = range(6)
ROUTE_ROWS = 8
TAB_START, TAB_COUNT, TAB_USED, TAB_SIZE = 0, N_EXPERTS, 2 * N_EXPERTS, 2 * N_EXPERTS + 8
ROW_UNROLL = 8


def _router_kernel(x_ref, g_ref, sh_ref, sc_ref, wr_ref, tri_ref, route_ref, idx_ref, cnt_ref, carry_scr):
    @pl.when(pl.program_id(0) == 0)
    def _():
        carry_scr[...] = jnp.zeros_like(carry_scr)

    h = _norm_mod(x_ref[...], g_ref[...], sh_ref[...], sc_ref[...])
    logits = jnp.dot(h, wr_ref[...], preferred_element_type=F32, precision=lax.Precision.HIGHEST)
    lane = lax.broadcasted_iota(jnp.int32, logits.shape, 1)
    lane_f = lane.astype(F32)
    logits = jnp.where(lane < N_EXPERTS, logits, -jnp.inf)
    m1 = jnp.max(logits, axis=-1, keepdims=True)
    i1 = jnp.min(jnp.where(logits == m1, lane_f, float(LANES)), axis=-1, keepdims=True)
    oh1 = lane_f == i1
    rest = jnp.where(oh1, -jnp.inf, logits)
    m2 = jnp.max(rest, axis=-1, keepdims=True)
    i2 = jnp.min(jnp.where(rest == m2, lane_f, float(LANES)), axis=-1, keepdims=True)
    oh2 = lane_f == i2
    e2 = jnp.exp(m2 - m1)
    g1 = 1.0 / (1.0 + e2)
    g2 = e2 / (1.0 + e2)
    sel = jnp.where(oh1, 1.0, 0.0) + jnp.where(oh2, 1.0, 0.0)
    before = jnp.dot(tri_ref[...], sel.astype(BF16), preferred_element_type=F32) + carry_scr[0:1, :]
    r1 = jnp.sum(jnp.where(oh1, before, 0.0), axis=-1, keepdims=True)
    r2 = jnp.sum(jnp.where(oh2, before, 0.0), axis=-1, keepdims=True)
    total = carry_scr[0:1, :] + jnp.sum(sel, axis=0, keepdims=True)
    carry_scr[...] = jnp.broadcast_to(total, carry_scr.shape)
    cnt_ref[...] = jnp.broadcast_to(total, cnt_ref.shape)
    rec = jnp.zeros_like(logits)
    for ln, val in ((ROUTE_I1, i1), (ROUTE_I2, i2), (ROUTE_R1, r1), (ROUTE_R2, r2), (ROUTE_G1, g1), (ROUTE_G2, g2)):
        rec = jnp.where(lane == ln, val, rec)
    route_ref[...] = rec
    idx_ref[...] = rec.T[:ROUTE_ROWS, :].astype(jnp.int32)


def _router(x, mods, g_all, w_router, layer, tiles_per_batch, ctx, tm):
    t, d = x.shape
    mi = lambda j: _mod_index(layer, j, tiles_per_batch // tm if not ctx else 1, ctx)
    wr = jnp.zeros((d, LANES), F32).at[:, :N_EXPERTS].set(w_router)
    tri = jnp.asarray(np.tril(np.ones((tm, tm), np.float32), -1), BF16)
    return pl.pallas_call(
        _router_kernel,
        grid=(t // tm,),
        in_specs=[pl.BlockSpec((tm, d), lambda i: (i, 0)),
                  pl.BlockSpec((None, 1, d), lambda i: (layer, 0, 0)),
                  _mod_spec(d, mi(3)), _mod_spec(d, mi(4)),
                  pl.BlockSpec((d, LANES), lambda i: (0, 0)),
                  pl.BlockSpec((tm, tm), lambda i: (0, 0))],
        out_specs=[pl.BlockSpec((tm, LANES), lambda i: (i, 0)),
                   pl.BlockSpec((None, ROUTE_ROWS, tm), lambda i: (i, 0, 0)),
                   pl.BlockSpec((8, LANES), lambda i: (0, 0))],
        out_shape=[jax.ShapeDtypeStruct((t, LANES), F32),
                   jax.ShapeDtypeStruct((t // tm, ROUTE_ROWS, tm), jnp.int32),
                   jax.ShapeDtypeStruct((8, LANES), F32)],
        scratch_shapes=[pltpu.VMEM((8, LANES), F32)],
        compiler_params=_cparams(("arbitrary",)),
        name="router",
    )(x, g_all, mods, mods, wr, tri)


SUB = 8


def _to_row_tiles(ref, val):
    n = val.shape[0]
    for k in range(SUB):
        ref[pl.ds(k, n, stride=SUB), :] = val[:, k * LANES:(k + 1) * LANES]


def _from_row_tiles(ref):
    n = ref.shape[0] // SUB
    return jnp.concatenate([ref[pl.ds(k, n, stride=SUB), :] for k in range(SUB)], axis=-1)


def _row_copy(src, dst, src_row, dst_row, sem):
    return pltpu.make_async_copy(src.at[pl.ds(pl.multiple_of(src_row * SUB, SUB), SUB)],
                                 dst.at[pl.ds(pl.multiple_of(dst_row * SUB, SUB), SUB)], sem)


def _for_rows(tm, body):
    def blk(i, _):
        base = pl.multiple_of(i * ROW_UNROLL, ROW_UNROLL)
        for j in range(ROW_UNROLL):
            body(base + j, j)
        return 0

    lax.fori_loop(0, tm // ROW_UNROLL, blk, 0)


def _dispatch_kernel(tab_ref, slot_ref, x_ref, g_ref, sh_ref, sc_ref, xs_ref, h_scr, zero_scr, sems, *, group):
    tm = x_ref.shape[0]
    n_groups = xs_ref.shape[0] // (group * SUB)
    _to_row_tiles(h_scr, _norm_mod(x_ref[...], g_ref[...], sh_ref[...], sc_ref[...]))

    def issue(r, j):
        for k in range(2):
            _row_copy(h_scr, xs_ref, r, slot_ref[0, k, r], sems.at[k]).start(priority=j % 2)

    _for_rows(tm, issue)
    for k in range(2):
        pltpu.make_async_copy(h_scr, xs_ref.at[pl.ds(0, tm * SUB)], sems.at[k]).wait()

    @pl.when(pl.program_id(0) == pl.num_programs(0) - 1)
    def _():
        zero_scr[...] = jnp.zeros_like(zero_scr)
        for e in range(N_EXPERTS):
            n = tab_ref[TAB_COUNT + e]
            n_pad = lax.rem(group - lax.rem(n, group), group)
            first = tab_ref[TAB_START + e] + n

            def fill(k, _, first=first):
                _row_copy(zero_scr, xs_ref, 0, first + k, sems.at[2]).start()
                return 0

            def fill_done(k, _):
                _row_copy(zero_scr, xs_ref, 0, 0, sems.at[2]).wait()
                return 0

            lax.fori_loop(0, n_pad, fill, 0)
            lax.fori_loop(0, n_pad, fill_done, 0)

        def clear(j, _):
            row = pl.multiple_of(j * (group * SUB), group * SUB)
            cp = pltpu.make_async_copy(zero_scr, xs_ref.at[pl.ds(row, group * SUB)], sems.at[2])
            cp.start()
            cp.wait()
            return 0

        lax.fori_loop(tab_ref[TAB_USED], n_groups, clear, 0)


def _dispatch(x, tab, slots, mods, g_all, layer, tiles_per_batch, ctx, n_groups, group, tm):
    t, d = x.shape
    mi = lambda j: _mod_index(layer, j, tiles_per_batch // tm if not ctx else 1, ctx)
    return pl.pallas_call(
        functools.partial(_dispatch_kernel, group=group),
        grid=(t // tm,),
        in_specs=[pl.BlockSpec(memory_space=pltpu.SMEM),
                  pl.BlockSpec((1, 2, tm), lambda i: (i, 0, 0), memory_space=pltpu.SMEM),
                  pl.BlockSpec((tm, d), lambda i: (i, 0)),
                  pl.BlockSpec((None, 1, d), lambda i: (layer, 0, 0)),
                  _mod_spec(d, mi(3)), _mod_spec(d, mi(4))],
        out_specs=pl.BlockSpec(memory_space=pl.ANY),
        out_shape=jax.ShapeDtypeStruct((n_groups * group * SUB, LANES), F32),
        scratch_shapes=[pltpu.VMEM((tm * SUB, LANES), F32), pltpu.VMEM((group * SUB, LANES), F32),
                        pltpu.SemaphoreType.DMA((3,))],
        compiler_params=_cparams(("arbitrary",)),
        name="moe_dispatch",
    )(tab, slots, x, g_all, mods, mods)


def _moe_ffn_kernel(be_ref, bv_ref, xs_ref, wg_ref, wu_ref, wd_ref, ys_ref, h_scr, acc_scr):
    del be_ref
    i = pl.program_id(0)
    f = pl.program_id(1)
    last = f == pl.num_programs(1) - 1
    valid = bv_ref[i] > 0

    @pl.when(jnp.logical_and(valid, f == 0))
    def _():
        h_scr[...] = _from_row_tiles(xs_ref).astype(BF16)
        acc_scr[...] = jnp.zeros_like(acc_scr)

    @pl.when(valid)
    def _():
        h = h_scr[...]
        a = jnp.dot(h, wg_ref[...], preferred_element_type=F32)
        u = jnp.dot(h, wu_ref[...], preferred_element_type=F32)
        t = (a * _sigmoid(a)) * u
        acc_scr[...] += jnp.dot(t.astype(BF16), wd_ref[...], preferred_element_type=F32)

    @pl.when(jnp.logical_and(valid, last))
    def _():
        _to_row_tiles(ys_ref, acc_scr[...])

    @pl.when(jnp.logical_and(jnp.logical_not(valid), last))
    def _():
        ys_ref[...] = jnp.zeros_like(ys_ref)


def _moe_ffn(xs, block_expert, block_valid, wg, wu, wd, layer, tm, tf=MOE_TF):
    d, ff = wg.shape[2:]
    assert d == SUB * LANES and xs.shape[1] == LANES
    cap = xs.shape[0] // SUB
    tf = min(tf, ff)
    nf = ff // tf
    sj = layer // 2
    fsel = lambda i, f, bv: jnp.where(bv[i] > 0, f, nf - 1)
    grid_spec = pltpu.PrefetchScalarGridSpec(
        num_scalar_prefetch=2,
        grid=(cap // tm, nf),
        in_specs=[pl.BlockSpec((tm * SUB, LANES), lambda i, f, be, bv: (i, 0)),
                  pl.BlockSpec((None, None, d, tf), lambda i, f, be, bv: (sj, be[i], 0, fsel(i, f, bv))),
                  pl.BlockSpec((None, None, d, tf), lambda i, f, be, bv: (sj, be[i], 0, fsel(i, f, bv))),
                  pl.BlockSpec((None, None, tf, d), lambda i, f, be, bv: (sj, be[i], fsel(i, f, bv), 0))],
        out_specs=pl.BlockSpec((tm * SUB, LANES), lambda i, f, be, bv: (i, 0)),
        scratch_shapes=[pltpu.VMEM((tm, d), BF16), pltpu.VMEM((tm, d), F32)],
    )
    return pl.pallas_call(
        _moe_ffn_kernel,
        grid_spec=grid_spec,
        out_shape=jax.ShapeDtypeStruct(xs.shape, F32),
        compiler_params=_cparams(("parallel", "arbitrary")),
        name="moe_ffn",
    )(block_expert, block_valid, xs, wg, wu, wd)


def _combine_kernel(slot_ref, route_ref, x_ref, gt_ref, fg_ref, ys_ref, o_ref, buf1, buf2, sems, *, final_norm):
    tm = x_ref.shape[0]

    def issue(r, j):
        for k, buf in enumerate((buf1, buf2)):
            _row_copy(ys_ref, buf, slot_ref[0, k, r], r, sems.at[k]).start(priority=j % 2)

    _for_rows(tm, issue)
    for k, buf in enumerate((buf1, buf2)):
        pltpu.make_async_copy(ys_ref.at[pl.ds(0, tm * SUB)], buf, sems.at[k]).wait()
    rec = route_ref[...]
    g1 = rec[:, ROUTE_G1:ROUTE_G1 + 1]
    g2 = rec[:, ROUTE_G2:ROUTE_G2 + 1]
    y = _from_row_tiles(buf1) * g1 + _from_row_tiles(buf2) * g2
    out = x_ref[...] + gt_ref[...] * y
    if final_norm:
        out = out * lax.rsqrt(jnp.mean(out * out, axis=-1, keepdims=True) + EPS) * fg_ref[...]
    o_ref[...] = out


def _combine(ys, slots, route, x, mods, final_g, layer, tiles_per_batch, ctx, final_norm, tm):
    t, d = x.shape
    mi = _mod_index(layer, 5, tiles_per_batch // tm if not ctx else 1, ctx)
    return pl.pallas_call(
        functools.partial(_combine_kernel, final_norm=final_norm),
        grid=(t // tm,),
        in_specs=[pl.BlockSpec((1, 2, tm), lambda i: (i, 0, 0), memory_space=pltpu.SMEM),
                  pl.BlockSpec((tm, LANES), lambda i: (i, 0)),
                  pl.BlockSpec((tm, d), lambda i: (i, 0)),
                  _mod_spec(d, mi),
                  pl.BlockSpec((1, d), lambda i: (0, 0)),
                  pl.BlockSpec(memory_space=pl.ANY)],
        out_specs=pl.BlockSpec((tm, d), lambda i: (i, 0)),
        out_shape=jax.ShapeDtypeStruct((t, d), F32),
        scratch_shapes=[pltpu.VMEM((tm * SUB, LANES), F32)] * 2 + [pltpu.SemaphoreType.DMA((2,))],
        compiler_params=_cparams(("arbitrary",)),
        name="moe_combine",
    )(slots, route, x, mods, final_g.reshape(1, d), ys)


def _moe(x, mods, g_all, w_router, wg, wu, wd, final_g, layer, tiles_per_batch, ctx, final_norm):
    t, d = x.shape
    group = min(MOE_TM, t)
    row_tm = min(ROW_TM, t)
    route, idx, counts = _router(x, mods, g_all, w_router, layer, tiles_per_batch, ctx, row_tm)
    counts = counts[0, :N_EXPERTS].astype(jnp.int32)
    groups = (counts + group - 1) // group
    ends = jnp.cumsum(groups)
    starts = (ends - groups) * group
    n_groups = (2 * t + group - 1) // group + N_EXPERTS
    tab = jnp.zeros((TAB_SIZE,), jnp.int32)
    tab = tab.at[TAB_START:TAB_START + N_EXPERTS].set(starts).at[TAB_COUNT:TAB_COUNT + N_EXPERTS].set(counts)
    tab = tab.at[TAB_USED].set(ends[-1])
    gi = jnp.arange(n_groups, dtype=jnp.int32)
    block_expert = jnp.minimum(jnp.searchsorted(ends, gi, side='right'), N_EXPERTS - 1).astype(jnp.int32)
    block_valid = (gi < ends[-1]).astype(jnp.int32)
    def slot_rows(e, r):
        base = functools.reduce(jnp.add, [jnp.where(e == k, starts[k], 0) for k in range(N_EXPERTS)])
        return base + r
    slots = jnp.stack([slot_rows(idx[:, ROUTE_I1, :], idx[:, ROUTE_R1, :]),
                       slot_rows(idx[:, ROUTE_I2, :], idx[:, ROUTE_R2, :])], axis=1)
    xs = _dispatch(x, tab, slots, mods, g_all, layer, tiles_per_batch, ctx, n_groups, group, row_tm)
    ys = _moe_ffn(xs, block_expert, block_valid, wg, wu, wd, layer, group)
    return _combine(ys, slots, route, x, mods, final_g, layer, tiles_per_batch, ctx, final_norm, row_tm)


def kernel(x, c, ctx, c_ctx, ada_w, ada_b, norm_mix_g, norm_ffn_g, fnet_w_out, attn_w_qkv, attn_w_o, attn_sinks,
           ffn_w_gate, ffn_w_up, ffn_w_down, moe_w_router, moe_w_gate, moe_w_up, moe_w_down, final_norm_g):
    n_batch, s, d = x.shape
    n_ctx = ctx.shape[1]
    depth = ada_w.shape[0]
    mods = _ada(c, c_ctx, ada_w, ada_b)
    g_mix = norm_mix_g.reshape(depth, 1, d)
    g_ffn = norm_ffn_g.reshape(depth, 1, d)
    rope_tabs = _rope_tables(s)
    x_lat = x.reshape(n_batch * s, d)
    x_ctx = ctx.reshape(n_batch * n_ctx, d)
    bf = lambda a: a.astype(BF16)
    w_out, w_qkv, w_o = bf(fnet_w_out), bf(attn_w_qkv), bf(attn_w_o)
    wg, wu, wd = bf(ffn_w_gate), bf(ffn_w_up), bf(ffn_w_down)
    eg, eu, ed = bf(moe_w_gate), bf(moe_w_up), bf(moe_w_down)
    for i in range(depth):
        j = i // 2
        last = i == depth - 1
        if i % 2 == 0:
            x_lat = _fnet_lat(x_lat, mods, g_mix, w_out, i, n_batch)
            if not last:
                x_ctx = _fnet_ctx(x_ctx, mods, g_mix, w_out, i, n_batch)
            x_lat = _ffn_dense(x_lat, mods, g_ffn, wg, wu, wd, i, s, False)
            if not last:
                x_ctx = _ffn_dense(x_ctx, mods, g_ffn, wg, wu, wd, i, n_ctx, True)
        else:
            q, kt, v = _qkv(x_lat, mods, g_mix, w_qkv, i, n_batch, False, rope_tabs)
            qx, ktx, vx = _qkv(x_ctx, mods, g_mix, w_qkv, i, n_batch, True)
            o = _attn_lat(q, kt, v, ktx, vx, attn_sinks[j], n_batch)
            x_lat = _proj_res(o, w_o, x_lat, mods, i, 2, s, False)
            if not last:
                ox = _attn_ctx(qx, ktx, vx, attn_sinks[j], n_batch)
                x_ctx = _proj_res(ox, w_o, x_ctx, mods, i, 2, n_ctx, True)
            x_lat = _moe(x_lat, mods, g_ffn, moe_w_router[j], eg, eu, ed, final_norm_g, i, s, False, last)
            if not last:
                x_ctx = _moe(x_ctx, mods, g_ffn, moe_w_router[j], eg, eu, ed, final_norm_g, i, n_ctx, True, False)
    return x_lat.reshape(n_batch, s, d)
```

```python
import functools

import numpy as np
import jax
import jax.numpy as jnp
from jax import lax
from jax.experimental import pallas as pl
from jax.experimental.pallas import tpu as pltpu

F32 = jnp.float32
BF16 = jnp.bfloat16

GRID_W = 64
N_MOD = 6
EPS = 1e-6
FNET_GROUPS = 4
HEAD_DIM = 64
N_KV_HEADS = 4
WINDOW = 128
ROPE_THETA = 10000.0
ROT_FREQS = HEAD_DIM // 4
N_EXPERTS = 8
MOD_ROWS = 8
CTX_ROW = MOD_ROWS - 1
LANES = 128
NEG = -1e30
VMEM_LIMIT = 56 * 1024 * 1024

FFN_TM = 512
FFN_TF = 1792
MOE_TM = 512
MOE_TF = 1792
ROW_TM = 512
PROJ_TM = 1024
DFT_P = 64
DFT_R = 128
FNET_RB = 16
FNET_KB = 8


def _cparams(sem):
    return pltpu.CompilerParams(dimension_semantics=sem, vmem_limit_bytes=VMEM_LIMIT)


def _sigmoid(a):
    return 1.0 / (1.0 + jnp.exp(-a))


def _norm_mod(x, g, shift, scale):
    xn = x * lax.rsqrt(jnp.mean(x * x, axis=-1, keepdims=True) + EPS)
    return (xn * g) * (1.0 + scale) + shift


def _mod_spec(d, idx_fn):
    return pl.BlockSpec((None, 1, d), lambda *ids: (idx_fn(*ids), 0, 0))


def _mod_index(layer, j, tiles_per_batch, ctx):
    def fn(i, *_):
        b = CTX_ROW if ctx else i // tiles_per_batch
        return (layer * MOD_ROWS + b) * N_MOD + j
    return fn


def _ada_kernel(c_ref, w_ref, b_ref, o_ref):
    cc = c_ref[...]
    s = cc * _sigmoid(cc)
    o_ref[...] = jnp.dot(s, w_ref[...], preferred_element_type=F32,
                         precision=lax.Precision.HIGHEST) + b_ref[...]


def _ada(c, c_ctx, ada_w, ada_b):
    depth, d, _ = ada_w.shape
    nb = c.shape[0]
    assert nb < MOD_ROWS
    cc = jnp.concatenate([c, jnp.zeros((CTX_ROW - nb, d), F32), c_ctx[None]], axis=0)
    out = pl.pallas_call(
        _ada_kernel,
        grid=(depth, N_MOD),
        in_specs=[pl.BlockSpec((MOD_ROWS, d), lambda l, j: (0, 0)),
                  pl.BlockSpec((None, d, d), lambda l, j: (l, 0, j)),
                  pl.BlockSpec((None, 1, d), lambda l, j: (l * N_MOD + j, 0, 0))],
        out_specs=pl.BlockSpec((None, MOD_ROWS, d), lambda l, j: (l, 0, j)),
        out_shape=jax.ShapeDtypeStruct((depth, MOD_ROWS, N_MOD * d), F32),
        compiler_params=_cparams(("parallel", "parallel")),
        name="ada",
    )(cc, ada_w, ada_b.reshape(depth * N_MOD, 1, d))
    return out.reshape(depth * MOD_ROWS * N_MOD, 1, d)


def _ffn_kernel(x_ref, g_ref, sh_ref, sc_ref, gt_ref, wg_ref, wu_ref, wd_ref, o_ref, h_scr, acc_scr):
    f = pl.program_id(1)

    @pl.when(f == 0)
    def _():
        h_scr[...] = _norm_mod(x_ref[...], g_ref[...], sh_ref[...], sc_ref[...]).astype(BF16)
        acc_scr[...] = jnp.zeros_like(acc_scr)

    h = h_scr[...]
    a = jnp.dot(h, wg_ref[...], preferred_element_type=F32)
    u = jnp.dot(h, wu_ref[...], preferred_element_type=F32)
    t = (a * _sigmoid(a)) * u
    acc_scr[...] += jnp.dot(t.astype(BF16), wd_ref[...], preferred_element_type=F32)

    @pl.when(f == pl.num_programs(1) - 1)
    def _():
        o_ref[...] = x_ref[...] + gt_ref[...] * acc_scr[...]


def _ffn_dense(x, mods, g_all, wg, wu, wd, layer, tiles_per_batch, ctx, tm=FFN_TM, tf=FFN_TF):
    t, d = x.shape
    ff = wg.shape[2]
    tm = min(tm, t)
    tf = min(tf, ff)
    sj = layer // 2
    mi = functools.partial(_mod_index, layer, tiles_per_batch=tiles_per_batch // tm if not ctx else 1, ctx=ctx)
    return pl.pallas_call(
        _ffn_kernel,
        grid=(t // tm, ff // tf),
        in_specs=[pl.BlockSpec((tm, d), lambda i, f: (i, 0)),
                  pl.BlockSpec((None, 1, d), lambda i, f: (layer, 0, 0)),
                  _mod_spec(d, mi(3)), _mod_spec(d, mi(4)), _mod_spec(d, mi(5)),
                  pl.BlockSpec((None, d, tf), lambda i, f: (sj, 0, f)),
                  pl.BlockSpec((None, d, tf), lambda i, f: (sj, 0, f)),
                  pl.BlockSpec((None, tf, d), lambda i, f: (sj, f, 0))],
        out_specs=pl.BlockSpec((tm, d), lambda i, f: (i, 0)),
        out_shape=jax.ShapeDtypeStruct((t, d), F32),
        scratch_shapes=[pltpu.VMEM((tm, d), BF16), pltpu.VMEM((tm, d), F32)],
        compiler_params=_cparams(("parallel", "arbitrary")),
        name="ffn_dense",
    )(x, g_all, mods, mods, mods, wg, wu, wd)


def _proj_res_kernel(a_ref, w_ref, x_ref, gt_ref, o_ref):
    y = jnp.dot(a_ref[...], w_ref[...], preferred_element_type=F32)
    o_ref[...] = x_ref[...] + gt_ref[...] * y


def _proj_res(a, w, x, mods, layer, j, tiles_per_batch, ctx, tm=PROJ_TM):
    t, d = x.shape
    k = a.shape[1]
    tm = min(tm, t)
    mi = _mod_index(layer, j, tiles_per_batch // tm if not ctx else 1, ctx)
    return pl.pallas_call(
        _proj_res_kernel,
        grid=(t // tm,),
        in_specs=[pl.BlockSpec((tm, k), lambda i: (i, 0)),
                  pl.BlockSpec((None, k, d), lambda i: (layer // 2, 0, 0)),
                  pl.BlockSpec((tm, d), lambda i: (i, 0)),
                  _mod_spec(d, mi)],
        out_specs=pl.BlockSpec((tm, d), lambda i: (i, 0)),
        out_shape=jax.ShapeDtypeStruct((t, d), F32),
        compiler_params=_cparams(("parallel",)),
        name="proj_res",
    )(a, w, x, mods)


def _dft_angles(n):
    a = np.arange(n)
    return 2.0 * np.pi * ((a[:, None] * a[None, :]) % n) / n


def _seq_dft_tables(n):
    kp = np.arange(DFT_P)[None, :, None]
    p = np.arange(DFT_P)[None, None, :]
    r = np.arange(DFT_R)[:, None, None]
    th = 2.0 * np.pi * ((kp * (DFT_R * p + r)) % n) / n
    tab1 = np.concatenate([np.cos(th), -np.sin(th)], axis=1)
    th2 = _dft_angles(DFT_R)
    c2, s2 = np.cos(th2), np.sin(th2)
    tab2 = np.block([[c2, s2], [-s2, c2]])
    return jnp.asarray(tab1, BF16), jnp.asarray(tab2, BF16)


def _chan_dft_tables(group_dim):
    th = _dft_angles(group_dim)
    return jnp.asarray(np.cos(th), BF16), jnp.asarray(np.sin(th), BF16)


def _ctx_dft_table(n_ctx):
    th = _dft_angles(n_ctx)
    return jnp.asarray(np.concatenate([np.cos(th), -np.sin(th)], axis=0), BF16)


def _cols_store(scr, val):
    for c in range(scr.shape[0]):
        scr[c] = val[:, c * LANES:(c + 1) * LANES]


def _cols_load(scr):
    return jnp.concatenate([scr[c] for c in range(scr.shape[0])], axis=-1)


def _cols_store_rows(scr, sel, val):
    for c in range(scr.shape[0]):
        scr[c, sel, :] = val[:, c * LANES:(c + 1) * LANES]


def _cols_load_rows(scr, sel):
    return jnp.concatenate([scr[c, sel, :] for c in range(scr.shape[0])], axis=-1)


def _fnet_stage1_kernel(x_ref, g_ref, sh_ref, sc_ref, tab_ref, zr_ref, zi_ref, h_scr, zr_scr, zi_scr):
    d = g_ref.shape[-1]
    rows = DFT_P * FNET_RB
    _cols_store(h_scr, _norm_mod(x_ref[...].reshape(rows, d), g_ref[...], sh_ref[...], sc_ref[...]))
    for j in range(FNET_RB):
        sel = pl.ds(j, DFT_P, stride=FNET_RB)
        z = jnp.dot(tab_ref[j], _cols_load_rows(h_scr, sel).astype(BF16), preferred_element_type=F32)
        _cols_store_rows(zr_scr, sel, z[:DFT_P])
        _cols_store_rows(zi_scr, sel, z[DFT_P:])
    zr_ref[...] = _cols_load(zr_scr).astype(BF16).reshape(DFT_P, FNET_RB, d)
    zi_ref[...] = _cols_load(zi_scr).astype(BF16).reshape(DFT_P, FNET_RB, d)


def _mix_tail(pr, pi, cc_ref, sc_ref, wout_ref, scale):
    gd = cc_ref.shape[0]
    ys = []
    for g in range(pr.shape[1] // gd):
        ys.append(jnp.dot(pr[:, g * gd:(g + 1) * gd], cc_ref[...], preferred_element_type=F32)
                  + jnp.dot(pi[:, g * gd:(g + 1) * gd], sc_ref[...], preferred_element_type=F32))
    mixed = (jnp.concatenate(ys, axis=-1) * scale).astype(BF16)
    return jnp.dot(mixed, wout_ref[...], preferred_element_type=F32)


def _fnet_stage2_kernel(zr_ref, zi_ref, tab2_ref, cc_ref, sc_ref, wout_ref, x_ref, gt_ref, o_ref,
                        p_scr, x_scr, o_scr, *, scale):
    d = gt_ref.shape[-1]
    rows = DFT_R * FNET_KB
    for j in range(FNET_KB):
        z = jnp.concatenate([zr_ref[j * DFT_R:(j + 1) * DFT_R, :], zi_ref[j * DFT_R:(j + 1) * DFT_R, :]], axis=0)
        p = jnp.dot(tab2_ref[...], z, preferred_element_type=F32)
        p_scr[j * DFT_R:(j + 1) * DFT_R, :d] = p[:DFT_R].astype(BF16)
        p_scr[j * DFT_R:(j + 1) * DFT_R, d:] = p[DFT_R:].astype(BF16)
    y = _mix_tail(p_scr[:, :d], p_scr[:, d:], cc_ref, sc_ref, wout_ref, scale)
    _cols_store(x_scr, x_ref[...].reshape(rows, d))
    for j in range(FNET_KB):
        sel = pl.ds(j, DFT_R, stride=FNET_KB)
        _cols_store_rows(o_scr, sel, _cols_load_rows(x_scr, sel) + gt_ref[...] * y[j * DFT_R:(j + 1) * DFT_R, :])
    o_ref[...] = _cols_load(o_scr).reshape(DFT_R, FNET_KB, d)


def _fnet_lat(x, mods, g_all, w_out, layer, n_batch):
    t, d = x.shape
    s = t // n_batch
    assert s == DFT_P * DFT_R
    tab1, tab2 = _seq_dft_tables(s)
    ccos, csin = _chan_dft_tables(d // FNET_GROUPS)
    mi = lambda j: (lambda b, *_: (layer * MOD_ROWS + b) * N_MOD + j)
    xv = x.reshape(n_batch, DFT_P, DFT_R, d)
    blk1 = (None, DFT_P, FNET_RB, d)
    rows1 = DFT_P * FNET_RB
    zr, zi = pl.pallas_call(
        _fnet_stage1_kernel,
        grid=(n_batch, DFT_R // FNET_RB),
        in_specs=[pl.BlockSpec(blk1, lambda b, r: (b, 0, r, 0)),
                  pl.BlockSpec((None, 1, d), lambda b, r: (layer, 0, 0)),
                  _mod_spec(d, mi(0)), _mod_spec(d, mi(1)),
                  pl.BlockSpec((FNET_RB, 2 * DFT_P, DFT_P), lambda b, r: (r, 0, 0))],
        out_specs=[pl.BlockSpec(blk1, lambda b, r: (b, 0, r, 0))] * 2,
        out_shape=[jax.ShapeDtypeStruct((n_batch, DFT_P, DFT_R, d), BF16)] * 2,
        scratch_shapes=[pltpu.VMEM((d // LANES, rows1, LANES), F32)] * 3,
        compiler_params=_cparams(("parallel", "parallel")),
        name="fnet_stage1",
    )(xv, g_all, mods, mods, tab1)
    zr = zr.reshape(n_batch, DFT_P * DFT_R, d)
    zi = zi.reshape(n_batch, DFT_P * DFT_R, d)
    xo = x.reshape(n_batch, DFT_R, DFT_P, d)
    blk2 = (None, DFT_R, FNET_KB, d)
    rows = FNET_KB * DFT_R
    scale = float(1.0 / np.sqrt(float(s) * (d // FNET_GROUPS)))
    out = pl.pallas_call(
        functools.partial(_fnet_stage2_kernel, scale=scale),
        grid=(n_batch, DFT_P // FNET_KB),
        in_specs=[pl.BlockSpec((None, rows, d), lambda b, k: (b, k, 0)),
                  pl.BlockSpec((None, rows, d), lambda b, k: (b, k, 0)),
                  pl.BlockSpec((2 * DFT_R, 2 * DFT_R), lambda b, k: (0, 0)),
                  pl.BlockSpec(ccos.shape, lambda b, k: (0, 0)),
                  pl.BlockSpec(csin.shape, lambda b, k: (0, 0)),
                  pl.BlockSpec((None, d, d), lambda b, k: (layer // 2, 0, 0)),
                  pl.BlockSpec(blk2, lambda b, k: (b, 0, k, 0)),
                  _mod_spec(d, mi(2))],
        out_specs=pl.BlockSpec(blk2, lambda b, k: (b, 0, k, 0)),
        out_shape=jax.ShapeDtypeStruct((n_batch, DFT_R, DFT_P, d), F32),
        scratch_shapes=[pltpu.VMEM((rows, 2 * d), BF16)] + [pltpu.VMEM((d // LANES, rows, LANES), F32)] * 2,
        compiler_params=_cparams(("parallel", "parallel")),
        name="fnet_stage2",
    )(zr, zi, tab2, ccos, csin, w_out, xo, mods)
    return out.reshape(t, d)


def _fnet_ctx_kernel(x_ref, g_ref, sh_ref, sc_ref, gt_ref, tab_ref, cc_ref, sc2_ref, wout_ref, o_ref, *, scale):
    n = x_ref.shape[0]
    h = _norm_mod(x_ref[...], g_ref[...], sh_ref[...], sc_ref[...]).astype(BF16)
    p = jnp.dot(tab_ref[...], h, preferred_element_type=F32)
    y = _mix_tail(p[:n].astype(BF16), p[n:].astype(BF16), cc_ref, sc2_ref, wout_ref, scale)
    o_ref[...] = x_ref[...] + gt_ref[...] * y


def _fnet_ctx(x, mods, g_all, w_out, layer, n_batch):
    t, d = x.shape
    n = t // n_batch
    gd = d // FNET_GROUPS
    ccos, csin = _chan_dft_tables(gd)
    tab = _ctx_dft_table(n)
    mi = lambda j: (lambda b: (layer * MOD_ROWS + CTX_ROW) * N_MOD + j)
    scale = float(1.0 / np.sqrt(float(n) * gd))
    return pl.pallas_call(
        functools.partial(_fnet_ctx_kernel, scale=scale),
        grid=(n_batch,),
        in_specs=[pl.BlockSpec((n, d), lambda b: (b, 0)),
                  pl.BlockSpec((None, 1, d), lambda b: (layer, 0, 0)),
                  _mod_spec(d, mi(0)), _mod_spec(d, mi(1)), _mod_spec(d, mi(2)),
                  pl.BlockSpec(tab.shape, lambda b: (0, 0)),
                  pl.BlockSpec(ccos.shape, lambda b: (0, 0)),
                  pl.BlockSpec(csin.shape, lambda b: (0, 0)),
                  pl.BlockSpec((None, d, d), lambda b: (layer // 2, 0, 0))],
        out_specs=pl.BlockSpec((n, d), lambda b: (b, 0)),
        out_shape=jax.ShapeDtypeStruct((t, d), F32),
        compiler_params=_cparams(("parallel",)),
        name="fnet_ctx",
    )(x, g_all, mods, mods, mods, tab, ccos, csin, w_out)


def _rope_tables(n_seq):
    rows = n_seq // GRID_W
    row = jnp.repeat(jnp.arange(rows, dtype=F32), GRID_W)
    col = jnp.tile(jnp.arange(GRID_W, dtype=F32), rows)
    inv_freq = ROPE_THETA ** (-jnp.arange(ROT_FREQS, dtype=F32) / ROT_FREQS)
    ang = jnp.stack([row[:, None] * inv_freq, col[:, None] * inv_freq], axis=1)
    cos, sin = jnp.cos(ang), jnp.sin(ang)
    zero = jnp.zeros_like(sin)
    cos_h = jnp.stack([cos, cos], axis=2).reshape(n_seq, HEAD_DIM)
    sin_lo = jnp.stack([-sin, zero], axis=2).reshape(n_seq, HEAD_DIM)
    sin_hi = jnp.stack([zero, sin], axis=2).reshape(n_seq, HEAD_DIM)
    rep = LANES // HEAD_DIM
    return jnp.tile(cos_h, (1, rep)), jnp.tile(sin_lo, (1, rep)), jnp.tile(sin_hi, (1, rep))


def _qkv_kernel(x_ref, g_ref, sh_ref, sc_ref, w_ref, *rest, rope, q_dim, kv_dim, q_scale):
    if rope:
        cos_ref, slo_ref, shi_ref, q_ref, kt_ref, v_ref = rest
    else:
        q_ref, kt_ref, v_ref = rest
    h = _norm_mod(x_ref[...], g_ref[...], sh_ref[...], sc_ref[...]).astype(BF16)
    qkv = jnp.dot(h, w_ref[...], preferred_element_type=F32)

    def rot(xs):
        if not rope:
            return xs
        return (xs * cos_ref[...] + pltpu.roll(xs, LANES - ROT_FREQS, axis=1) * slo_ref[...]
                + pltpu.roll(xs, ROT_FREQS, axis=1) * shi_ref[...])

    for j in range(q_dim // LANES):
        q_ref[:, j * LANES:(j + 1) * LANES] = (rot(qkv[:, j * LANES:(j + 1) * LANES]) * q_scale).astype(BF16)
    ks = [rot(qkv[:, q_dim + j * LANES:q_dim + (j + 1) * LANES]) for j in range(kv_dim // LANES)]
    kt_ref[...] = jnp.concatenate(ks, axis=-1).T.astype(BF16)
    v_ref[...] = qkv[:, q_dim + kv_dim:].astype(BF16)


def _qkv(x, mods, g_all, w_qkv, layer, n_batch, ctx, rope_tabs=None, tm=512):
    t, d = x.shape
    n = t // n_batch
    tm = min(tm, n)
    tpb = n // tm
    kv_dim = N_KV_HEADS * HEAD_DIM
    q_dim = w_qkv.shape[2] - 2 * kv_dim
    mi = lambda j: _mod_index(layer, j, tpb, ctx)
    in_specs = [pl.BlockSpec((tm, d), lambda i: (i, 0)),
                pl.BlockSpec((None, 1, d), lambda i: (layer, 0, 0)),
                _mod_spec(d, mi(0)), _mod_spec(d, mi(1)),
                pl.BlockSpec((None,) + w_qkv.shape[1:], lambda i: (layer // 2, 0, 0))]
    args = [x, g_all, mods, mods, w_qkv]
    rope = rope_tabs is not None
    if rope:
        in_specs += [pl.BlockSpec((tm, LANES), lambda i: (i % tpb, 0))] * 3
        args += list(rope_tabs)
    return pl.pallas_call(
        functools.partial(_qkv_kernel, rope=rope, q_dim=q_dim, kv_dim=kv_dim, q_scale=HEAD_DIM ** -0.5),
        grid=(t // tm,),
        in_specs=in_specs,
        out_specs=[pl.BlockSpec((tm, q_dim), lambda i: (i, 0)),
                   pl.BlockSpec((None, kv_dim, tm), lambda i: (i // tpb, 0, i % tpb)),
                   pl.BlockSpec((tm, kv_dim), lambda i: (i, 0))],
        out_shape=[jax.ShapeDtypeStruct((t, q_dim), BF16),
                   jax.ShapeDtypeStruct((n_batch, kv_dim, n), BF16),
                   jax.ShapeDtypeStruct((t, kv_dim), BF16)],
        compiler_params=_cparams(("parallel",)),
        name="qkv_ctx" if ctx else "qkv_lat",
    )(*args)


def _heads_attend(q_ref, kt, v, bias, sink_ref, o_ref):
    nq = q_ref.shape[0]
    n_heads = q_ref.shape[1] // HEAD_DIM
    group = n_heads // N_KV_HEADS
    outs = [None] * n_heads
    for g in range(N_KV_HEADS):
        heads = range(g * group, (g + 1) * group)
        qg = jnp.concatenate([q_ref[:, hd * HEAD_DIM:(hd + 1) * HEAD_DIM] for hd in heads], axis=0)
        s = jnp.dot(qg, kt[g * HEAD_DIM:(g + 1) * HEAD_DIM, :], preferred_element_type=F32)
        parts = [s[:, k * LANES:(k + 1) * LANES] for k in range(s.shape[1] // LANES)]
        if bias is not None:
            parts[0] = parts[0] + bias[0]
            parts[2] = parts[2] + bias[1]
        sink = jnp.concatenate([jnp.full((nq, 1), sink_ref[hd], F32) for hd in heads], axis=0)
        m = jnp.maximum(jnp.max(functools.reduce(jnp.maximum, parts), axis=-1, keepdims=True), sink)
        ps = [jnp.exp(part - m) for part in parts]
        den = jnp.sum(functools.reduce(jnp.add, ps), axis=-1, keepdims=True) + jnp.exp(sink - m)
        pv = jnp.dot(jnp.concatenate(ps, axis=-1).astype(BF16), v, preferred_element_type=F32)
        on = pv[:, g * HEAD_DIM:(g + 1) * HEAD_DIM] / den
        for k, hd in enumerate(heads):
            outs[hd] = on[k * nq:(k + 1) * nq, :]
    o_ref[...] = jnp.concatenate(outs, axis=-1).astype(BF16)


def _attn_lat_kernel(sink_ref, q_ref, ktp_ref, ktc_ref, ktn_ref, vp_ref, vc_ref, vn_ref, ktx_ref, vx_ref,
                     blo_ref, bhi_ref, o_ref, kt_scr, v_scr):
    w = ktc_ref.shape[1]
    n_ctx = vx_ref.shape[0]
    for c, (kr, vr) in enumerate(((ktp_ref, vp_ref), (ktc_ref, vc_ref), (ktn_ref, vn_ref))):
        kt_scr[:, c * w:(c + 1) * w] = kr[...]
        v_scr[c * w:(c + 1) * w, :] = vr[...]
    kt_scr[:, 3 * w:3 * w + n_ctx] = ktx_ref[...]
    v_scr[3 * w:3 * w + n_ctx, :] = vx_ref[...]
    _heads_attend(q_ref, kt_scr[...], v_scr[...], (blo_ref[...], bhi_ref[...]), sink_ref, o_ref)


def _attn_lat(q, kt, v, ktx, vx, sinks, n_batch):
    t, qd = q.shape
    s = t // n_batch
    n_ctx = vx.shape[0] // n_batch
    w = WINDOW
    nb = s // w
    kvd = v.shape[1]
    j_all = 3 * w + n_ctx
    group = qd // HEAD_DIM // N_KV_HEADS
    qi = np.arange(w)[:, None]
    ki = np.arange(w)[None, :]
    lo = np.where(ki >= qi, 0.0, NEG).astype(np.float32)
    hi = np.where(ki <= qi, 0.0, NEG).astype(np.float32)
    off = np.full((w, w), NEG, np.float32)
    blo = jnp.asarray(np.stack([np.tile(lo, (group, 1)), np.tile(off, (group, 1))]))
    bhi = jnp.asarray(np.stack([np.tile(hi, (group, 1)), np.tile(off, (group, 1))]))
    prev = lambda b, i: (b, 0, jnp.maximum(i - 1, 0))
    nxt = lambda b, i: (b, 0, jnp.minimum(i + 1, nb - 1))
    vprev = lambda b, i: (b * nb + jnp.maximum(i - 1, 0), 0)
    vnxt = lambda b, i: (b * nb + jnp.minimum(i + 1, nb - 1), 0)
    return pl.pallas_call(
        _attn_lat_kernel,
        grid=(n_batch, nb),
        in_specs=[pl.BlockSpec(memory_space=pltpu.SMEM),
                  pl.BlockSpec((w, qd), lambda b, i: (b * nb + i, 0)),
                  pl.BlockSpec((None, kvd, w), prev),
                  pl.BlockSpec((None, kvd, w), lambda b, i: (b, 0, i)),
                  pl.BlockSpec((None, kvd, w), nxt),
                  pl.BlockSpec((w, kvd), vprev),
                  pl.BlockSpec((w, kvd), lambda b, i: (b * nb + i, 0)),
                  pl.BlockSpec((w, kvd), vnxt),
                  pl.BlockSpec((None, kvd, n_ctx), lambda b, i: (b, 0, 0)),
                  pl.BlockSpec((n_ctx, kvd), lambda b, i: (b, 0)),
                  pl.BlockSpec((None, group * w, w), lambda b, i: (jnp.where(i == 0, 1, 0), 0, 0)),
                  pl.BlockSpec((None, group * w, w), lambda b, i: (jnp.where(i == nb - 1, 1, 0), 0, 0))],
        out_specs=pl.BlockSpec((w, qd), lambda b, i: (b * nb + i, 0)),
        out_shape=jax.ShapeDtypeStruct((t, qd), BF16),
        scratch_shapes=[pltpu.VMEM((kvd, j_all), BF16), pltpu.VMEM((j_all, kvd), BF16)],
        compiler_params=_cparams(("parallel", "parallel")),
        name="attn_lat",
    )(sinks, q, kt, kt, kt, v, v, v, ktx, vx, blo, bhi)


def _attn_ctx_kernel(sink_ref, q_ref, kt_ref, v_ref, o_ref):
    _heads_attend(q_ref, kt_ref[...], v_ref[...], None, sink_ref, o_ref)


def _attn_ctx(q, kt, v, sinks, n_batch):
    t, qd = q.shape
    n = t // n_batch
    kvd = v.shape[1]
    return pl.pallas_call(
        _attn_ctx_kernel,
        grid=(n_batch,),
        in_specs=[pl.BlockSpec(memory_space=pltpu.SMEM),
                  pl.BlockSpec((n, qd), lambda b: (b, 0)),
                  pl.BlockSpec((None, kvd, n), lambda b: (b, 0, 0)),
                  pl.BlockSpec((n, kvd), lambda b: (b, 0))],
        out_specs=pl.BlockSpec((n, qd), lambda b: (b, 0)),
        out_shape=jax.ShapeDtypeStruct((t, qd), BF16),
        compiler_params=_cparams(("parallel",)),
        name="attn_ctx",
    )(sinks, q, kt, v)


ROUTE_I1, ROUTE_I2, ROUTE_R1, ROUTE_R2, ROUTE_G1, ROUTE_G2 = range(6)
ROUTE_ROWS = 8
TAB_START, TAB_COUNT, TAB_USED, TAB_SIZE = 0, N_EXPERTS, 2 * N_EXPERTS, 2 * N_EXPERTS + 8
ROW_UNROLL = 8


def _router_kernel(x_ref, g_ref, sh_ref, sc_ref, wr_ref, tri_ref, route_ref, idx_ref, cnt_ref, carry_scr):
    @pl.when(pl.program_id(0) == 0)
    def _():
        carry_scr[...] = jnp.zeros_like(carry_scr)

    h = _norm_mod(x_ref[...], g_ref[...], sh_ref[...], sc_ref[...])
    hi = h.astype(BF16)
    lo = (h - hi.astype(F32)).astype(BF16)
    a = jnp.dot(hi, wr_ref[...], preferred_element_type=F32)
    logits = (a[:, :LANES] + a[:, LANES:]) + jnp.dot(lo, wr_ref[:, :LANES], preferred_element_type=F32)
    lane = lax.broadcasted_iota(jnp.int32, logits.shape, 1)
    lane_f = lane.astype(F32)
    logits = jnp.where(lane < N_EXPERTS, logits, -jnp.inf)
    m1 = jnp.max(logits, axis=-1, keepdims=True)
    i1 = jnp.min(jnp.where(logits == m1, lane_f, float(LANES)), axis=-1, keepdims=True)
    oh1 = lane_f == i1
    rest = jnp.where(oh1, -jnp.inf, logits)
    m2 = jnp.max(rest, axis=-1, keepdims=True)
    i2 = jnp.min(jnp.where(rest == m2, lane_f, float(LANES)), axis=-1, keepdims=True)
    oh2 = lane_f == i2
    e2 = jnp.exp(m2 - m1)
    g1 = 1.0 / (1.0 + e2)
    g2 = e2 / (1.0 + e2)
    sel = jnp.where(oh1, 1.0, 0.0) + jnp.where(oh2, 1.0, 0.0)
    before = jnp.dot(tri_ref[...], sel.astype(BF16), preferred_element_type=F32) + carry_scr[0:1, :]
    r1 = jnp.sum(jnp.where(oh1, before, 0.0), axis=-1, keepdims=True)
    r2 = jnp.sum(jnp.where(oh2, before, 0.0), axis=-1, keepdims=True)
    total = carry_scr[0:1, :] + jnp.sum(sel, axis=0, keepdims=True)
    carry_scr[...] = jnp.broadcast_to(total, carry_scr.shape)
    cnt_ref[...] = jnp.broadcast_to(total, cnt_ref.shape)
    rec = jnp.zeros_like(logits)
    for ln, val in ((ROUTE_I1, i1), (ROUTE_I2, i2), (ROUTE_R1, r1), (ROUTE_R2, r2), (ROUTE_G1, g1), (ROUTE_G2, g2)):
        rec = jnp.where(lane == ln, val, rec)
    route_ref[...] = rec
    idx_ref[...] = rec.T[:ROUTE_ROWS, :].astype(jnp.int32)


def _router(x, mods, g_all, w_router, layer, tiles_per_batch, ctx, tm):
    t, d = x.shape
    mi = lambda j: _mod_index(layer, j, tiles_per_batch // tm if not ctx else 1, ctx)
    wr = jnp.zeros((d, LANES), F32).at[:, :N_EXPERTS].set(w_router)
    whi = wr.astype(BF16)
    wr = jnp.concatenate([whi, (wr - whi.astype(F32)).astype(BF16)], axis=1)
    tri = jnp.asarray(np.tril(np.ones((tm, tm), np.float32), -1), BF16)
    return pl.pallas_call(
        _router_kernel,
        grid=(t // tm,),
        in_specs=[pl.BlockSpec((tm, d), lambda i: (i, 0)),
                  pl.BlockSpec((None, 1, d), lambda i: (layer, 0, 0)),
                  _mod_spec(d, mi(3)), _mod_spec(d, mi(4)),
                  pl.BlockSpec((d, 2 * LANES), lambda i: (0, 0)),
                  pl.BlockSpec((tm, tm), lambda i: (0, 0))],
        out_specs=[pl.BlockSpec((tm, LANES), lambda i: (i, 0)),
                   pl.BlockSpec((None, ROUTE_ROWS, tm), lambda i: (i, 0, 0)),
                   pl.BlockSpec((8, LANES), lambda i: (0, 0))],
        out_shape=[jax.ShapeDtypeStruct((t, LANES), F32),
                   jax.ShapeDtypeStruct((t // tm, ROUTE_ROWS, tm), jnp.int32),
                   jax.ShapeDtypeStruct((8, LANES), F32)],
        scratch_shapes=[pltpu.VMEM((8, LANES), F32)],
        compiler_params=_cparams(("arbitrary",)),
        name="router",
    )(x, g_all, mods, mods, wr, tri)


SUB = 8


def _to_row_tiles(ref, val):
    n = val.shape[0]
    for k in range(SUB):
        ref[pl.ds(k, n, stride=SUB), :] = val[:, k * LANES:(k + 1) * LANES]


def _from_row_tiles(ref):
    n = ref.shape[0] // SUB
    return jnp.concatenate([ref[pl.ds(k, n, stride=SUB), :] for k in range(SUB)], axis=-1)


def _row_copy(src, dst, src_row, dst_row, sem):
    return pltpu.make_async_copy(src.at[pl.ds(pl.multiple_of(src_row * SUB, SUB), SUB)],
                                 dst.at[pl.ds(pl.multiple_of(dst_row * SUB, SUB), SUB)], sem)


def _for_rows(tm, body):
    def blk(i, _):
        base = pl.multiple_of(i * ROW_UNROLL, ROW_UNROLL)
        for j in range(ROW_UNROLL):
            body(base + j, j)
        return 0

    lax.fori_loop(0, tm // ROW_UNROLL, blk, 0)


ZERO_SEM = 4


def _dispatch_kernel(tab_ref, slot_ref, x_ref, g_ref, sh_ref, sc_ref, xs_ref, h_scr, zero_scr, sems, *, group):
    tm = x_ref.shape[0]
    n_groups = xs_ref.shape[0] // (group * SUB)
    step = pl.program_id(0)
    n_steps = pl.num_programs(0)
    cur = lax.rem(step, 2)

    def wait_buffer(b):
        for k in range(2):
            pltpu.make_async_copy(h_scr.at[b], xs_ref.at[pl.ds(0, tm * SUB)], sems.at[2 * b + k]).wait()

    @pl.when(step >= 2)
    def _():
        wait_buffer(cur)

    _to_row_tiles(h_scr.at[cur], _norm_mod(x_ref[...], g_ref[...], sh_ref[...], sc_ref[...]))

    def issue(r, j):
        for k in range(2):
            _row_copy(h_scr.at[cur], xs_ref, r, slot_ref[0, k, r], sems.at[2 * cur + k]).start(priority=j % 2)

    _for_rows(tm, issue)

    @pl.when(jnp.logical_and(step == n_steps - 1, step >= 1))
    def _():
        wait_buffer(1 - cur)

    @pl.when(step == n_steps - 1)
    def _():
        wait_buffer(cur)
        zero_scr[...] = jnp.zeros_like(zero_scr)
        for e in range(N_EXPERTS):
            n = tab_ref[TAB_COUNT + e]
            n_pad = lax.rem(group - lax.rem(n, group), group)
            first = tab_ref[TAB_START + e] + n

            def fill(k, _, first=first):
                _row_copy(zero_scr, xs_ref, 0, first + k, sems.at[ZERO_SEM]).start()
                return 0

            def fill_done(k, _):
                _row_copy(zero_scr, xs_ref, 0, 0, sems.at[ZERO_SEM]).wait()
                return 0

            lax.fori_loop(0, n_pad, fill, 0)
            lax.fori_loop(0, n_pad, fill_done, 0)

        def clear(j, _):
            row = pl.multiple_of(j * (group * SUB), group * SUB)
            cp = pltpu.make_async_copy(zero_scr, xs_ref.at[pl.ds(row, group * SUB)], sems.at[ZERO_SEM])
            cp.start()
            cp.wait()
            return 0

        lax.fori_loop(tab_ref[TAB_USED], n_groups, clear, 0)


def _dispatch(x, tab, slots, mods, g_all, layer, tiles_per_batch, ctx, n_groups, group, tm):
    t, d = x.shape
    mi = lambda j: _mod_index(layer, j, tiles_per_batch // tm if not ctx else 1, ctx)
    return pl.pallas_call(
        functools.partial(_dispatch_kernel, group=group),
        grid=(t // tm,),
        in_specs=[pl.BlockSpec(memory_space=pltpu.SMEM),
                  pl.BlockSpec((1, 2, tm), lambda i: (i, 0, 0), memory_space=pltpu.SMEM),
                  pl.BlockSpec((tm, d), lambda i: (i, 0)),
                  pl.BlockSpec((None, 1, d), lambda i: (layer, 0, 0)),
                  _mod_spec(d, mi(3)), _mod_spec(d, mi(4))],
        out_specs=pl.BlockSpec(memory_space=pl.ANY),
        out_shape=jax.ShapeDtypeStruct((n_groups * group * SUB, LANES), F32),
        scratch_shapes=[pltpu.VMEM((2, tm * SUB, LANES), F32), pltpu.VMEM((group * SUB, LANES), F32),
                        pltpu.SemaphoreType.DMA((ZERO_SEM + 1,))],
        compiler_params=_cparams(("arbitrary",)),
        name="moe_dispatch",
    )(tab, slots, x, g_all, mods, mods)


def _moe_ffn_kernel(be_ref, bv_ref, xs_ref, wg_ref, wu_ref, wd_ref, ys_ref, h_scr, acc_scr):
    del be_ref
    i = pl.program_id(0)
    f = pl.program_id(1)
    last = f == pl.num_programs(1) - 1
    valid = bv_ref[i] > 0

    @pl.when(jnp.logical_and(valid, f == 0))
    def _():
        h_scr[...] = _from_row_tiles(xs_ref).astype(BF16)
        acc_scr[...] = jnp.zeros_like(acc_scr)

    @pl.when(valid)
    def _():
        h = h_scr[...]
        a = jnp.dot(h, wg_ref[...], preferred_element_type=F32)
        u = jnp.dot(h, wu_ref[...], preferred_element_type=F32)
        t = (a * _sigmoid(a)) * u
        acc_scr[...] += jnp.dot(t.astype(BF16), wd_ref[...], preferred_element_type=F32)

    @pl.when(jnp.logical_and(valid, last))
    def _():
        _to_row_tiles(ys_ref, acc_scr[...])

    @pl.when(jnp.logical_and(jnp.logical_not(valid), last))
    def _():
        ys_ref[...] = jnp.zeros_like(ys_ref)


def _moe_ffn(xs, block_expert, block_valid, wg, wu, wd, layer, tm, tf=MOE_TF):
    d, ff = wg.shape[2:]
    assert d == SUB * LANES and xs.shape[1] == LANES
    cap = xs.shape[0] // SUB
    tf = min(tf, ff)
    nf = ff // tf
    sj = layer // 2
    fsel = lambda i, f, bv: jnp.where(bv[i] > 0, f, nf - 1)
    grid_spec = pltpu.PrefetchScalarGridSpec(
        num_scalar_prefetch=2,
        grid=(cap // tm, nf),
        in_specs=[pl.BlockSpec((tm * SUB, LANES), lambda i, f, be, bv: (i, 0)),
                  pl.BlockSpec((None, None, d, tf), lambda i, f, be, bv: (sj, be[i], 0, fsel(i, f, bv))),
                  pl.BlockSpec((None, None, d, tf), lambda i, f, be, bv: (sj, be[i], 0, fsel(i, f, bv))),
                  pl.BlockSpec((None, None, tf, d), lambda i, f, be, bv: (sj, be[i], fsel(i, f, bv), 0))],
        out_specs=pl.BlockSpec((tm * SUB, LANES), lambda i, f, be, bv: (i, 0)),
        scratch_shapes=[pltpu.VMEM((tm, d), BF16), pltpu.VMEM((tm, d), F32)],
    )
    return pl.pallas_call(
        _moe_ffn_kernel,
        grid_spec=grid_spec,
        out_shape=jax.ShapeDtypeStruct(xs.shape, F32),
        compiler_params=_cparams(("parallel", "arbitrary")),
        name="moe_ffn",
    )(block_expert, block_valid, xs, wg, wu, wd)


def _combine_kernel(slot_ref, next_slot_ref, route_ref, x_ref, gt_ref, fg_ref, ys_ref, o_ref, buf, sems, *, final_norm):
    tm = x_ref.shape[0]
    step = pl.program_id(0)
    cur = lax.rem(step, 2)

    def gather(idx_ref, b):
        def issue(r, j):
            for k in range(2):
                _row_copy(ys_ref, buf.at[b, k], idx_ref[0, k, r], r, sems.at[2 * b + k]).start(priority=j % 2)

        _for_rows(tm, issue)

    @pl.when(step == 0)
    def _():
        gather(slot_ref, cur)

    @pl.when(step + 1 < pl.num_programs(0))
    def _():
        gather(next_slot_ref, 1 - cur)

    for k in range(2):
        pltpu.make_async_copy(ys_ref.at[pl.ds(0, tm * SUB)], buf.at[cur, k], sems.at[2 * cur + k]).wait()
    rec = route_ref[...]
    g1 = rec[:, ROUTE_G1:ROUTE_G1 + 1]
    g2 = rec[:, ROUTE_G2:ROUTE_G2 + 1]
    y = _from_row_tiles(buf.at[cur, 0]) * g1 + _from_row_tiles(buf.at[cur, 1]) * g2
    out = x_ref[...] + gt_ref[...] * y
    if final_norm:
        out = out * lax.rsqrt(jnp.mean(out * out, axis=-1, keepdims=True) + EPS) * fg_ref[...]
    o_ref[...] = out


def _combine(ys, slots, route, x, mods, final_g, layer, tiles_per_batch, ctx, final_norm, tm):
    t, d = x.shape
    mi = _mod_index(layer, 5, tiles_per_batch // tm if not ctx else 1, ctx)
    n_tiles = t // tm
    return pl.pallas_call(
        functools.partial(_combine_kernel, final_norm=final_norm),
        grid=(n_tiles,),
        in_specs=[pl.BlockSpec((1, 2, tm), lambda i: (i, 0, 0), memory_space=pltpu.SMEM),
                  pl.BlockSpec((1, 2, tm), lambda i: (jnp.minimum(i + 1, n_tiles - 1), 0, 0), memory_space=pltpu.SMEM),
                  pl.BlockSpec((tm, LANES), lambda i: (i, 0)),
                  pl.BlockSpec((tm, d), lambda i: (i, 0)),
                  _mod_spec(d, mi),
                  pl.BlockSpec((1, d), lambda i: (0, 0)),
                  pl.BlockSpec(memory_space=pl.ANY)],
        out_specs=pl.BlockSpec((tm, d), lambda i: (i, 0)),
        out_shape=jax.ShapeDtypeStruct((t, d), F32),
        scratch_shapes=[pltpu.VMEM((2, 2, tm * SUB, LANES), F32), pltpu.SemaphoreType.DMA((4,))],
        compiler_params=_cparams(("arbitrary",)),
        name="moe_combine",
    )(slots, slots, route, x, mods, final_g.reshape(1, d), ys)


def _moe(x, mods, g_all, w_router, wg, wu, wd, final_g, layer, tiles_per_batch, ctx, final_norm):
    t, d = x.shape
    group = min(MOE_TM, t)
    row_tm = min(ROW_TM, t)
    route, idx, counts = _router(x, mods, g_all, w_router, layer, tiles_per_batch, ctx, row_tm)
    counts = counts[0, :N_EXPERTS].astype(jnp.int32)
    groups = (counts + group - 1) // group
    ends = jnp.cumsum(groups)
    starts = (ends - groups) * group
    n_groups = (2 * t + group - 1) // group + N_EXPERTS
    tab = jnp.zeros((TAB_SIZE,), jnp.int32)
    tab = tab.at[TAB_START:TAB_START + N_EXPERTS].set(starts).at[TAB_COUNT:TAB_COUNT + N_EXPERTS].set(counts)
    tab = tab.at[TAB_USED].set(ends[-1])
    gi = jnp.arange(n_groups, dtype=jnp.int32)
    block_expert = jnp.minimum(jnp.searchsorted(ends, gi, side='right'), N_EXPERTS - 1).astype(jnp.int32)
    block_valid = (gi < ends[-1]).astype(jnp.int32)
    def slot_rows(e, r):
        base = functools.reduce(jnp.add, [jnp.where(e == k, starts[k], 0) for k in range(N_EXPERTS)])
        return base + r
    slots = jnp.stack([slot_rows(idx[:, ROUTE_I1, :], idx[:, ROUTE_R1, :]),
                       slot_rows(idx[:, ROUTE_I2, :], idx[:, ROUTE_R2, :])], axis=1)
    xs = _dispatch(x, tab, slots, mods, g_all, layer, tiles_per_batch, ctx, n_groups, group, row_tm)
    ys = _moe_ffn(xs, block_expert, block_valid, wg, wu, wd, layer, group)
    return _combine(ys, slots, route, x, mods, final_g, layer, tiles_per_batch, ctx, final_norm, row_tm)


def kernel(x, c, ctx, c_ctx, ada_w, ada_b, norm_mix_g, norm_ffn_g, fnet_w_out, attn_w_qkv, attn_w_o, attn_sinks,
           ffn_w_gate, ffn_w_up, ffn_w_down, moe_w_router, moe_w_gate, moe_w_up, moe_w_down, final_norm_g):
    n_batch, s, d = x.shape
    n_ctx = ctx.shape[1]
    depth = ada_w.shape[0]
    mods = _ada(c, c_ctx, ada_w, ada_b)
    g_mix = norm_mix_g.reshape(depth, 1, d)
    g_ffn = norm_ffn_g.reshape(depth, 1, d)
    rope_tabs = _rope_tables(s)
    x_lat = x.reshape(n_batch * s, d)
    x_ctx = ctx.reshape(n_batch * n_ctx, d)
    bf = lambda a: a.astype(BF16)
    w_out, w_qkv, w_o = bf(fnet_w_out), bf(attn_w_qkv), bf(attn_w_o)
    wg, wu, wd = bf(ffn_w_gate), bf(ffn_w_up), bf(ffn_w_down)
    eg, eu, ed = bf(moe_w_gate), bf(moe_w_up), bf(moe_w_down)
    for i in range(depth):
        j = i // 2
        last = i == depth - 1
        if i % 2 == 0:
            x_lat = _fnet_lat(x_lat, mods, g_mix, w_out, i, n_batch)
            if not last:
                x_ctx = _fnet_ctx(x_ctx, mods, g_mix, w_out, i, n_batch)
            x_lat = _ffn_dense(x_lat, mods, g_ffn, wg, wu, wd, i, s, False)
            if not last:
                x_ctx = _ffn_dense(x_ctx, mods, g_ffn, wg, wu, wd, i, n_ctx, True)
        else:
            q, kt, v = _qkv(x_lat, mods, g_mix, w_qkv, i, n_batch, False, rope_tabs)
            qx, ktx, vx = _qkv(x_ctx, mods, g_mix, w_qkv, i, n_batch, True)
            o = _attn_lat(q, kt, v, ktx, vx, attn_sinks[j], n_batch)
            x_lat = _proj_res(o, w_o, x_lat, mods, i, 2, s, False)
            if not last:
                ox = _attn_ctx(qx, ktx, vx, attn_sinks[j], n_batch)
                x_ctx = _proj_res(ox, w_o, x_ctx, mods, i, 2, n_ctx, True)
            x_lat = _moe(x_lat, mods, g_ffn, moe_w_router[j], eg, eu, ed, final_norm_g, i, s, False, last)
            if not last:
                x_ctx = _moe(x_ctx, mods, g_ffn, moe_w_router[j], eg, eu, ed, final_norm_g, i, n_ctx, True, False)
    return x_lat.reshape(n_batch, s, d)
```

```python
import functools

import numpy as np
import jax
import jax.numpy as jnp
from jax import lax
from jax.experimental import pallas as pl
from jax.experimental.pallas import tpu as pltpu

F32 = jnp.float32
BF16 = jnp.bfloat16

GRID_W = 64
N_MOD = 6
EPS = 1e-6
FNET_GROUPS = 4
HEAD_DIM = 64
N_KV_HEADS = 4
WINDOW = 128
ROPE_THETA = 10000.0
ROT_FREQS = HEAD_DIM // 4
N_EXPERTS = 8
MOD_ROWS = 8
CTX_ROW = MOD_ROWS - 1
LANES = 128
NEG = -1e30
VMEM_LIMIT = 56 * 1024 * 1024

FFN_TM = 512
FFN_TF = 896
MOE_TM = 512
MOE_TF = 1792
ROW_TM = 512
PROJ_TM = 1024
DFT_P = 64
DFT_R = 128
FNET_RB = 16
FNET_KB = 8


def _cparams(sem):
    return pltpu.CompilerParams(dimension_semantics=sem, vmem_limit_bytes=VMEM_LIMIT)


def _sigmoid(a):
    return 1.0 / (1.0 + jnp.exp(-a))


def _norm_mod(x, g, shift, scale):
    xn = x * lax.rsqrt(jnp.mean(x * x, axis=-1, keepdims=True) + EPS)
    return (xn * g) * (1.0 + scale) + shift


def _mod_spec(d, idx_fn):
    return pl.BlockSpec((None, 1, d), lambda *ids: (idx_fn(*ids), 0, 0))


def _mod_index(layer, j, tiles_per_batch, ctx):
    def fn(i, *_):
        b = CTX_ROW if ctx else i // tiles_per_batch
        return (layer * MOD_ROWS + b) * N_MOD + j
    return fn


def _ada_kernel(c_ref, w_ref, b_ref, o_ref):
    cc = c_ref[...]
    s = cc * _sigmoid(cc)
    o_ref[...] = jnp.dot(s, w_ref[...], preferred_element_type=F32,
                         precision=lax.Precision.HIGHEST) + b_ref[...]


def _ada(c, c_ctx, ada_w, ada_b):
    depth, d, _ = ada_w.shape
    nb = c.shape[0]
    assert nb < MOD_ROWS
    cc = jnp.concatenate([c, jnp.zeros((CTX_ROW - nb, d), F32), c_ctx[None]], axis=0)
    out = pl.pallas_call(
        _ada_kernel,
        grid=(depth, N_MOD),
        in_specs=[pl.BlockSpec((MOD_ROWS, d), lambda l, j: (0, 0)),
                  pl.BlockSpec((None, d, d), lambda l, j: (l, 0, j)),
                  pl.BlockSpec((None, 1, d), lambda l, j: (l * N_MOD + j, 0, 0))],
        out_specs=pl.BlockSpec((None, MOD_ROWS, d), lambda l, j: (l, 0, j)),
        out_shape=jax.ShapeDtypeStruct((depth, MOD_ROWS, N_MOD * d), F32),
        compiler_params=_cparams(("parallel", "parallel")),
        name="ada",
    )(cc, ada_w, ada_b.reshape(depth * N_MOD, 1, d))
    return out.reshape(depth * MOD_ROWS * N_MOD, 1, d)


def _ffn_kernel(x_ref, g_ref, sh_ref, sc_ref, gt_ref, wg_ref, wu_ref, wd_ref, o_ref, *, tf):
    h = _norm_mod(x_ref[...], g_ref[...], sh_ref[...], sc_ref[...]).astype(BF16)
    acc = None
    for c in range(wg_ref.shape[1] // tf):
        cols = slice(c * tf, (c + 1) * tf)
        a = jnp.dot(h, wg_ref[:, cols], preferred_element_type=F32)
        u = jnp.dot(h, wu_ref[:, cols], preferred_element_type=F32)
        t = (a * _sigmoid(a)) * u
        y = jnp.dot(t.astype(BF16), wd_ref[cols, :], preferred_element_type=F32)
        acc = y if acc is None else acc + y
    o_ref[...] = x_ref[...] + gt_ref[...] * acc


def _ffn_dense(x, mods, g_all, wg, wu, wd, layer, tiles_per_batch, ctx, tm=FFN_TM, tf=FFN_TF):
    t, d = x.shape
    ff = wg.shape[2]
    tm = min(tm, t)
    tf = min(tf, ff)
    assert ff % tf == 0
    sj = layer // 2
    mi = functools.partial(_mod_index, layer, tiles_per_batch=tiles_per_batch // tm if not ctx else 1, ctx=ctx)
    resident = dict(pipeline_mode=pl.Buffered(1))
    return pl.pallas_call(
        functools.partial(_ffn_kernel, tf=tf),
        grid=(t // tm,),
        in_specs=[pl.BlockSpec((tm, d), lambda i: (i, 0)),
                  pl.BlockSpec((None, 1, d), lambda i: (layer, 0, 0)),
                  _mod_spec(d, mi(3)), _mod_spec(d, mi(4)), _mod_spec(d, mi(5)),
                  pl.BlockSpec((None, d, ff), lambda i: (sj, 0, 0), **resident),
                  pl.BlockSpec((None, d, ff), lambda i: (sj, 0, 0), **resident),
                  pl.BlockSpec((None, ff, d), lambda i: (sj, 0, 0), **resident)],
        out_specs=pl.BlockSpec((tm, d), lambda i: (i, 0)),
        out_shape=jax.ShapeDtypeStruct((t, d), F32),
        compiler_params=_cparams(("parallel",)),
        name="ffn_dense",
    )(x, g_all, mods, mods, mods, wg, wu, wd)


def _proj_res_kernel(a_ref, w_ref, x_ref, gt_ref, o_ref):
    y = jnp.dot(a_ref[...], w_ref[...], preferred_element_type=F32)
    o_ref[...] = x_ref[...] + gt_ref[...] * y


def _proj_res(a, w, x, mods, layer, j, tiles_per_batch, ctx, tm=PROJ_TM):
    t, d = x.shape
    k = a.shape[1]
    tm = min(tm, t)
    mi = _mod_index(layer, j, tiles_per_batch // tm if not ctx else 1, ctx)
    return pl.pallas_call(
        _proj_res_kernel,
        grid=(t // tm,),
        in_specs=[pl.BlockSpec((tm, k), lambda i: (i, 0)),
                  pl.BlockSpec((None, k, d), lambda i: (layer // 2, 0, 0)),
                  pl.BlockSpec((tm, d), lambda i: (i, 0)),
                  _mod_spec(d, mi)],
        out_specs=pl.BlockSpec((tm, d), lambda i: (i, 0)),
        out_shape=jax.ShapeDtypeStruct((t, d), F32),
        compiler_params=_cparams(("parallel",)),
        name="proj_res",
    )(a, w, x, mods)


def _dft_angles(n):
    a = np.arange(n)
    return 2.0 * np.pi * ((a[:, None] * a[None, :]) % n) / n


def _seq_dft_tables(n):
    kp = np.arange(DFT_P)[None, :, None]
    p = np.arange(DFT_P)[None, None, :]
    r = np.arange(DFT_R)[:, None, None]
    th = 2.0 * np.pi * ((kp * (DFT_R * p + r)) % n) / n
    tab1 = np.concatenate([np.cos(th), -np.sin(th)], axis=1)
    th2 = _dft_angles(DFT_R)
    c2, s2 = np.cos(th2), np.sin(th2)
    tab2 = np.block([[c2, s2], [-s2, c2]])
    return jnp.asarray(tab1, BF16), jnp.asarray(tab2, BF16)


def _chan_dft_tables(group_dim):
    th = _dft_angles(group_dim)
    return jnp.asarray(np.cos(th), BF16), jnp.asarray(np.sin(th), BF16)


def _ctx_dft_table(n_ctx):
    th = _dft_angles(n_ctx)
    return jnp.asarray(np.concatenate([np.cos(th), -np.sin(th)], axis=0), BF16)


def _cols_store(scr, val):
    for c in range(scr.shape[0]):
        scr[c] = val[:, c * LANES:(c + 1) * LANES]


def _cols_load(scr):
    return jnp.concatenate([scr[c] for c in range(scr.shape[0])], axis=-1)


def _cols_store_rows(scr, sel, val):
    for c in range(scr.shape[0]):
        scr[c, sel, :] = val[:, c * LANES:(c + 1) * LANES]


def _cols_load_rows(scr, sel):
    return jnp.concatenate([scr[c, sel, :] for c in range(scr.shape[0])], axis=-1)


def _fnet_stage1_kernel(x_ref, g_ref, sh_ref, sc_ref, tab_ref, zr_ref, zi_ref, h_scr, zr_scr, zi_scr):
    d = g_ref.shape[-1]
    rows = DFT_P * FNET_RB
    _cols_store(h_scr, _norm_mod(x_ref[...].reshape(rows, d), g_ref[...], sh_ref[...], sc_ref[...]))
    for j in range(FNET_RB):
        sel = pl.ds(j, DFT_P, stride=FNET_RB)
        z = jnp.dot(tab_ref[j], _cols_load_rows(h_scr, sel).astype(BF16), preferred_element_type=F32)
        _cols_store_rows(zr_scr, sel, z[:DFT_P])
        _cols_store_rows(zi_scr, sel, z[DFT_P:])
    zr_ref[...] = _cols_load(zr_scr).astype(BF16).reshape(DFT_P, FNET_RB, d)
    zi_ref[...] = _cols_load(zi_scr).astype(BF16).reshape(DFT_P, FNET_RB, d)


def _mix_tail(pr, pi, cc_ref, sc_ref, wout_ref, scale):
    gd = cc_ref.shape[0]
    ys = []
    for g in range(pr.shape[1] // gd):
        ys.append(jnp.dot(pr[:, g * gd:(g + 1) * gd], cc_ref[...], preferred_element_type=F32)
                  + jnp.dot(pi[:, g * gd:(g + 1) * gd], sc_ref[...], preferred_element_type=F32))
    mixed = (jnp.concatenate(ys, axis=-1) * scale).astype(BF16)
    return jnp.dot(mixed, wout_ref[...], preferred_element_type=F32)


def _fnet_stage2_kernel(zr_ref, zi_ref, tab2_ref, cc_ref, sc_ref, wout_ref, x_ref, gt_ref, o_ref,
                        p_scr, x_scr, o_scr, *, scale):
    d = gt_ref.shape[-1]
    rows = DFT_R * FNET_KB
    for j in range(FNET_KB):
        z = jnp.concatenate([zr_ref[j * DFT_R:(j + 1) * DFT_R, :], zi_ref[j * DFT_R:(j + 1) * DFT_R, :]], axis=0)
        p = jnp.dot(tab2_ref[...], z, preferred_element_type=F32)
        p_scr[j * DFT_R:(j + 1) * DFT_R, :d] = p[:DFT_R].astype(BF16)
        p_scr[j * DFT_R:(j + 1) * DFT_R, d:] = p[DFT_R:].astype(BF16)
    y = _mix_tail(p_scr[:, :d], p_scr[:, d:], cc_ref, sc_ref, wout_ref, scale)
    _cols_store(x_scr, x_ref[...].reshape(rows, d))
    for j in range(FNET_KB):
        sel = pl.ds(j, DFT_R, stride=FNET_KB)
        _cols_store_rows(o_scr, sel, _cols_load_rows(x_scr, sel) + gt_ref[...] * y[j * DFT_R:(j + 1) * DFT_R, :])
    o_ref[...] = _cols_load(o_scr).reshape(DFT_R, FNET_KB, d)


def _fnet_lat(x, mods, g_all, w_out, layer, n_batch):
    t, d = x.shape
    s = t // n_batch
    assert s == DFT_P * DFT_R
    tab1, tab2 = _seq_dft_tables(s)
    ccos, csin = _chan_dft_tables(d // FNET_GROUPS)
    mi = lambda j: (lambda b, *_: (layer * MOD_ROWS + b) * N_MOD + j)
    xv = x.reshape(n_batch, DFT_P, DFT_R, d)
    blk1 = (None, DFT_P, FNET_RB, d)
    rows1 = DFT_P * FNET_RB
    zr, zi = pl.pallas_call(
        _fnet_stage1_kernel,
        grid=(n_batch, DFT_R // FNET_RB),
        in_specs=[pl.BlockSpec(blk1, lambda b, r: (b, 0, r, 0)),
                  pl.BlockSpec((None, 1, d), lambda b, r: (layer, 0, 0)),
                  _mod_spec(d, mi(0)), _mod_spec(d, mi(1)),
                  pl.BlockSpec((FNET_RB, 2 * DFT_P, DFT_P), lambda b, r: (r, 0, 0))],
        out_specs=[pl.BlockSpec(blk1, lambda b, r: (b, 0, r, 0))] * 2,
        out_shape=[jax.ShapeDtypeStruct((n_batch, DFT_P, DFT_R, d), BF16)] * 2,
        scratch_shapes=[pltpu.VMEM((d // LANES, rows1, LANES), F32)] * 3,
        compiler_params=_cparams(("parallel", "parallel")),
        name="fnet_stage1",
    )(xv, g_all, mods, mods, tab1)
    zr = zr.reshape(n_batch, DFT_P * DFT_R, d)
    zi = zi.reshape(n_batch, DFT_P * DFT_R, d)
    xo = x.reshape(n_batch, DFT_R, DFT_P, d)
    blk2 = (None, DFT_R, FNET_KB, d)
    rows = FNET_KB * DFT_R
    scale = float(1.0 / np.sqrt(float(s) * (d // FNET_GROUPS)))
    out = pl.pallas_call(
        functools.partial(_fnet_stage2_kernel, scale=scale),
        grid=(n_batch, DFT_P // FNET_KB),
        in_specs=[pl.BlockSpec((None, rows, d), lambda b, k: (b, k, 0)),
                  pl.BlockSpec((None, rows, d), lambda b, k: (b, k, 0)),
                  pl.BlockSpec((2 * DFT_R, 2 * DFT_R), lambda b, k: (0, 0)),
                  pl.BlockSpec(ccos.shape, lambda b, k: (0, 0)),
                  pl.BlockSpec(csin.shape, lambda b, k: (0, 0)),
                  pl.BlockSpec((None, d, d), lambda b, k: (layer // 2, 0, 0)),
                  pl.BlockSpec(blk2, lambda b, k: (b, 0, k, 0)),
                  _mod_spec(d, mi(2))],
        out_specs=pl.BlockSpec(blk2, lambda b, k: (b, 0, k, 0)),
        out_shape=jax.ShapeDtypeStruct((n_batch, DFT_R, DFT_P, d), F32),
        scratch_shapes=[pltpu.VMEM((rows, 2 * d), BF16)] + [pltpu.VMEM((d // LANES, rows, LANES), F32)] * 2,
        compiler_params=_cparams(("parallel", "parallel")),
        name="fnet_stage2",
    )(zr, zi, tab2, ccos, csin, w_out, xo, mods)
    return out.reshape(t, d)


def _fnet_ctx_kernel(x_ref, g_ref, sh_ref, sc_ref, gt_ref, tab_ref, cc_ref, sc2_ref, wout_ref, o_ref, *, scale):
    n = x_ref.shape[0]
    h = _norm_mod(x_ref[...], g_ref[...], sh_ref[...], sc_ref[...]).astype(BF16)
    p = jnp.dot(tab_ref[...], h, preferred_element_type=F32)
    y = _mix_tail(p[:n].astype(BF16), p[n:].astype(BF16), cc_ref, sc2_ref, wout_ref, scale)
    o_ref[...] = x_ref[...] + gt_ref[...] * y


def _fnet_ctx(x, mods, g_all, w_out, layer, n_batch):
    t, d = x.shape
    n = t // n_batch
    gd = d // FNET_GROUPS
    ccos, csin = _chan_dft_tables(gd)
    tab = _ctx_dft_table(n)
    mi = lambda j: (lambda b: (layer * MOD_ROWS + CTX_ROW) * N_MOD + j)
    scale = float(1.0 / np.sqrt(float(n) * gd))
    return pl.pallas_call(
        functools.partial(_fnet_ctx_kernel, scale=scale),
        grid=(n_batch,),
        in_specs=[pl.BlockSpec((n, d), lambda b: (b, 0)),
                  pl.BlockSpec((None, 1, d), lambda b: (layer, 0, 0)),
                  _mod_spec(d, mi(0)), _mod_spec(d, mi(1)), _mod_spec(d, mi(2)),
                  pl.BlockSpec(tab.shape, lambda b: (0, 0)),
                  pl.BlockSpec(ccos.shape, lambda b: (0, 0)),
                  pl.BlockSpec(csin.shape, lambda b: (0, 0)),
                  pl.BlockSpec((None, d, d), lambda b: (layer // 2, 0, 0))],
        out_specs=pl.BlockSpec((n, d), lambda b: (b, 0)),
        out_shape=jax.ShapeDtypeStruct((t, d), F32),
        compiler_params=_cparams(("parallel",)),
        name="fnet_ctx",
    )(x, g_all, mods, mods, mods, tab, ccos, csin, w_out)


def _rope_tables(n_seq):
    rows = n_seq // GRID_W
    row = jnp.repeat(jnp.arange(rows, dtype=F32), GRID_W)
    col = jnp.tile(jnp.arange(GRID_W, dtype=F32), rows)
    inv_freq = ROPE_THETA ** (-jnp.arange(ROT_FREQS, dtype=F32) / ROT_FREQS)
    ang = jnp.stack([row[:, None] * inv_freq, col[:, None] * inv_freq], axis=1)
    cos, sin = jnp.cos(ang), jnp.sin(ang)
    zero = jnp.zeros_like(sin)
    cos_h = jnp.stack([cos, cos], axis=2).reshape(n_seq, HEAD_DIM)
    sin_lo = jnp.stack([-sin, zero], axis=2).reshape(n_seq, HEAD_DIM)
    sin_hi = jnp.stack([zero, sin], axis=2).reshape(n_seq, HEAD_DIM)
    rep = LANES // HEAD_DIM
    return jnp.tile(cos_h, (1, rep)), jnp.tile(sin_lo, (1, rep)), jnp.tile(sin_hi, (1, rep))


def _qkv_kernel(x_ref, g_ref, sh_ref, sc_ref, w_ref, *rest, rope, q_dim, kv_dim, q_scale):
    if rope:
        cos_ref, slo_ref, shi_ref, q_ref, kt_ref, v_ref = rest
    else:
        q_ref, kt_ref, v_ref = rest
    h = _norm_mod(x_ref[...], g_ref[...], sh_ref[...], sc_ref[...]).astype(BF16)
    qkv = jnp.dot(h, w_ref[...], preferred_element_type=F32)

    def rot(xs):
        if not rope:
            return xs
        return (xs * cos_ref[...] + pltpu.roll(xs, LANES - ROT_FREQS, axis=1) * slo_ref[...]
                + pltpu.roll(xs, ROT_FREQS, axis=1) * shi_ref[...])

    for j in range(q_dim // LANES):
        q_ref[:, j * LANES:(j + 1) * LANES] = (rot(qkv[:, j * LANES:(j + 1) * LANES]) * q_scale).astype(BF16)
    ks = [rot(qkv[:, q_dim + j * LANES:q_dim + (j + 1) * LANES]) for j in range(kv_dim // LANES)]
    kt_ref[...] = jnp.concatenate(ks, axis=-1).T.astype(BF16)
    v_ref[...] = qkv[:, q_dim + kv_dim:].astype(BF16)


def _qkv(x, mods, g_all, w_qkv, layer, n_batch, ctx, rope_tabs=None, tm=512):
    t, d = x.shape
    n = t // n_batch
    tm = min(tm, n)
    tpb = n // tm
    kv_dim = N_KV_HEADS * HEAD_DIM
    q_dim = w_qkv.shape[2] - 2 * kv_dim
    mi = lambda j: _mod_index(layer, j, tpb, ctx)
    in_specs = [pl.BlockSpec((tm, d), lambda i: (i, 0)),
                pl.BlockSpec((None, 1, d), lambda i: (layer, 0, 0)),
                _mod_spec(d, mi(0)), _mod_spec(d, mi(1)),
                pl.BlockSpec((None,) + w_qkv.shape[1:], lambda i: (layer // 2, 0, 0))]
    args = [x, g_all, mods, mods, w_qkv]
    rope = rope_tabs is not None
    if rope:
        in_specs += [pl.BlockSpec((tm, LANES), lambda i: (i % tpb, 0))] * 3
        args += list(rope_tabs)
    return pl.pallas_call(
        functools.partial(_qkv_kernel, rope=rope, q_dim=q_dim, kv_dim=kv_dim, q_scale=HEAD_DIM ** -0.5),
        grid=(t // tm,),
        in_specs=in_specs,
        out_specs=[pl.BlockSpec((tm, q_dim), lambda i: (i, 0)),
                   pl.BlockSpec((None, kv_dim, tm), lambda i: (i // tpb, 0, i % tpb)),
                   pl.BlockSpec((tm, kv_dim), lambda i: (i, 0))],
        out_shape=[jax.ShapeDtypeStruct((t, q_dim), BF16),
                   jax.ShapeDtypeStruct((n_batch, kv_dim, n), BF16),
                   jax.ShapeDtypeStruct((t, kv_dim), BF16)],
        compiler_params=_cparams(("parallel",)),
        name="qkv_ctx" if ctx else "qkv_lat",
    )(*args)


ATTN_RB = 64


def _heads_attend(q_ref, kt, v, bias, sink_ref, o_ref):
    nq = q_ref.shape[0]
    n_heads = q_ref.shape[1] // HEAD_DIM
    group = n_heads // N_KV_HEADS
    outs = [None] * n_heads
    for g in range(N_KV_HEADS):
        heads = range(g * group, (g + 1) * group)
        qg = jnp.concatenate([q_ref[:, hd * HEAD_DIM:(hd + 1) * HEAD_DIM] for hd in heads], axis=0)
        s = jnp.dot(qg, kt[g * HEAD_DIM:(g + 1) * HEAD_DIM, :], preferred_element_type=F32)
        p_rows, den_rows = [], []
        for r0 in range(0, s.shape[0], ATTN_RB):
            rows = slice(r0, r0 + ATTN_RB)
            parts = [s[rows, k * LANES:(k + 1) * LANES] for k in range(s.shape[1] // LANES)]
            if bias is not None:
                parts[0] = parts[0] + bias[0][rows]
                parts[2] = parts[2] + bias[1][rows]
            sink = sink_ref[heads[r0 // nq]]
            m = jnp.maximum(jnp.max(functools.reduce(jnp.maximum, parts), axis=-1, keepdims=True), sink)
            ps = [jnp.exp(part - m) for part in parts]
            den_rows.append(jnp.sum(functools.reduce(jnp.add, ps), axis=-1, keepdims=True) + jnp.exp(sink - m))
            p_rows.append(jnp.concatenate(ps, axis=-1).astype(BF16))
        pv = jnp.dot(jnp.concatenate(p_rows, axis=0), v, preferred_element_type=F32)
        on = pv[:, g * HEAD_DIM:(g + 1) * HEAD_DIM] / jnp.concatenate(den_rows, axis=0)
        for k, hd in enumerate(heads):
            outs[hd] = on[k * nq:(k + 1) * nq, :]
    o_ref[...] = jnp.concatenate(outs, axis=-1).astype(BF16)


def _attn_lat_kernel(sink_ref, q_ref, ktp_ref, ktc_ref, ktn_ref, vp_ref, vc_ref, vn_ref, ktx_ref, vx_ref,
                     blo_ref, bhi_ref, o_ref, kt_scr, v_scr):
    w = ktc_ref.shape[1]
    n_ctx = vx_ref.shape[0]
    for c, (kr, vr) in enumerate(((ktp_ref, vp_ref), (ktc_ref, vc_ref), (ktn_ref, vn_ref))):
        kt_scr[:, c * w:(c + 1) * w] = kr[...]
        v_scr[c * w:(c + 1) * w, :] = vr[...]
    kt_scr[:, 3 * w:3 * w + n_ctx] = ktx_ref[...]
    v_scr[3 * w:3 * w + n_ctx, :] = vx_ref[...]
    _heads_attend(q_ref, kt_scr[...], v_scr[...], (blo_ref[...], bhi_ref[...]), sink_ref, o_ref)


def _attn_lat(q, kt, v, ktx, vx, sinks, n_batch):
    t, qd = q.shape
    s = t // n_batch
    n_ctx = vx.shape[0] // n_batch
    w = WINDOW
    nb = s // w
    kvd = v.shape[1]
    j_all = 3 * w + n_ctx
    group = qd // HEAD_DIM // N_KV_HEADS
    qi = np.arange(w)[:, None]
    ki = np.arange(w)[None, :]
    lo = np.where(ki >= qi, 0.0, NEG).astype(np.float32)
    hi = np.where(ki <= qi, 0.0, NEG).astype(np.float32)
    off = np.full((w, w), NEG, np.float32)
    blo = jnp.asarray(np.stack([np.tile(lo, (group, 1)), np.tile(off, (group, 1))]))
    bhi = jnp.asarray(np.stack([np.tile(hi, (group, 1)), np.tile(off, (group, 1))]))
    prev = lambda b, i: (b, 0, jnp.maximum(i - 1, 0))
    nxt = lambda b, i: (b, 0, jnp.minimum(i + 1, nb - 1))
    vprev = lambda b, i: (b * nb + jnp.maximum(i - 1, 0), 0)
    vnxt = lambda b, i: (b * nb + jnp.minimum(i + 1, nb - 1), 0)
    return pl.pallas_call(
        _attn_lat_kernel,
        grid=(n_batch, nb),
        in_specs=[pl.BlockSpec(memory_space=pltpu.SMEM),
                  pl.BlockSpec((w, qd), lambda b, i: (b * nb + i, 0)),
                  pl.BlockSpec((None, kvd, w), prev),
                  pl.BlockSpec((None, kvd, w), lambda b, i: (b, 0, i)),
                  pl.BlockSpec((None, kvd, w), nxt),
                  pl.BlockSpec((w, kvd), vprev),
                  pl.BlockSpec((w, kvd), lambda b, i: (b * nb + i, 0)),
                  pl.BlockSpec((w, kvd), vnxt),
                  pl.BlockSpec((None, kvd, n_ctx), lambda b, i: (b, 0, 0)),
                  pl.BlockSpec((n_ctx, kvd), lambda b, i: (b, 0)),
                  pl.BlockSpec((None, group * w, w), lambda b, i: (jnp.where(i == 0, 1, 0), 0, 0)),
                  pl.BlockSpec((None, group * w, w), lambda b, i: (jnp.where(i == nb - 1, 1, 0), 0, 0))],
        out_specs=pl.BlockSpec((w, qd), lambda b, i: (b * nb + i, 0)),
        out_shape=jax.ShapeDtypeStruct((t, qd), BF16),
        scratch_shapes=[pltpu.VMEM((kvd, j_all), BF16), pltpu.VMEM((j_all, kvd), BF16)],
        compiler_params=_cparams(("parallel", "parallel")),
        name="attn_lat",
    )(sinks, q, kt, kt, kt, v, v, v, ktx, vx, blo, bhi)


def _attn_ctx_kernel(sink_ref, q_ref, kt_ref, v_ref, o_ref):
    _heads_attend(q_ref, kt_ref[...], v_ref[...], None, sink_ref, o_ref)


def _attn_ctx(q, kt, v, sinks, n_batch):
    t, qd = q.shape
    n = t // n_batch
    kvd = v.shape[1]
    return pl.pallas_call(
        _attn_ctx_kernel,
        grid=(n_batch,),
        in_specs=[pl.BlockSpec(memory_space=pltpu.SMEM),
                  pl.BlockSpec((n, qd), lambda b: (b, 0)),
                  pl.BlockSpec((None, kvd, n), lambda b: (b, 0, 0)),
                  pl.BlockSpec((n, kvd), lambda b: (b, 0))],
        out_specs=pl.BlockSpec((n, qd), lambda b: (b, 0)),
        out_shape=jax.ShapeDtypeStruct((t, qd), BF16),
        compiler_params=_cparams(("parallel",)),
        name="attn_ctx",
    )(sinks, q, kt, v)


ROUTE_I1, ROUTE_I2, ROUTE_R1, ROUTE_R2, ROUTE_G1, ROUTE_G2 = range(6)
ROUTE_ROWS = 8
TAB_START, TAB_COUNT, TAB_USED, TAB_SIZE = 0, N_EXPERTS, 2 * N_EXPERTS, 2 * N_EXPERTS + 8
ROW_UNROLL = 8


def _router_kernel(x_ref, g_ref, sh_ref, sc_ref, wr_ref, tri_ref, route_ref, idx_ref, cnt_ref, carry_scr):
    @pl.when(pl.program_id(0) == 0)
    def _():
        carry_scr[...] = jnp.zeros_like(carry_scr)

    h = _norm_mod(x_ref[...], g_ref[...], sh_ref[...], sc_ref[...])
    hi = h.astype(BF16)
    lo = (h - hi.astype(F32)).astype(BF16)
    a = jnp.dot(hi, wr_ref[...], preferred_element_type=F32)
    logits = (a[:, :LANES] + a[:, LANES:]) + jnp.dot(lo, wr_ref[:, :LANES], preferred_element_type=F32)
    lane = lax.broadcasted_iota(jnp.int32, logits.shape, 1)
    lane_f = lane.astype(F32)
    logits = jnp.where(lane < N_EXPERTS, logits, -jnp.inf)
    m1 = jnp.max(logits, axis=-1, keepdims=True)
    i1 = jnp.min(jnp.where(logits == m1, lane_f, float(LANES)), axis=-1, keepdims=True)
    oh1 = lane_f == i1
    rest = jnp.where(oh1, -jnp.inf, logits)
    m2 = jnp.max(rest, axis=-1, keepdims=True)
    i2 = jnp.min(jnp.where(rest == m2, lane_f, float(LANES)), axis=-1, keepdims=True)
    oh2 = lane_f == i2
    e2 = jnp.exp(m2 - m1)
    g1 = 1.0 / (1.0 + e2)
    g2 = e2 / (1.0 + e2)
    sel = jnp.where(oh1, 1.0, 0.0) + jnp.where(oh2, 1.0, 0.0)
    before = jnp.dot(tri_ref[...], sel.astype(BF16), preferred_element_type=F32) + carry_scr[0:1, :]
    r1 = jnp.sum(jnp.where(oh1, before, 0.0), axis=-1, keepdims=True)
    r2 = jnp.sum(jnp.where(oh2, before, 0.0), axis=-1, keepdims=True)
    total = carry_scr[0:1, :] + jnp.sum(sel, axis=0, keepdims=True)
    carry_scr[...] = jnp.broadcast_to(total, carry_scr.shape)
    cnt_ref[...] = jnp.broadcast_to(total, cnt_ref.shape)
    rec = jnp.zeros_like(logits)
    for ln, val in ((ROUTE_I1, i1), (ROUTE_I2, i2), (ROUTE_R1, r1), (ROUTE_R2, r2), (ROUTE_G1, g1), (ROUTE_G2, g2)):
        rec = jnp.where(lane == ln, val, rec)
    route_ref[...] = rec
    idx_ref[...] = rec.T[:ROUTE_ROWS, :].astype(jnp.int32)


def _router(x, mods, g_all, w_router, layer, tiles_per_batch, ctx, tm):
    t, d = x.shape
    mi = lambda j: _mod_index(layer, j, tiles_per_batch // tm if not ctx else 1, ctx)
    wr = jnp.zeros((d, LANES), F32).at[:, :N_EXPERTS].set(w_router)
    whi = wr.astype(BF16)
    wr = jnp.concatenate([whi, (wr - whi.astype(F32)).astype(BF16)], axis=1)
    tri = jnp.asarray(np.tril(np.ones((tm, tm), np.float32), -1), BF16)
    return pl.pallas_call(
        _router_kernel,
        grid=(t // tm,),
        in_specs=[pl.BlockSpec((tm, d), lambda i: (i, 0)),
                  pl.BlockSpec((None, 1, d), lambda i: (layer, 0, 0)),
                  _mod_spec(d, mi(3)), _mod_spec(d, mi(4)),
                  pl.BlockSpec((d, 2 * LANES), lambda i: (0, 0)),
                  pl.BlockSpec((tm, tm), lambda i: (0, 0))],
        out_specs=[pl.BlockSpec((tm, LANES), lambda i: (i, 0)),
                   pl.BlockSpec((None, ROUTE_ROWS, tm), lambda i: (i, 0, 0)),
                   pl.BlockSpec((8, LANES), lambda i: (0, 0))],
        out_shape=[jax.ShapeDtypeStruct((t, LANES), F32),
                   jax.ShapeDtypeStruct((t // tm, ROUTE_ROWS, tm), jnp.int32),
                   jax.ShapeDtypeStruct((8, LANES), F32)],
        scratch_shapes=[pltpu.VMEM((8, LANES), F32)],
        compiler_params=_cparams(("arbitrary",)),
        name="router",
    )(x, g_all, mods, mods, wr, tri)


SUB = 8


def _to_row_tiles(ref, val):
    n = val.shape[0]
    for k in range(SUB):
        ref[pl.ds(k, n, stride=SUB), :] = val[:, k * LANES:(k + 1) * LANES]


def _from_row_tiles(ref):
    n = ref.shape[0] // SUB
    return jnp.concatenate([ref[pl.ds(k, n, stride=SUB), :] for k in range(SUB)], axis=-1)


def _row_copy(src, dst, src_row, dst_row, sem):
    return pltpu.make_async_copy(src.at[pl.ds(pl.multiple_of(src_row * SUB, SUB), SUB)],
                                 dst.at[pl.ds(pl.multiple_of(dst_row * SUB, SUB), SUB)], sem)


def _for_rows(tm, body):
    def blk(i, _):
        base = pl.multiple_of(i * ROW_UNROLL, ROW_UNROLL)
        for j in range(ROW_UNROLL):
            body(base + j, j)
        return 0

    lax.fori_loop(0, tm // ROW_UNROLL, blk, 0)


ZERO_SEM = 4


def _dispatch_kernel(tab_ref, slot_ref, x_ref, g_ref, sh_ref, sc_ref, xs_ref, h_scr, zero_scr, sems, *, group):
    tm = x_ref.shape[0]
    n_groups = xs_ref.shape[0] // (group * SUB)
    step = pl.program_id(0)
    n_steps = pl.num_programs(0)
    cur = lax.rem(step, 2)

    def wait_buffer(b):
        for k in range(2):
            pltpu.make_async_copy(h_scr.at[b], xs_ref.at[pl.ds(0, tm * SUB)], sems.at[2 * b + k]).wait()

    @pl.when(step >= 2)
    def _():
        wait_buffer(cur)

    _to_row_tiles(h_scr.at[cur], _norm_mod(x_ref[...], g_ref[...], sh_ref[...], sc_ref[...]))

    def issue(r, j):
        for k in range(2):
            _row_copy(h_scr.at[cur], xs_ref, r, slot_ref[0, k, r], sems.at[2 * cur + k]).start(priority=j % 2)

    _for_rows(tm, issue)

    @pl.when(jnp.logical_and(step == n_steps - 1, step >= 1))
    def _():
        wait_buffer(1 - cur)

    @pl.when(step == n_steps - 1)
    def _():
        wait_buffer(cur)
        zero_scr[...] = jnp.zeros_like(zero_scr)
        for e in range(N_EXPERTS):
            n = tab_ref[TAB_COUNT + e]
            n_pad = lax.rem(group - lax.rem(n, group), group)
            first = tab_ref[TAB_START + e] + n

            def fill(k, _, first=first):
                _row_copy(zero_scr, xs_ref, 0, first + k, sems.at[ZERO_SEM]).start()
                return 0

            def fill_done(k, _):
                _row_copy(zero_scr, xs_ref, 0, 0, sems.at[ZERO_SEM]).wait()
                return 0

            lax.fori_loop(0, n_pad, fill, 0)
            lax.fori_loop(0, n_pad, fill_done, 0)

        def clear(j, _):
            row = pl.multiple_of(j * (group * SUB), group * SUB)
            cp = pltpu.make_async_copy(zero_scr, xs_ref.at[pl.ds(row, group * SUB)], sems.at[ZERO_SEM])
            cp.start()
            cp.wait()
            return 0

        lax.fori_loop(tab_ref[TAB_USED], n_groups, clear, 0)


def _dispatch(x, tab, slots, mods, g_all, layer, tiles_per_batch, ctx, n_groups, group, tm):
    t, d = x.shape
    mi = lambda j: _mod_index(layer, j, tiles_per_batch // tm if not ctx else 1, ctx)
    return pl.pallas_call(
        functools.partial(_dispatch_kernel, group=group),
        grid=(t // tm,),
        in_specs=[pl.BlockSpec(memory_space=pltpu.SMEM),
                  pl.BlockSpec((1, 2, tm), lambda i: (i, 0, 0), memory_space=pltpu.SMEM),
                  pl.BlockSpec((tm, d), lambda i: (i, 0)),
                  pl.BlockSpec((None, 1, d), lambda i: (layer, 0, 0)),
                  _mod_spec(d, mi(3)), _mod_spec(d, mi(4))],
        out_specs=pl.BlockSpec(memory_space=pl.ANY),
        out_shape=jax.ShapeDtypeStruct((n_groups * group * SUB, LANES), F32),
        scratch_shapes=[pltpu.VMEM((2, tm * SUB, LANES), F32), pltpu.VMEM((group * SUB, LANES), F32),
                        pltpu.SemaphoreType.DMA((ZERO_SEM + 1,))],
        compiler_params=_cparams(("arbitrary",)),
        name="moe_dispatch",
    )(tab, slots, x, g_all, mods, mods)


def _moe_ffn_kernel(be_ref, bv_ref, xs_ref, wg_ref, wu_ref, wd_ref, ys_ref, h_scr, acc_scr):
    del be_ref
    i = pl.program_id(0)
    f = pl.program_id(1)
    last = f == pl.num_programs(1) - 1
    valid = bv_ref[i] > 0

    @pl.when(jnp.logical_and(valid, f == 0))
    def _():
        h_scr[...] = _from_row_tiles(xs_ref).astype(BF16)
        acc_scr[...] = jnp.zeros_like(acc_scr)

    @pl.when(valid)
    def _():
        h = h_scr[...]
        a = jnp.dot(h, wg_ref[...], preferred_element_type=F32)
        u = jnp.dot(h, wu_ref[...], preferred_element_type=F32)
        t = (a * _sigmoid(a)) * u
        acc_scr[...] += jnp.dot(t.astype(BF16), wd_ref[...], preferred_element_type=F32)

    @pl.when(jnp.logical_and(valid, last))
    def _():
        _to_row_tiles(ys_ref, acc_scr[...])

    @pl.when(jnp.logical_and(jnp.logical_not(valid), last))
    def _():
        ys_ref[...] = jnp.zeros_like(ys_ref)


def _moe_ffn(xs, block_expert, block_valid, wg, wu, wd, layer, tm, tf=MOE_TF):
    d, ff = wg.shape[2:]
    assert d == SUB * LANES and xs.shape[1] == LANES
    cap = xs.shape[0] // SUB
    tf = min(tf, ff)
    nf = ff // tf
    sj = layer // 2
    fsel = lambda i, f, bv: jnp.where(bv[i] > 0, f, nf - 1)
    grid_spec = pltpu.PrefetchScalarGridSpec(
        num_scalar_prefetch=2,
        grid=(cap // tm, nf),
        in_specs=[pl.BlockSpec((tm * SUB, LANES), lambda i, f, be, bv: (i, 0)),
                  pl.BlockSpec((None, None, d, tf), lambda i, f, be, bv: (sj, be[i], 0, fsel(i, f, bv))),
                  pl.BlockSpec((None, None, d, tf), lambda i, f, be, bv: (sj, be[i], 0, fsel(i, f, bv))),
                  pl.BlockSpec((None, None, tf, d), lambda i, f, be, bv: (sj, be[i], fsel(i, f, bv), 0))],
        out_specs=pl.BlockSpec((tm * SUB, LANES), lambda i, f, be, bv: (i, 0)),
        scratch_shapes=[pltpu.VMEM((tm, d), BF16), pltpu.VMEM((tm, d), F32)],
    )
    return pl.pallas_call(
        _moe_ffn_kernel,
        grid_spec=grid_spec,
        out_shape=jax.ShapeDtypeStruct(xs.shape, F32),
        compiler_params=_cparams(("parallel", "arbitrary")),
        name="moe_ffn",
    )(block_expert, block_valid, xs, wg, wu, wd)


def _combine_kernel(slot_ref, next_slot_ref, route_ref, x_ref, gt_ref, fg_ref, ys_ref, o_ref, buf, sems, *, final_norm):
    tm = x_ref.shape[0]
    step = pl.program_id(0)
    cur = lax.rem(step, 2)

    def gather(idx_ref, b):
        def issue(r, j):
            for k in range(2):
                _row_copy(ys_ref, buf.at[b, k], idx_ref[0, k, r], r, sems.at[2 * b + k]).start(priority=j % 2)

        _for_rows(tm, issue)

    @pl.when(step == 0)
    def _():
        gather(slot_ref, cur)

    @pl.when(step + 1 < pl.num_programs(0))
    def _():
        gather(next_slot_ref, 1 - cur)

    for k in range(2):
        pltpu.make_async_copy(ys_ref.at[pl.ds(0, tm * SUB)], buf.at[cur, k], sems.at[2 * cur + k]).wait()
    rec = route_ref[...]
    g1 = rec[:, ROUTE_G1:ROUTE_G1 + 1]
    g2 = rec[:, ROUTE_G2:ROUTE_G2 + 1]
    y = _from_row_tiles(buf.at[cur, 0]) * g1 + _from_row_tiles(buf.at[cur, 1]) * g2
    out = x_ref[...] + gt_ref[...] * y
    if final_norm:
        out = out * lax.rsqrt(jnp.mean(out * out, axis=-1, keepdims=True) + EPS) * fg_ref[...]
    o_ref[...] = out


def _combine(ys, slots, route, x, mods, final_g, layer, tiles_per_batch, ctx, final_norm, tm):
    t, d = x.shape
    mi = _mod_index(layer, 5, tiles_per_batch // tm if not ctx else 1, ctx)
    n_tiles = t // tm
    return pl.pallas_call(
        functools.partial(_combine_kernel, final_norm=final_norm),
        grid=(n_tiles,),
        in_specs=[pl.BlockSpec((1, 2, tm), lambda i: (i, 0, 0), memory_space=pltpu.SMEM),
                  pl.BlockSpec((1, 2, tm), lambda i: (jnp.minimum(i + 1, n_tiles - 1), 0, 0), memory_space=pltpu.SMEM),
                  pl.BlockSpec((tm, LANES), lambda i: (i, 0)),
                  pl.BlockSpec((tm, d), lambda i: (i, 0)),
                  _mod_spec(d, mi),
                  pl.BlockSpec((1, d), lambda i: (0, 0)),
                  pl.BlockSpec(memory_space=pl.ANY)],
        out_specs=pl.BlockSpec((tm, d), lambda i: (i, 0)),
        out_shape=jax.ShapeDtypeStruct((t, d), F32),
        scratch_shapes=[pltpu.VMEM((2, 2, tm * SUB, LANES), F32), pltpu.SemaphoreType.DMA((4,))],
        compiler_params=_cparams(("arbitrary",)),
        name="moe_combine",
    )(slots, slots, route, x, mods, final_g.reshape(1, d), ys)


def _moe(x, mods, g_all, w_router, wg, wu, wd, final_g, layer, tiles_per_batch, ctx, final_norm):
    t, d = x.shape
    group = min(MOE_TM, t)
    row_tm = min(ROW_TM, t)
    route, idx, counts = _router(x, mods, g_all, w_router, layer, tiles_per_batch, ctx, row_tm)
    counts = counts[0, :N_EXPERTS].astype(jnp.int32)
    groups = (counts + group - 1) // group
    ends = jnp.cumsum(groups)
    starts = (ends - groups) * group
    n_groups = (2 * t + group - 1) // group + N_EXPERTS
    tab = jnp.zeros((TAB_SIZE,), jnp.int32)
    tab = tab.at[TAB_START:TAB_START + N_EXPERTS].set(starts).at[TAB_COUNT:TAB_COUNT + N_EXPERTS].set(counts)
    tab = tab.at[TAB_USED].set(ends[-1])
    gi = jnp.arange(n_groups, dtype=jnp.int32)
    block_expert = jnp.minimum(jnp.sum(gi[:, None] >= ends[None, :], axis=1), N_EXPERTS - 1).astype(jnp.int32)
    block_valid = (gi < ends[-1]).astype(jnp.int32)
    def slot_rows(e, r):
        base = functools.reduce(jnp.add, [jnp.where(e == k, starts[k], 0) for k in range(N_EXPERTS)])
        return base + r
    slots = jnp.stack([slot_rows(idx[:, ROUTE_I1, :], idx[:, ROUTE_R1, :]),
                       slot_rows(idx[:, ROUTE_I2, :], idx[:, ROUTE_R2, :])], axis=1)
    xs = _dispatch(x, tab, slots, mods, g_all, layer, tiles_per_batch, ctx, n_groups, group, row_tm)
    ys = _moe_ffn(xs, block_expert, block_valid, wg, wu, wd, layer, group)
    return _combine(ys, slots, route, x, mods, final_g, layer, tiles_per_batch, ctx, final_norm, row_tm)


def kernel(x, c, ctx, c_ctx, ada_w, ada_b, norm_mix_g, norm_ffn_g, fnet_w_out, attn_w_qkv, attn_w_o, attn_sinks,
           ffn_w_gate, ffn_w_up, ffn_w_down, moe_w_router, moe_w_gate, moe_w_up, moe_w_down, final_norm_g):
    n_batch, s, d = x.shape
    n_ctx = ctx.shape[1]
    depth = ada_w.shape[0]
    mods = _ada(c, c_ctx, ada_w, ada_b)
    g_mix = norm_mix_g.reshape(depth, 1, d)
    g_ffn = norm_ffn_g.reshape(depth, 1, d)
    rope_tabs = _rope_tables(s)
    x_lat = x.reshape(n_batch * s, d)
    x_ctx = ctx.reshape(n_batch * n_ctx, d)
    bf = lambda a: a.astype(BF16)
    w_out, w_qkv, w_o = bf(fnet_w_out), bf(attn_w_qkv), bf(attn_w_o)
    wg, wu, wd = bf(ffn_w_gate), bf(ffn_w_up), bf(ffn_w_down)
    eg, eu, ed = bf(moe_w_gate), bf(moe_w_up), bf(moe_w_down)
    for i in range(depth):
        j = i // 2
        last = i == depth - 1
        if i % 2 == 0:
            x_lat = _fnet_lat(x_lat, mods, g_mix, w_out, i, n_batch)
            if not last:
                x_ctx = _fnet_ctx(x_ctx, mods, g_mix, w_out, i, n_batch)
            x_lat = _ffn_dense(x_lat, mods, g_ffn, wg, wu, wd, i, s, False)
            if not last:
                x_ctx = _ffn_dense(x_ctx, mods, g_ffn, wg, wu, wd, i, n_ctx, True)
        else:
            q, kt, v = _qkv(x_lat, mods, g_mix, w_qkv, i, n_batch, False, rope_tabs)
            qx, ktx, vx = _qkv(x_ctx, mods, g_mix, w_qkv, i, n_batch, True)
            o = _attn_lat(q, kt, v, ktx, vx, attn_sinks[j], n_batch)
            x_lat = _proj_res(o, w_o, x_lat, mods, i, 2, s, False)
            if not last:
                ox = _attn_ctx(qx, ktx, vx, attn_sinks[j], n_batch)
                x_ctx = _proj_res(ox, w_o, x_ctx, mods, i, 2, n_ctx, True)
            x_lat = _moe(x_lat, mods, g_ffn, moe_w_router[j], eg, eu, ed, final_norm_g, i, s, False, last)
            if not last:
                x_ctx = _moe(x_ctx, mods, g_ffn, moe_w_router[j], eg, eu, ed, final_norm_g, i, n_ctx, True, False)
    return x_lat.reshape(n_batch, s, d)
```

```python
import functools

import numpy as np
import jax
import jax.numpy as jnp
from jax import lax
from jax.experimental import pallas as pl
from jax.experimental.pallas import tpu as pltpu

F32 = jnp.float32
BF16 = jnp.bfloat16

GRID_W = 64
N_MOD = 6
EPS = 1e-6
FNET_GROUPS = 4
HEAD_DIM = 64
N_KV_HEADS = 4
WINDOW = 128
ROPE_THETA = 10000.0
ROT_FREQS = HEAD_DIM // 4
N_EXPERTS = 8
MOD_ROWS = 8
CTX_ROW = MOD_ROWS - 1
LANES = 128
NEG = -1e30
VMEM_LIMIT = 56 * 1024 * 1024

FFN_TM = 512
FFN_TF = 1792
MOE_TM = 512
MOE_TF = 1792
ROW_TM = 512
PROJ_TM = 1024
DFT_P = 64
DFT_R = 128
FNET_RB = 16
FNET_KB = 8


def _cparams(sem):
    return pltpu.CompilerParams(dimension_semantics=sem, vmem_limit_bytes=VMEM_LIMIT)


def _sigmoid(a):
    return 1.0 / (1.0 + jnp.exp(-a))


def _norm_mod(x, g, shift, scale):
    xn = x * lax.rsqrt(jnp.mean(x * x, axis=-1, keepdims=True) + EPS)
    return (xn * g) * (1.0 + scale) + shift


def _mod_spec(d, idx_fn):
    return pl.BlockSpec((None, 1, d), lambda *ids: (idx_fn(*ids), 0, 0))


def _mod_index(layer, j, tiles_per_batch, ctx):
    def fn(i, *_):
        b = CTX_ROW if ctx else i // tiles_per_batch
        return (layer * MOD_ROWS + b) * N_MOD + j
    return fn


def _ada_kernel(c_ref, w_ref, b_ref, o_ref):
    cc = c_ref[...]
    s = cc * _sigmoid(cc)
    o_ref[...] = jnp.dot(s, w_ref[...], preferred_element_type=F32,
                         precision=lax.Precision.HIGHEST) + b_ref[...]


def _ada(c, c_ctx, ada_w, ada_b):
    depth, d, _ = ada_w.shape
    nb = c.shape[0]
    assert nb < MOD_ROWS
    cc = jnp.concatenate([c, jnp.zeros((CTX_ROW - nb, d), F32), c_ctx[None]], axis=0)
    out = pl.pallas_call(
        _ada_kernel,
        grid=(depth, N_MOD),
        in_specs=[pl.BlockSpec((MOD_ROWS, d), lambda l, j: (0, 0)),
                  pl.BlockSpec((None, d, d), lambda l, j: (l, 0, j)),
                  pl.BlockSpec((None, 1, d), lambda l, j: (l * N_MOD + j, 0, 0))],
        out_specs=pl.BlockSpec((None, MOD_ROWS, d), lambda l, j: (l, 0, j)),
        out_shape=jax.ShapeDtypeStruct((depth, MOD_ROWS, N_MOD * d), F32),
        compiler_params=_cparams(("parallel", "parallel")),
        name="ada",
    )(cc, ada_w, ada_b.reshape(depth * N_MOD, 1, d))
    return out.reshape(depth * MOD_ROWS * N_MOD, 1, d)


def _ffn_kernel(x_ref, g_ref, sh_ref, sc_ref, gt_ref, wg_ref, wu_ref, wd_ref, o_ref, h_scr, acc_scr):
    f = pl.program_id(1)

    @pl.when(f == 0)
    def _():
        h_scr[...] = _norm_mod(x_ref[...], g_ref[...], sh_ref[...], sc_ref[...]).astype(BF16)
        acc_scr[...] = jnp.zeros_like(acc_scr)

    h = h_scr[...]
    a = jnp.dot(h, wg_ref[...], preferred_element_type=F32)
    u = jnp.dot(h, wu_ref[...], preferred_element_type=F32)
    t = (a * _sigmoid(a)) * u
    acc_scr[...] += jnp.dot(t.astype(BF16), wd_ref[...], preferred_element_type=F32)

    @pl.when(f == pl.num_programs(1) - 1)
    def _():
        o_ref[...] = x_ref[...] + gt_ref[...] * acc_scr[...]


def _ffn_dense(x, mods, g_all, wg, wu, wd, layer, tiles_per_batch, ctx, tm=FFN_TM, tf=FFN_TF):
    t, d = x.shape
    ff = wg.shape[2]
    tm = min(tm, t)
    tf = min(tf, ff)
    sj = layer // 2
    mi = functools.partial(_mod_index, layer, tiles_per_batch=tiles_per_batch // tm if not ctx else 1, ctx=ctx)
    return pl.pallas_call(
        _ffn_kernel,
        grid=(t // tm, ff // tf),
        in_specs=[pl.BlockSpec((tm, d), lambda i, f: (i, 0)),
                  pl.BlockSpec((None, 1, d), lambda i, f: (layer, 0, 0)),
                  _mod_spec(d, mi(3)), _mod_spec(d, mi(4)), _mod_spec(d, mi(5)),
                  pl.BlockSpec((None, d, tf), lambda i, f: (sj, 0, f)),
                  pl.BlockSpec((None, d, tf), lambda i, f: (sj, 0, f)),
                  pl.BlockSpec((None, tf, d), lambda i, f: (sj, f, 0))],
        out_specs=pl.BlockSpec((tm, d), lambda i, f: (i, 0)),
        out_shape=jax.ShapeDtypeStruct((t, d), F32),
        scratch_shapes=[pltpu.VMEM((tm, d), BF16), pltpu.VMEM((tm, d), F32)],
        compiler_params=_cparams(("parallel", "arbitrary")),
        name="ffn_dense",
    )(x, g_all, mods, mods, mods, wg, wu, wd)


def _proj_res_kernel(a_ref, w_ref, x_ref, gt_ref, o_ref):
    y = jnp.dot(a_ref[...], w_ref[...], preferred_element_type=F32)
    o_ref[...] = x_ref[...] + gt_ref[...] * y


def _proj_res(a, w, x, mods, layer, j, tiles_per_batch, ctx, tm=PROJ_TM):
    t, d = x.shape
    k = a.shape[1]
    tm = min(tm, t)
    mi = _mod_index(layer, j, tiles_per_batch // tm if not ctx else 1, ctx)
    return pl.pallas_call(
        _proj_res_kernel,
        grid=(t // tm,),
        in_specs=[pl.BlockSpec((tm, k), lambda i: (i, 0)),
                  pl.BlockSpec((None, k, d), lambda i: (layer // 2, 0, 0)),
                  pl.BlockSpec((tm, d), lambda i: (i, 0)),
                  _mod_spec(d, mi)],
        out_specs=pl.BlockSpec((tm, d), lambda i: (i, 0)),
        out_shape=jax.ShapeDtypeStruct((t, d), F32),
        compiler_params=_cparams(("parallel",)),
        name="proj_res",
    )(a, w, x, mods)


def _dft_angles(n):
    a = np.arange(n)
    return 2.0 * np.pi * ((a[:, None] * a[None, :]) % n) / n


def _seq_dft_tables(n):
    kp = np.arange(DFT_P)[None, :, None]
    p = np.arange(DFT_P)[None, None, :]
    r = np.arange(DFT_R)[:, None, None]
    th = 2.0 * np.pi * ((kp * (DFT_R * p + r)) % n) / n
    tab1 = np.stack([np.cos(th), -np.sin(th)], axis=1)
    tab1 = tab1.reshape(DFT_R // FNET_RB, FNET_RB, 2, DFT_P, DFT_P)
    kron = np.zeros((DFT_R // FNET_RB, 2, DFT_P, FNET_RB, DFT_P, FNET_RB), np.float32)
    for rl in range(FNET_RB):
        kron[:, :, :, rl, :, rl] = tab1[:, rl]
    tab1 = kron.reshape(DFT_R // FNET_RB, 2 * DFT_P * FNET_RB, DFT_P * FNET_RB)
    th2 = _dft_angles(DFT_R)
    c2, s2 = np.cos(th2), np.sin(th2)
    tab2 = np.block([[c2, s2], [-s2, c2]])
    return jnp.asarray(tab1, BF16), jnp.asarray(tab2, BF16)


def _chan_dft_tables(group_dim):
    th = _dft_angles(group_dim)
    return jnp.asarray(np.cos(th), BF16), jnp.asarray(np.sin(th), BF16)


def _ctx_dft_table(n_ctx):
    th = _dft_angles(n_ctx)
    return jnp.asarray(np.concatenate([np.cos(th), -np.sin(th)], axis=0), BF16)


def _cols_store(scr, val):
    for c in range(scr.shape[0]):
        scr[c] = val[:, c * LANES:(c + 1) * LANES]


def _cols_load(scr):
    return jnp.concatenate([scr[c] for c in range(scr.shape[0])], axis=-1)


def _cols_store_rows(scr, sel, val):
    for c in range(scr.shape[0]):
        scr[c, sel, :] = val[:, c * LANES:(c + 1) * LANES]


def _cols_load_rows(scr, sel):
    return jnp.concatenate([scr[c, sel, :] for c in range(scr.shape[0])], axis=-1)


def _fnet_stage1_kernel(x_ref, g_ref, sh_ref, sc_ref, tab_ref, zr_ref, zi_ref):
    d = g_ref.shape[-1]
    rows = DFT_P * FNET_RB
    h = _norm_mod(x_ref[...].reshape(rows, d), g_ref[...], sh_ref[...], sc_ref[...]).astype(BF16)
    z = jnp.dot(tab_ref[...], h, preferred_element_type=F32)
    zr_ref[...] = z[:rows].astype(BF16).reshape(DFT_P, FNET_RB, d)
    zi_ref[...] = z[rows:].astype(BF16).reshape(DFT_P, FNET_RB, d)


def _mix_tail(pr, pi, cc_ref, sc_ref, wout_ref, scale):
    gd = cc_ref.shape[0]
    ys = []
    for g in range(pr.shape[1] // gd):
        ys.append(jnp.dot(pr[:, g * gd:(g + 1) * gd], cc_ref[...], preferred_element_type=F32)
                  + jnp.dot(pi[:, g * gd:(g + 1) * gd], sc_ref[...], preferred_element_type=F32))
    mixed = (jnp.concatenate(ys, axis=-1) * scale).astype(BF16)
    return jnp.dot(mixed, wout_ref[...], preferred_element_type=F32)


def _fnet_stage2_kernel(zr_ref, zi_ref, tab2_ref, cc_ref, sc_ref, wout_ref, x_ref, gt_ref, o_ref,
                        p_scr, x_scr, o_scr, *, scale):
    d = gt_ref.shape[-1]
    rows = DFT_R * FNET_KB
    for j in range(FNET_KB):
        z = jnp.concatenate([zr_ref[j * DFT_R:(j + 1) * DFT_R, :], zi_ref[j * DFT_R:(j + 1) * DFT_R, :]], axis=0)
        p = jnp.dot(tab2_ref[...], z, preferred_element_type=F32)
        p_scr[j * DFT_R:(j + 1) * DFT_R, :d] = p[:DFT_R].astype(BF16)
        p_scr[j * DFT_R:(j + 1) * DFT_R, d:] = p[DFT_R:].astype(BF16)
    y = _mix_tail(p_scr[:, :d], p_scr[:, d:], cc_ref, sc_ref, wout_ref, scale)
    _cols_store(x_scr, x_ref[...].reshape(rows, d))
    for j in range(FNET_KB):
        sel = pl.ds(j, DFT_R, stride=FNET_KB)
        _cols_store_rows(o_scr, sel, _cols_load_rows(x_scr, sel) + gt_ref[...] * y[j * DFT_R:(j + 1) * DFT_R, :])
    o_ref[...] = _cols_load(o_scr).reshape(DFT_R, FNET_KB, d)


def _fnet_lat(x, mods, g_all, w_out, layer, n_batch):
    t, d = x.shape
    s = t // n_batch
    assert s == DFT_P * DFT_R
    tab1, tab2 = _seq_dft_tables(s)
    ccos, csin = _chan_dft_tables(d // FNET_GROUPS)
    mi = lambda j: (lambda b, *_: (layer * MOD_ROWS + b) * N_MOD + j)
    xv = x.reshape(n_batch, DFT_P, DFT_R, d)
    blk1 = (None, DFT_P, FNET_RB, d)
    rows1 = DFT_P * FNET_RB
    mi1 = lambda j: (lambda r, b: (layer * MOD_ROWS + b) * N_MOD + j)
    zr, zi = pl.pallas_call(
        _fnet_stage1_kernel,
        grid=(DFT_R // FNET_RB, n_batch),
        in_specs=[pl.BlockSpec(blk1, lambda r, b: (b, 0, r, 0)),
                  pl.BlockSpec((None, 1, d), lambda r, b: (layer, 0, 0)),
                  _mod_spec(d, mi1(0)), _mod_spec(d, mi1(1)),
                  pl.BlockSpec((None, 2 * rows1, rows1), lambda r, b: (r, 0, 0))],
        out_specs=[pl.BlockSpec(blk1, lambda r, b: (b, 0, r, 0))] * 2,
        out_shape=[jax.ShapeDtypeStruct((n_batch, DFT_P, DFT_R, d), BF16)] * 2,
        compiler_params=_cparams(("parallel", "parallel")),
        name="fnet_stage1",
    )(xv, g_all, mods, mods, tab1)
    zr = zr.reshape(n_batch, DFT_P * DFT_R, d)
    zi = zi.reshape(n_batch, DFT_P * DFT_R, d)
    xo = x.reshape(n_batch, DFT_R, DFT_P, d)
    blk2 = (None, DFT_R, FNET_KB, d)
    rows = FNET_KB * DFT_R
    scale = float(1.0 / np.sqrt(float(s) * (d // FNET_GROUPS)))
    out = pl.pallas_call(
        functools.partial(_fnet_stage2_kernel, scale=scale),
        grid=(n_batch, DFT_P // FNET_KB),
        in_specs=[pl.BlockSpec((None, rows, d), lambda b, k: (b, k, 0)),
                  pl.BlockSpec((None, rows, d), lambda b, k: (b, k, 0)),
                  pl.BlockSpec((2 * DFT_R, 2 * DFT_R), lambda b, k: (0, 0)),
                  pl.BlockSpec(ccos.shape, lambda b, k: (0, 0)),
                  pl.BlockSpec(csin.shape, lambda b, k: (0, 0)),
                  pl.BlockSpec((None, d, d), lambda b, k: (layer // 2, 0, 0)),
                  pl.BlockSpec(blk2, lambda b, k: (b, 0, k, 0)),
                  _mod_spec(d, mi(2))],
        out_specs=pl.BlockSpec(blk2, lambda b, k: (b, 0, k, 0)),
        out_shape=jax.ShapeDtypeStruct((n_batch, DFT_R, DFT_P, d), F32),
        scratch_shapes=[pltpu.VMEM((rows, 2 * d), BF16)] + [pltpu.VMEM((d // LANES, rows, LANES), F32)] * 2,
        compiler_params=_cparams(("parallel", "parallel")),
        name="fnet_stage2",
    )(zr, zi, tab2, ccos, csin, w_out, xo, mods)
    return out.reshape(t, d)


def _fnet_ctx_kernel(x_ref, g_ref, sh_ref, sc_ref, gt_ref, tab_ref, cc_ref, sc2_ref, wout_ref, o_ref, *, scale):
    n = x_ref.shape[0]
    h = _norm_mod(x_ref[...], g_ref[...], sh_ref[...], sc_ref[...]).astype(BF16)
    p = jnp.dot(tab_ref[...], h, preferred_element_type=F32)
    y = _mix_tail(p[:n].astype(BF16), p[n:].astype(BF16), cc_ref, sc2_ref, wout_ref, scale)
    o_ref[...] = x_ref[...] + gt_ref[...] * y


def _fnet_ctx(x, mods, g_all, w_out, layer, n_batch):
    t, d = x.shape
    n = t // n_batch
    gd = d // FNET_GROUPS
    ccos, csin = _chan_dft_tables(gd)
    tab = _ctx_dft_table(n)
    mi = lambda j: (lambda b: (layer * MOD_ROWS + CTX_ROW) * N_MOD + j)
    scale = float(1.0 / np.sqrt(float(n) * gd))
    return pl.pallas_call(
        functools.partial(_fnet_ctx_kernel, scale=scale),
        grid=(n_batch,),
        in_specs=[pl.BlockSpec((n, d), lambda b: (b, 0)),
                  pl.BlockSpec((None, 1, d), lambda b: (layer, 0, 0)),
                  _mod_spec(d, mi(0)), _mod_spec(d, mi(1)), _mod_spec(d, mi(2)),
                  pl.BlockSpec(tab.shape, lambda b: (0, 0)),
                  pl.BlockSpec(ccos.shape, lambda b: (0, 0)),
                  pl.BlockSpec(csin.shape, lambda b: (0, 0)),
                  pl.BlockSpec((None, d, d), lambda b: (layer // 2, 0, 0))],
        out_specs=pl.BlockSpec((n, d), lambda b: (b, 0)),
        out_shape=jax.ShapeDtypeStruct((t, d), F32),
        compiler_params=_cparams(("parallel",)),
        name="fnet_ctx",
    )(x, g_all, mods, mods, mods, tab, ccos, csin, w_out)


def _rope_tables(n_seq):
    rows = n_seq // GRID_W
    row = jnp.repeat(jnp.arange(rows, dtype=F32), GRID_W)
    col = jnp.tile(jnp.arange(GRID_W, dtype=F32), rows)
    inv_freq = ROPE_THETA ** (-jnp.arange(ROT_FREQS, dtype=F32) / ROT_FREQS)
    ang = jnp.stack([row[:, None] * inv_freq, col[:, None] * inv_freq], axis=1)
    cos, sin = jnp.cos(ang), jnp.sin(ang)
    zero = jnp.zeros_like(sin)
    cos_h = jnp.stack([cos, cos], axis=2).reshape(n_seq, HEAD_DIM)
    sin_lo = jnp.stack([-sin, zero], axis=2).reshape(n_seq, HEAD_DIM)
    sin_hi = jnp.stack([zero, sin], axis=2).reshape(n_seq, HEAD_DIM)
    rep = LANES // HEAD_DIM
    return jnp.tile(cos_h, (1, rep)), jnp.tile(sin_lo, (1, rep)), jnp.tile(sin_hi, (1, rep))


def _qkv_kernel(x_ref, g_ref, sh_ref, sc_ref, w_ref, *rest, rope, q_dim, kv_dim, q_scale):
    if rope:
        cos_ref, slo_ref, shi_ref, q_ref, kt_ref, v_ref = rest
    else:
        q_ref, kt_ref, v_ref = rest
    h = _norm_mod(x_ref[...], g_ref[...], sh_ref[...], sc_ref[...]).astype(BF16)
    qkv = jnp.dot(h, w_ref[...], preferred_element_type=F32)

    def rot(xs):
        if not rope:
            return xs
        return (xs * cos_ref[...] + pltpu.roll(xs, LANES - ROT_FREQS, axis=1) * slo_ref[...]
                + pltpu.roll(xs, ROT_FREQS, axis=1) * shi_ref[...])

    for j in range(q_dim // LANES):
        q_ref[:, j * LANES:(j + 1) * LANES] = (rot(qkv[:, j * LANES:(j + 1) * LANES]) * q_scale).astype(BF16)
    ks = [rot(qkv[:, q_dim + j * LANES:q_dim + (j + 1) * LANES]) for j in range(kv_dim // LANES)]
    kt_ref[...] = jnp.concatenate(ks, axis=-1).T.astype(BF16)
    v_ref[...] = qkv[:, q_dim + kv_dim:].astype(BF16)


def _qkv(x, mods, g_all, w_qkv, layer, n_batch, ctx, rope_tabs=None, tm=512):
    t, d = x.shape
    n = t // n_batch
    tm = min(tm, n)
    tpb = n // tm
    kv_dim = N_KV_HEADS * HEAD_DIM
    q_dim = w_qkv.shape[2] - 2 * kv_dim
    mi = lambda j: _mod_index(layer, j, tpb, ctx)
    in_specs = [pl.BlockSpec((tm, d), lambda i: (i, 0)),
                pl.BlockSpec((None, 1, d), lambda i: (layer, 0, 0)),
                _mod_spec(d, mi(0)), _mod_spec(d, mi(1)),
                pl.BlockSpec((None,) + w_qkv.shape[1:], lambda i: (layer // 2, 0, 0))]
    args = [x, g_all, mods, mods, w_qkv]
    rope = rope_tabs is not None
    if rope:
        in_specs += [pl.BlockSpec((tm, LANES), lambda i: (i % tpb, 0))] * 3
        args += list(rope_tabs)
    return pl.pallas_call(
        functools.partial(_qkv_kernel, rope=rope, q_dim=q_dim, kv_dim=kv_dim, q_scale=HEAD_DIM ** -0.5),
        grid=(t // tm,),
        in_specs=in_specs,
        out_specs=[pl.BlockSpec((tm, q_dim), lambda i: (i, 0)),
                   pl.BlockSpec((None, kv_dim, tm), lambda i: (i // tpb, 0, i % tpb)),
                   pl.BlockSpec((tm, kv_dim), lambda i: (i, 0))],
        out_shape=[jax.ShapeDtypeStruct((t, q_dim), BF16),
                   jax.ShapeDtypeStruct((n_batch, kv_dim, n), BF16),
                   jax.ShapeDtypeStruct((t, kv_dim), BF16)],
        compiler_params=_cparams(("parallel",)),
        name="qkv_ctx" if ctx else "qkv_lat",
    )(*args)


ATTN_RB = 64


def _heads_attend(q_ref, kt, v, bias, sink_ref, o_ref):
    nq = q_ref.shape[0]
    n_heads = q_ref.shape[1] // HEAD_DIM
    group = n_heads // N_KV_HEADS
    outs = [None] * n_heads
    for g in range(N_KV_HEADS):
        heads = range(g * group, (g + 1) * group)
        qg = jnp.concatenate([q_ref[:, hd * HEAD_DIM:(hd + 1) * HEAD_DIM] for hd in heads], axis=0)
        s = jnp.dot(qg, kt[g * HEAD_DIM:(g + 1) * HEAD_DIM, :], preferred_element_type=F32)
        p_rows, den_rows = [], []
        for r0 in range(0, s.shape[0], ATTN_RB):
            rows = slice(r0, r0 + ATTN_RB)
            parts = [s[rows, k * LANES:(k + 1) * LANES] for k in range(s.shape[1] // LANES)]
            if bias is not None:
                parts[0] = parts[0] + bias[0][rows]
                parts[2] = parts[2] + bias[1][rows]
            sink = sink_ref[heads[r0 // nq]]
            m = jnp.maximum(jnp.max(functools.reduce(jnp.maximum, parts), axis=-1, keepdims=True), sink)
            ps = [jnp.exp(part - m) for part in parts]
            den_rows.append(jnp.sum(functools.reduce(jnp.add, ps), axis=-1, keepdims=True) + jnp.exp(sink - m))
            p_rows.append(jnp.concatenate(ps, axis=-1).astype(BF16))
        pv = jnp.dot(jnp.concatenate(p_rows, axis=0), v, preferred_element_type=F32)
        on = pv[:, g * HEAD_DIM:(g + 1) * HEAD_DIM] / jnp.concatenate(den_rows, axis=0)
        for k, hd in enumerate(heads):
            outs[hd] = on[k * nq:(k + 1) * nq, :]
    o_ref[...] = jnp.concatenate(outs, axis=-1).astype(BF16)


def _attn_lat_kernel(sink_ref, q_ref, ktp_ref, ktc_ref, ktn_ref, vp_ref, vc_ref, vn_ref, ktx_ref, vx_ref,
                     blo_ref, bhi_ref, o_ref, kt_scr, v_scr):
    w = ktc_ref.shape[1]
    n_ctx = vx_ref.shape[0]
    for c, (kr, vr) in enumerate(((ktp_ref, vp_ref), (ktc_ref, vc_ref), (ktn_ref, vn_ref))):
        kt_scr[:, c * w:(c + 1) * w] = kr[...]
        v_scr[c * w:(c + 1) * w, :] = vr[...]
    kt_scr[:, 3 * w:3 * w + n_ctx] = ktx_ref[...]
    v_scr[3 * w:3 * w + n_ctx, :] = vx_ref[...]
    _heads_attend(q_ref, kt_scr[...], v_scr[...], (blo_ref[...], bhi_ref[...]), sink_ref, o_ref)


def _attn_lat(q, kt, v, ktx, vx, sinks, n_batch):
    t, qd = q.shape
    s = t // n_batch
    n_ctx = vx.shape[0] // n_batch
    w = WINDOW
    nb = s // w
    kvd = v.shape[1]
    j_all = 3 * w + n_ctx
    group = qd // HEAD_DIM // N_KV_HEADS
    qi = np.arange(w)[:, None]
    ki = np.arange(w)[None, :]
    lo = np.where(ki >= qi, 0.0, NEG).astype(np.float32)
    hi = np.where(ki <= qi, 0.0, NEG).astype(np.float32)
    off = np.full((w, w), NEG, np.float32)
    blo = jnp.asarray(np.stack([np.tile(lo, (group, 1)), np.tile(off, (group, 1))]))
    bhi = jnp.asarray(np.stack([np.tile(hi, (group, 1)), np.tile(off, (group, 1))]))
    prev = lambda b, i: (b, 0, jnp.maximum(i - 1, 0))
    nxt = lambda b, i: (b, 0, jnp.minimum(i + 1, nb - 1))
    vprev = lambda b, i: (b * nb + jnp.maximum(i - 1, 0), 0)
    vnxt = lambda b, i: (b * nb + jnp.minimum(i + 1, nb - 1), 0)
    return pl.pallas_call(
        _attn_lat_kernel,
        grid=(n_batch, nb),
        in_specs=[pl.BlockSpec(memory_space=pltpu.SMEM),
                  pl.BlockSpec((w, qd), lambda b, i: (b * nb + i, 0)),
                  pl.BlockSpec((None, kvd, w), prev),
                  pl.BlockSpec((None, kvd, w), lambda b, i: (b, 0, i)),
                  pl.BlockSpec((None, kvd, w), nxt),
                  pl.BlockSpec((w, kvd), vprev),
                  pl.BlockSpec((w, kvd), lambda b, i: (b * nb + i, 0)),
                  pl.BlockSpec((w, kvd), vnxt),
                  pl.BlockSpec((None, kvd, n_ctx), lambda b, i: (b, 0, 0)),
                  pl.BlockSpec((n_ctx, kvd), lambda b, i: (b, 0)),
                  pl.BlockSpec((None, group * w, w), lambda b, i: (jnp.where(i == 0, 1, 0), 0, 0)),
                  pl.BlockSpec((None, group * w, w), lambda b, i: (jnp.where(i == nb - 1, 1, 0), 0, 0))],
        out_specs=pl.BlockSpec((w, qd), lambda b, i: (b * nb + i, 0)),
        out_shape=jax.ShapeDtypeStruct((t, qd), BF16),
        scratch_shapes=[pltpu.VMEM((kvd, j_all), BF16), pltpu.VMEM((j_all, kvd), BF16)],
        compiler_params=_cparams(("parallel", "parallel")),
        name="attn_lat",
    )(sinks, q, kt, kt, kt, v, v, v, ktx, vx, blo, bhi)


def _attn_ctx_kernel(sink_ref, q_ref, kt_ref, v_ref, o_ref):
    _heads_attend(q_ref, kt_ref[...], v_ref[...], None, sink_ref, o_ref)


def _attn_ctx(q, kt, v, sinks, n_batch):
    t, qd = q.shape
    n = t // n_batch
    kvd = v.shape[1]
    return pl.pallas_call(
        _attn_ctx_kernel,
        grid=(n_batch,),
        in_specs=[pl.BlockSpec(memory_space=pltpu.SMEM),
                  pl.BlockSpec((n, qd), lambda b: (b, 0)),
                  pl.BlockSpec((None, kvd, n), lambda b: (b, 0, 0)),
                  pl.BlockSpec((n, kvd), lambda b: (b, 0))],
        out_specs=pl.BlockSpec((n, qd), lambda b: (b, 0)),
        out_shape=jax.ShapeDtypeStruct((t, qd), BF16),
        compiler_params=_cparams(("parallel",)),
        name="attn_ctx",
    )(sinks, q, kt, v)


ROUTE_I1, ROUTE_I2, ROUTE_R1, ROUTE_R2, ROUTE_G1, ROUTE_G2 = range(6)
ROUTE_ROWS = 8
TAB_START, TAB_COUNT, TAB_USED, TAB_SIZE = 0, N_EXPERTS, 2 * N_EXPERTS, 2 * N_EXPERTS + 8
ROW_UNROLL = 8


def _router_kernel(x_ref, g_ref, sh_ref, sc_ref, wr_ref, tri_ref, route_ref, idx_ref, cnt_ref, carry_scr):
    @pl.when(pl.program_id(0) == 0)
    def _():
        carry_scr[...] = jnp.zeros_like(carry_scr)

    h = _norm_mod(x_ref[...], g_ref[...], sh_ref[...], sc_ref[...])
    hi = h.astype(BF16)
    lo = (h - hi.astype(F32)).astype(BF16)
    a = jnp.dot(hi, wr_ref[...], preferred_element_type=F32)
    logits = (a[:, :LANES] + a[:, LANES:]) + jnp.dot(lo, wr_ref[:, :LANES], preferred_element_type=F32)
    lane = lax.broadcasted_iota(jnp.int32, logits.shape, 1)
    lane_f = lane.astype(F32)
    logits = jnp.where(lane < N_EXPERTS, logits, -jnp.inf)
    m1 = jnp.max(logits, axis=-1, keepdims=True)
    i1 = jnp.min(jnp.where(logits == m1, lane_f, float(LANES)), axis=-1, keepdims=True)
    oh1 = lane_f == i1
    rest = jnp.where(oh1, -jnp.inf, logits)
    m2 = jnp.max(rest, axis=-1, keepdims=True)
    i2 = jnp.min(jnp.where(rest == m2, lane_f, float(LANES)), axis=-1, keepdims=True)
    oh2 = lane_f == i2
    e2 = jnp.exp(m2 - m1)
    g1 = 1.0 / (1.0 + e2)
    g2 = e2 / (1.0 + e2)
    sel = jnp.where(oh1, 1.0, 0.0) + jnp.where(oh2, 1.0, 0.0)
    before = jnp.dot(tri_ref[...], sel.astype(BF16), preferred_element_type=F32) + carry_scr[0:1, :]
    r1 = jnp.sum(jnp.where(oh1, before, 0.0), axis=-1, keepdims=True)
    r2 = jnp.sum(jnp.where(oh2, before, 0.0), axis=-1, keepdims=True)
    total = carry_scr[0:1, :] + jnp.sum(sel, axis=0, keepdims=True)
    carry_scr[...] = jnp.broadcast_to(total, carry_scr.shape)
    cnt_ref[...] = jnp.broadcast_to(total, cnt_ref.shape)
    rec = jnp.zeros_like(logits)
    for ln, val in ((ROUTE_I1, i1), (ROUTE_I2, i2), (ROUTE_R1, r1), (ROUTE_R2, r2), (ROUTE_G1, g1), (ROUTE_G2, g2)):
        rec = jnp.where(lane == ln, val, rec)
    route_ref[...] = rec
    idx_ref[...] = rec.T[:ROUTE_ROWS, :].astype(jnp.int32)


def _router(x, mods, g_all, w_router, layer, tiles_per_batch, ctx, tm):
    t, d = x.shape
    mi = lambda j: _mod_index(layer, j, tiles_per_batch // tm if not ctx else 1, ctx)
    wr = jnp.zeros((d, LANES), F32).at[:, :N_EXPERTS].set(w_router)
    whi = wr.astype(BF16)
    wr = jnp.concatenate([whi, (wr - whi.astype(F32)).astype(BF16)], axis=1)
    tri = jnp.asarray(np.tril(np.ones((tm, tm), np.float32), -1), BF16)
    return pl.pallas_call(
        _router_kernel,
        grid=(t // tm,),
        in_specs=[pl.BlockSpec((tm, d), lambda i: (i, 0)),
                  pl.BlockSpec((None, 1, d), lambda i: (layer, 0, 0)),
                  _mod_spec(d, mi(3)), _mod_spec(d, mi(4)),
                  pl.BlockSpec((d, 2 * LANES), lambda i: (0, 0)),
                  pl.BlockSpec((tm, tm), lambda i: (0, 0))],
        out_specs=[pl.BlockSpec((tm, LANES), lambda i: (i, 0)),
                   pl.BlockSpec((None, ROUTE_ROWS, tm), lambda i: (i, 0, 0)),
                   pl.BlockSpec((8, LANES), lambda i: (0, 0))],
        out_shape=[jax.ShapeDtypeStruct((t, LANES), F32),
                   jax.ShapeDtypeStruct((t // tm, ROUTE_ROWS, tm), jnp.int32),
                   jax.ShapeDtypeStruct((8, LANES), F32)],
        scratch_shapes=[pltpu.VMEM((8, LANES), F32)],
        compiler_params=_cparams(("arbitrary",)),
        name="router",
    )(x, g_all, mods, mods, wr, tri)


SUB = 8


def _to_row_tiles(ref, val):
    n = val.shape[0]
    for k in range(SUB):
        ref[pl.ds(k, n, stride=SUB), :] = val[:, k * LANES:(k + 1) * LANES]


def _from_row_tiles(ref):
    n = ref.shape[0] // SUB
    return jnp.concatenate([ref[pl.ds(k, n, stride=SUB), :] for k in range(SUB)], axis=-1)


def _row_copy(src, dst, src_row, dst_row, sem):
    return pltpu.make_async_copy(src.at[pl.ds(pl.multiple_of(src_row * SUB, SUB), SUB)],
                                 dst.at[pl.ds(pl.multiple_of(dst_row * SUB, SUB), SUB)], sem)


def _for_rows(tm, body):
    def blk(i, _):
        base = pl.multiple_of(i * ROW_UNROLL, ROW_UNROLL)
        for j in range(ROW_UNROLL):
            body(base + j, j)
        return 0

    lax.fori_loop(0, tm // ROW_UNROLL, blk, 0)


ZERO_SEM = 4


def _dispatch_kernel(tab_ref, slot_ref, x_ref, g_ref, sh_ref, sc_ref, xs_ref, h_scr, zero_scr, sems, *, group):
    tm = x_ref.shape[0]
    n_groups = xs_ref.shape[0] // (group * SUB)
    step = pl.program_id(0)
    n_steps = pl.num_programs(0)
    cur = lax.rem(step, 2)

    def wait_buffer(b):
        for k in range(2):
            pltpu.make_async_copy(h_scr.at[b], xs_ref.at[pl.ds(0, tm * SUB)], sems.at[2 * b + k]).wait()

    @pl.when(step >= 2)
    def _():
        wait_buffer(cur)

    _to_row_tiles(h_scr.at[cur], _norm_mod(x_ref[...], g_ref[...], sh_ref[...], sc_ref[...]))

    def issue(r, j):
        for k in range(2):
            _row_copy(h_scr.at[cur], xs_ref, r, slot_ref[0, k, r], sems.at[2 * cur + k]).start(priority=j % 2)

    _for_rows(tm, issue)

    @pl.when(jnp.logical_and(step == n_steps - 1, step >= 1))
    def _():
        wait_buffer(1 - cur)

    @pl.when(step == n_steps - 1)
    def _():
        wait_buffer(cur)
        zero_scr[...] = jnp.zeros_like(zero_scr)
        for e in range(N_EXPERTS):
            n = tab_ref[TAB_COUNT + e]
            n_pad = lax.rem(group - lax.rem(n, group), group)
            first = tab_ref[TAB_START + e] + n

            def fill(k, _, first=first):
                _row_copy(zero_scr, xs_ref, 0, first + k, sems.at[ZERO_SEM]).start()
                return 0

            def fill_done(k, _):
                _row_copy(zero_scr, xs_ref, 0, 0, sems.at[ZERO_SEM]).wait()
                return 0

            lax.fori_loop(0, n_pad, fill, 0)
            lax.fori_loop(0, n_pad, fill_done, 0)

        def clear(j, _):
            row = pl.multiple_of(j * (group * SUB), group * SUB)
            cp = pltpu.make_async_copy(zero_scr, xs_ref.at[pl.ds(row, group * SUB)], sems.at[ZERO_SEM])
            cp.start()
            cp.wait()
            return 0

        lax.fori_loop(tab_ref[TAB_USED], n_groups, clear, 0)


def _dispatch(x, tab, slots, mods, g_all, layer, tiles_per_batch, ctx, n_groups, group, tm):
    t, d = x.shape
    mi = lambda j: _mod_index(layer, j, tiles_per_batch // tm if not ctx else 1, ctx)
    return pl.pallas_call(
        functools.partial(_dispatch_kernel, group=group),
        grid=(t // tm,),
        in_specs=[pl.BlockSpec(memory_space=pltpu.SMEM),
                  pl.BlockSpec((1, 2, tm), lambda i: (i, 0, 0), memory_space=pltpu.SMEM),
                  pl.BlockSpec((tm, d), lambda i: (i, 0)),
                  pl.BlockSpec((None, 1, d), lambda i: (layer, 0, 0)),
                  _mod_spec(d, mi(3)), _mod_spec(d, mi(4))],
        out_specs=pl.BlockSpec(memory_space=pl.ANY),
        out_shape=jax.ShapeDtypeStruct((n_groups * group * SUB, LANES), F32),
        scratch_shapes=[pltpu.VMEM((2, tm * SUB, LANES), F32), pltpu.VMEM((group * SUB, LANES), F32),
                        pltpu.SemaphoreType.DMA((ZERO_SEM + 1,))],
        compiler_params=_cparams(("arbitrary",)),
        name="moe_dispatch",
    )(tab, slots, x, g_all, mods, mods)


def _moe_ffn_kernel(be_ref, bv_ref, xs_ref, wg_ref, wu_ref, wd_ref, ys_ref, h_scr, acc_scr):
    del be_ref
    i = pl.program_id(0)
    f = pl.program_id(1)
    last = f == pl.num_programs(1) - 1
    valid = bv_ref[i] > 0

    @pl.when(jnp.logical_and(valid, f == 0))
    def _():
        h_scr[...] = _from_row_tiles(xs_ref).astype(BF16)
        acc_scr[...] = jnp.zeros_like(acc_scr)

    @pl.when(valid)
    def _():
        h = h_scr[...]
        a = jnp.dot(h, wg_ref[...], preferred_element_type=F32)
        u = jnp.dot(h, wu_ref[...], preferred_element_type=F32)
        t = (a * _sigmoid(a)) * u
        acc_scr[...] += jnp.dot(t.astype(BF16), wd_ref[...], preferred_element_type=F32)

    @pl.when(jnp.logical_and(valid, last))
    def _():
        _to_row_tiles(ys_ref, acc_scr[...])

    @pl.when(jnp.logical_and(jnp.logical_not(valid), last))
    def _():
        ys_ref[...] = jnp.zeros_like(ys_ref)


def _moe_ffn(xs, block_expert, block_valid, wg, wu, wd, layer, tm, tf=MOE_TF):
    d, ff = wg.shape[2:]
    assert d == SUB * LANES and xs.shape[1] == LANES
    cap = xs.shape[0] // SUB
    tf = min(tf, ff)
    nf = ff // tf
    sj = layer // 2
    fsel = lambda i, f, bv: jnp.where(bv[i] > 0, f, nf - 1)
    grid_spec = pltpu.PrefetchScalarGridSpec(
        num_scalar_prefetch=2,
        grid=(cap // tm, nf),
        in_specs=[pl.BlockSpec((tm * SUB, LANES), lambda i, f, be, bv: (i, 0)),
                  pl.BlockSpec((None, None, d, tf), lambda i, f, be, bv: (sj, be[i], 0, fsel(i, f, bv))),
                  pl.BlockSpec((None, None, d, tf), lambda i, f, be, bv: (sj, be[i], 0, fsel(i, f, bv))),
                  pl.BlockSpec((None, None, tf, d), lambda i, f, be, bv: (sj, be[i], fsel(i, f, bv), 0))],
        out_specs=pl.BlockSpec((tm * SUB, LANES), lambda i, f, be, bv: (i, 0)),
        scratch_shapes=[pltpu.VMEM((tm, d), BF16), pltpu.VMEM((tm, d), F32)],
    )
    return pl.pallas_call(
        _moe_ffn_kernel,
        grid_spec=grid_spec,
        out_shape=jax.ShapeDtypeStruct(xs.shape, F32),
        compiler_params=_cparams(("parallel", "arbitrary")),
        name="moe_ffn",
    )(block_expert, block_valid, xs, wg, wu, wd)


def _combine_kernel(slot_ref, next_slot_ref, route_ref, x_ref, gt_ref, fg_ref, ys_ref, o_ref, buf, sems, *, final_norm):
    tm = x_ref.shape[0]
    step = pl.program_id(0)
    cur = lax.rem(step, 2)

    def gather(idx_ref, b):
        def issue(r, j):
            for k in range(2):
                _row_copy(ys_ref, buf.at[b, k], idx_ref[0, k, r], r, sems.at[2 * b + k]).start(priority=j % 2)

        _for_rows(tm, issue)

    @pl.when(step == 0)
    def _():
        gather(slot_ref, cur)

    @pl.when(step + 1 < pl.num_programs(0))
    def _():
        gather(next_slot_ref, 1 - cur)

    for k in range(2):
        pltpu.make_async_copy(ys_ref.at[pl.ds(0, tm * SUB)], buf.at[cur, k], sems.at[2 * cur + k]).wait()
    rec = route_ref[...]
    g1 = rec[:, ROUTE_G1:ROUTE_G1 + 1]
    g2 = rec[:, ROUTE_G2:ROUTE_G2 + 1]
    y = _from_row_tiles(buf.at[cur, 0]) * g1 + _from_row_tiles(buf.at[cur, 1]) * g2
    out = x_ref[...] + gt_ref[...] * y
    if final_norm:
        out = out * lax.rsqrt(jnp.mean(out * out, axis=-1, keepdims=True) + EPS) * fg_ref[...]
    o_ref[...] = out


def _combine(ys, slots, route, x, mods, final_g, layer, tiles_per_batch, ctx, final_norm, tm):
    t, d = x.shape
    mi = _mod_index(layer, 5, tiles_per_batch // tm if not ctx else 1, ctx)
    n_tiles = t // tm
    return pl.pallas_call(
        functools.partial(_combine_kernel, final_norm=final_norm),
        grid=(n_tiles,),
        in_specs=[pl.BlockSpec((1, 2, tm), lambda i: (i, 0, 0), memory_space=pltpu.SMEM),
                  pl.BlockSpec((1, 2, tm), lambda i: (jnp.minimum(i + 1, n_tiles - 1), 0, 0), memory_space=pltpu.SMEM),
                  pl.BlockSpec((tm, LANES), lambda i: (i, 0)),
                  pl.BlockSpec((tm, d), lambda i: (i, 0)),
                  _mod_spec(d, mi),
                  pl.BlockSpec((1, d), lambda i: (0, 0)),
                  pl.BlockSpec(memory_space=pl.ANY)],
        out_specs=pl.BlockSpec((tm, d), lambda i: (i, 0)),
        out_shape=jax.ShapeDtypeStruct((t, d), F32),
        scratch_shapes=[pltpu.VMEM((2, 2, tm * SUB, LANES), F32), pltpu.SemaphoreType.DMA((4,))],
        compiler_params=_cparams(("arbitrary",)),
        name="moe_combine",
    )(slots, slots, route, x, mods, final_g.reshape(1, d), ys)


def _moe(x, mods, g_all, w_router, wg, wu, wd, final_g, layer, tiles_per_batch, ctx, final_norm):
    t, d = x.shape
    group = MOE_TM if 2 * t >= 4 * N_EXPERTS * MOE_TM else MOE_TM // 2
    row_tm = min(ROW_TM, t)
    route, idx, counts = _router(x, mods, g_all, w_router, layer, tiles_per_batch, ctx, row_tm)
    counts = counts[0, :N_EXPERTS].astype(jnp.int32)
    groups = (counts + group - 1) // group
    ends = jnp.cumsum(groups)
    starts = (ends - groups) * group
    n_groups = (2 * t + group - 1) // group + N_EXPERTS
    tab = jnp.zeros((TAB_SIZE,), jnp.int32)
    tab = tab.at[TAB_START:TAB_START + N_EXPERTS].set(starts).at[TAB_COUNT:TAB_COUNT + N_EXPERTS].set(counts)
    tab = tab.at[TAB_USED].set(ends[-1])
    gi = jnp.arange(n_groups, dtype=jnp.int32)
    block_expert = jnp.minimum(jnp.sum(gi[:, None] >= ends[None, :], axis=1), N_EXPERTS - 1).astype(jnp.int32)
    block_valid = (gi < ends[-1]).astype(jnp.int32)
    def slot_rows(e, r):
        base = functools.reduce(jnp.add, [jnp.where(e == k, starts[k], 0) for k in range(N_EXPERTS)])
        return base + r
    slots = jnp.stack([slot_rows(idx[:, ROUTE_I1, :], idx[:, ROUTE_R1, :]),
                       slot_rows(idx[:, ROUTE_I2, :], idx[:, ROUTE_R2, :])], axis=1)
    xs = _dispatch(x, tab, slots, mods, g_all, layer, tiles_per_batch, ctx, n_groups, group, row_tm)
    ys = _moe_ffn(xs, block_expert, block_valid, wg, wu, wd, layer, group)
    return _combine(ys, slots, route, x, mods, final_g, layer, tiles_per_batch, ctx, final_norm, row_tm)


def kernel(x, c, ctx, c_ctx, ada_w, ada_b, norm_mix_g, norm_ffn_g, fnet_w_out, attn_w_qkv, attn_w_o, attn_sinks,
           ffn_w_gate, ffn_w_up, ffn_w_down, moe_w_router, moe_w_gate, moe_w_up, moe_w_down, final_norm_g):
    n_batch, s, d = x.shape
    n_ctx = ctx.shape[1]
    depth = ada_w.shape[0]
    mods = _ada(c, c_ctx, ada_w, ada_b)
    g_mix = norm_mix_g.reshape(depth, 1, d)
    g_ffn = norm_ffn_g.reshape(depth, 1, d)
    rope_tabs = _rope_tables(s)
    x_lat = x.reshape(n_batch * s, d)
    x_ctx = ctx.reshape(n_batch * n_ctx, d)
    bf = lambda a: a.astype(BF16)
    w_out, w_qkv, w_o = bf(fnet_w_out), bf(attn_w_qkv), bf(attn_w_o)
    wg, wu, wd = bf(ffn_w_gate), bf(ffn_w_up), bf(ffn_w_down)
    eg, eu, ed = bf(moe_w_gate), bf(moe_w_up), bf(moe_w_down)
    for i in range(depth):
        j = i // 2
        last = i == depth - 1
        if i % 2 == 0:
            x_lat = _fnet_lat(x_lat, mods, g_mix, w_out, i, n_batch)
            if not last:
                x_ctx = _fnet_ctx(x_ctx, mods, g_mix, w_out, i, n_batch)
            x_lat = _ffn_dense(x_lat, mods, g_ffn, wg, wu, wd, i, s, False)
            if not last:
                x_ctx = _ffn_dense(x_ctx, mods, g_ffn, wg, wu, wd, i, n_ctx, True)
        else:
            q, kt, v = _qkv(x_lat, mods, g_mix, w_qkv, i, n_batch, False, rope_tabs)
            qx, ktx, vx = _qkv(x_ctx, mods, g_mix, w_qkv, i, n_batch, True)
            o = _attn_lat(q, kt, v, ktx, vx, attn_sinks[j], n_batch)
            x_lat = _proj_res(o, w_o, x_lat, mods, i, 2, s, False)
            if not last:
                ox = _attn_ctx(qx, ktx, vx, attn_sinks[j], n_batch)
                x_ctx = _proj_res(ox, w_o, x_ctx, mods, i, 2, n_ctx, True)
            x_lat = _moe(x_lat, mods, g_ffn, moe_w_router[j], eg, eu, ed, final_norm_g, i, s, False, last)
            if not last:
                x_ctx = _moe(x_ctx, mods, g_ffn, moe_w_router[j], eg, eu, ed, final_norm_g, i, n_ctx, True, False)
    return x_lat.reshape(n_batch, s, d)
```

```python
import functools

import numpy as np
import jax
import jax.numpy as jnp
from jax import lax
from jax.experimental import pallas as pl
from jax.experimental.pallas import tpu as pltpu

F32 = jnp.float32
BF16 = jnp.bfloat16

GRID_W = 64
N_MOD = 6
EPS = 1e-6
FNET_GROUPS = 4
HEAD_DIM = 64
N_KV_HEADS = 4
WINDOW = 128
ROPE_THETA = 10000.0
ROT_FREQS = HEAD_DIM // 4
N_EXPERTS = 8
MOD_ROWS = 8
CTX_ROW = MOD_ROWS - 1
LANES = 128
NEG = -1e30
VMEM_LIMIT = 56 * 1024 * 1024

FFN_TM = 512
FFN_TF = 1792
MOE_TM = 512
MOE_TF = 1792
ROW_TM = 512
PROJ_TM = 1024
QKV_TM = 1024
DFT_P = 64
DFT_R = 128
FNET_RB = 16
FNET_KB = 8


def _cparams(sem):
    return pltpu.CompilerParams(dimension_semantics=sem, vmem_limit_bytes=VMEM_LIMIT)


def _sigmoid(a):
    return 1.0 / (1.0 + jnp.exp(-a))


def _norm_mod(x, g, shift, scale):
    xn = x * lax.rsqrt(jnp.mean(x * x, axis=-1, keepdims=True) + EPS)
    return (xn * g) * (1.0 + scale) + shift


def _mod_spec(d, idx_fn):
    return pl.BlockSpec((None, 1, d), lambda *ids: (idx_fn(*ids), 0, 0))


def _mod_index(layer, j, tiles_per_batch, ctx):
    def fn(i, *_):
        b = CTX_ROW if ctx else i // tiles_per_batch
        return (layer * MOD_ROWS + b) * N_MOD + j
    return fn


def _ada_kernel(c_ref, w_ref, b_ref, o_ref):
    cc = c_ref[...]
    s = cc * _sigmoid(cc)
    o_ref[...] = jnp.dot(s, w_ref[...], preferred_element_type=F32,
                         precision=lax.Precision.HIGHEST) + b_ref[...]


def _ada(c, c_ctx, ada_w, ada_b):
    depth, d, _ = ada_w.shape
    nb = c.shape[0]
    assert nb < MOD_ROWS
    cc = jnp.concatenate([c, jnp.zeros((CTX_ROW - nb, d), F32), c_ctx[None]], axis=0)
    out = pl.pallas_call(
        _ada_kernel,
        grid=(depth, N_MOD),
        in_specs=[pl.BlockSpec((MOD_ROWS, d), lambda l, j: (0, 0)),
                  pl.BlockSpec((None, d, d), lambda l, j: (l, 0, j)),
                  pl.BlockSpec((None, 1, d), lambda l, j: (l * N_MOD + j, 0, 0))],
        out_specs=pl.BlockSpec((None, MOD_ROWS, d), lambda l, j: (l, 0, j)),
        out_shape=jax.ShapeDtypeStruct((depth, MOD_ROWS, N_MOD * d), F32),
        compiler_params=_cparams(("parallel", "parallel")),
        name="ada",
    )(cc, ada_w, ada_b.reshape(depth * N_MOD, 1, d))
    return out.reshape(depth * MOD_ROWS * N_MOD, 1, d)


CAST_ROWS = 512


def _cast_kernel(w_ref, o_ref):
    o_ref[...] = w_ref[...].astype(BF16)


def _to_bf16(w):
    cols = w.shape[-1]
    w2 = w.reshape(-1, cols)
    rows = w2.shape[0]
    tr = min(CAST_ROWS, rows)
    assert rows % tr == 0
    out = pl.pallas_call(
        _cast_kernel,
        grid=(rows // tr,),
        in_specs=[pl.BlockSpec((tr, cols), lambda i: (i, 0))],
        out_specs=pl.BlockSpec((tr, cols), lambda i: (i, 0)),
        out_shape=jax.ShapeDtypeStruct((rows, cols), BF16),
        compiler_params=_cparams(("parallel",)),
        name="cast_bf16",
    )(w2)
    return out.reshape(w.shape)


def _ffn_kernel(x_ref, g_ref, sh_ref, sc_ref, gt_ref, wg_ref, wu_ref, wd_ref, o_ref, h_scr, acc_scr):
    f = pl.program_id(1)

    @pl.when(f == 0)
    def _():
        h_scr[...] = _norm_mod(x_ref[...], g_ref[...], sh_ref[...], sc_ref[...]).astype(BF16)
        acc_scr[...] = jnp.zeros_like(acc_scr)

    h = h_scr[...]
    a = jnp.dot(h, wg_ref[...], preferred_element_type=F32)
    u = jnp.dot(h, wu_ref[...], preferred_element_type=F32)
    t = (a * _sigmoid(a)) * u
    acc_scr[...] += jnp.dot(t.astype(BF16), wd_ref[...], preferred_element_type=F32)

    @pl.when(f == pl.num_programs(1) - 1)
    def _():
        o_ref[...] = x_ref[...] + gt_ref[...] * acc_scr[...]


def _ffn_dense(x, mods, g_all, wg, wu, wd, layer, tiles_per_batch, ctx, tm=FFN_TM, tf=FFN_TF):
    t, d = x.shape
    ff = wg.shape[2]
    tm = min(tm, t)
    tf = min(tf, ff)
    sj = layer // 2
    mi = functools.partial(_mod_index, layer, tiles_per_batch=tiles_per_batch // tm if not ctx else 1, ctx=ctx)
    return pl.pallas_call(
        _ffn_kernel,
        grid=(t // tm, ff // tf),
        in_specs=[pl.BlockSpec((tm, d), lambda i, f: (i, 0)),
                  pl.BlockSpec((None, 1, d), lambda i, f: (layer, 0, 0)),
                  _mod_spec(d, mi(3)), _mod_spec(d, mi(4)), _mod_spec(d, mi(5)),
                  pl.BlockSpec((None, d, tf), lambda i, f: (sj, 0, f)),
                  pl.BlockSpec((None, d, tf), lambda i, f: (sj, 0, f)),
                  pl.BlockSpec((None, tf, d), lambda i, f: (sj, f, 0))],
        out_specs=pl.BlockSpec((tm, d), lambda i, f: (i, 0)),
        out_shape=jax.ShapeDtypeStruct((t, d), F32),
        scratch_shapes=[pltpu.VMEM((tm, d), BF16), pltpu.VMEM((tm, d), F32)],
        compiler_params=_cparams(("parallel", "arbitrary")),
        name="ffn_dense",
    )(x, g_all, mods, mods, mods, wg, wu, wd)


def _proj_res_kernel(a_ref, w_ref, x_ref, gt_ref, o_ref):
    y = jnp.dot(a_ref[...], w_ref[...], preferred_element_type=F32)
    o_ref[...] = x_ref[...] + gt_ref[...] * y


def _proj_res(a, w, x, mods, layer, j, tiles_per_batch, ctx, tm=PROJ_TM):
    t, d = x.shape
    k = a.shape[1]
    tm = min(tm, t)
    mi = _mod_index(layer, j, tiles_per_batch // tm if not ctx else 1, ctx)
    return pl.pallas_call(
        _proj_res_kernel,
        grid=(t // tm,),
        in_specs=[pl.BlockSpec((tm, k), lambda i: (i, 0)),
                  pl.BlockSpec((None, k, d), lambda i: (layer // 2, 0, 0)),
                  pl.BlockSpec((tm, d), lambda i: (i, 0)),
                  _mod_spec(d, mi)],
        out_specs=pl.BlockSpec((tm, d), lambda i: (i, 0)),
        out_shape=jax.ShapeDtypeStruct((t, d), F32),
        compiler_params=_cparams(("parallel",)),
        name="proj_res",
    )(a, w, x, mods)


def _dft_angles(n):
    a = np.arange(n)
    return 2.0 * np.pi * ((a[:, None] * a[None, :]) % n) / n


def _seq_dft_tables(n):
    kp = np.arange(DFT_P)[None, :, None]
    p = np.arange(DFT_P)[None, None, :]
    r = np.arange(DFT_R)[:, None, None]
    th = 2.0 * np.pi * ((kp * (DFT_R * p + r)) % n) / n
    tab1 = np.stack([np.cos(th), -np.sin(th)], axis=1)
    tab1 = tab1.reshape(DFT_R // FNET_RB, FNET_RB, 2, DFT_P, DFT_P)
    kron = np.zeros((DFT_R // FNET_RB, 2, DFT_P, FNET_RB, DFT_P, FNET_RB), np.float32)
    for rl in range(FNET_RB):
        kron[:, :, :, rl, :, rl] = tab1[:, rl]
    tab1 = kron.reshape(DFT_R // FNET_RB, 2 * DFT_P * FNET_RB, DFT_P * FNET_RB)
    th2 = _dft_angles(DFT_R)
    c2, s2 = np.cos(th2), np.sin(th2)
    tab2 = np.block([[c2, s2], [-s2, c2]])
    return jnp.asarray(tab1, BF16), jnp.asarray(tab2, BF16)


def _chan_dft_tables(group_dim):
    th = _dft_angles(group_dim)
    return jnp.asarray(np.cos(th), BF16), jnp.asarray(np.sin(th), BF16)


def _ctx_dft_table(n_ctx):
    th = _dft_angles(n_ctx)
    return jnp.asarray(np.concatenate([np.cos(th), -np.sin(th)], axis=0), BF16)


def _cols_store(scr, val):
    for c in range(scr.shape[0]):
        scr[c] = val[:, c * LANES:(c + 1) * LANES]


def _cols_load(scr):
    return jnp.concatenate([scr[c] for c in range(scr.shape[0])], axis=-1)


def _cols_store_rows(scr, sel, val):
    for c in range(scr.shape[0]):
        scr[c, sel, :] = val[:, c * LANES:(c + 1) * LANES]


def _cols_load_rows(scr, sel):
    return jnp.concatenate([scr[c, sel, :] for c in range(scr.shape[0])], axis=-1)


def _fnet_stage1_kernel(x_ref, g_ref, sh_ref, sc_ref, tab_ref, zr_ref, zi_ref):
    d = g_ref.shape[-1]
    rows = DFT_P * FNET_RB
    h = _norm_mod(x_ref[...].reshape(rows, d), g_ref[...], sh_ref[...], sc_ref[...]).astype(BF16)
    z = jnp.dot(tab_ref[...], h, preferred_element_type=F32)
    zr_ref[...] = z[:rows].astype(BF16).reshape(DFT_P, FNET_RB, d)
    zi_ref[...] = z[rows:].astype(BF16).reshape(DFT_P, FNET_RB, d)


def _mix_tail(pr, pi, cc_ref, sc_ref, wout_ref, scale):
    gd = cc_ref.shape[0]
    ys = []
    for g in range(pr.shape[1] // gd):
        ys.append(jnp.dot(pr[:, g * gd:(g + 1) * gd], cc_ref[...], preferred_element_type=F32)
                  + jnp.dot(pi[:, g * gd:(g + 1) * gd], sc_ref[...], preferred_element_type=F32))
    mixed = (jnp.concatenate(ys, axis=-1) * scale).astype(BF16)
    return jnp.dot(mixed, wout_ref[...], preferred_element_type=F32)


def _fnet_stage2_kernel(zr_ref, zi_ref, tab2_ref, cc_ref, sc_ref, wout_ref, x_ref, gt_ref, o_ref,
                        p_scr, x_scr, o_scr, *, scale):
    d = gt_ref.shape[-1]
    rows = DFT_R * FNET_KB
    for j in range(FNET_KB):
        z = jnp.concatenate([zr_ref[j * DFT_R:(j + 1) * DFT_R, :], zi_ref[j * DFT_R:(j + 1) * DFT_R, :]], axis=0)
        p = jnp.dot(tab2_ref[...], z, preferred_element_type=F32)
        p_scr[j * DFT_R:(j + 1) * DFT_R, :d] = p[:DFT_R].astype(BF16)
        p_scr[j * DFT_R:(j + 1) * DFT_R, d:] = p[DFT_R:].astype(BF16)
    y = _mix_tail(p_scr[:, :d], p_scr[:, d:], cc_ref, sc_ref, wout_ref, scale)
    _cols_store(x_scr, x_ref[...].reshape(rows, d))
    for j in range(FNET_KB):
        sel = pl.ds(j, DFT_R, stride=FNET_KB)
        _cols_store_rows(o_scr, sel, _cols_load_rows(x_scr, sel) + gt_ref[...] * y[j * DFT_R:(j + 1) * DFT_R, :])
    o_ref[...] = _cols_load(o_scr).reshape(DFT_R, FNET_KB, d)


def _fnet_lat(x, mods, g_all, w_out, layer, n_batch):
    t, d = x.shape
    s = t // n_batch
    assert s == DFT_P * DFT_R
    tab1, tab2 = _seq_dft_tables(s)
    ccos, csin = _chan_dft_tables(d // FNET_GROUPS)
    mi = lambda j: (lambda b, *_: (layer * MOD_ROWS + b) * N_MOD + j)
    xv = x.reshape(n_batch, DFT_P, DFT_R, d)
    blk1 = (None, DFT_P, FNET_RB, d)
    rows1 = DFT_P * FNET_RB
    mi1 = lambda j: (lambda r, b: (layer * MOD_ROWS + b) * N_MOD + j)
    zr, zi = pl.pallas_call(
        _fnet_stage1_kernel,
        grid=(DFT_R // FNET_RB, n_batch),
        in_specs=[pl.BlockSpec(blk1, lambda r, b: (b, 0, r, 0)),
                  pl.BlockSpec((None, 1, d), lambda r, b: (layer, 0, 0)),
                  _mod_spec(d, mi1(0)), _mod_spec(d, mi1(1)),
                  pl.BlockSpec((None, 2 * rows1, rows1), lambda r, b: (r, 0, 0))],
        out_specs=[pl.BlockSpec(blk1, lambda r, b: (b, 0, r, 0))] * 2,
        out_shape=[jax.ShapeDtypeStruct((n_batch, DFT_P, DFT_R, d), BF16)] * 2,
        compiler_params=_cparams(("parallel", "parallel")),
        name="fnet_stage1",
    )(xv, g_all, mods, mods, tab1)
    zr = zr.reshape(n_batch, DFT_P * DFT_R, d)
    zi = zi.reshape(n_batch, DFT_P * DFT_R, d)
    xo = x.reshape(n_batch, DFT_R, DFT_P, d)
    blk2 = (None, DFT_R, FNET_KB, d)
    rows = FNET_KB * DFT_R
    scale = float(1.0 / np.sqrt(float(s) * (d // FNET_GROUPS)))
    out = pl.pallas_call(
        functools.partial(_fnet_stage2_kernel, scale=scale),
        grid=(n_batch, DFT_P // FNET_KB),
        in_specs=[pl.BlockSpec((None, rows, d), lambda b, k: (b, k, 0)),
                  pl.BlockSpec((None, rows, d), lambda b, k: (b, k, 0)),
                  pl.BlockSpec((2 * DFT_R, 2 * DFT_R), lambda b, k: (0, 0)),
                  pl.BlockSpec(ccos.shape, lambda b, k: (0, 0)),
                  pl.BlockSpec(csin.shape, lambda b, k: (0, 0)),
                  pl.BlockSpec((None, d, d), lambda b, k: (layer // 2, 0, 0)),
                  pl.BlockSpec(blk2, lambda b, k: (b, 0, k, 0)),
                  _mod_spec(d, mi(2))],
        out_specs=pl.BlockSpec(blk2, lambda b, k: (b, 0, k, 0)),
        out_shape=jax.ShapeDtypeStruct((n_batch, DFT_R, DFT_P, d), F32),
        scratch_shapes=[pltpu.VMEM((rows, 2 * d), BF16)] + [pltpu.VMEM((d // LANES, rows, LANES), F32)] * 2,
        compiler_params=_cparams(("parallel", "parallel")),
        name="fnet_stage2",
    )(zr, zi, tab2, ccos, csin, w_out, xo, mods)
    return out.reshape(t, d)


def _fnet_ctx_kernel(x_ref, g_ref, sh_ref, sc_ref, gt_ref, tab_ref, cc_ref, sc2_ref, wout_ref, o_ref, *, scale):
    n = x_ref.shape[0]
    h = _norm_mod(x_ref[...], g_ref[...], sh_ref[...], sc_ref[...]).astype(BF16)
    p = jnp.dot(tab_ref[...], h, preferred_element_type=F32)
    y = _mix_tail(p[:n].astype(BF16), p[n:].astype(BF16), cc_ref, sc2_ref, wout_ref, scale)
    o_ref[...] = x_ref[...] + gt_ref[...] * y


def _fnet_ctx(x, mods, g_all, w_out, layer, n_batch):
    t, d = x.shape
    n = t // n_batch
    gd = d // FNET_GROUPS
    ccos, csin = _chan_dft_tables(gd)
    tab = _ctx_dft_table(n)
    mi = lambda j: (lambda b: (layer * MOD_ROWS + CTX_ROW) * N_MOD + j)
    scale = float(1.0 / np.sqrt(float(n) * gd))
    return pl.pallas_call(
        functools.partial(_fnet_ctx_kernel, scale=scale),
        grid=(n_batch,),
        in_specs=[pl.BlockSpec((n, d), lambda b: (b, 0)),
                  pl.BlockSpec((None, 1, d), lambda b: (layer, 0, 0)),
                  _mod_spec(d, mi(0)), _mod_spec(d, mi(1)), _mod_spec(d, mi(2)),
                  pl.BlockSpec(tab.shape, lambda b: (0, 0)),
                  pl.BlockSpec(ccos.shape, lambda b: (0, 0)),
                  pl.BlockSpec(csin.shape, lambda b: (0, 0)),
                  pl.BlockSpec((None, d, d), lambda b: (layer // 2, 0, 0))],
        out_specs=pl.BlockSpec((n, d), lambda b: (b, 0)),
        out_shape=jax.ShapeDtypeStruct((t, d), F32),
        compiler_params=_cparams(("parallel",)),
        name="fnet_ctx",
    )(x, g_all, mods, mods, mods, tab, ccos, csin, w_out)


def _rope_tables(n_seq):
    rows = n_seq // GRID_W
    row = jnp.repeat(jnp.arange(rows, dtype=F32), GRID_W)
    col = jnp.tile(jnp.arange(GRID_W, dtype=F32), rows)
    inv_freq = ROPE_THETA ** (-jnp.arange(ROT_FREQS, dtype=F32) / ROT_FREQS)
    ang = jnp.stack([row[:, None] * inv_freq, col[:, None] * inv_freq], axis=1)
    cos, sin = jnp.cos(ang), jnp.sin(ang)
    zero = jnp.zeros_like(sin)
    cos_h = jnp.stack([cos, cos], axis=2).reshape(n_seq, HEAD_DIM)
    sin_lo = jnp.stack([-sin, zero], axis=2).reshape(n_seq, HEAD_DIM)
    sin_hi = jnp.stack([zero, sin], axis=2).reshape(n_seq, HEAD_DIM)
    rep = LANES // HEAD_DIM
    return jnp.tile(cos_h, (1, rep)), jnp.tile(sin_lo, (1, rep)), jnp.tile(sin_hi, (1, rep))


def _qkv_kernel(x_ref, g_ref, sh_ref, sc_ref, w_ref, *rest, rope, q_dim, kv_dim, q_scale):
    if rope:
        cos_ref, slo_ref, shi_ref, q_ref, kt_ref, v_ref = rest
    else:
        q_ref, kt_ref, v_ref = rest
    h = _norm_mod(x_ref[...], g_ref[...], sh_ref[...], sc_ref[...]).astype(BF16)
    qkv = jnp.dot(h, w_ref[...], preferred_element_type=F32)

    def rot(xs):
        if not rope:
            return xs
        return (xs * cos_ref[...] + pltpu.roll(xs, LANES - ROT_FREQS, axis=1) * slo_ref[...]
                + pltpu.roll(xs, ROT_FREQS, axis=1) * shi_ref[...])

    for j in range(q_dim // LANES):
        q_ref[:, j * LANES:(j + 1) * LANES] = (rot(qkv[:, j * LANES:(j + 1) * LANES]) * q_scale).astype(BF16)
    ks = [rot(qkv[:, q_dim + j * LANES:q_dim + (j + 1) * LANES]) for j in range(kv_dim // LANES)]
    kt_ref[...] = jnp.concatenate(ks, axis=-1).T.astype(BF16)
    v_ref[...] = qkv[:, q_dim + kv_dim:].astype(BF16)


def _qkv(x, mods, g_all, w_qkv, layer, n_batch, ctx, rope_tabs=None, tm=QKV_TM):
    t, d = x.shape
    n = t // n_batch
    tm = min(tm, n)
    tpb = n // tm
    kv_dim = N_KV_HEADS * HEAD_DIM
    q_dim = w_qkv.shape[2] - 2 * kv_dim
    mi = lambda j: _mod_index(layer, j, tpb, ctx)
    in_specs = [pl.BlockSpec((tm, d), lambda i: (i, 0)),
                pl.BlockSpec((None, 1, d), lambda i: (layer, 0, 0)),
                _mod_spec(d, mi(0)), _mod_spec(d, mi(1)),
                pl.BlockSpec((None,) + w_qkv.shape[1:], lambda i: (layer // 2, 0, 0))]
    args = [x, g_all, mods, mods, w_qkv]
    rope = rope_tabs is not None
    if rope:
        in_specs += [pl.BlockSpec((tm, LANES), lambda i: (i % tpb, 0))] * 3
        args += list(rope_tabs)
    return pl.pallas_call(
        functools.partial(_qkv_kernel, rope=rope, q_dim=q_dim, kv_dim=kv_dim, q_scale=HEAD_DIM ** -0.5),
        grid=(t // tm,),
        in_specs=in_specs,
        out_specs=[pl.BlockSpec((tm, q_dim), lambda i: (i, 0)),
                   pl.BlockSpec((None, kv_dim, tm), lambda i: (i // tpb, 0, i % tpb)),
                   pl.BlockSpec((tm, kv_dim), lambda i: (i, 0))],
        out_shape=[jax.ShapeDtypeStruct((t, q_dim), BF16),
                   jax.ShapeDtypeStruct((n_batch, kv_dim, n), BF16),
                   jax.ShapeDtypeStruct((t, kv_dim), BF16)],
        compiler_params=_cparams(("parallel",)),
        name="qkv_ctx" if ctx else "qkv_lat",
    )(*args)


ATTN_RB = 64


def _heads_attend(q_ref, kt, v, bias, sink_ref, o_ref):
    nq = q_ref.shape[0]
    n_heads = q_ref.shape[1] // HEAD_DIM
    group = n_heads // N_KV_HEADS
    outs = [None] * n_heads
    for g in range(N_KV_HEADS):
        heads = range(g * group, (g + 1) * group)
        qg = jnp.concatenate([q_ref[:, hd * HEAD_DIM:(hd + 1) * HEAD_DIM] for hd in heads], axis=0)
        s = jnp.dot(qg, kt[g * HEAD_DIM:(g + 1) * HEAD_DIM, :], preferred_element_type=F32)
        p_rows, den_rows = [], []
        for r0 in range(0, s.shape[0], ATTN_RB):
            rows = slice(r0, r0 + ATTN_RB)
            parts = [s[rows, k * LANES:(k + 1) * LANES] for k in range(s.shape[1] // LANES)]
            if bias is not None:
                parts[0] = parts[0] + bias[0][rows]
                parts[2] = parts[2] + bias[1][rows]
            sink = sink_ref[heads[r0 // nq]]
            m = jnp.maximum(jnp.max(functools.reduce(jnp.maximum, parts), axis=-1, keepdims=True), sink)
            ps = [jnp.exp(part - m) for part in parts]
            den_rows.append(jnp.sum(functools.reduce(jnp.add, ps), axis=-1, keepdims=True) + jnp.exp(sink - m))
            p_rows.append(jnp.concatenate(ps, axis=-1).astype(BF16))
        pv = jnp.dot(jnp.concatenate(p_rows, axis=0), v, preferred_element_type=F32)
        on = pv[:, g * HEAD_DIM:(g + 1) * HEAD_DIM] / jnp.concatenate(den_rows, axis=0)
        for k, hd in enumerate(heads):
            outs[hd] = on[k * nq:(k + 1) * nq, :]
    o_ref[...] = jnp.concatenate(outs, axis=-1).astype(BF16)


def _attn_lat_kernel(sink_ref, q_ref, ktp_ref, ktc_ref, ktn_ref, vp_ref, vc_ref, vn_ref, ktx_ref, vx_ref,
                     blo_ref, bhi_ref, o_ref, kt_scr, v_scr):
    w = ktc_ref.shape[1]
    n_ctx = vx_ref.shape[0]
    for c, (kr, vr) in enumerate(((ktp_ref, vp_ref), (ktc_ref, vc_ref), (ktn_ref, vn_ref))):
        kt_scr[:, c * w:(c + 1) * w] = kr[...]
        v_scr[c * w:(c + 1) * w, :] = vr[...]
    kt_scr[:, 3 * w:3 * w + n_ctx] = ktx_ref[...]
    v_scr[3 * w:3 * w + n_ctx, :] = vx_ref[...]
    _heads_attend(q_ref, kt_scr[...], v_scr[...], (blo_ref[...], bhi_ref[...]), sink_ref, o_ref)


def _attn_lat(q, kt, v, ktx, vx, sinks, n_batch):
    t, qd = q.shape
    s = t // n_batch
    n_ctx = vx.shape[0] // n_batch
    w = WINDOW
    nb = s // w
    kvd = v.shape[1]
    j_all = 3 * w + n_ctx
    group = qd // HEAD_DIM // N_KV_HEADS
    qi = np.arange(w)[:, None]
    ki = np.arange(w)[None, :]
    lo = np.where(ki >= qi, 0.0, NEG).astype(np.float32)
    hi = np.where(ki <= qi, 0.0, NEG).astype(np.float32)
    off = np.full((w, w), NEG, np.float32)
    blo = jnp.asarray(np.stack([np.tile(lo, (group, 1)), np.tile(off, (group, 1))]))
    bhi = jnp.asarray(np.stack([np.tile(hi, (group, 1)), np.tile(off, (group, 1))]))
    prev = lambda b, i: (b, 0, jnp.maximum(i - 1, 0))
    nxt = lambda b, i: (b, 0, jnp.minimum(i + 1, nb - 1))
    vprev = lambda b, i: (b * nb + jnp.maximum(i - 1, 0), 0)
    vnxt = lambda b, i: (b * nb + jnp.minimum(i + 1, nb - 1), 0)
    return pl.pallas_call(
        _attn_lat_kernel,
        grid=(n_batch, nb),
        in_specs=[pl.BlockSpec(memory_space=pltpu.SMEM),
                  pl.BlockSpec((w, qd), lambda b, i: (b * nb + i, 0)),
                  pl.BlockSpec((None, kvd, w), prev),
                  pl.BlockSpec((None, kvd, w), lambda b, i: (b, 0, i)),
                  pl.BlockSpec((None, kvd, w), nxt),
                  pl.BlockSpec((w, kvd), vprev),
                  pl.BlockSpec((w, kvd), lambda b, i: (b * nb + i, 0)),
                  pl.BlockSpec((w, kvd), vnxt),
                  pl.BlockSpec((None, kvd, n_ctx), lambda b, i: (b, 0, 0)),
                  pl.BlockSpec((n_ctx, kvd), lambda b, i: (b, 0)),
                  pl.BlockSpec((None, group * w, w), lambda b, i: (jnp.where(i == 0, 1, 0), 0, 0)),
                  pl.BlockSpec((None, group * w, w), lambda b, i: (jnp.where(i == nb - 1, 1, 0), 0, 0))],
        out_specs=pl.BlockSpec((w, qd), lambda b, i: (b * nb + i, 0)),
        out_shape=jax.ShapeDtypeStruct((t, qd), BF16),
        scratch_shapes=[pltpu.VMEM((kvd, j_all), BF16), pltpu.VMEM((j_all, kvd), BF16)],
        compiler_params=_cparams(("parallel", "parallel")),
        name="attn_lat",
    )(sinks, q, kt, kt, kt, v, v, v, ktx, vx, blo, bhi)


def _attn_ctx_kernel(sink_ref, q_ref, kt_ref, v_ref, o_ref):
    _heads_attend(q_ref, kt_ref[...], v_ref[...], None, sink_ref, o_ref)


def _attn_ctx(q, kt, v, sinks, n_batch):
    t, qd = q.shape
    n = t // n_batch
    kvd = v.shape[1]
    return pl.pallas_call(
        _attn_ctx_kernel,
        grid=(n_batch,),
        in_specs=[pl.BlockSpec(memory_space=pltpu.SMEM),
                  pl.BlockSpec((n, qd), lambda b: (b, 0)),
                  pl.BlockSpec((None, kvd, n), lambda b: (b, 0, 0)),
                  pl.BlockSpec((n, kvd), lambda b: (b, 0))],
        out_specs=pl.BlockSpec((n, qd), lambda b: (b, 0)),
        out_shape=jax.ShapeDtypeStruct((t, qd), BF16),
        compiler_params=_cparams(("parallel",)),
        name="attn_ctx",
    )(sinks, q, kt, v)


ROUTE_I1, ROUTE_I2, ROUTE_R1, ROUTE_R2, ROUTE_G1, ROUTE_G2 = range(6)
ROUTE_ROWS = 8
TAB_START, TAB_COUNT, TAB_USED, TAB_SIZE = 0, N_EXPERTS, 2 * N_EXPERTS, 2 * N_EXPERTS + 8
ROW_UNROLL = 8


def _router_kernel(x_ref, g_ref, sh_ref, sc_ref, wr_ref, tri_ref, route_ref, idx_ref, cnt_ref, carry_scr):
    @pl.when(pl.program_id(0) == 0)
    def _():
        carry_scr[...] = jnp.zeros_like(carry_scr)

    h = _norm_mod(x_ref[...], g_ref[...], sh_ref[...], sc_ref[...])
    hi = h.astype(BF16)
    lo = (h - hi.astype(F32)).astype(BF16)
    a = jnp.dot(hi, wr_ref[...], preferred_element_type=F32)
    logits = (a[:, :LANES] + a[:, LANES:]) + jnp.dot(lo, wr_ref[:, :LANES], preferred_element_type=F32)
    lane = lax.broadcasted_iota(jnp.int32, logits.shape, 1)
    lane_f = lane.astype(F32)
    logits = jnp.where(lane < N_EXPERTS, logits, -jnp.inf)
    m1 = jnp.max(logits, axis=-1, keepdims=True)
    i1 = jnp.min(jnp.where(logits == m1, lane_f, float(LANES)), axis=-1, keepdims=True)
    oh1 = lane_f == i1
    rest = jnp.where(oh1, -jnp.inf, logits)
    m2 = jnp.max(rest, axis=-1, keepdims=True)
    i2 = jnp.min(jnp.where(rest == m2, lane_f, float(LANES)), axis=-1, keepdims=True)
    oh2 = lane_f == i2
    e2 = jnp.exp(m2 - m1)
    g1 = 1.0 / (1.0 + e2)
    g2 = e2 / (1.0 + e2)
    sel = jnp.where(oh1, 1.0, 0.0) + jnp.where(oh2, 1.0, 0.0)
    before = jnp.dot(tri_ref[...], sel.astype(BF16), preferred_element_type=F32) + carry_scr[0:1, :]
    r1 = jnp.sum(jnp.where(oh1, before, 0.0), axis=-1, keepdims=True)
    r2 = jnp.sum(jnp.where(oh2, before, 0.0), axis=-1, keepdims=True)
    total = carry_scr[0:1, :] + jnp.sum(sel, axis=0, keepdims=True)
    carry_scr[...] = jnp.broadcast_to(total, carry_scr.shape)
    cnt_ref[...] = jnp.broadcast_to(total, cnt_ref.shape)
    rec = jnp.zeros_like(logits)
    for ln, val in ((ROUTE_I1, i1), (ROUTE_I2, i2), (ROUTE_R1, r1), (ROUTE_R2, r2), (ROUTE_G1, g1), (ROUTE_G2, g2)):
        rec = jnp.where(lane == ln, val, rec)
    route_ref[...] = rec
    idx_ref[...] = rec.T[:ROUTE_ROWS, :].astype(jnp.int32)


def _router(x, mods, g_all, w_router, layer, tiles_per_batch, ctx, tm):
    t, d = x.shape
    mi = lambda j: _mod_index(layer, j, tiles_per_batch // tm if not ctx else 1, ctx)
    wr = jnp.zeros((d, LANES), F32).at[:, :N_EXPERTS].set(w_router)
    whi = wr.astype(BF16)
    wr = jnp.concatenate([whi, (wr - whi.astype(F32)).astype(BF16)], axis=1)
    tri = jnp.asarray(np.tril(np.ones((tm, tm), np.float32), -1), BF16)
    return pl.pallas_call(
        _router_kernel,
        grid=(t // tm,),
        in_specs=[pl.BlockSpec((tm, d), lambda i: (i, 0)),
                  pl.BlockSpec((None, 1, d), lambda i: (layer, 0, 0)),
                  _mod_spec(d, mi(3)), _mod_spec(d, mi(4)),
                  pl.BlockSpec((d, 2 * LANES), lambda i: (0, 0)),
                  pl.BlockSpec((tm, tm), lambda i: (0, 0))],
        out_specs=[pl.BlockSpec((tm, LANES), lambda i: (i, 0)),
                   pl.BlockSpec((None, ROUTE_ROWS, tm), lambda i: (i, 0, 0)),
                   pl.BlockSpec((8, LANES), lambda i: (0, 0))],
        out_shape=[jax.ShapeDtypeStruct((t, LANES), F32),
                   jax.ShapeDtypeStruct((t // tm, ROUTE_ROWS, tm), jnp.int32),
                   jax.ShapeDtypeStruct((8, LANES), F32)],
        scratch_shapes=[pltpu.VMEM((8, LANES), F32)],
        compiler_params=_cparams(("arbitrary",)),
        name="router",
    )(x, g_all, mods, mods, wr, tri)


SUB = 8


def _to_row_tiles(ref, val):
    n = val.shape[0]
    for k in range(SUB):
        ref[pl.ds(k, n, stride=SUB), :] = val[:, k * LANES:(k + 1) * LANES]


def _from_row_tiles(ref):
    n = ref.shape[0] // SUB
    return jnp.concatenate([ref[pl.ds(k, n, stride=SUB), :] for k in range(SUB)], axis=-1)


def _row_copy(src, dst, src_row, dst_row, sem):
    return pltpu.make_async_copy(src.at[pl.ds(pl.multiple_of(src_row * SUB, SUB), SUB)],
                                 dst.at[pl.ds(pl.multiple_of(dst_row * SUB, SUB), SUB)], sem)


def _for_rows(tm, body):
    def blk(i, _):
        base = pl.multiple_of(i * ROW_UNROLL, ROW_UNROLL)
        for j in range(ROW_UNROLL):
            body(base + j, j)
        return 0

    lax.fori_loop(0, tm // ROW_UNROLL, blk, 0)


ZERO_SEM = 4


def _dispatch_kernel(tab_ref, slot_ref, x_ref, g_ref, sh_ref, sc_ref, xs_ref, h_scr, zero_scr, sems, *, group):
    tm = x_ref.shape[0]
    n_groups = xs_ref.shape[0] // (group * SUB)
    step = pl.program_id(0)
    n_steps = pl.num_programs(0)
    cur = lax.rem(step, 2)

    def wait_buffer(b):
        for k in range(2):
            pltpu.make_async_copy(h_scr.at[b], xs_ref.at[pl.ds(0, tm * SUB)], sems.at[2 * b + k]).wait()

    @pl.when(step >= 2)
    def _():
        wait_buffer(cur)

    _to_row_tiles(h_scr.at[cur], _norm_mod(x_ref[...], g_ref[...], sh_ref[...], sc_ref[...]))

    def issue(r, j):
        for k in range(2):
            _row_copy(h_scr.at[cur], xs_ref, r, slot_ref[0, k, r], sems.at[2 * cur + k]).start(priority=j % 2)

    _for_rows(tm, issue)

    @pl.when(jnp.logical_and(step == n_steps - 1, step >= 1))
    def _():
        wait_buffer(1 - cur)

    @pl.when(step == n_steps - 1)
    def _():
        wait_buffer(cur)
        zero_scr[...] = jnp.zeros_like(zero_scr)
        for e in range(N_EXPERTS):
            n = tab_ref[TAB_COUNT + e]
            n_pad = lax.rem(group - lax.rem(n, group), group)
            first = tab_ref[TAB_START + e] + n

            def fill(k, _, first=first):
                _row_copy(zero_scr, xs_ref, 0, first + k, sems.at[ZERO_SEM]).start()
                return 0

            def fill_done(k, _):
                _row_copy(zero_scr, xs_ref, 0, 0, sems.at[ZERO_SEM]).wait()
                return 0

            lax.fori_loop(0, n_pad, fill, 0)
            lax.fori_loop(0, n_pad, fill_done, 0)

        def clear(j, _):
            row = pl.multiple_of(j * (group * SUB), group * SUB)
            cp = pltpu.make_async_copy(zero_scr, xs_ref.at[pl.ds(row, group * SUB)], sems.at[ZERO_SEM])
            cp.start()
            cp.wait()
            return 0

        lax.fori_loop(tab_ref[TAB_USED], n_groups, clear, 0)


def _dispatch(x, tab, slots, mods, g_all, layer, tiles_per_batch, ctx, n_groups, group, tm):
    t, d = x.shape
    mi = lambda j: _mod_index(layer, j, tiles_per_batch // tm if not ctx else 1, ctx)
    return pl.pallas_call(
        functools.partial(_dispatch_kernel, group=group),
        grid=(t // tm,),
        in_specs=[pl.BlockSpec(memory_space=pltpu.SMEM),
                  pl.BlockSpec((1, 2, tm), lambda i: (i, 0, 0), memory_space=pltpu.SMEM),
                  pl.BlockSpec((tm, d), lambda i: (i, 0)),
                  pl.BlockSpec((None, 1, d), lambda i: (layer, 0, 0)),
                  _mod_spec(d, mi(3)), _mod_spec(d, mi(4))],
        out_specs=pl.BlockSpec(memory_space=pl.ANY),
        out_shape=jax.ShapeDtypeStruct((n_groups * group * SUB, LANES), F32),
        scratch_shapes=[pltpu.VMEM((2, tm * SUB, LANES), F32), pltpu.VMEM((group * SUB, LANES), F32),
                        pltpu.SemaphoreType.DMA((ZERO_SEM + 1,))],
        compiler_params=_cparams(("arbitrary",)),
        name="moe_dispatch",
    )(tab, slots, x, g_all, mods, mods)


def _moe_ffn_kernel(be_ref, bv_ref, xs_ref, wg_ref, wu_ref, wd_ref, ys_ref, h_scr, acc_scr):
    del be_ref
    i = pl.program_id(0)
    f = pl.program_id(1)
    last = f == pl.num_programs(1) - 1
    valid = bv_ref[i] > 0

    @pl.when(jnp.logical_and(valid, f == 0))
    def _():
        h_scr[...] = _from_row_tiles(xs_ref).astype(BF16)
        acc_scr[...] = jnp.zeros_like(acc_scr)

    @pl.when(valid)
    def _():
        h = h_scr[...]
        a = jnp.dot(h, wg_ref[...], preferred_element_type=F32)
        u = jnp.dot(h, wu_ref[...], preferred_element_type=F32)
        t = (a * _sigmoid(a)) * u
        acc_scr[...] += jnp.dot(t.astype(BF16), wd_ref[...], preferred_element_type=F32)

    @pl.when(jnp.logical_and(valid, last))
    def _():
        _to_row_tiles(ys_ref, acc_scr[...])

    @pl.when(jnp.logical_and(jnp.logical_not(valid), last))
    def _():
        ys_ref[...] = jnp.zeros_like(ys_ref)


def _moe_ffn(xs, block_expert, block_valid, wg, wu, wd, layer, tm, tf=MOE_TF):
    d, ff = wg.shape[2:]
    assert d == SUB * LANES and xs.shape[1] == LANES
    cap = xs.shape[0] // SUB
    tf = min(tf, ff)
    nf = ff // tf
    sj = layer // 2
    fsel = lambda i, f, bv: jnp.where(bv[i] > 0, f, nf - 1)
    grid_spec = pltpu.PrefetchScalarGridSpec(
        num_scalar_prefetch=2,
        grid=(cap // tm, nf),
        in_specs=[pl.BlockSpec((tm * SUB, LANES), lambda i, f, be, bv: (i, 0)),
                  pl.BlockSpec((None, None, d, tf), lambda i, f, be, bv: (sj, be[i], 0, fsel(i, f, bv))),
                  pl.BlockSpec((None, None, d, tf), lambda i, f, be, bv: (sj, be[i], 0, fsel(i, f, bv))),
                  pl.BlockSpec((None, None, tf, d), lambda i, f, be, bv: (sj, be[i], fsel(i, f, bv), 0))],
        out_specs=pl.BlockSpec((tm * SUB, LANES), lambda i, f, be, bv: (i, 0)),
        scratch_shapes=[pltpu.VMEM((tm, d), BF16), pltpu.VMEM((tm, d), F32)],
    )
    return pl.pallas_call(
        _moe_ffn_kernel,
        grid_spec=grid_spec,
        out_shape=jax.ShapeDtypeStruct(xs.shape, F32),
        compiler_params=_cparams(("parallel", "arbitrary")),
        name="moe_ffn",
    )(block_expert, block_valid, xs, wg, wu, wd)


def _combine_kernel(slot_ref, next_slot_ref, route_ref, x_ref, gt_ref, fg_ref, ys_ref, o_ref, buf, sems, *, final_norm):
    tm = x_ref.shape[0]
    step = pl.program_id(0)
    cur = lax.rem(step, 2)

    def gather(idx_ref, b):
        def issue(r, j):
            for k in range(2):
                _row_copy(ys_ref, buf.at[b, k], idx_ref[0, k, r], r, sems.at[2 * b + k]).start(priority=j % 2)

        _for_rows(tm, issue)

    @pl.when(step == 0)
    def _():
        gather(slot_ref, cur)

    @pl.when(step + 1 < pl.num_programs(0))
    def _():
        gather(next_slot_ref, 1 - cur)

    for k in range(2):
        pltpu.make_async_copy(ys_ref.at[pl.ds(0, tm * SUB)], buf.at[cur, k], sems.at[2 * cur + k]).wait()
    rec = route_ref[...]
    g1 = rec[:, ROUTE_G1:ROUTE_G1 + 1]
    g2 = rec[:, ROUTE_G2:ROUTE_G2 + 1]
    y = _from_row_tiles(buf.at[cur, 0]) * g1 + _from_row_tiles(buf.at[cur, 1]) * g2
    out = x_ref[...] + gt_ref[...] * y
    if final_norm:
        out = out * lax.rsqrt(jnp.mean(out * out, axis=-1, keepdims=True) + EPS) * fg_ref[...]
    o_ref[...] = out


def _combine(ys, slots, route, x, mods, final_g, layer, tiles_per_batch, ctx, final_norm, tm):
    t, d = x.shape
    mi = _mod_index(layer, 5, tiles_per_batch // tm if not ctx else 1, ctx)
    n_tiles = t // tm
    return pl.pallas_call(
        functools.partial(_combine_kernel, final_norm=final_norm),
        grid=(n_tiles,),
        in_specs=[pl.BlockSpec((1, 2, tm), lambda i: (i, 0, 0), memory_space=pltpu.SMEM),
                  pl.BlockSpec((1, 2, tm), lambda i: (jnp.minimum(i + 1, n_tiles - 1), 0, 0), memory_space=pltpu.SMEM),
                  pl.BlockSpec((tm, LANES), lambda i: (i, 0)),
                  pl.BlockSpec((tm, d), lambda i: (i, 0)),
                  _mod_spec(d, mi),
                  pl.BlockSpec((1, d), lambda i: (0, 0)),
                  pl.BlockSpec(memory_space=pl.ANY)],
        out_specs=pl.BlockSpec((tm, d), lambda i: (i, 0)),
        out_shape=jax.ShapeDtypeStruct((t, d), F32),
        scratch_shapes=[pltpu.VMEM((2, 2, tm * SUB, LANES), F32), pltpu.SemaphoreType.DMA((4,))],
        compiler_params=_cparams(("arbitrary",)),
        name="moe_combine",
    )(slots, slots, route, x, mods, final_g.reshape(1, d), ys)


def _moe(x, mods, g_all, w_router, wg, wu, wd, final_g, layer, tiles_per_batch, ctx, final_norm):
    t, d = x.shape
    group = MOE_TM if 2 * t >= 4 * N_EXPERTS * MOE_TM else MOE_TM // 2
    row_tm = min(ROW_TM, t)
    route, idx, counts = _router(x, mods, g_all, w_router, layer, tiles_per_batch, ctx, row_tm)
    counts = counts[0, :N_EXPERTS].astype(jnp.int32)
    groups = (counts + group - 1) // group
    ends = jnp.cumsum(groups)
    starts = (ends - groups) * group
    n_groups = (2 * t + group - 1) // group + N_EXPERTS
    tab = jnp.zeros((TAB_SIZE,), jnp.int32)
    tab = tab.at[TAB_START:TAB_START + N_EXPERTS].set(starts).at[TAB_COUNT:TAB_COUNT + N_EXPERTS].set(counts)
    tab = tab.at[TAB_USED].set(ends[-1])
    gi = jnp.arange(n_groups, dtype=jnp.int32)
    block_expert = jnp.minimum(jnp.sum(gi[:, None] >= ends[None, :], axis=1), N_EXPERTS - 1).astype(jnp.int32)
    block_valid = (gi < ends[-1]).astype(jnp.int32)
    def slot_rows(e, r):
        base = functools.reduce(jnp.add, [jnp.where(e == k, starts[k], 0) for k in range(N_EXPERTS)])
        return base + r
    slots = jnp.stack([slot_rows(idx[:, ROUTE_I1, :], idx[:, ROUTE_R1, :]),
                       slot_rows(idx[:, ROUTE_I2, :], idx[:, ROUTE_R2, :])], axis=1)
    xs = _dispatch(x, tab, slots, mods, g_all, layer, tiles_per_batch, ctx, n_groups, group, row_tm)
    ys = _moe_ffn(xs, block_expert, block_valid, wg, wu, wd, layer, group)
    return _combine(ys, slots, route, x, mods, final_g, layer, tiles_per_batch, ctx, final_norm, row_tm)


def kernel(x, c, ctx, c_ctx, ada_w, ada_b, norm_mix_g, norm_ffn_g, fnet_w_out, attn_w_qkv, attn_w_o, attn_sinks,
           ffn_w_gate, ffn_w_up, ffn_w_down, moe_w_router, moe_w_gate, moe_w_up, moe_w_down, final_norm_g):
    n_batch, s, d = x.shape
    n_ctx = ctx.shape[1]
    depth = ada_w.shape[0]
    mods = _ada(c, c_ctx, ada_w, ada_b)
    g_mix = norm_mix_g.reshape(depth, 1, d)
    g_ffn = norm_ffn_g.reshape(depth, 1, d)
    rope_tabs = _rope_tables(s)
    x_lat = x.reshape(n_batch * s, d)
    x_ctx = ctx.reshape(n_batch * n_ctx, d)
    bf = _to_bf16
    w_out, w_qkv, w_o = bf(fnet_w_out), bf(attn_w_qkv), bf(attn_w_o)
    wg, wu, wd = bf(ffn_w_gate), bf(ffn_w_up), bf(ffn_w_down)
    eg, eu, ed = bf(moe_w_gate), bf(moe_w_up), bf(moe_w_down)
    for i in range(depth):
        j = i // 2
        last = i == depth - 1
        if i % 2 == 0:
            x_lat = _fnet_lat(x_lat, mods, g_mix, w_out, i, n_batch)
            if not last:
                x_ctx = _fnet_ctx(x_ctx, mods, g_mix, w_out, i, n_batch)
            x_lat = _ffn_dense(x_lat, mods, g_ffn, wg, wu, wd, i, s, False)
            if not last:
                x_ctx = _ffn_dense(x_ctx, mods, g_ffn, wg, wu, wd, i, n_ctx, True)
        else:
            q, kt, v = _qkv(x_lat, mods, g_mix, w_qkv, i, n_batch, False, rope_tabs)
            qx, ktx, vx = _qkv(x_ctx, mods, g_mix, w_qkv, i, n_batch, True)
            o = _attn_lat(q, kt, v, ktx, vx, attn_sinks[j], n_batch)
            x_lat = _proj_res(o, w_o, x_lat, mods, i, 2, s, False)
            if not last:
                ox = _attn_ctx(qx, ktx, vx, attn_sinks[j], n_batch)
                x_ctx = _proj_res(ox, w_o, x_ctx, mods, i, 2, n_ctx, True)
            x_lat = _moe(x_lat, mods, g_ffn, moe_w_router[j], eg, eu, ed, final_norm_g, i, s, False, last)
            if not last:
                x_ctx = _moe(x_ctx, mods, g_ffn, moe_w_router[j], eg, eu, ed, final_norm_g, i, n_ctx, True, False)
    return x_lat.reshape(n_batch, s, d)
```

```python
import functools

import numpy as np
import jax
import jax.numpy as jnp
from jax import lax
from jax.experimental import pallas as pl
from jax.experimental.pallas import tpu as pltpu

F32 = jnp.float32
BF16 = jnp.bfloat16

GRID_W = 64
N_MOD = 6
EPS = 1e-6
FNET_GROUPS = 4
HEAD_DIM = 64
N_KV_HEADS = 4
WINDOW = 128
ROPE_THETA = 10000.0
ROT_FREQS = HEAD_DIM // 4
N_EXPERTS = 8
MOD_ROWS = 8
CTX_ROW = MOD_ROWS - 1
LANES = 128
NEG = -1e30
VMEM_LIMIT = 56 * 1024 * 1024

FFN_TM = 512
FFN_TF = 1792
MOE_TM = 512
MOE_TF = 1792
ROW_TM = 512
PROJ_TM = 1024
QKV_TM = 1024
DFT_P = 64
DFT_R = 128
FNET_RB = 16
FNET_KB = 8


def _cparams(sem):
    return pltpu.CompilerParams(dimension_semantics=sem, vmem_limit_bytes=VMEM_LIMIT)


def _sigmoid(a):
    return 1.0 / (1.0 + jnp.exp(-a))


def _norm_mod(x, g, shift, scale):
    xn = x * lax.rsqrt(jnp.mean(x * x, axis=-1, keepdims=True) + EPS)
    return (xn * g) * (1.0 + scale) + shift


def _mod_spec(d, idx_fn):
    return pl.BlockSpec((None, 1, d), lambda *ids: (idx_fn(*ids), 0, 0))


def _mod_index(layer, j, tiles_per_batch, ctx):
    def fn(i, *_):
        b = CTX_ROW if ctx else i // tiles_per_batch
        return (layer * MOD_ROWS + b) * N_MOD + j
    return fn


def _ada_kernel(c_ref, w_ref, b_ref, o_ref):
    cc = c_ref[...]
    s = cc * _sigmoid(cc)
    o_ref[...] = jnp.dot(s, w_ref[...], preferred_element_type=F32,
                         precision=lax.Precision.HIGHEST) + b_ref[...]


def _ada(c, c_ctx, ada_w, ada_b):
    depth, d, _ = ada_w.shape
    nb = c.shape[0]
    assert nb < MOD_ROWS
    cc = jnp.concatenate([c, jnp.zeros((CTX_ROW - nb, d), F32), c_ctx[None]], axis=0)
    out = pl.pallas_call(
        _ada_kernel,
        grid=(depth, N_MOD),
        in_specs=[pl.BlockSpec((MOD_ROWS, d), lambda l, j: (0, 0)),
                  pl.BlockSpec((None, d, d), lambda l, j: (l, 0, j)),
                  pl.BlockSpec((None, 1, d), lambda l, j: (l * N_MOD + j, 0, 0))],
        out_specs=pl.BlockSpec((None, MOD_ROWS, d), lambda l, j: (l, 0, j)),
        out_shape=jax.ShapeDtypeStruct((depth, MOD_ROWS, N_MOD * d), F32),
        compiler_params=_cparams(("parallel", "parallel")),
        name="ada",
    )(cc, ada_w, ada_b.reshape(depth * N_MOD, 1, d))
    return out.reshape(depth * MOD_ROWS * N_MOD, 1, d)


def _ffn_kernel(x_ref, g_ref, sh_ref, sc_ref, gt_ref, wg_ref, wu_ref, wd_ref, o_ref, h_scr, acc_scr):
    f = pl.program_id(1)

    @pl.when(f == 0)
    def _():
        h_scr[...] = _norm_mod(x_ref[...], g_ref[...], sh_ref[...], sc_ref[...]).astype(BF16)
        acc_scr[...] = jnp.zeros_like(acc_scr)

    h = h_scr[...]
    a = jnp.dot(h, wg_ref[...], preferred_element_type=F32)
    u = jnp.dot(h, wu_ref[...], preferred_element_type=F32)
    t = (a * _sigmoid(a)) * u
    acc_scr[...] += jnp.dot(t.astype(BF16), wd_ref[...], preferred_element_type=F32)

    @pl.when(f == pl.num_programs(1) - 1)
    def _():
        o_ref[...] = x_ref[...] + gt_ref[...] * acc_scr[...]


def _ffn_dense(x, mods, g_all, wg, wu, wd, layer, tiles_per_batch, ctx, tm=FFN_TM, tf=FFN_TF):
    t, d = x.shape
    ff = wg.shape[2]
    tm = min(tm, t)
    tf = min(tf, ff)
    sj = layer // 2
    mi = functools.partial(_mod_index, layer, tiles_per_batch=tiles_per_batch // tm if not ctx else 1, ctx=ctx)
    return pl.pallas_call(
        _ffn_kernel,
        grid=(t // tm, ff // tf),
        in_specs=[pl.BlockSpec((tm, d), lambda i, f: (i, 0)),
                  pl.BlockSpec((None, 1, d), lambda i, f: (layer, 0, 0)),
                  _mod_spec(d, mi(3)), _mod_spec(d, mi(4)), _mod_spec(d, mi(5)),
                  pl.BlockSpec((None, d, tf), lambda i, f: (sj, 0, f)),
                  pl.BlockSpec((None, d, tf), lambda i, f: (sj, 0, f)),
                  pl.BlockSpec((None, tf, d), lambda i, f: (sj, f, 0))],
        out_specs=pl.BlockSpec((tm, d), lambda i, f: (i, 0)),
        out_shape=jax.ShapeDtypeStruct((t, d), F32),
        scratch_shapes=[pltpu.VMEM((tm, d), BF16), pltpu.VMEM((tm, d), F32)],
        compiler_params=_cparams(("parallel", "arbitrary")),
        name="ffn_dense",
    )(x, g_all, mods, mods, mods, wg, wu, wd)


def _proj_res_kernel(a_ref, w_ref, x_ref, gt_ref, o_ref, *, transposed):
    dims = (((0,), (0,)), ((), ())) if transposed else (((1,), (0,)), ((), ()))
    y = lax.dot_general(a_ref[...], w_ref[...], dims, preferred_element_type=F32)
    o_ref[...] = x_ref[...] + gt_ref[...] * y


def _proj_res(a, w, x, mods, layer, j, tiles_per_batch, ctx, tm=PROJ_TM):
    t, d = x.shape
    transposed = a.ndim == 3
    k = a.shape[1]
    tm = min(tm, t)
    tpb = tiles_per_batch // tm
    mi = _mod_index(layer, j, tpb if not ctx else 1, ctx)
    a_spec = (pl.BlockSpec((None, k, tm), lambda i: (i // tpb, 0, i % tpb)) if transposed
              else pl.BlockSpec((tm, k), lambda i: (i, 0)))
    return pl.pallas_call(
        functools.partial(_proj_res_kernel, transposed=transposed),
        grid=(t // tm,),
        in_specs=[a_spec,
                  pl.BlockSpec((None, k, d), lambda i: (layer // 2, 0, 0)),
                  pl.BlockSpec((tm, d), lambda i: (i, 0)),
                  _mod_spec(d, mi)],
        out_specs=pl.BlockSpec((tm, d), lambda i: (i, 0)),
        out_shape=jax.ShapeDtypeStruct((t, d), F32),
        compiler_params=_cparams(("parallel",)),
        name="proj_res",
    )(a, w, x, mods)


def _dft_angles(n):
    a = np.arange(n)
    return 2.0 * np.pi * ((a[:, None] * a[None, :]) % n) / n


def _seq_dft_tables(n):
    kp = np.arange(DFT_P)[None, :, None]
    p = np.arange(DFT_P)[None, None, :]
    r = np.arange(DFT_R)[:, None, None]
    th = 2.0 * np.pi * ((kp * (DFT_R * p + r)) % n) / n
    tab1 = np.stack([np.cos(th), -np.sin(th)], axis=1)
    tab1 = tab1.reshape(DFT_R // FNET_RB, FNET_RB, 2, DFT_P, DFT_P)
    kron = np.zeros((DFT_R // FNET_RB, 2, DFT_P, FNET_RB, DFT_P, FNET_RB), np.float32)
    for rl in range(FNET_RB):
        kron[:, :, :, rl, :, rl] = tab1[:, rl]
    tab1 = kron.reshape(DFT_R // FNET_RB, 2 * DFT_P * FNET_RB, DFT_P * FNET_RB)
    th2 = _dft_angles(DFT_R)
    c2, s2 = np.cos(th2), np.sin(th2)
    tab2 = np.block([[c2, s2], [-s2, c2]])
    return jnp.asarray(tab1, BF16), jnp.asarray(tab2, BF16)


def _chan_dft_tables(group_dim):
    th = _dft_angles(group_dim)
    return jnp.asarray(np.cos(th), BF16), jnp.asarray(np.sin(th), BF16)


def _ctx_dft_table(n_ctx):
    th = _dft_angles(n_ctx)
    return jnp.asarray(np.concatenate([np.cos(th), -np.sin(th)], axis=0), BF16)


def _cols_store(scr, val):
    for c in range(scr.shape[0]):
        scr[c] = val[:, c * LANES:(c + 1) * LANES]


def _cols_load(scr):
    return jnp.concatenate([scr[c] for c in range(scr.shape[0])], axis=-1)


def _cols_store_rows(scr, sel, val):
    for c in range(scr.shape[0]):
        scr[c, sel, :] = val[:, c * LANES:(c + 1) * LANES]


def _cols_load_rows(scr, sel):
    return jnp.concatenate([scr[c, sel, :] for c in range(scr.shape[0])], axis=-1)


def _fnet_stage1_kernel(x_ref, g_ref, sh_ref, sc_ref, tab_ref, zr_ref, zi_ref):
    d = g_ref.shape[-1]
    rows = DFT_P * FNET_RB
    h = _norm_mod(x_ref[...].reshape(rows, d), g_ref[...], sh_ref[...], sc_ref[...]).astype(BF16)
    z = jnp.dot(tab_ref[...], h, preferred_element_type=F32)
    zr_ref[...] = z[:rows].astype(BF16).reshape(DFT_P, FNET_RB, d)
    zi_ref[...] = z[rows:].astype(BF16).reshape(DFT_P, FNET_RB, d)


def _mix_tail(pr, pi, cc_ref, sc_ref, wout_ref, scale):
    gd = cc_ref.shape[0]
    ys = []
    for g in range(pr.shape[1] // gd):
        ys.append(jnp.dot(pr[:, g * gd:(g + 1) * gd], cc_ref[...], preferred_element_type=F32)
                  + jnp.dot(pi[:, g * gd:(g + 1) * gd], sc_ref[...], preferred_element_type=F32))
    mixed = (jnp.concatenate(ys, axis=-1) * scale).astype(BF16)
    return jnp.dot(mixed, wout_ref[...], preferred_element_type=F32)


def _fnet_stage2_kernel(zr_ref, zi_ref, tab2_ref, cc_ref, sc_ref, wout_ref, x_ref, gt_ref, o_ref,
                        p_scr, x_scr, o_scr, *, scale):
    d = gt_ref.shape[-1]
    rows = DFT_R * FNET_KB
    for j in range(FNET_KB):
        z = jnp.concatenate([zr_ref[j * DFT_R:(j + 1) * DFT_R, :], zi_ref[j * DFT_R:(j + 1) * DFT_R, :]], axis=0)
        p = jnp.dot(tab2_ref[...], z, preferred_element_type=F32)
        p_scr[j * DFT_R:(j + 1) * DFT_R, :d] = p[:DFT_R].astype(BF16)
        p_scr[j * DFT_R:(j + 1) * DFT_R, d:] = p[DFT_R:].astype(BF16)
    y = _mix_tail(p_scr[:, :d], p_scr[:, d:], cc_ref, sc_ref, wout_ref, scale)
    _cols_store(x_scr, x_ref[...].reshape(rows, d))
    for j in range(FNET_KB):
        sel = pl.ds(j, DFT_R, stride=FNET_KB)
        _cols_store_rows(o_scr, sel, _cols_load_rows(x_scr, sel) + gt_ref[...] * y[j * DFT_R:(j + 1) * DFT_R, :])
    o_ref[...] = _cols_load(o_scr).reshape(DFT_R, FNET_KB, d)


def _fnet_lat(x, mods, g_all, w_out, layer, n_batch):
    t, d = x.shape
    s = t // n_batch
    assert s == DFT_P * DFT_R
    tab1, tab2 = _seq_dft_tables(s)
    ccos, csin = _chan_dft_tables(d // FNET_GROUPS)
    mi = lambda j: (lambda b, *_: (layer * MOD_ROWS + b) * N_MOD + j)
    xv = x.reshape(n_batch, DFT_P, DFT_R, d)
    blk1 = (None, DFT_P, FNET_RB, d)
    rows1 = DFT_P * FNET_RB
    mi1 = lambda j: (lambda r, b: (layer * MOD_ROWS + b) * N_MOD + j)
    zr, zi = pl.pallas_call(
        _fnet_stage1_kernel,
        grid=(DFT_R // FNET_RB, n_batch),
        in_specs=[pl.BlockSpec(blk1, lambda r, b: (b, 0, r, 0)),
                  pl.BlockSpec((None, 1, d), lambda r, b: (layer, 0, 0)),
                  _mod_spec(d, mi1(0)), _mod_spec(d, mi1(1)),
                  pl.BlockSpec((None, 2 * rows1, rows1), lambda r, b: (r, 0, 0))],
        out_specs=[pl.BlockSpec(blk1, lambda r, b: (b, 0, r, 0))] * 2,
        out_shape=[jax.ShapeDtypeStruct((n_batch, DFT_P, DFT_R, d), BF16)] * 2,
        compiler_params=_cparams(("parallel", "parallel")),
        name="fnet_stage1",
    )(xv, g_all, mods, mods, tab1)
    zr = zr.reshape(n_batch, DFT_P * DFT_R, d)
    zi = zi.reshape(n_batch, DFT_P * DFT_R, d)
    xo = x.reshape(n_batch, DFT_R, DFT_P, d)
    blk2 = (None, DFT_R, FNET_KB, d)
    rows = FNET_KB * DFT_R
    scale = float(1.0 / np.sqrt(float(s) * (d // FNET_GROUPS)))
    out = pl.pallas_call(
        functools.partial(_fnet_stage2_kernel, scale=scale),
        grid=(n_batch, DFT_P // FNET_KB),
        in_specs=[pl.BlockSpec((None, rows, d), lambda b, k: (b, k, 0)),
                  pl.BlockSpec((None, rows, d), lambda b, k: (b, k, 0)),
                  pl.BlockSpec((2 * DFT_R, 2 * DFT_R), lambda b, k: (0, 0)),
                  pl.BlockSpec(ccos.shape, lambda b, k: (0, 0)),
                  pl.BlockSpec(csin.shape, lambda b, k: (0, 0)),
                  pl.BlockSpec((None, d, d), lambda b, k: (layer // 2, 0, 0)),
                  pl.BlockSpec(blk2, lambda b, k: (b, 0, k, 0)),
                  _mod_spec(d, mi(2))],
        out_specs=pl.BlockSpec(blk2, lambda b, k: (b, 0, k, 0)),
        out_shape=jax.ShapeDtypeStruct((n_batch, DFT_R, DFT_P, d), F32),
        scratch_shapes=[pltpu.VMEM((rows, 2 * d), BF16)] + [pltpu.VMEM((d // LANES, rows, LANES), F32)] * 2,
        compiler_params=_cparams(("parallel", "parallel")),
        name="fnet_stage2",
    )(zr, zi, tab2, ccos, csin, w_out, xo, mods)
    return out.reshape(t, d)


def _fnet_ctx_kernel(x_ref, g_ref, sh_ref, sc_ref, gt_ref, tab_ref, cc_ref, sc2_ref, wout_ref, o_ref, *, scale):
    n = x_ref.shape[0]
    h = _norm_mod(x_ref[...], g_ref[...], sh_ref[...], sc_ref[...]).astype(BF16)
    p = jnp.dot(tab_ref[...], h, preferred_element_type=F32)
    y = _mix_tail(p[:n].astype(BF16), p[n:].astype(BF16), cc_ref, sc2_ref, wout_ref, scale)
    o_ref[...] = x_ref[...] + gt_ref[...] * y


def _fnet_ctx(x, mods, g_all, w_out, layer, n_batch):
    t, d = x.shape
    n = t // n_batch
    gd = d // FNET_GROUPS
    ccos, csin = _chan_dft_tables(gd)
    tab = _ctx_dft_table(n)
    mi = lambda j: (lambda b: (layer * MOD_ROWS + CTX_ROW) * N_MOD + j)
    scale = float(1.0 / np.sqrt(float(n) * gd))
    return pl.pallas_call(
        functools.partial(_fnet_ctx_kernel, scale=scale),
        grid=(n_batch,),
        in_specs=[pl.BlockSpec((n, d), lambda b: (b, 0)),
                  pl.BlockSpec((None, 1, d), lambda b: (layer, 0, 0)),
                  _mod_spec(d, mi(0)), _mod_spec(d, mi(1)), _mod_spec(d, mi(2)),
                  pl.BlockSpec(tab.shape, lambda b: (0, 0)),
                  pl.BlockSpec(ccos.shape, lambda b: (0, 0)),
                  pl.BlockSpec(csin.shape, lambda b: (0, 0)),
                  pl.BlockSpec((None, d, d), lambda b: (layer // 2, 0, 0))],
        out_specs=pl.BlockSpec((n, d), lambda b: (b, 0)),
        out_shape=jax.ShapeDtypeStruct((t, d), F32),
        compiler_params=_cparams(("parallel",)),
        name="fnet_ctx",
    )(x, g_all, mods, mods, mods, tab, ccos, csin, w_out)


def _rope_tables(n_seq):
    rows = n_seq // GRID_W
    row = jnp.repeat(jnp.arange(rows, dtype=F32), GRID_W)
    col = jnp.tile(jnp.arange(GRID_W, dtype=F32), rows)
    inv_freq = ROPE_THETA ** (-jnp.arange(ROT_FREQS, dtype=F32) / ROT_FREQS)
    ang = jnp.stack([row[:, None] * inv_freq, col[:, None] * inv_freq], axis=1)
    cos, sin = jnp.cos(ang), jnp.sin(ang)
    zero = jnp.zeros_like(sin)
    cos_h = jnp.stack([cos, cos], axis=2).reshape(n_seq, HEAD_DIM)
    sin_lo = jnp.stack([-sin, zero], axis=2).reshape(n_seq, HEAD_DIM)
    sin_hi = jnp.stack([zero, sin], axis=2).reshape(n_seq, HEAD_DIM)
    rep = LANES // HEAD_DIM
    return jnp.tile(cos_h, (1, rep)), jnp.tile(sin_lo, (1, rep)), jnp.tile(sin_hi, (1, rep))


def _qkv_kernel(x_ref, g_ref, sh_ref, sc_ref, w_ref, *rest, rope, q_dim, kv_dim, q_scale):
    if rope:
        cos_ref, slo_ref, shi_ref, q_ref, k_ref, kt_ref, v_ref, vt_ref = rest
    else:
        q_ref, k_ref, kt_ref, v_ref, vt_ref = rest
    h = _norm_mod(x_ref[...], g_ref[...], sh_ref[...], sc_ref[...]).astype(BF16)
    qkv = jnp.dot(h, w_ref[...], preferred_element_type=F32)

    def rot(xs):
        if not rope:
            return xs
        return (xs * cos_ref[...] + pltpu.roll(xs, LANES - ROT_FREQS, axis=1) * slo_ref[...]
                + pltpu.roll(xs, ROT_FREQS, axis=1) * shi_ref[...])

    for j in range(q_dim // LANES):
        qj = rot(qkv[:, j * LANES:(j + 1) * LANES]) * q_scale
        if rope:
            q_ref[j * LANES:(j + 1) * LANES, :] = qj.T.astype(BF16)
        else:
            q_ref[:, j * LANES:(j + 1) * LANES] = qj.astype(BF16)
    k = jnp.concatenate([rot(qkv[:, q_dim + j * LANES:q_dim + (j + 1) * LANES]) for j in range(kv_dim // LANES)],
                        axis=-1)
    v = qkv[:, q_dim + kv_dim:]
    k_ref[...] = k.astype(BF16)
    kt_ref[...] = k.T.astype(BF16)
    v_ref[...] = v.astype(BF16)
    vt_ref[...] = v.T.astype(BF16)


def _qkv(x, mods, g_all, w_qkv, layer, n_batch, ctx, rope_tabs=None, tm=QKV_TM):
    t, d = x.shape
    n = t // n_batch
    tm = min(tm, n)
    tpb = n // tm
    kv_dim = N_KV_HEADS * HEAD_DIM
    q_dim = w_qkv.shape[2] - 2 * kv_dim
    mi = lambda j: _mod_index(layer, j, tpb, ctx)
    in_specs = [pl.BlockSpec((tm, d), lambda i: (i, 0)),
                pl.BlockSpec((None, 1, d), lambda i: (layer, 0, 0)),
                _mod_spec(d, mi(0)), _mod_spec(d, mi(1)),
                pl.BlockSpec((None,) + w_qkv.shape[1:], lambda i: (layer // 2, 0, 0))]
    args = [x, g_all, mods, mods, w_qkv]
    rope = rope_tabs is not None
    if rope:
        in_specs += [pl.BlockSpec((tm, LANES), lambda i: (i % tpb, 0))] * 3
        args += list(rope_tabs)
    return pl.pallas_call(
        functools.partial(_qkv_kernel, rope=rope, q_dim=q_dim, kv_dim=kv_dim, q_scale=HEAD_DIM ** -0.5),
        grid=(t // tm,),
        in_specs=in_specs,
        out_specs=[(pl.BlockSpec((None, q_dim, tm), lambda i: (i // tpb, 0, i % tpb)) if rope
                    else pl.BlockSpec((tm, q_dim), lambda i: (i, 0))),
                   pl.BlockSpec((tm, kv_dim), lambda i: (i, 0)),
                   pl.BlockSpec((None, kv_dim, tm), lambda i: (i // tpb, 0, i % tpb)),
                   pl.BlockSpec((tm, kv_dim), lambda i: (i, 0)),
                   pl.BlockSpec((None, kv_dim, tm), lambda i: (i // tpb, 0, i % tpb))],
        out_shape=[jax.ShapeDtypeStruct((n_batch, q_dim, n) if rope else (t, q_dim), BF16),
                   jax.ShapeDtypeStruct((t, kv_dim), BF16),
                   jax.ShapeDtypeStruct((n_batch, kv_dim, n), BF16),
                   jax.ShapeDtypeStruct((t, kv_dim), BF16),
                   jax.ShapeDtypeStruct((n_batch, kv_dim, n), BF16)],
        compiler_params=_cparams(("parallel",)),
        name="qkv_ctx" if ctx else "qkv_lat",
    )(*args)


ATTN_RB = 64


def _heads_attend(q_ref, kt, v, bias, sink_ref, o_ref):
    nq = q_ref.shape[0]
    n_heads = q_ref.shape[1] // HEAD_DIM
    group = n_heads // N_KV_HEADS
    outs = [None] * n_heads
    for g in range(N_KV_HEADS):
        heads = range(g * group, (g + 1) * group)
        qg = jnp.concatenate([q_ref[:, hd * HEAD_DIM:(hd + 1) * HEAD_DIM] for hd in heads], axis=0)
        s = jnp.dot(qg, kt[g * HEAD_DIM:(g + 1) * HEAD_DIM, :], preferred_element_type=F32)
        p_rows, den_rows = [], []
        for r0 in range(0, s.shape[0], ATTN_RB):
            rows = slice(r0, r0 + ATTN_RB)
            parts = [s[rows, k * LANES:(k + 1) * LANES] for k in range(s.shape[1] // LANES)]
            if bias is not None:
                parts[0] = parts[0] + bias[0][rows]
                parts[2] = parts[2] + bias[1][rows]
            sink = sink_ref[heads[r0 // nq]]
            m = jnp.maximum(jnp.max(functools.reduce(jnp.maximum, parts), axis=-1, keepdims=True), sink)
            ps = [jnp.exp(part - m) for part in parts]
            den_rows.append(jnp.sum(functools.reduce(jnp.add, ps), axis=-1, keepdims=True) + jnp.exp(sink - m))
            p_rows.append(jnp.concatenate(ps, axis=-1).astype(BF16))
        pv = jnp.dot(jnp.concatenate(p_rows, axis=0), v, preferred_element_type=F32)
        on = pv[:, g * HEAD_DIM:(g + 1) * HEAD_DIM] / jnp.concatenate(den_rows, axis=0)
        for k, hd in enumerate(heads):
            outs[hd] = on[k * nq:(k + 1) * nq, :]
    o_ref[...] = jnp.concatenate(outs, axis=-1).astype(BF16)


def _attn_lat_kernel(sink_ref, qt_ref, kp_ref, kc_ref, kn_ref, vtp_ref, vtc_ref, vtn_ref, kx_ref, vtx_ref,
                     blo_ref, bhi_ref, ot_ref, k_scr, vt_scr):
    w = kc_ref.shape[0]
    n_ctx = kx_ref.shape[0]
    nq = qt_ref.shape[1]
    n_heads = qt_ref.shape[0] // HEAD_DIM
    group = n_heads // N_KV_HEADS
    for c, (kr, vr) in enumerate(((kp_ref, vtp_ref), (kc_ref, vtc_ref), (kn_ref, vtn_ref))):
        k_scr[c * w:(c + 1) * w, :] = kr[...]
        vt_scr[:, c * w:(c + 1) * w] = vr[...]
    k_scr[3 * w:3 * w + n_ctx, :] = kx_ref[...]
    vt_scr[:, 3 * w:3 * w + n_ctx] = vtx_ref[...]
    n_chunks = k_scr.shape[0] // w
    for g in range(N_KV_HEADS):
        heads = range(g * group, (g + 1) * group)
        qg = jnp.concatenate([qt_ref[hd * HEAD_DIM:(hd + 1) * HEAD_DIM, :] for hd in heads], axis=1)
        st = jnp.dot(k_scr[:, g * HEAD_DIM:(g + 1) * HEAD_DIM], qg, preferred_element_type=F32)
        p_cols, den_cols = [], []
        for k, hd in enumerate(heads):
            cols = slice(k * nq, (k + 1) * nq)
            parts = [st[c * w:(c + 1) * w, cols] for c in range(n_chunks)]
            parts[0] = parts[0] + blo_ref[...]
            parts[2] = parts[2] + bhi_ref[...]
            sink = sink_ref[hd]
            m = jnp.maximum(jnp.max(functools.reduce(jnp.maximum, parts), axis=0, keepdims=True), sink)
            ps = [jnp.exp(part - m) for part in parts]
            den_cols.append(jnp.sum(functools.reduce(jnp.add, ps), axis=0, keepdims=True) + jnp.exp(sink - m))
            p_cols.append(jnp.concatenate(ps, axis=0).astype(BF16))
        ot = jnp.dot(vt_scr[g * HEAD_DIM:(g + 1) * HEAD_DIM, :], jnp.concatenate(p_cols, axis=1),
                     preferred_element_type=F32)
        ot = ot / jnp.concatenate(den_cols, axis=1)
        for k, hd in enumerate(heads):
            ot_ref[hd * HEAD_DIM:(hd + 1) * HEAD_DIM, :] = ot[:, k * nq:(k + 1) * nq].astype(BF16)


def _attn_lat(qt, k, vt, kx, vtx, sinks, n_batch):
    _, qd, s = qt.shape
    n_ctx = kx.shape[0] // n_batch
    w = WINDOW
    nb = s // w
    kvd = k.shape[1]
    j_all = 3 * w + n_ctx
    ki = np.arange(w)[:, None]
    qi = np.arange(w)[None, :]
    lo = np.where(ki >= qi, 0.0, NEG).astype(np.float32)
    hi = np.where(ki <= qi, 0.0, NEG).astype(np.float32)
    off = np.full((w, w), NEG, np.float32)
    blo = jnp.asarray(np.stack([lo, off]))
    bhi = jnp.asarray(np.stack([hi, off]))
    prev = lambda b, i: (b * nb + jnp.maximum(i - 1, 0), 0)
    nxt = lambda b, i: (b * nb + jnp.minimum(i + 1, nb - 1), 0)
    tprev = lambda b, i: (b, 0, jnp.maximum(i - 1, 0))
    tnxt = lambda b, i: (b, 0, jnp.minimum(i + 1, nb - 1))
    return pl.pallas_call(
        _attn_lat_kernel,
        grid=(n_batch, nb),
        in_specs=[pl.BlockSpec(memory_space=pltpu.SMEM),
                  pl.BlockSpec((None, qd, w), lambda b, i: (b, 0, i)),
                  pl.BlockSpec((w, kvd), prev),
                  pl.BlockSpec((w, kvd), lambda b, i: (b * nb + i, 0)),
                  pl.BlockSpec((w, kvd), nxt),
                  pl.BlockSpec((None, kvd, w), tprev),
                  pl.BlockSpec((None, kvd, w), lambda b, i: (b, 0, i)),
                  pl.BlockSpec((None, kvd, w), tnxt),
                  pl.BlockSpec((n_ctx, kvd), lambda b, i: (b, 0)),
                  pl.BlockSpec((None, kvd, n_ctx), lambda b, i: (b, 0, 0)),
                  pl.BlockSpec((None, w, w), lambda b, i: (jnp.where(i == 0, 1, 0), 0, 0)),
                  pl.BlockSpec((None, w, w), lambda b, i: (jnp.where(i == nb - 1, 1, 0), 0, 0))],
        out_specs=pl.BlockSpec((None, qd, w), lambda b, i: (b, 0, i)),
        out_shape=jax.ShapeDtypeStruct((n_batch, qd, s), BF16),
        scratch_shapes=[pltpu.VMEM((j_all, kvd), BF16), pltpu.VMEM((kvd, j_all), BF16)],
        compiler_params=_cparams(("parallel", "parallel")),
        name="attn_lat",
    )(sinks, qt, k, k, k, vt, vt, vt, kx, vtx, blo, bhi)


def _attn_ctx_kernel(sink_ref, q_ref, kt_ref, v_ref, o_ref):
    _heads_attend(q_ref, kt_ref[...], v_ref[...], None, sink_ref, o_ref)


def _attn_ctx(q, kt, v, sinks, n_batch):
    t, qd = q.shape
    n = t // n_batch
    kvd = v.shape[1]
    return pl.pallas_call(
        _attn_ctx_kernel,
        grid=(n_batch,),
        in_specs=[pl.BlockSpec(memory_space=pltpu.SMEM),
                  pl.BlockSpec((n, qd), lambda b: (b, 0)),
                  pl.BlockSpec((None, kvd, n), lambda b: (b, 0, 0)),
                  pl.BlockSpec((n, kvd), lambda b: (b, 0))],
        out_specs=pl.BlockSpec((n, qd), lambda b: (b, 0)),
        out_shape=jax.ShapeDtypeStruct((t, qd), BF16),
        compiler_params=_cparams(("parallel",)),
        name="attn_ctx",
    )(sinks, q, kt, v)


ROUTE_I1, ROUTE_I2, ROUTE_R1, ROUTE_R2, ROUTE_G1, ROUTE_G2 = range(6)
ROUTE_ROWS = 8
TAB_START, TAB_COUNT, TAB_USED, TAB_SIZE = 0, N_EXPERTS, 2 * N_EXPERTS, 2 * N_EXPERTS + 8
ROW_UNROLL = 8


def _router_kernel(x_ref, g_ref, sh_ref, sc_ref, wr_ref, tri_ref, route_ref, idx_ref, cnt_ref, carry_scr):
    @pl.when(pl.program_id(0) == 0)
    def _():
        carry_scr[...] = jnp.zeros_like(carry_scr)

    h = _norm_mod(x_ref[...], g_ref[...], sh_ref[...], sc_ref[...])
    hi = h.astype(BF16)
    lo = (h - hi.astype(F32)).astype(BF16)
    a = jnp.dot(hi, wr_ref[...], preferred_element_type=F32)
    logits = (a[:, :LANES] + a[:, LANES:]) + jnp.dot(lo, wr_ref[:, :LANES], preferred_element_type=F32)
    lane = lax.broadcasted_iota(jnp.int32, logits.shape, 1)
    lane_f = lane.astype(F32)
    logits = jnp.where(lane < N_EXPERTS, logits, -jnp.inf)
    m1 = jnp.max(logits, axis=-1, keepdims=True)
    i1 = jnp.min(jnp.where(logits == m1, lane_f, float(LANES)), axis=-1, keepdims=True)
    oh1 = lane_f == i1
    rest = jnp.where(oh1, -jnp.inf, logits)
    m2 = jnp.max(rest, axis=-1, keepdims=True)
    i2 = jnp.min(jnp.where(rest == m2, lane_f, float(LANES)), axis=-1, keepdims=True)
    oh2 = lane_f == i2
    e2 = jnp.exp(m2 - m1)
    g1 = 1.0 / (1.0 + e2)
    g2 = e2 / (1.0 + e2)
    sel = jnp.where(oh1, 1.0, 0.0) + jnp.where(oh2, 1.0, 0.0)
    before = jnp.dot(tri_ref[...], sel.astype(BF16), preferred_element_type=F32) + carry_scr[0:1, :]
    r1 = jnp.sum(jnp.where(oh1, before, 0.0), axis=-1, keepdims=True)
    r2 = jnp.sum(jnp.where(oh2, before, 0.0), axis=-1, keepdims=True)
    total = carry_scr[0:1, :] + jnp.sum(sel, axis=0, keepdims=True)
    carry_scr[...] = jnp.broadcast_to(total, carry_scr.shape)
    cnt_ref[...] = jnp.broadcast_to(total, cnt_ref.shape)
    rec = jnp.zeros_like(logits)
    for ln, val in ((ROUTE_I1, i1), (ROUTE_I2, i2), (ROUTE_R1, r1), (ROUTE_R2, r2), (ROUTE_G1, g1), (ROUTE_G2, g2)):
        rec = jnp.where(lane == ln, val, rec)
    route_ref[...] = rec
    idx_ref[...] = rec.T[:ROUTE_ROWS, :].astype(jnp.int32)


def _router(x, mods, g_all, w_router, layer, tiles_per_batch, ctx, tm):
    t, d = x.shape
    mi = lambda j: _mod_index(layer, j, tiles_per_batch // tm if not ctx else 1, ctx)
    wr = jnp.zeros((d, LANES), F32).at[:, :N_EXPERTS].set(w_router)
    whi = wr.astype(BF16)
    wr = jnp.concatenate([whi, (wr - whi.astype(F32)).astype(BF16)], axis=1)
    tri = jnp.asarray(np.tril(np.ones((tm, tm), np.float32), -1), BF16)
    return pl.pallas_call(
        _router_kernel,
        grid=(t // tm,),
        in_specs=[pl.BlockSpec((tm, d), lambda i: (i, 0)),
                  pl.BlockSpec((None, 1, d), lambda i: (layer, 0, 0)),
                  _mod_spec(d, mi(3)), _mod_spec(d, mi(4)),
                  pl.BlockSpec((d, 2 * LANES), lambda i: (0, 0)),
                  pl.BlockSpec((tm, tm), lambda i: (0, 0))],
        out_specs=[pl.BlockSpec((tm, LANES), lambda i: (i, 0)),
                   pl.BlockSpec((None, ROUTE_ROWS, tm), lambda i: (i, 0, 0)),
                   pl.BlockSpec((8, LANES), lambda i: (0, 0))],
        out_shape=[jax.ShapeDtypeStruct((t, LANES), F32),
                   jax.ShapeDtypeStruct((t // tm, ROUTE_ROWS, tm), jnp.int32),
                   jax.ShapeDtypeStruct((8, LANES), F32)],
        scratch_shapes=[pltpu.VMEM((8, LANES), F32)],
        compiler_params=_cparams(("arbitrary",)),
        name="router",
    )(x, g_all, mods, mods, wr, tri)


SUB = 8


def _to_row_tiles(ref, val):
    n = val.shape[0]
    for k in range(SUB):
        ref[pl.ds(k, n, stride=SUB), :] = val[:, k * LANES:(k + 1) * LANES]


def _from_row_tiles(ref):
    n = ref.shape[0] // SUB
    return jnp.concatenate([ref[pl.ds(k, n, stride=SUB), :] for k in range(SUB)], axis=-1)


def _row_copy(src, dst, src_row, dst_row, sem):
    return pltpu.make_async_copy(src.at[pl.ds(pl.multiple_of(src_row * SUB, SUB), SUB)],
                                 dst.at[pl.ds(pl.multiple_of(dst_row * SUB, SUB), SUB)], sem)


def _for_rows(tm, body):
    def blk(i, _):
        base = pl.multiple_of(i * ROW_UNROLL, ROW_UNROLL)
        for j in range(ROW_UNROLL):
            body(base + j, j)
        return 0

    lax.fori_loop(0, tm // ROW_UNROLL, blk, 0)


ZERO_SEM = 4


def _dispatch_kernel(tab_ref, slot_ref, x_ref, g_ref, sh_ref, sc_ref, xs_ref, h_scr, zero_scr, sems, *, group):
    tm = x_ref.shape[0]
    n_groups = xs_ref.shape[0] // (group * SUB)
    step = pl.program_id(0)
    n_steps = pl.num_programs(0)
    cur = lax.rem(step, 2)

    def wait_buffer(b):
        for k in range(2):
            pltpu.make_async_copy(h_scr.at[b], xs_ref.at[pl.ds(0, tm * SUB)], sems.at[2 * b + k]).wait()

    @pl.when(step >= 2)
    def _():
        wait_buffer(cur)

    _to_row_tiles(h_scr.at[cur], _norm_mod(x_ref[...], g_ref[...], sh_ref[...], sc_ref[...]))

    def issue(r, j):
        for k in range(2):
            _row_copy(h_scr.at[cur], xs_ref, r, slot_ref[0, k, r], sems.at[2 * cur + k]).start(priority=j % 2)

    _for_rows(tm, issue)

    @pl.when(jnp.logical_and(step == n_steps - 1, step >= 1))
    def _():
        wait_buffer(1 - cur)

    @pl.when(step == n_steps - 1)
    def _():
        wait_buffer(cur)
        zero_scr[...] = jnp.zeros_like(zero_scr)
        for e in range(N_EXPERTS):
            n = tab_ref[TAB_COUNT + e]
            n_pad = lax.rem(group - lax.rem(n, group), group)
            first = tab_ref[TAB_START + e] + n

            def fill(k, _, first=first):
                _row_copy(zero_scr, xs_ref, 0, first + k, sems.at[ZERO_SEM]).start()
                return 0

            def fill_done(k, _):
                _row_copy(zero_scr, xs_ref, 0, 0, sems.at[ZERO_SEM]).wait()
                return 0

            lax.fori_loop(0, n_pad, fill, 0)
            lax.fori_loop(0, n_pad, fill_done, 0)

        def clear(j, _):
            row = pl.multiple_of(j * (group * SUB), group * SUB)
            cp = pltpu.make_async_copy(zero_scr, xs_ref.at[pl.ds(row, group * SUB)], sems.at[ZERO_SEM])
            cp.start()
            cp.wait()
            return 0

        lax.fori_loop(tab_ref[TAB_USED], n_groups, clear, 0)


def _dispatch(x, tab, slots, mods, g_all, layer, tiles_per_batch, ctx, n_groups, group, tm):
    t, d = x.shape
    mi = lambda j: _mod_index(layer, j, tiles_per_batch // tm if not ctx else 1, ctx)
    return pl.pallas_call(
        functools.partial(_dispatch_kernel, group=group),
        grid=(t // tm,),
        in_specs=[pl.BlockSpec(memory_space=pltpu.SMEM),
                  pl.BlockSpec((1, 2, tm), lambda i: (i, 0, 0), memory_space=pltpu.SMEM),
                  pl.BlockSpec((tm, d), lambda i: (i, 0)),
                  pl.BlockSpec((None, 1, d), lambda i: (layer, 0, 0)),
                  _mod_spec(d, mi(3)), _mod_spec(d, mi(4))],
        out_specs=pl.BlockSpec(memory_space=pl.ANY),
        out_shape=jax.ShapeDtypeStruct((n_groups * group * SUB, LANES), F32),
        scratch_shapes=[pltpu.VMEM((2, tm * SUB, LANES), F32), pltpu.VMEM((group * SUB, LANES), F32),
                        pltpu.SemaphoreType.DMA((ZERO_SEM + 1,))],
        compiler_params=_cparams(("arbitrary",)),
        name="moe_dispatch",
    )(tab, slots, x, g_all, mods, mods)


def _moe_ffn_kernel(be_ref, bv_ref, xs_ref, wg_ref, wu_ref, wd_ref, ys_ref, h_scr, acc_scr):
    del be_ref
    i = pl.program_id(0)
    f = pl.program_id(1)
    last = f == pl.num_programs(1) - 1
    valid = bv_ref[i] > 0

    @pl.when(jnp.logical_and(valid, f == 0))
    def _():
        h_scr[...] = _from_row_tiles(xs_ref).astype(BF16)
        acc_scr[...] = jnp.zeros_like(acc_scr)

    @pl.when(valid)
    def _():
        h = h_scr[...]
        a = jnp.dot(h, wg_ref[...], preferred_element_type=F32)
        u = jnp.dot(h, wu_ref[...], preferred_element_type=F32)
        t = (a * _sigmoid(a)) * u
        acc_scr[...] += jnp.dot(t.astype(BF16), wd_ref[...], preferred_element_type=F32)

    @pl.when(jnp.logical_and(valid, last))
    def _():
        _to_row_tiles(ys_ref, acc_scr[...])

    @pl.when(jnp.logical_and(jnp.logical_not(valid), last))
    def _():
        ys_ref[...] = jnp.zeros_like(ys_ref)


def _moe_ffn(xs, block_expert, block_valid, wg, wu, wd, layer, tm, tf=MOE_TF):
    d, ff = wg.shape[2:]
    assert d == SUB * LANES and xs.shape[1] == LANES
    cap = xs.shape[0] // SUB
    tf = min(tf, ff)
    nf = ff // tf
    sj = layer // 2
    fsel = lambda i, f, bv: jnp.where(bv[i] > 0, f, nf - 1)
    grid_spec = pltpu.PrefetchScalarGridSpec(
        num_scalar_prefetch=2,
        grid=(cap // tm, nf),
        in_specs=[pl.BlockSpec((tm * SUB, LANES), lambda i, f, be, bv: (i, 0)),
                  pl.BlockSpec((None, None, d, tf), lambda i, f, be, bv: (sj, be[i], 0, fsel(i, f, bv))),
                  pl.BlockSpec((None, None, d, tf), lambda i, f, be, bv: (sj, be[i], 0, fsel(i, f, bv))),
                  pl.BlockSpec((None, None, tf, d), lambda i, f, be, bv: (sj, be[i], fsel(i, f, bv), 0))],
        out_specs=pl.BlockSpec((tm * SUB, LANES), lambda i, f, be, bv: (i, 0)),
        scratch_shapes=[pltpu.VMEM((tm, d), BF16), pltpu.VMEM((tm, d), F32)],
    )
    return pl.pallas_call(
        _moe_ffn_kernel,
        grid_spec=grid_spec,
        out_shape=jax.ShapeDtypeStruct(xs.shape, F32),
        compiler_params=_cparams(("parallel", "arbitrary")),
        name="moe_ffn",
    )(block_expert, block_valid, xs, wg, wu, wd)


def _combine_kernel(slot_ref, next_slot_ref, route_ref, x_ref, gt_ref, fg_ref, ys_ref, o_ref, buf, sems, *, final_norm):
    tm = x_ref.shape[0]
    step = pl.program_id(0)
    cur = lax.rem(step, 2)

    def gather(idx_ref, b):
        def issue(r, j):
            for k in range(2):
                _row_copy(ys_ref, buf.at[b, k], idx_ref[0, k, r], r, sems.at[2 * b + k]).start(priority=j % 2)

        _for_rows(tm, issue)

    @pl.when(step == 0)
    def _():
        gather(slot_ref, cur)

    @pl.when(step + 1 < pl.num_programs(0))
    def _():
        gather(next_slot_ref, 1 - cur)

    for k in range(2):
        pltpu.make_async_copy(ys_ref.at[pl.ds(0, tm * SUB)], buf.at[cur, k], sems.at[2 * cur + k]).wait()
    rec = route_ref[...]
    g1 = rec[:, ROUTE_G1:ROUTE_G1 + 1]
    g2 = rec[:, ROUTE_G2:ROUTE_G2 + 1]
    y = _from_row_tiles(buf.at[cur, 0]) * g1 + _from_row_tiles(buf.at[cur, 1]) * g2
    out = x_ref[...] + gt_ref[...] * y
    if final_norm:
        out = out * lax.rsqrt(jnp.mean(out * out, axis=-1, keepdims=True) + EPS) * fg_ref[...]
    o_ref[...] = out


def _combine(ys, slots, route, x, mods, final_g, layer, tiles_per_batch, ctx, final_norm, tm):
    t, d = x.shape
    mi = _mod_index(layer, 5, tiles_per_batch // tm if not ctx else 1, ctx)
    n_tiles = t // tm
    return pl.pallas_call(
        functools.partial(_combine_kernel, final_norm=final_norm),
        grid=(n_tiles,),
        in_specs=[pl.BlockSpec((1, 2, tm), lambda i: (i, 0, 0), memory_space=pltpu.SMEM),
                  pl.BlockSpec((1, 2, tm), lambda i: (jnp.minimum(i + 1, n_tiles - 1), 0, 0), memory_space=pltpu.SMEM),
                  pl.BlockSpec((tm, LANES), lambda i: (i, 0)),
                  pl.BlockSpec((tm, d), lambda i: (i, 0)),
                  _mod_spec(d, mi),
                  pl.BlockSpec((1, d), lambda i: (0, 0)),
                  pl.BlockSpec(memory_space=pl.ANY)],
        out_specs=pl.BlockSpec((tm, d), lambda i: (i, 0)),
        out_shape=jax.ShapeDtypeStruct((t, d), F32),
        scratch_shapes=[pltpu.VMEM((2, 2, tm * SUB, LANES), F32), pltpu.SemaphoreType.DMA((4,))],
        compiler_params=_cparams(("arbitrary",)),
        name="moe_combine",
    )(slots, slots, route, x, mods, final_g.reshape(1, d), ys)


def _moe(x, mods, g_all, w_router, wg, wu, wd, final_g, layer, tiles_per_batch, ctx, final_norm):
    t, d = x.shape
    group = MOE_TM if 2 * t >= 4 * N_EXPERTS * MOE_TM else MOE_TM // 2
    row_tm = min(ROW_TM, t)
    route, idx, counts = _router(x, mods, g_all, w_router, layer, tiles_per_batch, ctx, row_tm)
    counts = counts[0, :N_EXPERTS].astype(jnp.int32)
    groups = (counts + group - 1) // group
    ends = jnp.cumsum(groups)
    starts = (ends - groups) * group
    n_groups = (2 * t + group - 1) // group + N_EXPERTS
    tab = jnp.zeros((TAB_SIZE,), jnp.int32)
    tab = tab.at[TAB_START:TAB_START + N_EXPERTS].set(starts).at[TAB_COUNT:TAB_COUNT + N_EXPERTS].set(counts)
    tab = tab.at[TAB_USED].set(ends[-1])
    gi = jnp.arange(n_groups, dtype=jnp.int32)
    block_expert = jnp.minimum(jnp.sum(gi[:, None] >= ends[None, :], axis=1), N_EXPERTS - 1).astype(jnp.int32)
    block_valid = (gi < ends[-1]).astype(jnp.int32)
    def slot_rows(e, r):
        base = functools.reduce(jnp.add, [jnp.where(e == k, starts[k], 0) for k in range(N_EXPERTS)])
        return base + r
    slots = jnp.stack([slot_rows(idx[:, ROUTE_I1, :], idx[:, ROUTE_R1, :]),
                       slot_rows(idx[:, ROUTE_I2, :], idx[:, ROUTE_R2, :])], axis=1)
    xs = _dispatch(x, tab, slots, mods, g_all, layer, tiles_per_batch, ctx, n_groups, group, row_tm)
    ys = _moe_ffn(xs, block_expert, block_valid, wg, wu, wd, layer, group)
    return _combine(ys, slots, route, x, mods, final_g, layer, tiles_per_batch, ctx, final_norm, row_tm)


def kernel(x, c, ctx, c_ctx, ada_w, ada_b, norm_mix_g, norm_ffn_g, fnet_w_out, attn_w_qkv, attn_w_o, attn_sinks,
           ffn_w_gate, ffn_w_up, ffn_w_down, moe_w_router, moe_w_gate, moe_w_up, moe_w_down, final_norm_g):
    n_batch, s, d = x.shape
    n_ctx = ctx.shape[1]
    depth = ada_w.shape[0]
    mods = _ada(c, c_ctx, ada_w, ada_b)
    g_mix = norm_mix_g.reshape(depth, 1, d)
    g_ffn = norm_ffn_g.reshape(depth, 1, d)
    rope_tabs = _rope_tables(s)
    x_lat = x.reshape(n_batch * s, d)
    x_ctx = ctx.reshape(n_batch * n_ctx, d)
    bf = lambda a: a.astype(BF16)
    w_out, w_qkv, w_o = bf(fnet_w_out), bf(attn_w_qkv), bf(attn_w_o)
    wg, wu, wd = bf(ffn_w_gate), bf(ffn_w_up), bf(ffn_w_down)
    eg, eu, ed = bf(moe_w_gate), bf(moe_w_up), bf(moe_w_down)
    for i in range(depth):
        j = i // 2
        last = i == depth - 1
        if i % 2 == 0:
            x_lat = _fnet_lat(x_lat, mods, g_mix, w_out, i, n_batch)
            if not last:
                x_ctx = _fnet_ctx(x_ctx, mods, g_mix, w_out, i, n_batch)
            x_lat = _ffn_dense(x_lat, mods, g_ffn, wg, wu, wd, i, s, False)
            if not last:
                x_ctx = _ffn_dense(x_ctx, mods, g_ffn, wg, wu, wd, i, n_ctx, True)
        else:
            qt, k, _, _, vt = _qkv(x_lat, mods, g_mix, w_qkv, i, n_batch, False, rope_tabs)
            qx, kx, ktx, vx, vtx = _qkv(x_ctx, mods, g_mix, w_qkv, i, n_batch, True)
            ot = _attn_lat(qt, k, vt, kx, vtx, attn_sinks[j], n_batch)
            x_lat = _proj_res(ot, w_o, x_lat, mods, i, 2, s, False)
            if not last:
                ox = _attn_ctx(qx, ktx, vx, attn_sinks[j], n_batch)
                x_ctx = _proj_res(ox, w_o, x_ctx, mods, i, 2, n_ctx, True)
            x_lat = _moe(x_lat, mods, g_ffn, moe_w_router[j], eg, eu, ed, final_norm_g, i, s, False, last)
            if not last:
                x_ctx = _moe(x_ctx, mods, g_ffn, moe_w_router[j], eg, eu, ed, final_norm_g, i, n_ctx, True, False)
    return x_lat.reshape(n_batch, s, d)
```

```python
import functools

import numpy as np
import jax
import jax.numpy as jnp
from jax import lax
from jax.experimental import pallas as pl
from jax.experimental.pallas import tpu as pltpu

F32 = jnp.float32
BF16 = jnp.bfloat16

GRID_W = 64
N_MOD = 6
EPS = 1e-6
FNET_GROUPS = 4
HEAD_DIM = 64
N_KV_HEADS = 4
WINDOW = 128
ROPE_THETA = 10000.0
ROT_FREQS = HEAD_DIM // 4
N_EXPERTS = 8
MOD_ROWS = 8
CTX_ROW = MOD_ROWS - 1
LANES = 128
NEG = -1e30
VMEM_LIMIT = 56 * 1024 * 1024

FFN_TM = 512
FFN_TF = 1792
MOE_TM = 512
MOE_TF = 1792
ROW_TM = 1024
PROJ_TM = 1024
QKV_TM = 1024
DFT_P = 64
DFT_R = 128
FNET_RB = 16
FNET_KB = 8


def _cparams(sem):
    return pltpu.CompilerParams(dimension_semantics=sem, vmem_limit_bytes=VMEM_LIMIT)


def _sigmoid(a):
    return 1.0 / (1.0 + jnp.exp(-a))


def _norm_mod(x, g, shift, scale):
    xn = x * lax.rsqrt(jnp.mean(x * x, axis=-1, keepdims=True) + EPS)
    return (xn * g) * (1.0 + scale) + shift


def _mod_spec(d, idx_fn):
    return pl.BlockSpec((None, 1, d), lambda *ids: (idx_fn(*ids), 0, 0))


def _mod_index(layer, j, tiles_per_batch, ctx):
    def fn(i, *_):
        b = CTX_ROW if ctx else i // tiles_per_batch
        return (layer * MOD_ROWS + b) * N_MOD + j
    return fn


def _ada_kernel(c_ref, w_ref, b_ref, o_ref):
    cc = c_ref[...]
    s = cc * _sigmoid(cc)
    o_ref[...] = jnp.dot(s, w_ref[...], preferred_element_type=F32,
                         precision=lax.Precision.HIGHEST) + b_ref[...]


def _ada(c, c_ctx, ada_w, ada_b):
    depth, d, _ = ada_w.shape
    nb = c.shape[0]
    assert nb < MOD_ROWS
    cc = jnp.concatenate([c, jnp.zeros((CTX_ROW - nb, d), F32), c_ctx[None]], axis=0)
    out = pl.pallas_call(
        _ada_kernel,
        grid=(depth, N_MOD),
        in_specs=[pl.BlockSpec((MOD_ROWS, d), lambda l, j: (0, 0)),
                  pl.BlockSpec((None, d, d), lambda l, j: (l, 0, j)),
                  pl.BlockSpec((None, 1, d), lambda l, j: (l * N_MOD + j, 0, 0))],
        out_specs=pl.BlockSpec((None, MOD_ROWS, d), lambda l, j: (l, 0, j)),
        out_shape=jax.ShapeDtypeStruct((depth, MOD_ROWS, N_MOD * d), F32),
        compiler_params=_cparams(("parallel", "parallel")),
        name="ada",
    )(cc, ada_w, ada_b.reshape(depth * N_MOD, 1, d))
    return out.reshape(depth * MOD_ROWS * N_MOD, 1, d)


def _ffn_kernel(x_ref, g_ref, sh_ref, sc_ref, gt_ref, wg_ref, wu_ref, wd_ref, o_ref, h_scr, acc_scr):
    f = pl.program_id(1)

    @pl.when(f == 0)
    def _():
        h_scr[...] = _norm_mod(x_ref[...], g_ref[...], sh_ref[...], sc_ref[...]).astype(BF16)
        acc_scr[...] = jnp.zeros_like(acc_scr)

    h = h_scr[...]
    a = jnp.dot(h, wg_ref[...], preferred_element_type=F32)
    u = jnp.dot(h, wu_ref[...], preferred_element_type=F32)
    t = (a * _sigmoid(a)) * u
    acc_scr[...] += jnp.dot(t.astype(BF16), wd_ref[...], preferred_element_type=F32)

    @pl.when(f == pl.num_programs(1) - 1)
    def _():
        o_ref[...] = x_ref[...] + gt_ref[...] * acc_scr[...]


def _ffn_dense(x, mods, g_all, wg, wu, wd, layer, tiles_per_batch, ctx, tm=FFN_TM, tf=FFN_TF):
    t, d = x.shape
    ff = wg.shape[2]
    tm = min(tm, t)
    tf = min(tf, ff)
    sj = layer // 2
    mi = functools.partial(_mod_index, layer, tiles_per_batch=tiles_per_batch // tm if not ctx else 1, ctx=ctx)
    return pl.pallas_call(
        _ffn_kernel,
        grid=(t // tm, ff // tf),
        in_specs=[pl.BlockSpec((tm, d), lambda i, f: (i, 0)),
                  pl.BlockSpec((None, 1, d), lambda i, f: (layer, 0, 0)),
                  _mod_spec(d, mi(3)), _mod_spec(d, mi(4)), _mod_spec(d, mi(5)),
                  pl.BlockSpec((None, d, tf), lambda i, f: (sj, 0, f)),
                  pl.BlockSpec((None, d, tf), lambda i, f: (sj, 0, f)),
                  pl.BlockSpec((None, tf, d), lambda i, f: (sj, f, 0))],
        out_specs=pl.BlockSpec((tm, d), lambda i, f: (i, 0)),
        out_shape=jax.ShapeDtypeStruct((t, d), F32),
        scratch_shapes=[pltpu.VMEM((tm, d), BF16), pltpu.VMEM((tm, d), F32)],
        compiler_params=_cparams(("parallel", "arbitrary")),
        name="ffn_dense",
    )(x, g_all, mods, mods, mods, wg, wu, wd)


def _proj_res_kernel(a_ref, w_ref, x_ref, gt_ref, o_ref, *, transposed):
    dims = (((0,), (0,)), ((), ())) if transposed else (((1,), (0,)), ((), ()))
    y = lax.dot_general(a_ref[...], w_ref[...], dims, preferred_element_type=F32)
    o_ref[...] = x_ref[...] + gt_ref[...] * y


def _proj_res(a, w, x, mods, layer, j, tiles_per_batch, ctx, tm=PROJ_TM):
    t, d = x.shape
    transposed = a.ndim == 3
    k = a.shape[1]
    tm = min(tm, t)
    tpb = tiles_per_batch // tm
    mi = _mod_index(layer, j, tpb if not ctx else 1, ctx)
    a_spec = (pl.BlockSpec((None, k, tm), lambda i: (i // tpb, 0, i % tpb)) if transposed
              else pl.BlockSpec((tm, k), lambda i: (i, 0)))
    return pl.pallas_call(
        functools.partial(_proj_res_kernel, transposed=transposed),
        grid=(t // tm,),
        in_specs=[a_spec,
                  pl.BlockSpec((None, k, d), lambda i: (layer // 2, 0, 0)),
                  pl.BlockSpec((tm, d), lambda i: (i, 0)),
                  _mod_spec(d, mi)],
        out_specs=pl.BlockSpec((tm, d), lambda i: (i, 0)),
        out_shape=jax.ShapeDtypeStruct((t, d), F32),
        compiler_params=_cparams(("parallel",)),
        name="proj_res",
    )(a, w, x, mods)


def _dft_angles(n):
    a = np.arange(n)
    return 2.0 * np.pi * ((a[:, None] * a[None, :]) % n) / n


def _seq_dft_tables(n):
    kp = np.arange(DFT_P)[None, :, None]
    p = np.arange(DFT_P)[None, None, :]
    r = np.arange(DFT_R)[:, None, None]
    th = 2.0 * np.pi * ((kp * (DFT_R * p + r)) % n) / n
    tab1 = np.stack([np.cos(th), -np.sin(th)], axis=1)
    tab1 = tab1.reshape(DFT_R // FNET_RB, FNET_RB, 2, DFT_P, DFT_P)
    kron = np.zeros((DFT_R // FNET_RB, 2, DFT_P, FNET_RB, DFT_P, FNET_RB), np.float32)
    for rl in range(FNET_RB):
        kron[:, :, :, rl, :, rl] = tab1[:, rl]
    tab1 = kron.reshape(DFT_R // FNET_RB, 2 * DFT_P * FNET_RB, DFT_P * FNET_RB)
    th2 = _dft_angles(DFT_R)
    c2, s2 = np.cos(th2), np.sin(th2)
    tab2 = np.block([[c2, s2], [-s2, c2]])
    return jnp.asarray(tab1, BF16), jnp.asarray(tab2, BF16)


def _chan_dft_tables(group_dim):
    th = _dft_angles(group_dim)
    return jnp.asarray(np.cos(th), BF16), jnp.asarray(np.sin(th), BF16)


def _ctx_dft_table(n_ctx):
    th = _dft_angles(n_ctx)
    return jnp.asarray(np.concatenate([np.cos(th), -np.sin(th)], axis=0), BF16)


def _cols_store(scr, val):
    for c in range(scr.shape[0]):
        scr[c] = val[:, c * LANES:(c + 1) * LANES]


def _cols_load(scr):
    return jnp.concatenate([scr[c] for c in range(scr.shape[0])], axis=-1)


def _cols_store_rows(scr, sel, val):
    for c in range(scr.shape[0]):
        scr[c, sel, :] = val[:, c * LANES:(c + 1) * LANES]


def _cols_load_rows(scr, sel):
    return jnp.concatenate([scr[c, sel, :] for c in range(scr.shape[0])], axis=-1)


def _fnet_stage1_kernel(x_ref, g_ref, sh_ref, sc_ref, tab_ref, zr_ref, zi_ref):
    d = g_ref.shape[-1]
    rows = DFT_P * FNET_RB
    h = _norm_mod(x_ref[...].reshape(rows, d), g_ref[...], sh_ref[...], sc_ref[...]).astype(BF16)
    z = jnp.dot(tab_ref[...], h, preferred_element_type=F32)
    zr_ref[...] = z[:rows].astype(BF16).reshape(DFT_P, FNET_RB, d)
    zi_ref[...] = z[rows:].astype(BF16).reshape(DFT_P, FNET_RB, d)


def _mix_tail(pr, pi, cc_ref, sc_ref, wout_ref, scale):
    gd = cc_ref.shape[0]
    ys = []
    for g in range(pr.shape[1] // gd):
        ys.append(jnp.dot(pr[:, g * gd:(g + 1) * gd], cc_ref[...], preferred_element_type=F32)
                  + jnp.dot(pi[:, g * gd:(g + 1) * gd], sc_ref[...], preferred_element_type=F32))
    mixed = (jnp.concatenate(ys, axis=-1) * scale).astype(BF16)
    return jnp.dot(mixed, wout_ref[...], preferred_element_type=F32)


def _fnet_stage2_kernel(zr_ref, zi_ref, tab2_ref, cc_ref, sc_ref, wout_ref, x_ref, gt_ref, o_ref,
                        p_scr, x_scr, o_scr, *, scale):
    d = gt_ref.shape[-1]
    rows = DFT_R * FNET_KB
    for j in range(FNET_KB):
        z = jnp.concatenate([zr_ref[j * DFT_R:(j + 1) * DFT_R, :], zi_ref[j * DFT_R:(j + 1) * DFT_R, :]], axis=0)
        p = jnp.dot(tab2_ref[...], z, preferred_element_type=F32)
        p_scr[j * DFT_R:(j + 1) * DFT_R, :d] = p[:DFT_R].astype(BF16)
        p_scr[j * DFT_R:(j + 1) * DFT_R, d:] = p[DFT_R:].astype(BF16)
    y = _mix_tail(p_scr[:, :d], p_scr[:, d:], cc_ref, sc_ref, wout_ref, scale)
    _cols_store(x_scr, x_ref[...].reshape(rows, d))
    for j in range(FNET_KB):
        sel = pl.ds(j, DFT_R, stride=FNET_KB)
        _cols_store_rows(o_scr, sel, _cols_load_rows(x_scr, sel) + gt_ref[...] * y[j * DFT_R:(j + 1) * DFT_R, :])
    o_ref[...] = _cols_load(o_scr).reshape(DFT_R, FNET_KB, d)


def _fnet_lat(x, mods, g_all, w_out, layer, n_batch):
    t, d = x.shape
    s = t // n_batch
    assert s == DFT_P * DFT_R
    tab1, tab2 = _seq_dft_tables(s)
    ccos, csin = _chan_dft_tables(d // FNET_GROUPS)
    mi = lambda j: (lambda b, *_: (layer * MOD_ROWS + b) * N_MOD + j)
    xv = x.reshape(n_batch, DFT_P, DFT_R, d)
    blk1 = (None, DFT_P, FNET_RB, d)
    rows1 = DFT_P * FNET_RB
    mi1 = lambda j: (lambda r, b: (layer * MOD_ROWS + b) * N_MOD + j)
    zr, zi = pl.pallas_call(
        _fnet_stage1_kernel,
        grid=(DFT_R // FNET_RB, n_batch),
        in_specs=[pl.BlockSpec(blk1, lambda r, b: (b, 0, r, 0)),
                  pl.BlockSpec((None, 1, d), lambda r, b: (layer, 0, 0)),
                  _mod_spec(d, mi1(0)), _mod_spec(d, mi1(1)),
                  pl.BlockSpec((None, 2 * rows1, rows1), lambda r, b: (r, 0, 0))],
        out_specs=[pl.BlockSpec(blk1, lambda r, b: (b, 0, r, 0))] * 2,
        out_shape=[jax.ShapeDtypeStruct((n_batch, DFT_P, DFT_R, d), BF16)] * 2,
        compiler_params=_cparams(("parallel", "parallel")),
        name="fnet_stage1",
    )(xv, g_all, mods, mods, tab1)
    zr = zr.reshape(n_batch, DFT_P * DFT_R, d)
    zi = zi.reshape(n_batch, DFT_P * DFT_R, d)
    xo = x.reshape(n_batch, DFT_R, DFT_P, d)
    blk2 = (None, DFT_R, FNET_KB, d)
    rows = FNET_KB * DFT_R
    scale = float(1.0 / np.sqrt(float(s) * (d // FNET_GROUPS)))
    out = pl.pallas_call(
        functools.partial(_fnet_stage2_kernel, scale=scale),
        grid=(n_batch, DFT_P // FNET_KB),
        in_specs=[pl.BlockSpec((None, rows, d), lambda b, k: (b, k, 0)),
                  pl.BlockSpec((None, rows, d), lambda b, k: (b, k, 0)),
                  pl.BlockSpec((2 * DFT_R, 2 * DFT_R), lambda b, k: (0, 0)),
                  pl.BlockSpec(ccos.shape, lambda b, k: (0, 0)),
                  pl.BlockSpec(csin.shape, lambda b, k: (0, 0)),
                  pl.BlockSpec((None, d, d), lambda b, k: (layer // 2, 0, 0)),
                  pl.BlockSpec(blk2, lambda b, k: (b, 0, k, 0)),
                  _mod_spec(d, mi(2))],
        out_specs=pl.BlockSpec(blk2, lambda b, k: (b, 0, k, 0)),
        out_shape=jax.ShapeDtypeStruct((n_batch, DFT_R, DFT_P, d), F32),
        scratch_shapes=[pltpu.VMEM((rows, 2 * d), BF16)] + [pltpu.VMEM((d // LANES, rows, LANES), F32)] * 2,
        compiler_params=_cparams(("parallel", "parallel")),
        name="fnet_stage2",
    )(zr, zi, tab2, ccos, csin, w_out, xo, mods)
    return out.reshape(t, d)


def _fnet_ctx_kernel(x_ref, g_ref, sh_ref, sc_ref, gt_ref, tab_ref, cc_ref, sc2_ref, wout_ref, o_ref, *, scale):
    n = x_ref.shape[0]
    h = _norm_mod(x_ref[...], g_ref[...], sh_ref[...], sc_ref[...]).astype(BF16)
    p = jnp.dot(tab_ref[...], h, preferred_element_type=F32)
    y = _mix_tail(p[:n].astype(BF16), p[n:].astype(BF16), cc_ref, sc2_ref, wout_ref, scale)
    o_ref[...] = x_ref[...] + gt_ref[...] * y


def _fnet_ctx(x, mods, g_all, w_out, layer, n_batch):
    t, d = x.shape
    n = t // n_batch
    gd = d // FNET_GROUPS
    ccos, csin = _chan_dft_tables(gd)
    tab = _ctx_dft_table(n)
    mi = lambda j: (lambda b: (layer * MOD_ROWS + CTX_ROW) * N_MOD + j)
    scale = float(1.0 / np.sqrt(float(n) * gd))
    return pl.pallas_call(
        functools.partial(_fnet_ctx_kernel, scale=scale),
        grid=(n_batch,),
        in_specs=[pl.BlockSpec((n, d), lambda b: (b, 0)),
                  pl.BlockSpec((None, 1, d), lambda b: (layer, 0, 0)),
                  _mod_spec(d, mi(0)), _mod_spec(d, mi(1)), _mod_spec(d, mi(2)),
                  pl.BlockSpec(tab.shape, lambda b: (0, 0)),
                  pl.BlockSpec(ccos.shape, lambda b: (0, 0)),
                  pl.BlockSpec(csin.shape, lambda b: (0, 0)),
                  pl.BlockSpec((None, d, d), lambda b: (layer // 2, 0, 0))],
        out_specs=pl.BlockSpec((n, d), lambda b: (b, 0)),
        out_shape=jax.ShapeDtypeStruct((t, d), F32),
        compiler_params=_cparams(("parallel",)),
        name="fnet_ctx",
    )(x, g_all, mods, mods, mods, tab, ccos, csin, w_out)


def _rope_tables(n_seq):
    rows = n_seq // GRID_W
    row = jnp.repeat(jnp.arange(rows, dtype=F32), GRID_W)
    col = jnp.tile(jnp.arange(GRID_W, dtype=F32), rows)
    inv_freq = ROPE_THETA ** (-jnp.arange(ROT_FREQS, dtype=F32) / ROT_FREQS)
    ang = jnp.stack([row[:, None] * inv_freq, col[:, None] * inv_freq], axis=1)
    cos, sin = jnp.cos(ang), jnp.sin(ang)
    zero = jnp.zeros_like(sin)
    cos_h = jnp.stack([cos, cos], axis=2).reshape(n_seq, HEAD_DIM)
    sin_lo = jnp.stack([-sin, zero], axis=2).reshape(n_seq, HEAD_DIM)
    sin_hi = jnp.stack([zero, sin], axis=2).reshape(n_seq, HEAD_DIM)
    rep = LANES // HEAD_DIM
    return jnp.tile(cos_h, (1, rep)), jnp.tile(sin_lo, (1, rep)), jnp.tile(sin_hi, (1, rep))


def _qkv_kernel(x_ref, g_ref, sh_ref, sc_ref, w_ref, *rest, rope, q_dim, kv_dim, q_scale):
    if rope:
        cos_ref, slo_ref, shi_ref, q_ref, k_ref, kt_ref, v_ref, vt_ref = rest
    else:
        q_ref, k_ref, kt_ref, v_ref, vt_ref = rest
    h = _norm_mod(x_ref[...], g_ref[...], sh_ref[...], sc_ref[...]).astype(BF16)
    qkv = jnp.dot(h, w_ref[...], preferred_element_type=F32)

    def rot(xs):
        if not rope:
            return xs
        return (xs * cos_ref[...] + pltpu.roll(xs, LANES - ROT_FREQS, axis=1) * slo_ref[...]
                + pltpu.roll(xs, ROT_FREQS, axis=1) * shi_ref[...])

    for j in range(q_dim // LANES):
        qj = rot(qkv[:, j * LANES:(j + 1) * LANES]) * q_scale
        if rope:
            q_ref[j * LANES:(j + 1) * LANES, :] = qj.T.astype(BF16)
        else:
            q_ref[:, j * LANES:(j + 1) * LANES] = qj.astype(BF16)
    k = jnp.concatenate([rot(qkv[:, q_dim + j * LANES:q_dim + (j + 1) * LANES]) for j in range(kv_dim // LANES)],
                        axis=-1)
    v = qkv[:, q_dim + kv_dim:]
    k_ref[...] = k.astype(BF16)
    kt_ref[...] = k.T.astype(BF16)
    v_ref[...] = v.astype(BF16)
    vt_ref[...] = v.T.astype(BF16)


def _qkv(x, mods, g_all, w_qkv, layer, n_batch, ctx, rope_tabs=None, tm=QKV_TM):
    t, d = x.shape
    n = t // n_batch
    tm = min(tm, n)
    tpb = n // tm
    kv_dim = N_KV_HEADS * HEAD_DIM
    q_dim = w_qkv.shape[2] - 2 * kv_dim
    mi = lambda j: _mod_index(layer, j, tpb, ctx)
    in_specs = [pl.BlockSpec((tm, d), lambda i: (i, 0)),
                pl.BlockSpec((None, 1, d), lambda i: (layer, 0, 0)),
                _mod_spec(d, mi(0)), _mod_spec(d, mi(1)),
                pl.BlockSpec((None,) + w_qkv.shape[1:], lambda i: (layer // 2, 0, 0))]
    args = [x, g_all, mods, mods, w_qkv]
    rope = rope_tabs is not None
    if rope:
        in_specs += [pl.BlockSpec((tm, LANES), lambda i: (i % tpb, 0))] * 3
        args += list(rope_tabs)
    return pl.pallas_call(
        functools.partial(_qkv_kernel, rope=rope, q_dim=q_dim, kv_dim=kv_dim, q_scale=HEAD_DIM ** -0.5),
        grid=(t // tm,),
        in_specs=in_specs,
        out_specs=[(pl.BlockSpec((None, q_dim, tm), lambda i: (i // tpb, 0, i % tpb)) if rope
                    else pl.BlockSpec((tm, q_dim), lambda i: (i, 0))),
                   pl.BlockSpec((tm, kv_dim), lambda i: (i, 0)),
                   pl.BlockSpec((None, kv_dim, tm), lambda i: (i // tpb, 0, i % tpb)),
                   pl.BlockSpec((tm, kv_dim), lambda i: (i, 0)),
                   pl.BlockSpec((None, kv_dim, tm), lambda i: (i // tpb, 0, i % tpb))],
        out_shape=[jax.ShapeDtypeStruct((n_batch, q_dim, n) if rope else (t, q_dim), BF16),
                   jax.ShapeDtypeStruct((t, kv_dim), BF16),
                   jax.ShapeDtypeStruct((n_batch, kv_dim, n), BF16),
                   jax.ShapeDtypeStruct((t, kv_dim), BF16),
                   jax.ShapeDtypeStruct((n_batch, kv_dim, n), BF16)],
        compiler_params=_cparams(("parallel",)),
        name="qkv_ctx" if ctx else "qkv_lat",
    )(*args)


ATTN_RB = 64


def _heads_attend(q_ref, kt, v, bias, sink_ref, o_ref):
    nq = q_ref.shape[0]
    n_heads = q_ref.shape[1] // HEAD_DIM
    group = n_heads // N_KV_HEADS
    outs = [None] * n_heads
    for g in range(N_KV_HEADS):
        heads = range(g * group, (g + 1) * group)
        qg = jnp.concatenate([q_ref[:, hd * HEAD_DIM:(hd + 1) * HEAD_DIM] for hd in heads], axis=0)
        s = jnp.dot(qg, kt[g * HEAD_DIM:(g + 1) * HEAD_DIM, :], preferred_element_type=F32)
        p_rows, den_rows = [], []
        for r0 in range(0, s.shape[0], ATTN_RB):
            rows = slice(r0, r0 + ATTN_RB)
            parts = [s[rows, k * LANES:(k + 1) * LANES] for k in range(s.shape[1] // LANES)]
            if bias is not None:
                parts[0] = parts[0] + bias[0][rows]
                parts[2] = parts[2] + bias[1][rows]
            sink = sink_ref[heads[r0 // nq]]
            m = jnp.maximum(jnp.max(functools.reduce(jnp.maximum, parts), axis=-1, keepdims=True), sink)
            ps = [jnp.exp(part - m) for part in parts]
            den_rows.append(jnp.sum(functools.reduce(jnp.add, ps), axis=-1, keepdims=True) + jnp.exp(sink - m))
            p_rows.append(jnp.concatenate(ps, axis=-1).astype(BF16))
        pv = jnp.dot(jnp.concatenate(p_rows, axis=0), v, preferred_element_type=F32)
        on = pv[:, g * HEAD_DIM:(g + 1) * HEAD_DIM] / jnp.concatenate(den_rows, axis=0)
        for k, hd in enumerate(heads):
            outs[hd] = on[k * nq:(k + 1) * nq, :]
    o_ref[...] = jnp.concatenate(outs, axis=-1).astype(BF16)


def _attn_lat_kernel(sink_ref, qt_ref, kp_ref, kc_ref, kn_ref, vtp_ref, vtc_ref, vtn_ref, kx_ref, vtx_ref,
                     blo_ref, bhi_ref, ot_ref, k_scr, vt_scr):
    w = kc_ref.shape[0]
    n_ctx = kx_ref.shape[0]
    nq = qt_ref.shape[1]
    n_heads = qt_ref.shape[0] // HEAD_DIM
    group = n_heads // N_KV_HEADS
    for c, (kr, vr) in enumerate(((kp_ref, vtp_ref), (kc_ref, vtc_ref), (kn_ref, vtn_ref))):
        k_scr[c * w:(c + 1) * w, :] = kr[...]
        vt_scr[:, c * w:(c + 1) * w] = vr[...]
    k_scr[3 * w:3 * w + n_ctx, :] = kx_ref[...]
    vt_scr[:, 3 * w:3 * w + n_ctx] = vtx_ref[...]
    n_chunks = k_scr.shape[0] // w
    for g in range(N_KV_HEADS):
        heads = range(g * group, (g + 1) * group)
        qg = jnp.concatenate([qt_ref[hd * HEAD_DIM:(hd + 1) * HEAD_DIM, :] for hd in heads], axis=1)
        st = jnp.dot(k_scr[:, g * HEAD_DIM:(g + 1) * HEAD_DIM], qg, preferred_element_type=F32)
        p_cols, den_cols = [], []
        for k, hd in enumerate(heads):
            cols = slice(k * nq, (k + 1) * nq)
            parts = [st[c * w:(c + 1) * w, cols] for c in range(n_chunks)]
            parts[0] = parts[0] + blo_ref[...]
            parts[2] = parts[2] + bhi_ref[...]
            sink = sink_ref[hd]
            m = jnp.maximum(jnp.max(functools.reduce(jnp.maximum, parts), axis=0, keepdims=True), sink)
            ps = [jnp.exp(part - m) for part in parts]
            den_cols.append(jnp.sum(functools.reduce(jnp.add, ps), axis=0, keepdims=True) + jnp.exp(sink - m))
            p_cols.append(jnp.concatenate(ps, axis=0).astype(BF16))
        ot = jnp.dot(vt_scr[g * HEAD_DIM:(g + 1) * HEAD_DIM, :], jnp.concatenate(p_cols, axis=1),
                     preferred_element_type=F32)
        ot = ot / jnp.concatenate(den_cols, axis=1)
        for k, hd in enumerate(heads):
            ot_ref[hd * HEAD_DIM:(hd + 1) * HEAD_DIM, :] = ot[:, k * nq:(k + 1) * nq].astype(BF16)


def _attn_lat(qt, k, vt, kx, vtx, sinks, n_batch):
    _, qd, s = qt.shape
    n_ctx = kx.shape[0] // n_batch
    w = WINDOW
    nb = s // w
    kvd = k.shape[1]
    j_all = 3 * w + n_ctx
    ki = np.arange(w)[:, None]
    qi = np.arange(w)[None, :]
    lo = np.where(ki >= qi, 0.0, NEG).astype(np.float32)
    hi = np.where(ki <= qi, 0.0, NEG).astype(np.float32)
    off = np.full((w, w), NEG, np.float32)
    blo = jnp.asarray(np.stack([lo, off]))
    bhi = jnp.asarray(np.stack([hi, off]))
    prev = lambda b, i: (b * nb + jnp.maximum(i - 1, 0), 0)
    nxt = lambda b, i: (b * nb + jnp.minimum(i + 1, nb - 1), 0)
    tprev = lambda b, i: (b, 0, jnp.maximum(i - 1, 0))
    tnxt = lambda b, i: (b, 0, jnp.minimum(i + 1, nb - 1))
    return pl.pallas_call(
        _attn_lat_kernel,
        grid=(n_batch, nb),
        in_specs=[pl.BlockSpec(memory_space=pltpu.SMEM),
                  pl.BlockSpec((None, qd, w), lambda b, i: (b, 0, i)),
                  pl.BlockSpec((w, kvd), prev),
                  pl.BlockSpec((w, kvd), lambda b, i: (b * nb + i, 0)),
                  pl.BlockSpec((w, kvd), nxt),
                  pl.BlockSpec((None, kvd, w), tprev),
                  pl.BlockSpec((None, kvd, w), lambda b, i: (b, 0, i)),
                  pl.BlockSpec((None, kvd, w), tnxt),
                  pl.BlockSpec((n_ctx, kvd), lambda b, i: (b, 0)),
                  pl.BlockSpec((None, kvd, n_ctx), lambda b, i: (b, 0, 0)),
                  pl.BlockSpec((None, w, w), lambda b, i: (jnp.where(i == 0, 1, 0), 0, 0)),
                  pl.BlockSpec((None, w, w), lambda b, i: (jnp.where(i == nb - 1, 1, 0), 0, 0))],
        out_specs=pl.BlockSpec((None, qd, w), lambda b, i: (b, 0, i)),
        out_shape=jax.ShapeDtypeStruct((n_batch, qd, s), BF16),
        scratch_shapes=[pltpu.VMEM((j_all, kvd), BF16), pltpu.VMEM((kvd, j_all), BF16)],
        compiler_params=_cparams(("parallel", "parallel")),
        name="attn_lat",
    )(sinks, qt, k, k, k, vt, vt, vt, kx, vtx, blo, bhi)


def _attn_ctx_kernel(sink_ref, q_ref, kt_ref, v_ref, o_ref):
    _heads_attend(q_ref, kt_ref[...], v_ref[...], None, sink_ref, o_ref)


def _attn_ctx(q, kt, v, sinks, n_batch):
    t, qd = q.shape
    n = t // n_batch
    kvd = v.shape[1]
    return pl.pallas_call(
        _attn_ctx_kernel,
        grid=(n_batch,),
        in_specs=[pl.BlockSpec(memory_space=pltpu.SMEM),
                  pl.BlockSpec((n, qd), lambda b: (b, 0)),
                  pl.BlockSpec((None, kvd, n), lambda b: (b, 0, 0)),
                  pl.BlockSpec((n, kvd), lambda b: (b, 0))],
        out_specs=pl.BlockSpec((n, qd), lambda b: (b, 0)),
        out_shape=jax.ShapeDtypeStruct((t, qd), BF16),
        compiler_params=_cparams(("parallel",)),
        name="attn_ctx",
    )(sinks, q, kt, v)


ROUTE_I1, ROUTE_I2, ROUTE_R1, ROUTE_R2, ROUTE_G1, ROUTE_G2 = range(6)
ROUTE_ROWS = 8
TAB_START, TAB_COUNT, TAB_USED, TAB_SIZE = 0, N_EXPERTS, 2 * N_EXPERTS, 2 * N_EXPERTS + 8
ROW_UNROLL = 8


def _router_kernel(x_ref, g_ref, sh_ref, sc_ref, wr_ref, tri_ref, route_ref, idx_ref, cnt_ref, carry_scr):
    @pl.when(pl.program_id(0) == 0)
    def _():
        carry_scr[...] = jnp.zeros_like(carry_scr)

    h = _norm_mod(x_ref[...], g_ref[...], sh_ref[...], sc_ref[...])
    hi = h.astype(BF16)
    lo = (h - hi.astype(F32)).astype(BF16)
    a = jnp.dot(hi, wr_ref[...], preferred_element_type=F32)
    logits = (a[:, :LANES] + a[:, LANES:]) + jnp.dot(lo, wr_ref[:, :LANES], preferred_element_type=F32)
    lane = lax.broadcasted_iota(jnp.int32, logits.shape, 1)
    lane_f = lane.astype(F32)
    logits = jnp.where(lane < N_EXPERTS, logits, -jnp.inf)
    m1 = jnp.max(logits, axis=-1, keepdims=True)
    i1 = jnp.min(jnp.where(logits == m1, lane_f, float(LANES)), axis=-1, keepdims=True)
    oh1 = lane_f == i1
    rest = jnp.where(oh1, -jnp.inf, logits)
    m2 = jnp.max(rest, axis=-1, keepdims=True)
    i2 = jnp.min(jnp.where(rest == m2, lane_f, float(LANES)), axis=-1, keepdims=True)
    oh2 = lane_f == i2
    e2 = jnp.exp(m2 - m1)
    g1 = 1.0 / (1.0 + e2)
    g2 = e2 / (1.0 + e2)
    sel = jnp.where(oh1, 1.0, 0.0) + jnp.where(oh2, 1.0, 0.0)
    before = jnp.dot(tri_ref[...], sel.astype(BF16), preferred_element_type=F32) + carry_scr[0:1, :]
    r1 = jnp.sum(jnp.where(oh1, before, 0.0), axis=-1, keepdims=True)
    r2 = jnp.sum(jnp.where(oh2, before, 0.0), axis=-1, keepdims=True)
    total = carry_scr[0:1, :] + jnp.sum(sel, axis=0, keepdims=True)
    carry_scr[...] = jnp.broadcast_to(total, carry_scr.shape)
    cnt_ref[...] = jnp.broadcast_to(total, cnt_ref.shape)
    rec = jnp.zeros_like(logits)
    for ln, val in ((ROUTE_I1, i1), (ROUTE_I2, i2), (ROUTE_R1, r1), (ROUTE_R2, r2), (ROUTE_G1, g1), (ROUTE_G2, g2)):
        rec = jnp.where(lane == ln, val, rec)
    route_ref[...] = rec
    idx_ref[...] = rec.T[:ROUTE_ROWS, :].astype(jnp.int32)


def _router(x, mods, g_all, w_router, layer, tiles_per_batch, ctx, tm):
    t, d = x.shape
    mi = lambda j: _mod_index(layer, j, tiles_per_batch // tm if not ctx else 1, ctx)
    wr = jnp.zeros((d, LANES), F32).at[:, :N_EXPERTS].set(w_router)
    whi = wr.astype(BF16)
    wr = jnp.concatenate([whi, (wr - whi.astype(F32)).astype(BF16)], axis=1)
    tri = jnp.asarray(np.tril(np.ones((tm, tm), np.float32), -1), BF16)
    return pl.pallas_call(
        _router_kernel,
        grid=(t // tm,),
        in_specs=[pl.BlockSpec((tm, d), lambda i: (i, 0)),
                  pl.BlockSpec((None, 1, d), lambda i: (layer, 0, 0)),
                  _mod_spec(d, mi(3)), _mod_spec(d, mi(4)),
                  pl.BlockSpec((d, 2 * LANES), lambda i: (0, 0)),
                  pl.BlockSpec((tm, tm), lambda i: (0, 0))],
        out_specs=[pl.BlockSpec((tm, LANES), lambda i: (i, 0)),
                   pl.BlockSpec((None, ROUTE_ROWS, tm), lambda i: (i, 0, 0)),
                   pl.BlockSpec((8, LANES), lambda i: (0, 0))],
        out_shape=[jax.ShapeDtypeStruct((t, LANES), F32),
                   jax.ShapeDtypeStruct((t // tm, ROUTE_ROWS, tm), jnp.int32),
                   jax.ShapeDtypeStruct((8, LANES), F32)],
        scratch_shapes=[pltpu.VMEM((8, LANES), F32)],
        compiler_params=_cparams(("arbitrary",)),
        name="router",
    )(x, g_all, mods, mods, wr, tri)


SUB = 8


def _to_row_tiles(ref, val):
    n = val.shape[0]
    for k in range(SUB):
        ref[pl.ds(k, n, stride=SUB), :] = val[:, k * LANES:(k + 1) * LANES]


def _from_row_tiles(ref):
    n = ref.shape[0] // SUB
    return jnp.concatenate([ref[pl.ds(k, n, stride=SUB), :] for k in range(SUB)], axis=-1)


def _row_copy(src, dst, src_row, dst_row, sem):
    return pltpu.make_async_copy(src.at[pl.ds(pl.multiple_of(src_row * SUB, SUB), SUB)],
                                 dst.at[pl.ds(pl.multiple_of(dst_row * SUB, SUB), SUB)], sem)


def _for_rows(tm, body):
    def blk(i, _):
        base = pl.multiple_of(i * ROW_UNROLL, ROW_UNROLL)
        for j in range(ROW_UNROLL):
            body(base + j, j)
        return 0

    lax.fori_loop(0, tm // ROW_UNROLL, blk, 0)


ZERO_SEM = 4


def _dispatch_kernel(tab_ref, slot_ref, x_ref, g_ref, sh_ref, sc_ref, xs_ref, h_scr, zero_scr, sems, *, group):
    tm = x_ref.shape[0]
    n_groups = xs_ref.shape[0] // (group * SUB)
    step = pl.program_id(0)
    n_steps = pl.num_programs(0)
    cur = lax.rem(step, 2)

    def wait_buffer(b):
        for k in range(2):
            pltpu.make_async_copy(h_scr.at[b], xs_ref.at[pl.ds(0, tm * SUB)], sems.at[2 * b + k]).wait()

    @pl.when(step >= 2)
    def _():
        wait_buffer(cur)

    _to_row_tiles(h_scr.at[cur], _norm_mod(x_ref[...], g_ref[...], sh_ref[...], sc_ref[...]))

    def issue(r, j):
        for k in range(2):
            _row_copy(h_scr.at[cur], xs_ref, r, slot_ref[0, k, r], sems.at[2 * cur + k]).start(priority=j % 2)

    _for_rows(tm, issue)

    @pl.when(jnp.logical_and(step == n_steps - 1, step >= 1))
    def _():
        wait_buffer(1 - cur)

    @pl.when(step == n_steps - 1)
    def _():
        wait_buffer(cur)
        zero_scr[...] = jnp.zeros_like(zero_scr)
        for e in range(N_EXPERTS):
            n = tab_ref[TAB_COUNT + e]
            n_pad = lax.rem(group - lax.rem(n, group), group)
            first = tab_ref[TAB_START + e] + n

            def fill(k, _, first=first):
                _row_copy(zero_scr, xs_ref, 0, first + k, sems.at[ZERO_SEM]).start()
                return 0

            def fill_done(k, _):
                _row_copy(zero_scr, xs_ref, 0, 0, sems.at[ZERO_SEM]).wait()
                return 0

            lax.fori_loop(0, n_pad, fill, 0)
            lax.fori_loop(0, n_pad, fill_done, 0)

        def clear(j, _):
            row = pl.multiple_of(j * (group * SUB), group * SUB)
            cp = pltpu.make_async_copy(zero_scr, xs_ref.at[pl.ds(row, group * SUB)], sems.at[ZERO_SEM])
            cp.start()
            cp.wait()
            return 0

        lax.fori_loop(tab_ref[TAB_USED], n_groups, clear, 0)


def _dispatch(x, tab, slots, mods, g_all, layer, tiles_per_batch, ctx, n_groups, group, tm):
    t, d = x.shape
    mi = lambda j: _mod_index(layer, j, tiles_per_batch // tm if not ctx else 1, ctx)
    return pl.pallas_call(
        functools.partial(_dispatch_kernel, group=group),
        grid=(t // tm,),
        in_specs=[pl.BlockSpec(memory_space=pltpu.SMEM),
                  pl.BlockSpec((1, 2, tm), lambda i: (i, 0, 0), memory_space=pltpu.SMEM),
                  pl.BlockSpec((tm, d), lambda i: (i, 0)),
                  pl.BlockSpec((None, 1, d), lambda i: (layer, 0, 0)),
                  _mod_spec(d, mi(3)), _mod_spec(d, mi(4))],
        out_specs=pl.BlockSpec(memory_space=pl.ANY),
        out_shape=jax.ShapeDtypeStruct((n_groups * group * SUB, LANES), F32),
        scratch_shapes=[pltpu.VMEM((2, tm * SUB, LANES), F32), pltpu.VMEM((group * SUB, LANES), F32),
                        pltpu.SemaphoreType.DMA((ZERO_SEM + 1,))],
        compiler_params=_cparams(("arbitrary",)),
        name="moe_dispatch",
    )(tab, slots, x, g_all, mods, mods)


def _moe_ffn_kernel(be_ref, bv_ref, xs_ref, wg_ref, wu_ref, wd_ref, ys_ref, h_scr, acc_scr):
    del be_ref
    i = pl.program_id(0)
    f = pl.program_id(1)
    last = f == pl.num_programs(1) - 1
    valid = bv_ref[i] > 0

    @pl.when(jnp.logical_and(valid, f == 0))
    def _():
        h_scr[...] = _from_row_tiles(xs_ref).astype(BF16)
        acc_scr[...] = jnp.zeros_like(acc_scr)

    @pl.when(valid)
    def _():
        h = h_scr[...]
        a = jnp.dot(h, wg_ref[...], preferred_element_type=F32)
        u = jnp.dot(h, wu_ref[...], preferred_element_type=F32)
        t = (a * _sigmoid(a)) * u
        acc_scr[...] += jnp.dot(t.astype(BF16), wd_ref[...], preferred_element_type=F32)

    @pl.when(jnp.logical_and(valid, last))
    def _():
        _to_row_tiles(ys_ref, acc_scr[...])

    @pl.when(jnp.logical_and(jnp.logical_not(valid), last))
    def _():
        ys_ref[...] = jnp.zeros_like(ys_ref)


def _moe_ffn(xs, block_expert, block_valid, wg, wu, wd, layer, tm, tf=MOE_TF):
    d, ff = wg.shape[2:]
    assert d == SUB * LANES and xs.shape[1] == LANES
    cap = xs.shape[0] // SUB
    tf = min(tf, ff)
    nf = ff // tf
    sj = layer // 2
    fsel = lambda i, f, bv: jnp.where(bv[i] > 0, f, nf - 1)
    grid_spec = pltpu.PrefetchScalarGridSpec(
        num_scalar_prefetch=2,
        grid=(cap // tm, nf),
        in_specs=[pl.BlockSpec((tm * SUB, LANES), lambda i, f, be, bv: (i, 0)),
                  pl.BlockSpec((None, None, d, tf), lambda i, f, be, bv: (sj, be[i], 0, fsel(i, f, bv))),
                  pl.BlockSpec((None, None, d, tf), lambda i, f, be, bv: (sj, be[i], 0, fsel(i, f, bv))),
                  pl.BlockSpec((None, None, tf, d), lambda i, f, be, bv: (sj, be[i], fsel(i, f, bv), 0))],
        out_specs=pl.BlockSpec((tm * SUB, LANES), lambda i, f, be, bv: (i, 0)),
        scratch_shapes=[pltpu.VMEM((tm, d), BF16), pltpu.VMEM((tm, d), F32)],
    )
    return pl.pallas_call(
        _moe_ffn_kernel,
        grid_spec=grid_spec,
        out_shape=jax.ShapeDtypeStruct(xs.shape, F32),
        compiler_params=_cparams(("parallel", "arbitrary")),
        name="moe_ffn",
    )(block_expert, block_valid, xs, wg, wu, wd)


def _combine_kernel(slot_ref, next_slot_ref, route_ref, x_ref, gt_ref, fg_ref, ys_ref, o_ref, buf, sems, *, final_norm):
    tm = x_ref.shape[0]
    step = pl.program_id(0)
    cur = lax.rem(step, 2)

    def gather(idx_ref, b):
        def issue(r, j):
            for k in range(2):
                _row_copy(ys_ref, buf.at[b, k], idx_ref[0, k, r], r, sems.at[2 * b + k]).start(priority=j % 2)

        _for_rows(tm, issue)

    @pl.when(step == 0)
    def _():
        gather(slot_ref, cur)

    @pl.when(step + 1 < pl.num_programs(0))
    def _():
        gather(next_slot_ref, 1 - cur)

    for k in range(2):
        pltpu.make_async_copy(ys_ref.at[pl.ds(0, tm * SUB)], buf.at[cur, k], sems.at[2 * cur + k]).wait()
    rec = route_ref[...]
    g1 = rec[:, ROUTE_G1:ROUTE_G1 + 1]
    g2 = rec[:, ROUTE_G2:ROUTE_G2 + 1]
    y = _from_row_tiles(buf.at[cur, 0]) * g1 + _from_row_tiles(buf.at[cur, 1]) * g2
    out = x_ref[...] + gt_ref[...] * y
    if final_norm:
        out = out * lax.rsqrt(jnp.mean(out * out, axis=-1, keepdims=True) + EPS) * fg_ref[...]
    o_ref[...] = out


def _combine(ys, slots, route, x, mods, final_g, layer, tiles_per_batch, ctx, final_norm, tm):
    t, d = x.shape
    mi = _mod_index(layer, 5, tiles_per_batch // tm if not ctx else 1, ctx)
    n_tiles = t // tm
    return pl.pallas_call(
        functools.partial(_combine_kernel, final_norm=final_norm),
        grid=(n_tiles,),
        in_specs=[pl.BlockSpec((1, 2, tm), lambda i: (i, 0, 0), memory_space=pltpu.SMEM),
                  pl.BlockSpec((1, 2, tm), lambda i: (jnp.minimum(i + 1, n_tiles - 1), 0, 0), memory_space=pltpu.SMEM),
                  pl.BlockSpec((tm, LANES), lambda i: (i, 0)),
                  pl.BlockSpec((tm, d), lambda i: (i, 0)),
                  _mod_spec(d, mi),
                  pl.BlockSpec((1, d), lambda i: (0, 0)),
                  pl.BlockSpec(memory_space=pl.ANY)],
        out_specs=pl.BlockSpec((tm, d), lambda i: (i, 0)),
        out_shape=jax.ShapeDtypeStruct((t, d), F32),
        scratch_shapes=[pltpu.VMEM((2, 2, tm * SUB, LANES), F32), pltpu.SemaphoreType.DMA((4,))],
        compiler_params=_cparams(("arbitrary",)),
        name="moe_combine",
    )(slots, slots, route, x, mods, final_g.reshape(1, d), ys)


def _moe(x, mods, g_all, w_router, wg, wu, wd, final_g, layer, tiles_per_batch, ctx, final_norm):
    t, d = x.shape
    group = MOE_TM if 2 * t >= 4 * N_EXPERTS * MOE_TM else MOE_TM // 2
    row_tm = min(ROW_TM, t)
    route, idx, counts = _router(x, mods, g_all, w_router, layer, tiles_per_batch, ctx, row_tm)
    counts = counts[0, :N_EXPERTS].astype(jnp.int32)
    groups = (counts + group - 1) // group
    ends = jnp.cumsum(groups)
    starts = (ends - groups) * group
    n_groups = (2 * t + group - 1) // group + N_EXPERTS
    tab = jnp.zeros((TAB_SIZE,), jnp.int32)
    tab = tab.at[TAB_START:TAB_START + N_EXPERTS].set(starts).at[TAB_COUNT:TAB_COUNT + N_EXPERTS].set(counts)
    tab = tab.at[TAB_USED].set(ends[-1])
    gi = jnp.arange(n_groups, dtype=jnp.int32)
    block_expert = jnp.minimum(jnp.sum(gi[:, None] >= ends[None, :], axis=1), N_EXPERTS - 1).astype(jnp.int32)
    block_valid = (gi < ends[-1]).astype(jnp.int32)
    def slot_rows(e, r):
        base = functools.reduce(jnp.add, [jnp.where(e == k, starts[k], 0) for k in range(N_EXPERTS)])
        return base + r
    slots = jnp.stack([slot_rows(idx[:, ROUTE_I1, :], idx[:, ROUTE_R1, :]),
                       slot_rows(idx[:, ROUTE_I2, :], idx[:, ROUTE_R2, :])], axis=1)
    xs = _dispatch(x, tab, slots, mods, g_all, layer, tiles_per_batch, ctx, n_groups, group, row_tm)
    ys = _moe_ffn(xs, block_expert, block_valid, wg, wu, wd, layer, group)
    return _combine(ys, slots, route, x, mods, final_g, layer, tiles_per_batch, ctx, final_norm, row_tm)


def kernel(x, c, ctx, c_ctx, ada_w, ada_b, norm_mix_g, norm_ffn_g, fnet_w_out, attn_w_qkv, attn_w_o, attn_sinks,
           ffn_w_gate, ffn_w_up, ffn_w_down, moe_w_router, moe_w_gate, moe_w_up, moe_w_down, final_norm_g):
    n_batch, s, d = x.shape
    n_ctx = ctx.shape[1]
    depth = ada_w.shape[0]
    mods = _ada(c, c_ctx, ada_w, ada_b)
    g_mix = norm_mix_g.reshape(depth, 1, d)
    g_ffn = norm_ffn_g.reshape(depth, 1, d)
    rope_tabs = _rope_tables(s)
    x_lat = x.reshape(n_batch * s, d)
    x_ctx = ctx.reshape(n_batch * n_ctx, d)
    bf = lambda a: a.astype(BF16)
    w_out, w_qkv, w_o = bf(fnet_w_out), bf(attn_w_qkv), bf(attn_w_o)
    wg, wu, wd = bf(ffn_w_gate), bf(ffn_w_up), bf(ffn_w_down)
    eg, eu, ed = bf(moe_w_gate), bf(moe_w_up), bf(moe_w_down)
    for i in range(depth):
        j = i // 2
        last = i == depth - 1
        if i % 2 == 0:
            x_lat = _fnet_lat(x_lat, mods, g_mix, w_out, i, n_batch)
            if not last:
                x_ctx = _fnet_ctx(x_ctx, mods, g_mix, w_out, i, n_batch)
            x_lat = _ffn_dense(x_lat, mods, g_ffn, wg, wu, wd, i, s, False)
            if not last:
                x_ctx = _ffn_dense(x_ctx, mods, g_ffn, wg, wu, wd, i, n_ctx, True)
        else:
            qt, k, _, _, vt = _qkv(x_lat, mods, g_mix, w_qkv, i, n_batch, False, rope_tabs)
            qx, kx, ktx, vx, vtx = _qkv(x_ctx, mods, g_mix, w_qkv, i, n_batch, True)
            ot = _attn_lat(qt, k, vt, kx, vtx, attn_sinks[j], n_batch)
            x_lat = _proj_res(ot, w_o, x_lat, mods, i, 2, s, False)
            if not last:
                ox = _attn_ctx(qx, ktx, vx, attn_sinks[j], n_batch)
                x_ctx = _proj_res(ox, w_o, x_ctx, mods, i, 2, n_ctx, True)
            x_lat = _moe(x_lat, mods, g_ffn, moe_w_router[j], eg, eu, ed, final_norm_g, i, s, False, last)
            if not last:
                x_ctx = _moe(x_ctx, mods, g_ffn, moe_w_router[j], eg, eu, ed, final_norm_g, i, n_ctx, True, False)
    return x_lat.reshape(n_batch, s, d)
```

```python
import functools

import numpy as np
import jax
import jax.numpy as jnp
from jax import lax
from jax.experimental import pallas as pl
from jax.experimental.pallas import tpu as pltpu

F32 = jnp.float32
BF16 = jnp.bfloat16

GRID_W = 64
N_MOD = 6
EPS = 1e-6
FNET_GROUPS = 4
HEAD_DIM = 64
N_KV_HEADS = 4
WINDOW = 128
ROPE_THETA = 10000.0
ROT_FREQS = HEAD_DIM // 4
N_EXPERTS = 8
MOD_ROWS = 8
CTX_ROW = MOD_ROWS - 1
LANES = 128
NEG = -1e30
VMEM_LIMIT = 56 * 1024 * 1024

FFN_TM = 512
FFN_TF = 1792
MOE_TM = 512
MOE_TF = 1792
ROW_TM = 512
PROJ_TM = 1024
QKV_TM = 1024
DFT_P = 64
DFT_R = 128
FNET_RB = 16
FNET_KB = 8


def _cparams(sem):
    return pltpu.CompilerParams(dimension_semantics=sem, vmem_limit_bytes=VMEM_LIMIT)


def _sigmoid(a):
    return 1.0 / (1.0 + jnp.exp(-a))


def _norm_mod(x, g, shift, scale):
    xn = x * lax.rsqrt(jnp.mean(x * x, axis=-1, keepdims=True) + EPS)
    return (xn * g) * (1.0 + scale) + shift


def _mod_spec(d, idx_fn):
    return pl.BlockSpec((None, 1, d), lambda *ids: (idx_fn(*ids), 0, 0))


def _mod_index(layer, j, tiles_per_batch, ctx):
    def fn(i, *_):
        b = CTX_ROW if ctx else i // tiles_per_batch
        return (layer * MOD_ROWS + b) * N_MOD + j
    return fn


def _ada_kernel(c_ref, w_ref, b_ref, o_ref):
    cc = c_ref[...]
    s = cc * _sigmoid(cc)
    o_ref[...] = jnp.dot(s, w_ref[...], preferred_element_type=F32,
                         precision=lax.Precision.HIGHEST) + b_ref[...]


def _ada(c, c_ctx, ada_w, ada_b):
    depth, d, _ = ada_w.shape
    nb = c.shape[0]
    assert nb < MOD_ROWS
    cc = jnp.concatenate([c, jnp.zeros((CTX_ROW - nb, d), F32), c_ctx[None]], axis=0)
    out = pl.pallas_call(
        _ada_kernel,
        grid=(depth, N_MOD),
        in_specs=[pl.BlockSpec((MOD_ROWS, d), lambda l, j: (0, 0)),
                  pl.BlockSpec((None, d, d), lambda l, j: (l, 0, j)),
                  pl.BlockSpec((None, 1, d), lambda l, j: (l * N_MOD + j, 0, 0))],
        out_specs=pl.BlockSpec((None, MOD_ROWS, d), lambda l, j: (l, 0, j)),
        out_shape=jax.ShapeDtypeStruct((depth, MOD_ROWS, N_MOD * d), F32),
        compiler_params=_cparams(("parallel", "parallel")),
        name="ada",
    )(cc, ada_w, ada_b.reshape(depth * N_MOD, 1, d))
    return out.reshape(depth * MOD_ROWS * N_MOD, 1, d)


def _ffn_kernel(x_ref, g_ref, sh_ref, sc_ref, gt_ref, wg_ref, wu_ref, wd_ref, o_ref, h_scr, acc_scr):
    f = pl.program_id(1)

    @pl.when(f == 0)
    def _():
        h_scr[...] = _norm_mod(x_ref[...], g_ref[...], sh_ref[...], sc_ref[...]).astype(BF16)
        acc_scr[...] = jnp.zeros_like(acc_scr)

    h = h_scr[...]
    a = jnp.dot(h, wg_ref[...], preferred_element_type=F32)
    u = jnp.dot(h, wu_ref[...], preferred_element_type=F32)
    t = (a * _sigmoid(a)) * u
    acc_scr[...] += jnp.dot(t.astype(BF16), wd_ref[...], preferred_element_type=F32)

    @pl.when(f == pl.num_programs(1) - 1)
    def _():
        o_ref[...] = x_ref[...] + gt_ref[...] * acc_scr[...]


def _ffn_dense(x, mods, g_all, wg, wu, wd, layer, tiles_per_batch, ctx, tm=FFN_TM, tf=FFN_TF):
    t, d = x.shape
    ff = wg.shape[2]
    tm = min(tm, t)
    tf = min(tf, ff)
    sj = layer // 2
    mi = functools.partial(_mod_index, layer, tiles_per_batch=tiles_per_batch // tm if not ctx else 1, ctx=ctx)
    return pl.pallas_call(
        _ffn_kernel,
        grid=(t // tm, ff // tf),
        in_specs=[pl.BlockSpec((tm, d), lambda i, f: (i, 0)),
                  pl.BlockSpec((None, 1, d), lambda i, f: (layer, 0, 0)),
                  _mod_spec(d, mi(3)), _mod_spec(d, mi(4)), _mod_spec(d, mi(5)),
                  pl.BlockSpec((None, d, tf), lambda i, f: (sj, 0, f)),
                  pl.BlockSpec((None, d, tf), lambda i, f: (sj, 0, f)),
                  pl.BlockSpec((None, tf, d), lambda i, f: (sj, f, 0))],
        out_specs=pl.BlockSpec((tm, d), lambda i, f: (i, 0)),
        out_shape=jax.ShapeDtypeStruct((t, d), F32),
        scratch_shapes=[pltpu.VMEM((tm, d), BF16), pltpu.VMEM((tm, d), F32)],
        compiler_params=_cparams(("parallel", "arbitrary")),
        name="ffn_dense",
    )(x, g_all, mods, mods, mods, wg, wu, wd)


def _proj_res_kernel(a_ref, w_ref, x_ref, gt_ref, o_ref, *, transposed):
    dims = (((0,), (0,)), ((), ())) if transposed else (((1,), (0,)), ((), ()))
    y = lax.dot_general(a_ref[...], w_ref[...], dims, preferred_element_type=F32)
    o_ref[...] = x_ref[...] + gt_ref[...] * y


def _proj_res(a, w, x, mods, layer, j, tiles_per_batch, ctx, tm=PROJ_TM):
    t, d = x.shape
    transposed = a.ndim == 3
    k = a.shape[1]
    tm = min(tm, t)
    tpb = tiles_per_batch // tm
    mi = _mod_index(layer, j, tpb if not ctx else 1, ctx)
    a_spec = (pl.BlockSpec((None, k, tm), lambda i: (i // tpb, 0, i % tpb)) if transposed
              else pl.BlockSpec((tm, k), lambda i: (i, 0)))
    return pl.pallas_call(
        functools.partial(_proj_res_kernel, transposed=transposed),
        grid=(t // tm,),
        in_specs=[a_spec,
                  pl.BlockSpec((None, k, d), lambda i: (layer // 2, 0, 0)),
                  pl.BlockSpec((tm, d), lambda i: (i, 0)),
                  _mod_spec(d, mi)],
        out_specs=pl.BlockSpec((tm, d), lambda i: (i, 0)),
        out_shape=jax.ShapeDtypeStruct((t, d), F32),
        compiler_params=_cparams(("parallel",)),
        name="proj_res",
    )(a, w, x, mods)


def _dft_angles(n):
    a = np.arange(n)
    return 2.0 * np.pi * ((a[:, None] * a[None, :]) % n) / n


def _seq_dft_tables(n):
    kp = np.arange(DFT_P)[None, :, None]
    p = np.arange(DFT_P)[None, None, :]
    r = np.arange(DFT_R)[:, None, None]
    th = 2.0 * np.pi * ((kp * (DFT_R * p + r)) % n) / n
    tab1 = np.stack([np.cos(th), -np.sin(th)], axis=1)
    tab1 = tab1.reshape(DFT_R // FNET_RB, FNET_RB, 2, DFT_P, DFT_P)
    kron = np.zeros((DFT_R // FNET_RB, 2, DFT_P, FNET_RB, DFT_P, FNET_RB), np.float32)
    for rl in range(FNET_RB):
        kron[:, :, :, rl, :, rl] = tab1[:, rl]
    tab1 = kron.reshape(DFT_R // FNET_RB, 2 * DFT_P * FNET_RB, DFT_P * FNET_RB)
    th2 = _dft_angles(DFT_R)
    c2, s2 = np.cos(th2), np.sin(th2)
    tab2 = np.block([[c2, s2], [-s2, c2]])
    return jnp.asarray(tab1, BF16), jnp.asarray(tab2, BF16)


def _chan_dft_tables(group_dim):
    th = _dft_angles(group_dim)
    return jnp.asarray(np.cos(th), BF16), jnp.asarray(np.sin(th), BF16)


def _ctx_dft_table(n_ctx):
    th = _dft_angles(n_ctx)
    return jnp.asarray(np.concatenate([np.cos(th), -np.sin(th)], axis=0), BF16)


def _cols_store(scr, val):
    for c in range(scr.shape[0]):
        scr[c] = val[:, c * LANES:(c + 1) * LANES]


def _cols_load(scr):
    return jnp.concatenate([scr[c] for c in range(scr.shape[0])], axis=-1)


def _cols_store_rows(scr, sel, val):
    for c in range(scr.shape[0]):
        scr[c, sel, :] = val[:, c * LANES:(c + 1) * LANES]


def _cols_load_rows(scr, sel):
    return jnp.concatenate([scr[c, sel, :] for c in range(scr.shape[0])], axis=-1)


def _fnet_stage1_kernel(x_ref, g_ref, sh_ref, sc_ref, tab_ref, zr_ref, zi_ref):
    d = g_ref.shape[-1]
    rows = DFT_P * FNET_RB
    h = _norm_mod(x_ref[...].reshape(rows, d), g_ref[...], sh_ref[...], sc_ref[...]).astype(BF16)
    z = jnp.dot(tab_ref[...], h, preferred_element_type=F32)
    zr_ref[...] = z[:rows].astype(BF16).reshape(DFT_P, FNET_RB, d)
    zi_ref[...] = z[rows:].astype(BF16).reshape(DFT_P, FNET_RB, d)


def _mix_tail(pr, pi, cc_ref, sc_ref, wout_ref, scale):
    gd = cc_ref.shape[0]
    ys = []
    for g in range(pr.shape[1] // gd):
        ys.append(jnp.dot(pr[:, g * gd:(g + 1) * gd], cc_ref[...], preferred_element_type=F32)
                  + jnp.dot(pi[:, g * gd:(g + 1) * gd], sc_ref[...], preferred_element_type=F32))
    mixed = (jnp.concatenate(ys, axis=-1) * scale).astype(BF16)
    return jnp.dot(mixed, wout_ref[...], preferred_element_type=F32)


def _fnet_stage2_kernel(zr_ref, zi_ref, tab2_ref, cc_ref, sc_ref, wout_ref, x_ref, gt_ref, o_ref,
                        p_scr, x_scr, o_scr, *, scale):
    d = gt_ref.shape[-1]
    rows = DFT_R * FNET_KB
    for j in range(FNET_KB):
        z = jnp.concatenate([zr_ref[j * DFT_R:(j + 1) * DFT_R, :], zi_ref[j * DFT_R:(j + 1) * DFT_R, :]], axis=0)
        p = jnp.dot(tab2_ref[...], z, preferred_element_type=F32)
        p_scr[j * DFT_R:(j + 1) * DFT_R, :d] = p[:DFT_R].astype(BF16)
        p_scr[j * DFT_R:(j + 1) * DFT_R, d:] = p[DFT_R:].astype(BF16)
    y = _mix_tail(p_scr[:, :d], p_scr[:, d:], cc_ref, sc_ref, wout_ref, scale)
    _cols_store(x_scr, x_ref[...].reshape(rows, d))
    for j in range(FNET_KB):
        sel = pl.ds(j, DFT_R, stride=FNET_KB)
        _cols_store_rows(o_scr, sel, _cols_load_rows(x_scr, sel) + gt_ref[...] * y[j * DFT_R:(j + 1) * DFT_R, :])
    o_ref[...] = _cols_load(o_scr).reshape(DFT_R, FNET_KB, d)


def _fnet_lat(x, mods, g_all, w_out, layer, n_batch):
    t, d = x.shape
    s = t // n_batch
    assert s == DFT_P * DFT_R
    tab1, tab2 = _seq_dft_tables(s)
    ccos, csin = _chan_dft_tables(d // FNET_GROUPS)
    mi = lambda j: (lambda b, *_: (layer * MOD_ROWS + b) * N_MOD + j)
    xv = x.reshape(n_batch, DFT_P, DFT_R, d)
    blk1 = (None, DFT_P, FNET_RB, d)
    rows1 = DFT_P * FNET_RB
    mi1 = lambda j: (lambda r, b: (layer * MOD_ROWS + b) * N_MOD + j)
    zr, zi = pl.pallas_call(
        _fnet_stage1_kernel,
        grid=(DFT_R // FNET_RB, n_batch),
        in_specs=[pl.BlockSpec(blk1, lambda r, b: (b, 0, r, 0)),
                  pl.BlockSpec((None, 1, d), lambda r, b: (layer, 0, 0)),
                  _mod_spec(d, mi1(0)), _mod_spec(d, mi1(1)),
                  pl.BlockSpec((None, 2 * rows1, rows1), lambda r, b: (r, 0, 0))],
        out_specs=[pl.BlockSpec(blk1, lambda r, b: (b, 0, r, 0))] * 2,
        out_shape=[jax.ShapeDtypeStruct((n_batch, DFT_P, DFT_R, d), BF16)] * 2,
        compiler_params=_cparams(("parallel", "parallel")),
        name="fnet_stage1",
    )(xv, g_all, mods, mods, tab1)
    zr = zr.reshape(n_batch, DFT_P * DFT_R, d)
    zi = zi.reshape(n_batch, DFT_P * DFT_R, d)
    xo = x.reshape(n_batch, DFT_R, DFT_P, d)
    blk2 = (None, DFT_R, FNET_KB, d)
    rows = FNET_KB * DFT_R
    scale = float(1.0 / np.sqrt(float(s) * (d // FNET_GROUPS)))
    out = pl.pallas_call(
        functools.partial(_fnet_stage2_kernel, scale=scale),
        grid=(n_batch, DFT_P // FNET_KB),
        in_specs=[pl.BlockSpec((None, rows, d), lambda b, k: (b, k, 0)),
                  pl.BlockSpec((None, rows, d), lambda b, k: (b, k, 0)),
                  pl.BlockSpec((2 * DFT_R, 2 * DFT_R), lambda b, k: (0, 0)),
                  pl.BlockSpec(ccos.shape, lambda b, k: (0, 0)),
                  pl.BlockSpec(csin.shape, lambda b, k: (0, 0)),
                  pl.BlockSpec((None, d, d), lambda b, k: (layer // 2, 0, 0)),
                  pl.BlockSpec(blk2, lambda b, k: (b, 0, k, 0)),
                  _mod_spec(d, mi(2))],
        out_specs=pl.BlockSpec(blk2, lambda b, k: (b, 0, k, 0)),
        out_shape=jax.ShapeDtypeStruct((n_batch, DFT_R, DFT_P, d), F32),
        scratch_shapes=[pltpu.VMEM((rows, 2 * d), BF16)] + [pltpu.VMEM((d // LANES, rows, LANES), F32)] * 2,
        compiler_params=_cparams(("parallel", "parallel")),
        name="fnet_stage2",
    )(zr, zi, tab2, ccos, csin, w_out, xo, mods)
    return out.reshape(t, d)


def _fnet_ctx_kernel(x_ref, g_ref, sh_ref, sc_ref, gt_ref, tab_ref, cc_ref, sc2_ref, wout_ref, o_ref, *, scale):
    n = x_ref.shape[0]
    h = _norm_mod(x_ref[...], g_ref[...], sh_ref[...], sc_ref[...]).astype(BF16)
    p = jnp.dot(tab_ref[...], h, preferred_element_type=F32)
    y = _mix_tail(p[:n].astype(BF16), p[n:].astype(BF16), cc_ref, sc2_ref, wout_ref, scale)
    o_ref[...] = x_ref[...] + gt_ref[...] * y


def _fnet_ctx(x, mods, g_all, w_out, layer, n_batch):
    t, d = x.shape
    n = t // n_batch
    gd = d // FNET_GROUPS
    ccos, csin = _chan_dft_tables(gd)
    tab = _ctx_dft_table(n)
    mi = lambda j: (lambda b: (layer * MOD_ROWS + CTX_ROW) * N_MOD + j)
    scale = float(1.0 / np.sqrt(float(n) * gd))
    return pl.pallas_call(
        functools.partial(_fnet_ctx_kernel, scale=scale),
        grid=(n_batch,),
        in_specs=[pl.BlockSpec((n, d), lambda b: (b, 0)),
                  pl.BlockSpec((None, 1, d), lambda b: (layer, 0, 0)),
                  _mod_spec(d, mi(0)), _mod_spec(d, mi(1)), _mod_spec(d, mi(2)),
                  pl.BlockSpec(tab.shape, lambda b: (0, 0)),
                  pl.BlockSpec(ccos.shape, lambda b: (0, 0)),
                  pl.BlockSpec(csin.shape, lambda b: (0, 0)),
                  pl.BlockSpec((None, d, d), lambda b: (layer // 2, 0, 0))],
        out_specs=pl.BlockSpec((n, d), lambda b: (b, 0)),
        out_shape=jax.ShapeDtypeStruct((t, d), F32),
        compiler_params=_cparams(("parallel",)),
        name="fnet_ctx",
    )(x, g_all, mods, mods, mods, tab, ccos, csin, w_out)


def _rope_tables(n_seq):
    rows = n_seq // GRID_W
    row = jnp.repeat(jnp.arange(rows, dtype=F32), GRID_W)
    col = jnp.tile(jnp.arange(GRID_W, dtype=F32), rows)
    inv_freq = ROPE_THETA ** (-jnp.arange(ROT_FREQS, dtype=F32) / ROT_FREQS)
    ang = jnp.stack([row[:, None] * inv_freq, col[:, None] * inv_freq], axis=1)
    cos, sin = jnp.cos(ang), jnp.sin(ang)
    zero = jnp.zeros_like(sin)
    cos_h = jnp.stack([cos, cos], axis=2).reshape(n_seq, HEAD_DIM)
    sin_lo = jnp.stack([-sin, zero], axis=2).reshape(n_seq, HEAD_DIM)
    sin_hi = jnp.stack([zero, sin], axis=2).reshape(n_seq, HEAD_DIM)
    rep = LANES // HEAD_DIM
    return jnp.tile(cos_h, (1, rep)), jnp.tile(sin_lo, (1, rep)), jnp.tile(sin_hi, (1, rep))


def _qkv_kernel(x_ref, g_ref, sh_ref, sc_ref, w_ref, *rest, rope, q_dim, kv_dim, q_scale):
    if rope:
        cos_ref, slo_ref, shi_ref, q_ref, k_ref, kt_ref, v_ref, vt_ref = rest
    else:
        q_ref, k_ref, kt_ref, v_ref, vt_ref = rest
    h = _norm_mod(x_ref[...], g_ref[...], sh_ref[...], sc_ref[...]).astype(BF16)
    qkv = jnp.dot(h, w_ref[...], preferred_element_type=F32)

    def rot(xs):
        if not rope:
            return xs
        return (xs * cos_ref[...] + pltpu.roll(xs, LANES - ROT_FREQS, axis=1) * slo_ref[...]
                + pltpu.roll(xs, ROT_FREQS, axis=1) * shi_ref[...])

    for j in range(q_dim // LANES):
        qj = rot(qkv[:, j * LANES:(j + 1) * LANES]) * q_scale
        if rope:
            q_ref[j * LANES:(j + 1) * LANES, :] = qj.T.astype(BF16)
        else:
            q_ref[:, j * LANES:(j + 1) * LANES] = qj.astype(BF16)
    k = jnp.concatenate([rot(qkv[:, q_dim + j * LANES:q_dim + (j + 1) * LANES]) for j in range(kv_dim // LANES)],
                        axis=-1)
    v = qkv[:, q_dim + kv_dim:]
    k_ref[...] = k.astype(BF16)
    kt_ref[...] = k.T.astype(BF16)
    v_ref[...] = v.astype(BF16)
    vt_ref[...] = v.T.astype(BF16)


def _qkv(x, mods, g_all, w_qkv, layer, n_batch, ctx, rope_tabs=None, tm=QKV_TM):
    t, d = x.shape
    n = t // n_batch
    tm = min(tm, n)
    tpb = n // tm
    kv_dim = N_KV_HEADS * HEAD_DIM
    q_dim = w_qkv.shape[2] - 2 * kv_dim
    mi = lambda j: _mod_index(layer, j, tpb, ctx)
    in_specs = [pl.BlockSpec((tm, d), lambda i: (i, 0)),
                pl.BlockSpec((None, 1, d), lambda i: (layer, 0, 0)),
                _mod_spec(d, mi(0)), _mod_spec(d, mi(1)),
                pl.BlockSpec((None,) + w_qkv.shape[1:], lambda i: (layer // 2, 0, 0))]
    args = [x, g_all, mods, mods, w_qkv]
    rope = rope_tabs is not None
    if rope:
        in_specs += [pl.BlockSpec((tm, LANES), lambda i: (i % tpb, 0))] * 3
        args += list(rope_tabs)
    return pl.pallas_call(
        functools.partial(_qkv_kernel, rope=rope, q_dim=q_dim, kv_dim=kv_dim, q_scale=HEAD_DIM ** -0.5),
        grid=(t // tm,),
        in_specs=in_specs,
        out_specs=[(pl.BlockSpec((None, q_dim, tm), lambda i: (i // tpb, 0, i % tpb)) if rope
                    else pl.BlockSpec((tm, q_dim), lambda i: (i, 0))),
                   pl.BlockSpec((tm, kv_dim), lambda i: (i, 0)),
                   pl.BlockSpec((None, kv_dim, tm), lambda i: (i // tpb, 0, i % tpb)),
                   pl.BlockSpec((tm, kv_dim), lambda i: (i, 0)),
                   pl.BlockSpec((None, kv_dim, tm), lambda i: (i // tpb, 0, i % tpb))],
        out_shape=[jax.ShapeDtypeStruct((n_batch, q_dim, n) if rope else (t, q_dim), BF16),
                   jax.ShapeDtypeStruct((t, kv_dim), BF16),
                   jax.ShapeDtypeStruct((n_batch, kv_dim, n), BF16),
                   jax.ShapeDtypeStruct((t, kv_dim), BF16),
                   jax.ShapeDtypeStruct((n_batch, kv_dim, n), BF16)],
        compiler_params=_cparams(("parallel",)),
        name="qkv_ctx" if ctx else "qkv_lat",
    )(*args)


ATTN_RB = 64


def _heads_attend(q_ref, kt, v, bias, sink_ref, o_ref):
    nq = q_ref.shape[0]
    n_heads = q_ref.shape[1] // HEAD_DIM
    group = n_heads // N_KV_HEADS
    outs = [None] * n_heads
    for g in range(N_KV_HEADS):
        heads = range(g * group, (g + 1) * group)
        qg = jnp.concatenate([q_ref[:, hd * HEAD_DIM:(hd + 1) * HEAD_DIM] for hd in heads], axis=0)
        s = jnp.dot(qg, kt[g * HEAD_DIM:(g + 1) * HEAD_DIM, :], preferred_element_type=F32)
        p_rows, den_rows = [], []
        for r0 in range(0, s.shape[0], ATTN_RB):
            rows = slice(r0, r0 + ATTN_RB)
            parts = [s[rows, k * LANES:(k + 1) * LANES] for k in range(s.shape[1] // LANES)]
            if bias is not None:
                parts[0] = parts[0] + bias[0][rows]
                parts[2] = parts[2] + bias[1][rows]
            sink = sink_ref[heads[r0 // nq]]
            m = jnp.maximum(jnp.max(functools.reduce(jnp.maximum, parts), axis=-1, keepdims=True), sink)
            ps = [jnp.exp(part - m) for part in parts]
            den_rows.append(jnp.sum(functools.reduce(jnp.add, ps), axis=-1, keepdims=True) + jnp.exp(sink - m))
            p_rows.append(jnp.concatenate(ps, axis=-1).astype(BF16))
        pv = jnp.dot(jnp.concatenate(p_rows, axis=0), v, preferred_element_type=F32)
        on = pv[:, g * HEAD_DIM:(g + 1) * HEAD_DIM] / jnp.concatenate(den_rows, axis=0)
        for k, hd in enumerate(heads):
            outs[hd] = on[k * nq:(k + 1) * nq, :]
    o_ref[...] = jnp.concatenate(outs, axis=-1).astype(BF16)


def _attn_lat_kernel(sink_ref, qt_ref, kp_ref, kc_ref, kn_ref, vtp_ref, vtc_ref, vtn_ref, kx_ref, vtx_ref,
                     blo_ref, bhi_ref, ot_ref, k_scr, vt_scr):
    w = kc_ref.shape[0]
    n_ctx = kx_ref.shape[0]
    nq = qt_ref.shape[1]
    n_heads = qt_ref.shape[0] // HEAD_DIM
    group = n_heads // N_KV_HEADS
    for c, (kr, vr) in enumerate(((kp_ref, vtp_ref), (kc_ref, vtc_ref), (kn_ref, vtn_ref))):
        k_scr[c * w:(c + 1) * w, :] = kr[...]
        vt_scr[:, c * w:(c + 1) * w] = vr[...]
    k_scr[3 * w:3 * w + n_ctx, :] = kx_ref[...]
    vt_scr[:, 3 * w:3 * w + n_ctx] = vtx_ref[...]
    n_chunks = k_scr.shape[0] // w
    for g in range(N_KV_HEADS):
        heads = range(g * group, (g + 1) * group)
        qg = jnp.concatenate([qt_ref[hd * HEAD_DIM:(hd + 1) * HEAD_DIM, :] for hd in heads], axis=1)
        st = jnp.dot(k_scr[:, g * HEAD_DIM:(g + 1) * HEAD_DIM], qg, preferred_element_type=F32)
        p_cols, den_cols = [], []
        for k, hd in enumerate(heads):
            cols = slice(k * nq, (k + 1) * nq)
            parts = [st[c * w:(c + 1) * w, cols] for c in range(n_chunks)]
            parts[0] = parts[0] + blo_ref[...]
            parts[2] = parts[2] + bhi_ref[...]
            sink = sink_ref[hd]
            m = jnp.maximum(jnp.max(functools.reduce(jnp.maximum, parts), axis=0, keepdims=True), sink)
            ps = [jnp.exp(part - m) for part in parts]
            den_cols.append(jnp.sum(functools.reduce(jnp.add, ps), axis=0, keepdims=True) + jnp.exp(sink - m))
            p_cols.append(jnp.concatenate(ps, axis=0).astype(BF16))
        ot = jnp.dot(vt_scr[g * HEAD_DIM:(g + 1) * HEAD_DIM, :], jnp.concatenate(p_cols, axis=1),
                     preferred_element_type=F32)
        ot = ot / jnp.concatenate(den_cols, axis=1)
        for k, hd in enumerate(heads):
            ot_ref[hd * HEAD_DIM:(hd + 1) * HEAD_DIM, :] = ot[:, k * nq:(k + 1) * nq].astype(BF16)


def _attn_lat(qt, k, vt, kx, vtx, sinks, n_batch):
    _, qd, s = qt.shape
    n_ctx = kx.shape[0] // n_batch
    w = WINDOW
    nb = s // w
    kvd = k.shape[1]
    j_all = 3 * w + n_ctx
    ki = np.arange(w)[:, None]
    qi = np.arange(w)[None, :]
    lo = np.where(ki >= qi, 0.0, NEG).astype(np.float32)
    hi = np.where(ki <= qi, 0.0, NEG).astype(np.float32)
    off = np.full((w, w), NEG, np.float32)
    blo = jnp.asarray(np.stack([lo, off]))
    bhi = jnp.asarray(np.stack([hi, off]))
    prev = lambda b, i: (b * nb + jnp.maximum(i - 1, 0), 0)
    nxt = lambda b, i: (b * nb + jnp.minimum(i + 1, nb - 1), 0)
    tprev = lambda b, i: (b, 0, jnp.maximum(i - 1, 0))
    tnxt = lambda b, i: (b, 0, jnp.minimum(i + 1, nb - 1))
    return pl.pallas_call(
        _attn_lat_kernel,
        grid=(n_batch, nb),
        in_specs=[pl.BlockSpec(memory_space=pltpu.SMEM),
                  pl.BlockSpec((None, qd, w), lambda b, i: (b, 0, i)),
                  pl.BlockSpec((w, kvd), prev),
                  pl.BlockSpec((w, kvd), lambda b, i: (b * nb + i, 0)),
                  pl.BlockSpec((w, kvd), nxt),
                  pl.BlockSpec((None, kvd, w), tprev),
                  pl.BlockSpec((None, kvd, w), lambda b, i: (b, 0, i)),
                  pl.BlockSpec((None, kvd, w), tnxt),
                  pl.BlockSpec((n_ctx, kvd), lambda b, i: (b, 0)),
                  pl.BlockSpec((None, kvd, n_ctx), lambda b, i: (b, 0, 0)),
                  pl.BlockSpec((None, w, w), lambda b, i: (jnp.where(i == 0, 1, 0), 0, 0)),
                  pl.BlockSpec((None, w, w), lambda b, i: (jnp.where(i == nb - 1, 1, 0), 0, 0))],
        out_specs=pl.BlockSpec((None, qd, w), lambda b, i: (b, 0, i)),
        out_shape=jax.ShapeDtypeStruct((n_batch, qd, s), BF16),
        scratch_shapes=[pltpu.VMEM((j_all, kvd), BF16), pltpu.VMEM((kvd, j_all), BF16)],
        compiler_params=_cparams(("parallel", "parallel")),
        name="attn_lat",
    )(sinks, qt, k, k, k, vt, vt, vt, kx, vtx, blo, bhi)


def _attn_ctx_kernel(sink_ref, q_ref, kt_ref, v_ref, o_ref):
    _heads_attend(q_ref, kt_ref[...], v_ref[...], None, sink_ref, o_ref)


def _attn_ctx(q, kt, v, sinks, n_batch):
    t, qd = q.shape
    n = t // n_batch
    kvd = v.shape[1]
    return pl.pallas_call(
        _attn_ctx_kernel,
        grid=(n_batch,),
        in_specs=[pl.BlockSpec(memory_space=pltpu.SMEM),
                  pl.BlockSpec((n, qd), lambda b: (b, 0)),
                  pl.BlockSpec((None, kvd, n), lambda b: (b, 0, 0)),
                  pl.BlockSpec((n, kvd), lambda b: (b, 0))],
        out_specs=pl.BlockSpec((n, qd), lambda b: (b, 0)),
        out_shape=jax.ShapeDtypeStruct((t, qd), BF16),
        compiler_params=_cparams(("parallel",)),
        name="attn_ctx",
    )(sinks, q, kt, v)


ROUTE_I1, ROUTE_I2, ROUTE_R1, ROUTE_R2, ROUTE_G1, ROUTE_G2 = range(6)
ROUTE_ROWS = 8
TAB_START, TAB_COUNT, TAB_USED, TAB_SIZE = 0, N_EXPERTS, 2 * N_EXPERTS, 2 * N_EXPERTS + 8
ROW_UNROLL = 8
COMBINE_ROWS = 32


def _router_kernel(x_ref, g_ref, sh_ref, sc_ref, wr_ref, tri_ref, route_ref, idx_ref, cnt_ref, carry_scr):
    @pl.when(pl.program_id(0) == 0)
    def _():
        carry_scr[...] = jnp.zeros_like(carry_scr)

    h = _norm_mod(x_ref[...], g_ref[...], sh_ref[...], sc_ref[...])
    hi = h.astype(BF16)
    lo = (h - hi.astype(F32)).astype(BF16)
    a = jnp.dot(hi, wr_ref[...], preferred_element_type=F32)
    logits = (a[:, :LANES] + a[:, LANES:]) + jnp.dot(lo, wr_ref[:, :LANES], preferred_element_type=F32)
    lane = lax.broadcasted_iota(jnp.int32, logits.shape, 1)
    lane_f = lane.astype(F32)
    logits = jnp.where(lane < N_EXPERTS, logits, -jnp.inf)
    m1 = jnp.max(logits, axis=-1, keepdims=True)
    i1 = jnp.min(jnp.where(logits == m1, lane_f, float(LANES)), axis=-1, keepdims=True)
    oh1 = lane_f == i1
    rest = jnp.where(oh1, -jnp.inf, logits)
    m2 = jnp.max(rest, axis=-1, keepdims=True)
    i2 = jnp.min(jnp.where(rest == m2, lane_f, float(LANES)), axis=-1, keepdims=True)
    oh2 = lane_f == i2
    e2 = jnp.exp(m2 - m1)
    g1 = 1.0 / (1.0 + e2)
    g2 = e2 / (1.0 + e2)
    sel = jnp.where(oh1, 1.0, 0.0) + jnp.where(oh2, 1.0, 0.0)
    before = jnp.dot(tri_ref[...], sel.astype(BF16), preferred_element_type=F32) + carry_scr[0:1, :]
    r1 = jnp.sum(jnp.where(oh1, before, 0.0), axis=-1, keepdims=True)
    r2 = jnp.sum(jnp.where(oh2, before, 0.0), axis=-1, keepdims=True)
    total = carry_scr[0:1, :] + jnp.sum(sel, axis=0, keepdims=True)
    carry_scr[...] = jnp.broadcast_to(total, carry_scr.shape)
    cnt_ref[...] = jnp.broadcast_to(total, cnt_ref.shape)
    rec = jnp.zeros_like(logits)
    for ln, val in ((ROUTE_I1, i1), (ROUTE_I2, i2), (ROUTE_R1, r1), (ROUTE_R2, r2), (ROUTE_G1, g1), (ROUTE_G2, g2)):
        rec = jnp.where(lane == ln, val, rec)
    route_ref[...] = rec
    idx_ref[...] = rec.T[:ROUTE_ROWS, :].astype(jnp.int32)


def _router(x, mods, g_all, w_router, layer, tiles_per_batch, ctx, tm):
    t, d = x.shape
    mi = lambda j: _mod_index(layer, j, tiles_per_batch // tm if not ctx else 1, ctx)
    wr = jnp.zeros((d, LANES), F32).at[:, :N_EXPERTS].set(w_router)
    whi = wr.astype(BF16)
    wr = jnp.concatenate([whi, (wr - whi.astype(F32)).astype(BF16)], axis=1)
    tri = jnp.asarray(np.tril(np.ones((tm, tm), np.float32), -1), BF16)
    return pl.pallas_call(
        _router_kernel,
        grid=(t // tm,),
        in_specs=[pl.BlockSpec((tm, d), lambda i: (i, 0)),
                  pl.BlockSpec((None, 1, d), lambda i: (layer, 0, 0)),
                  _mod_spec(d, mi(3)), _mod_spec(d, mi(4)),
                  pl.BlockSpec((d, 2 * LANES), lambda i: (0, 0)),
                  pl.BlockSpec((tm, tm), lambda i: (0, 0))],
        out_specs=[pl.BlockSpec((tm, LANES), lambda i: (i, 0)),
                   pl.BlockSpec((None, ROUTE_ROWS, tm), lambda i: (i, 0, 0)),
                   pl.BlockSpec((8, LANES), lambda i: (0, 0))],
        out_shape=[jax.ShapeDtypeStruct((t, LANES), F32),
                   jax.ShapeDtypeStruct((t // tm, ROUTE_ROWS, tm), jnp.int32),
                   jax.ShapeDtypeStruct((8, LANES), F32)],
        scratch_shapes=[pltpu.VMEM((8, LANES), F32)],
        compiler_params=_cparams(("arbitrary",)),
        name="router",
    )(x, g_all, mods, mods, wr, tri)


SUB = 8


def _to_row_tiles(ref, val):
    n = val.shape[0]
    for k in range(SUB):
        ref[pl.ds(k, n, stride=SUB), :] = val[:, k * LANES:(k + 1) * LANES]


def _from_row_tiles(ref):
    n = ref.shape[0] // SUB
    return jnp.concatenate([ref[pl.ds(k, n, stride=SUB), :] for k in range(SUB)], axis=-1)


def _row_copy(src, dst, src_row, dst_row, sem):
    return pltpu.make_async_copy(src.at[pl.ds(pl.multiple_of(src_row * SUB, SUB), SUB)],
                                 dst.at[pl.ds(pl.multiple_of(dst_row * SUB, SUB), SUB)], sem)


def _for_rows(tm, body):
    def blk(i, _):
        base = pl.multiple_of(i * ROW_UNROLL, ROW_UNROLL)
        for j in range(ROW_UNROLL):
            body(base + j, j)
        return 0

    lax.fori_loop(0, tm // ROW_UNROLL, blk, 0)


ZERO_SEM = 4


def _dispatch_kernel(tab_ref, slot_ref, x_ref, g_ref, sh_ref, sc_ref, xs_ref, h_scr, zero_scr, sems, *, group):
    tm = x_ref.shape[0]
    n_groups = xs_ref.shape[0] // (group * SUB)
    step = pl.program_id(0)
    n_steps = pl.num_programs(0)
    cur = lax.rem(step, 2)

    def wait_buffer(b):
        for k in range(2):
            pltpu.make_async_copy(h_scr.at[b], xs_ref.at[pl.ds(0, tm * SUB)], sems.at[2 * b + k]).wait()

    @pl.when(step >= 2)
    def _():
        wait_buffer(cur)

    _to_row_tiles(h_scr.at[cur], _norm_mod(x_ref[...], g_ref[...], sh_ref[...], sc_ref[...]))

    def issue(r, j):
        for k in range(2):
            _row_copy(h_scr.at[cur], xs_ref, r, slot_ref[0, k, r], sems.at[2 * cur + k]).start(priority=j % 2)

    _for_rows(tm, issue)

    @pl.when(jnp.logical_and(step == n_steps - 1, step >= 1))
    def _():
        wait_buffer(1 - cur)

    @pl.when(step == n_steps - 1)
    def _():
        wait_buffer(cur)
        zero_scr[...] = jnp.zeros_like(zero_scr)
        for e in range(N_EXPERTS):
            n = tab_ref[TAB_COUNT + e]
            n_pad = lax.rem(group - lax.rem(n, group), group)
            first = tab_ref[TAB_START + e] + n

            def fill(k, _, first=first):
                _row_copy(zero_scr, xs_ref, 0, first + k, sems.at[ZERO_SEM]).start()
                return 0

            def fill_done(k, _):
                _row_copy(zero_scr, xs_ref, 0, 0, sems.at[ZERO_SEM]).wait()
                return 0

            lax.fori_loop(0, n_pad, fill, 0)
            lax.fori_loop(0, n_pad, fill_done, 0)

        def clear(j, _):
            row = pl.multiple_of(j * (group * SUB), group * SUB)
            cp = pltpu.make_async_copy(zero_scr, xs_ref.at[pl.ds(row, group * SUB)], sems.at[ZERO_SEM])
            cp.start()
            cp.wait()
            return 0

        lax.fori_loop(tab_ref[TAB_USED], n_groups, clear, 0)


def _dispatch(x, tab, slots, mods, g_all, layer, tiles_per_batch, ctx, n_groups, group, tm):
    t, d = x.shape
    mi = lambda j: _mod_index(layer, j, tiles_per_batch // tm if not ctx else 1, ctx)
    return pl.pallas_call(
        functools.partial(_dispatch_kernel, group=group),
        grid=(t // tm,),
        in_specs=[pl.BlockSpec(memory_space=pltpu.SMEM),
                  pl.BlockSpec((1, 2, tm), lambda i: (i, 0, 0), memory_space=pltpu.SMEM),
                  pl.BlockSpec((tm, d), lambda i: (i, 0)),
                  pl.BlockSpec((None, 1, d), lambda i: (layer, 0, 0)),
                  _mod_spec(d, mi(3)), _mod_spec(d, mi(4))],
        out_specs=pl.BlockSpec(memory_space=pl.ANY),
        out_shape=jax.ShapeDtypeStruct((n_groups * group * SUB, LANES), F32),
        scratch_shapes=[pltpu.VMEM((2, tm * SUB, LANES), F32), pltpu.VMEM((group * SUB, LANES), F32),
                        pltpu.SemaphoreType.DMA((ZERO_SEM + 1,))],
        compiler_params=_cparams(("arbitrary",)),
        name="moe_dispatch",
    )(tab, slots, x, g_all, mods, mods)


def _moe_ffn_kernel(be_ref, bv_ref, xs_ref, wg_ref, wu_ref, wd_ref, ys_ref, h_scr, acc_scr):
    del be_ref
    i = pl.program_id(0)
    f = pl.program_id(1)
    last = f == pl.num_programs(1) - 1
    valid = bv_ref[i] > 0

    @pl.when(jnp.logical_and(valid, f == 0))
    def _():
        h_scr[...] = _from_row_tiles(xs_ref).astype(BF16)
        acc_scr[...] = jnp.zeros_like(acc_scr)

    @pl.when(valid)
    def _():
        h = h_scr[...]
        a = jnp.dot(h, wg_ref[...], preferred_element_type=F32)
        u = jnp.dot(h, wu_ref[...], preferred_element_type=F32)
        t = (a * _sigmoid(a)) * u
        acc_scr[...] += jnp.dot(t.astype(BF16), wd_ref[...], preferred_element_type=F32)

    @pl.when(jnp.logical_and(valid, last))
    def _():
        _to_row_tiles(ys_ref, acc_scr[...])

    @pl.when(jnp.logical_and(jnp.logical_not(valid), last))
    def _():
        ys_ref[...] = jnp.zeros_like(ys_ref)


def _moe_ffn(xs, block_expert, block_valid, wg, wu, wd, layer, tm, tf=MOE_TF):
    d, ff = wg.shape[2:]
    assert d == SUB * LANES and xs.shape[1] == LANES
    cap = xs.shape[0] // SUB
    tf = min(tf, ff)
    nf = ff // tf
    sj = layer // 2
    fsel = lambda i, f, bv: jnp.where(bv[i] > 0, f, nf - 1)
    grid_spec = pltpu.PrefetchScalarGridSpec(
        num_scalar_prefetch=2,
        grid=(cap // tm, nf),
        in_specs=[pl.BlockSpec((tm * SUB, LANES), lambda i, f, be, bv: (i, 0)),
                  pl.BlockSpec((None, None, d, tf), lambda i, f, be, bv: (sj, be[i], 0, fsel(i, f, bv))),
                  pl.BlockSpec((None, None, d, tf), lambda i, f, be, bv: (sj, be[i], 0, fsel(i, f, bv))),
                  pl.BlockSpec((None, None, tf, d), lambda i, f, be, bv: (sj, be[i], fsel(i, f, bv), 0))],
        out_specs=pl.BlockSpec((tm * SUB, LANES), lambda i, f, be, bv: (i, 0)),
        scratch_shapes=[pltpu.VMEM((tm, d), BF16), pltpu.VMEM((tm, d), F32)],
    )
    return pl.pallas_call(
        _moe_ffn_kernel,
        grid_spec=grid_spec,
        out_shape=jax.ShapeDtypeStruct(xs.shape, F32),
        compiler_params=_cparams(("parallel", "arbitrary")),
        name="moe_ffn",
    )(block_expert, block_valid, xs, wg, wu, wd)


def _combine_kernel(slot_ref, next_slot_ref, route_ref, x_ref, gt_ref, fg_ref, ys_ref, o_ref, buf, sems, *, final_norm):
    tm = x_ref.shape[0]
    step = pl.program_id(0)
    cur = lax.rem(step, 2)

    def issue(idx_ref, b, r, j):
        for k in range(2):
            _row_copy(ys_ref, buf.at[b, k], idx_ref[0, k, r], r, sems.at[2 * b + k]).start(priority=j % 2)

    def rows_of(tiles, base):
        return jnp.concatenate([tiles[pl.ds(base * SUB + k, COMBINE_ROWS, stride=SUB), :] for k in range(SUB)],
                               axis=-1)

    def finish(base):
        rows = pl.ds(base, COMBINE_ROWS)
        rec = route_ref[rows, :]
        y = (rows_of(buf.at[cur, 0], base) * rec[:, ROUTE_G1:ROUTE_G1 + 1]
             + rows_of(buf.at[cur, 1], base) * rec[:, ROUTE_G2:ROUTE_G2 + 1])
        out = x_ref[rows, :] + gt_ref[...] * y
        if final_norm:
            out = out * lax.rsqrt(jnp.mean(out * out, axis=-1, keepdims=True) + EPS) * fg_ref[...]
        o_ref[rows, :] = out

    @pl.when(step == 0)
    def _():
        _for_rows(tm, lambda r, j: issue(slot_ref, cur, r, j))

    for k in range(2):
        pltpu.make_async_copy(ys_ref.at[pl.ds(0, tm * SUB)], buf.at[cur, k], sems.at[2 * cur + k]).wait()

    def fused(i, _):
        base = pl.multiple_of(i * COMBINE_ROWS, COMBINE_ROWS)
        for j in range(COMBINE_ROWS):
            issue(next_slot_ref, 1 - cur, base + j, j)
        finish(base)
        return 0

    def plain(i, _):
        finish(pl.multiple_of(i * COMBINE_ROWS, COMBINE_ROWS))
        return 0

    has_next = step + 1 < pl.num_programs(0)

    @pl.when(has_next)
    def _():
        lax.fori_loop(0, tm // COMBINE_ROWS, fused, 0)

    @pl.when(jnp.logical_not(has_next))
    def _():
        lax.fori_loop(0, tm // COMBINE_ROWS, plain, 0)


def _combine(ys, slots, route, x, mods, final_g, layer, tiles_per_batch, ctx, final_norm, tm):
    t, d = x.shape
    mi = _mod_index(layer, 5, tiles_per_batch // tm if not ctx else 1, ctx)
    n_tiles = t // tm
    return pl.pallas_call(
        functools.partial(_combine_kernel, final_norm=final_norm),
        grid=(n_tiles,),
        in_specs=[pl.BlockSpec((1, 2, tm), lambda i: (i, 0, 0), memory_space=pltpu.SMEM),
                  pl.BlockSpec((1, 2, tm), lambda i: (jnp.minimum(i + 1, n_tiles - 1), 0, 0), memory_space=pltpu.SMEM),
                  pl.BlockSpec((tm, LANES), lambda i: (i, 0)),
                  pl.BlockSpec((tm, d), lambda i: (i, 0)),
                  _mod_spec(d, mi),
                  pl.BlockSpec((1, d), lambda i: (0, 0)),
                  pl.BlockSpec(memory_space=pl.ANY)],
        out_specs=pl.BlockSpec((tm, d), lambda i: (i, 0)),
        out_shape=jax.ShapeDtypeStruct((t, d), F32),
        scratch_shapes=[pltpu.VMEM((2, 2, tm * SUB, LANES), F32), pltpu.SemaphoreType.DMA((4,))],
        compiler_params=_cparams(("arbitrary",)),
        name="moe_combine",
    )(slots, slots, route, x, mods, final_g.reshape(1, d), ys)


def _moe(x, mods, g_all, w_router, wg, wu, wd, final_g, layer, tiles_per_batch, ctx, final_norm):
    t, d = x.shape
    group = MOE_TM if 2 * t >= 4 * N_EXPERTS * MOE_TM else MOE_TM // 2
    row_tm = min(ROW_TM, t)
    route, idx, counts = _router(x, mods, g_all, w_router, layer, tiles_per_batch, ctx, row_tm)
    counts = counts[0, :N_EXPERTS].astype(jnp.int32)
    groups = (counts + group - 1) // group
    ends = jnp.cumsum(groups)
    starts = (ends - groups) * group
    n_groups = (2 * t + group - 1) // group + N_EXPERTS
    tab = jnp.zeros((TAB_SIZE,), jnp.int32)
    tab = tab.at[TAB_START:TAB_START + N_EXPERTS].set(starts).at[TAB_COUNT:TAB_COUNT + N_EXPERTS].set(counts)
    tab = tab.at[TAB_USED].set(ends[-1])
    gi = jnp.arange(n_groups, dtype=jnp.int32)
    block_expert = jnp.minimum(jnp.sum(gi[:, None] >= ends[None, :], axis=1), N_EXPERTS - 1).astype(jnp.int32)
    block_valid = (gi < ends[-1]).astype(jnp.int32)
    def slot_rows(e, r):
        base = functools.reduce(jnp.add, [jnp.where(e == k, starts[k], 0) for k in range(N_EXPERTS)])
        return base + r
    slots = jnp.stack([slot_rows(idx[:, ROUTE_I1, :], idx[:, ROUTE_R1, :]),
                       slot_rows(idx[:, ROUTE_I2, :], idx[:, ROUTE_R2, :])], axis=1)
    xs = _dispatch(x, tab, slots, mods, g_all, layer, tiles_per_batch, ctx, n_groups, group, row_tm)
    ys = _moe_ffn(xs, block_expert, block_valid, wg, wu, wd, layer, group)
    return _combine(ys, slots, route, x, mods, final_g, layer, tiles_per_batch, ctx, final_norm, row_tm)


def kernel(x, c, ctx, c_ctx, ada_w, ada_b, norm_mix_g, norm_ffn_g, fnet_w_out, attn_w_qkv, attn_w_o, attn_sinks,
           ffn_w_gate, ffn_w_up, ffn_w_down, moe_w_router, moe_w_gate, moe_w_up, moe_w_down, final_norm_g):
    n_batch, s, d = x.shape
    n_ctx = ctx.shape[1]
    depth = ada_w.shape[0]
    mods = _ada(c, c_ctx, ada_w, ada_b)
    g_mix = norm_mix_g.reshape(depth, 1, d)
    g_ffn = norm_ffn_g.reshape(depth, 1, d)
    rope_tabs = _rope_tables(s)
    x_lat = x.reshape(n_batch * s, d)
    x_ctx = ctx.reshape(n_batch * n_ctx, d)
    bf = lambda a: a.astype(BF16)
    w_out, w_qkv, w_o = bf(fnet_w_out), bf(attn_w_qkv), bf(attn_w_o)
    wg, wu, wd = bf(ffn_w_gate), bf(ffn_w_up), bf(ffn_w_down)
    eg, eu, ed = bf(moe_w_gate), bf(moe_w_up), bf(moe_w_down)
    for i in range(depth):
        j = i // 2
        last = i == depth - 1
        if i % 2 == 0:
            x_lat = _fnet_lat(x_lat, mods, g_mix, w_out, i, n_batch)
            if not last:
                x_ctx = _fnet_ctx(x_ctx, mods, g_mix, w_out, i, n_batch)
            x_lat = _ffn_dense(x_lat, mods, g_ffn, wg, wu, wd, i, s, False)
            if not last:
                x_ctx = _ffn_dense(x_ctx, mods, g_ffn, wg, wu, wd, i, n_ctx, True)
        else:
            qt, k, _, _, vt = _qkv(x_lat, mods, g_mix, w_qkv, i, n_batch, False, rope_tabs)
            qx, kx, ktx, vx, vtx = _qkv(x_ctx, mods, g_mix, w_qkv, i, n_batch, True)
            ot = _attn_lat(qt, k, vt, kx, vtx, attn_sinks[j], n_batch)
            x_lat = _proj_res(ot, w_o, x_lat, mods, i, 2, s, False)
            if not last:
                ox = _attn_ctx(qx, ktx, vx, attn_sinks[j], n_batch)
                x_ctx = _proj_res(ox, w_o, x_ctx, mods, i, 2, n_ctx, True)
            x_lat = _moe(x_lat, mods, g_ffn, moe_w_router[j], eg, eu, ed, final_norm_g, i, s, False, last)
            if not last:
                x_ctx = _moe(x_ctx, mods, g_ffn, moe_w_router[j], eg, eu, ed, final_norm_g, i, n_ctx, True, False)
    return x_lat.reshape(n_batch, s, d)
```

```python
import functools

import numpy as np
import jax
import jax.numpy as jnp
from jax import lax
from jax.experimental import pallas as pl
from jax.experimental.pallas import tpu as pltpu

F32 = jnp.float32
BF16 = jnp.bfloat16

GRID_W = 64
N_MOD = 6
EPS = 1e-6
FNET_GROUPS = 4
HEAD_DIM = 64
N_KV_HEADS = 4
WINDOW = 128
ROPE_THETA = 10000.0
ROT_FREQS = HEAD_DIM // 4
N_EXPERTS = 8
MOD_ROWS = 8
CTX_ROW = MOD_ROWS - 1
LANES = 128
NEG = -1e30
VMEM_LIMIT = 56 * 1024 * 1024

FFN_TM = 512
FFN_TF = 1792
MOE_TM = 512
MOE_TF = 1792
ROW_TM = 512
PROJ_TM = 1024
QKV_TM = 1024
DFT_P = 64
DFT_R = 128
FNET_RB = 16
FNET_KB = 8


def _cparams(sem):
    return pltpu.CompilerParams(dimension_semantics=sem, vmem_limit_bytes=VMEM_LIMIT)


def _sigmoid(a):
    return 1.0 / (1.0 + jnp.exp(-a))


def _norm_mod(x, g, shift, scale):
    xn = x * lax.rsqrt(jnp.mean(x * x, axis=-1, keepdims=True) + EPS)
    return (xn * g) * (1.0 + scale) + shift


def _mod_spec(d, idx_fn):
    return pl.BlockSpec((None, 1, d), lambda *ids: (idx_fn(*ids), 0, 0))


def _mod_index(layer, j, tiles_per_batch, ctx):
    def fn(i, *_):
        b = CTX_ROW if ctx else i // tiles_per_batch
        return (layer * MOD_ROWS + b) * N_MOD + j
    return fn


def _ada_kernel(c_ref, w_ref, b_ref, o_ref):
    cc = c_ref[...]
    s = cc * _sigmoid(cc)
    w = w_ref[...]
    whi = w.astype(BF16)
    wlo = (w - whi.astype(F32)).astype(BF16)
    hi = s.astype(BF16)
    lo = (s - hi.astype(F32)).astype(BF16)
    n = s.shape[0]
    a = jnp.dot(jnp.concatenate([hi, lo], axis=0), whi, preferred_element_type=F32)
    o_ref[...] = (a[:n] + a[n:]) + jnp.dot(hi, wlo, preferred_element_type=F32) + b_ref[...]


def _ada(c, c_ctx, ada_w, ada_b):
    depth, d, _ = ada_w.shape
    nb = c.shape[0]
    assert nb < MOD_ROWS
    cc = jnp.concatenate([c, jnp.zeros((CTX_ROW - nb, d), F32), c_ctx[None]], axis=0)
    out = pl.pallas_call(
        _ada_kernel,
        grid=(depth, N_MOD),
        in_specs=[pl.BlockSpec((MOD_ROWS, d), lambda l, j: (0, 0)),
                  pl.BlockSpec((None, d, d), lambda l, j: (l, 0, j)),
                  pl.BlockSpec((None, 1, d), lambda l, j: (l * N_MOD + j, 0, 0))],
        out_specs=pl.BlockSpec((None, MOD_ROWS, d), lambda l, j: (l, 0, j)),
        out_shape=jax.ShapeDtypeStruct((depth, MOD_ROWS, N_MOD * d), F32),
        compiler_params=_cparams(("parallel", "parallel")),
        name="ada",
    )(cc, ada_w, ada_b.reshape(depth * N_MOD, 1, d))
    return out.reshape(depth * MOD_ROWS * N_MOD, 1, d)


def _ffn_kernel(x_ref, g_ref, sh_ref, sc_ref, gt_ref, wg_ref, wu_ref, wd_ref, o_ref, h_scr, acc_scr):
    f = pl.program_id(1)

    @pl.when(f == 0)
    def _():
        h_scr[...] = _norm_mod(x_ref[...], g_ref[...], sh_ref[...], sc_ref[...]).astype(BF16)
        acc_scr[...] = jnp.zeros_like(acc_scr)

    h = h_scr[...]
    a = jnp.dot(h, wg_ref[...], preferred_element_type=F32)
    u = jnp.dot(h, wu_ref[...], preferred_element_type=F32)
    t = (a * _sigmoid(a)) * u
    y = jnp.dot(t.astype(BF16), wd_ref[...], preferred_element_type=F32)
    last = f == pl.num_programs(1) - 1

    @pl.when(jnp.logical_not(last))
    def _():
        acc_scr[...] += y

    @pl.when(last)
    def _():
        o_ref[...] = x_ref[...] + gt_ref[...] * (acc_scr[...] + y)


def _ffn_dense(x, mods, g_all, wg, wu, wd, layer, tiles_per_batch, ctx, tm=FFN_TM, tf=FFN_TF):
    t, d = x.shape
    ff = wg.shape[2]
    tm = min(tm, t)
    tf = min(tf, ff)
    sj = layer // 2
    mi = functools.partial(_mod_index, layer, tiles_per_batch=tiles_per_batch // tm if not ctx else 1, ctx=ctx)
    return pl.pallas_call(
        _ffn_kernel,
        grid=(t // tm, ff // tf),
        in_specs=[pl.BlockSpec((tm, d), lambda i, f: (i, 0)),
                  pl.BlockSpec((None, 1, d), lambda i, f: (layer, 0, 0)),
                  _mod_spec(d, mi(3)), _mod_spec(d, mi(4)), _mod_spec(d, mi(5)),
                  pl.BlockSpec((None, d, tf), lambda i, f: (sj, 0, f)),
                  pl.BlockSpec((None, d, tf), lambda i, f: (sj, 0, f)),
                  pl.BlockSpec((None, tf, d), lambda i, f: (sj, f, 0))],
        out_specs=pl.BlockSpec((tm, d), lambda i, f: (i, 0)),
        out_shape=jax.ShapeDtypeStruct((t, d), F32),
        scratch_shapes=[pltpu.VMEM((tm, d), BF16), pltpu.VMEM((tm, d), F32)],
        compiler_params=_cparams(("parallel", "arbitrary")),
        name="ffn_dense",
    )(x, g_all, mods, mods, mods, wg, wu, wd)


def _proj_res_kernel(a_ref, w_ref, x_ref, gt_ref, o_ref, *, transposed):
    dims = (((0,), (0,)), ((), ())) if transposed else (((1,), (0,)), ((), ()))
    y = lax.dot_general(a_ref[...], w_ref[...], dims, preferred_element_type=F32)
    o_ref[...] = x_ref[...] + gt_ref[...] * y


def _proj_res(a, w, x, mods, layer, j, tiles_per_batch, ctx, tm=PROJ_TM):
    t, d = x.shape
    transposed = a.ndim == 3
    k = a.shape[1]
    tm = min(tm, t)
    tpb = tiles_per_batch // tm
    mi = _mod_index(layer, j, tpb if not ctx else 1, ctx)
    a_spec = (pl.BlockSpec((None, k, tm), lambda i: (i // tpb, 0, i % tpb)) if transposed
              else pl.BlockSpec((tm, k), lambda i: (i, 0)))
    return pl.pallas_call(
        functools.partial(_proj_res_kernel, transposed=transposed),
        grid=(t // tm,),
        in_specs=[a_spec,
                  pl.BlockSpec((None, k, d), lambda i: (layer // 2, 0, 0)),
                  pl.BlockSpec((tm, d), lambda i: (i, 0)),
                  _mod_spec(d, mi)],
        out_specs=pl.BlockSpec((tm, d), lambda i: (i, 0)),
        out_shape=jax.ShapeDtypeStruct((t, d), F32),
        compiler_params=_cparams(("parallel",)),
        name="proj_res",
    )(a, w, x, mods)


def _dft_angles(n):
    a = np.arange(n)
    return 2.0 * np.pi * ((a[:, None] * a[None, :]) % n) / n


def _seq_dft_tables(n):
    kp = np.arange(DFT_P)[None, :, None]
    p = np.arange(DFT_P)[None, None, :]
    r = np.arange(DFT_R)[:, None, None]
    th = 2.0 * np.pi * ((kp * (DFT_R * p + r)) % n) / n
    tab1 = np.stack([np.cos(th), -np.sin(th)], axis=1)
    tab1 = tab1.reshape(DFT_R // FNET_RB, FNET_RB, 2, DFT_P, DFT_P)
    kron = np.zeros((DFT_R // FNET_RB, 2, DFT_P, FNET_RB, DFT_P, FNET_RB), np.float32)
    for rl in range(FNET_RB):
        kron[:, :, :, rl, :, rl] = tab1[:, rl]
    tab1 = kron.reshape(DFT_R // FNET_RB, 2 * DFT_P * FNET_RB, DFT_P * FNET_RB)
    th2 = _dft_angles(DFT_R)
    c2, s2 = np.cos(th2), np.sin(th2)
    tab2 = np.block([[c2, s2], [-s2, c2]])
    return jnp.asarray(tab1, BF16), jnp.asarray(tab2, BF16)


def _chan_dft_tables(group_dim):
    th = _dft_angles(group_dim)
    return jnp.asarray(np.cos(th), BF16), jnp.asarray(np.sin(th), BF16)


def _ctx_dft_table(n_ctx):
    th = _dft_angles(n_ctx)
    return jnp.asarray(np.concatenate([np.cos(th), -np.sin(th)], axis=0), BF16)


def _cols_store(scr, val):
    for c in range(scr.shape[0]):
        scr[c] = val[:, c * LANES:(c + 1) * LANES]


def _cols_load(scr):
    return jnp.concatenate([scr[c] for c in range(scr.shape[0])], axis=-1)


def _cols_store_rows(scr, sel, val):
    for c in range(scr.shape[0]):
        scr[c, sel, :] = val[:, c * LANES:(c + 1) * LANES]


def _cols_load_rows(scr, sel):
    return jnp.concatenate([scr[c, sel, :] for c in range(scr.shape[0])], axis=-1)


def _fnet_stage1_kernel(x_ref, g_ref, sh_ref, sc_ref, tab_ref, zr_ref, zi_ref):
    d = g_ref.shape[-1]
    rows = DFT_P * FNET_RB
    h = _norm_mod(x_ref[...].reshape(rows, d), g_ref[...], sh_ref[...], sc_ref[...]).astype(BF16)
    z = jnp.dot(tab_ref[...], h, preferred_element_type=F32)
    zr_ref[...] = z[:rows].astype(BF16).reshape(DFT_P, FNET_RB, d)
    zi_ref[...] = z[rows:].astype(BF16).reshape(DFT_P, FNET_RB, d)


def _mix_tail(pr, pi, cc_ref, sc_ref, wout_ref, scale):
    gd = cc_ref.shape[0]
    ys = []
    for g in range(pr.shape[1] // gd):
        ys.append(jnp.dot(pr[:, g * gd:(g + 1) * gd], cc_ref[...], preferred_element_type=F32)
                  + jnp.dot(pi[:, g * gd:(g + 1) * gd], sc_ref[...], preferred_element_type=F32))
    mixed = (jnp.concatenate(ys, axis=-1) * scale).astype(BF16)
    return jnp.dot(mixed, wout_ref[...], preferred_element_type=F32)


def _fnet_stage2_kernel(zr_ref, zi_ref, tab2_ref, cc_ref, sc_ref, wout_ref, x_ref, gt_ref, o_ref,
                        p_scr, x_scr, o_scr, *, scale):
    d = gt_ref.shape[-1]
    rows = DFT_R * FNET_KB
    for j in range(FNET_KB):
        z = jnp.concatenate([zr_ref[j * DFT_R:(j + 1) * DFT_R, :], zi_ref[j * DFT_R:(j + 1) * DFT_R, :]], axis=0)
        p = jnp.dot(tab2_ref[...], z, preferred_element_type=F32)
        p_scr[j * DFT_R:(j + 1) * DFT_R, :d] = p[:DFT_R].astype(BF16)
        p_scr[j * DFT_R:(j + 1) * DFT_R, d:] = p[DFT_R:].astype(BF16)
    y = _mix_tail(p_scr[:, :d], p_scr[:, d:], cc_ref, sc_ref, wout_ref, scale)
    _cols_store(x_scr, x_ref[...].reshape(rows, d))
    for j in range(FNET_KB):
        sel = pl.ds(j, DFT_R, stride=FNET_KB)
        _cols_store_rows(o_scr, sel, _cols_load_rows(x_scr, sel) + gt_ref[...] * y[j * DFT_R:(j + 1) * DFT_R, :])
    o_ref[...] = _cols_load(o_scr).reshape(DFT_R, FNET_KB, d)


def _fnet_lat(x, mods, g_all, w_out, layer, n_batch):
    t, d = x.shape
    s = t // n_batch
    assert s == DFT_P * DFT_R
    tab1, tab2 = _seq_dft_tables(s)
    ccos, csin = _chan_dft_tables(d // FNET_GROUPS)
    mi = lambda j: (lambda b, *_: (layer * MOD_ROWS + b) * N_MOD + j)
    xv = x.reshape(n_batch, DFT_P, DFT_R, d)
    blk1 = (None, DFT_P, FNET_RB, d)
    rows1 = DFT_P * FNET_RB
    mi1 = lambda j: (lambda r, b: (layer * MOD_ROWS + b) * N_MOD + j)
    zr, zi = pl.pallas_call(
        _fnet_stage1_kernel,
        grid=(DFT_R // FNET_RB, n_batch),
        in_specs=[pl.BlockSpec(blk1, lambda r, b: (b, 0, r, 0)),
                  pl.BlockSpec((None, 1, d), lambda r, b: (layer, 0, 0)),
                  _mod_spec(d, mi1(0)), _mod_spec(d, mi1(1)),
                  pl.BlockSpec((None, 2 * rows1, rows1), lambda r, b: (r, 0, 0))],
        out_specs=[pl.BlockSpec(blk1, lambda r, b: (b, 0, r, 0))] * 2,
        out_shape=[jax.ShapeDtypeStruct((n_batch, DFT_P, DFT_R, d), BF16)] * 2,
        compiler_params=_cparams(("parallel", "parallel")),
        name="fnet_stage1",
    )(xv, g_all, mods, mods, tab1)
    zr = zr.reshape(n_batch, DFT_P * DFT_R, d)
    zi = zi.reshape(n_batch, DFT_P * DFT_R, d)
    xo = x.reshape(n_batch, DFT_R, DFT_P, d)
    blk2 = (None, DFT_R, FNET_KB, d)
    rows = FNET_KB * DFT_R
    scale = float(1.0 / np.sqrt(float(s) * (d // FNET_GROUPS)))
    out = pl.pallas_call(
        functools.partial(_fnet_stage2_kernel, scale=scale),
        grid=(n_batch, DFT_P // FNET_KB),
        in_specs=[pl.BlockSpec((None, rows, d), lambda b, k: (b, k, 0)),
                  pl.BlockSpec((None, rows, d), lambda b, k: (b, k, 0)),
                  pl.BlockSpec((2 * DFT_R, 2 * DFT_R), lambda b, k: (0, 0)),
                  pl.BlockSpec(ccos.shape, lambda b, k: (0, 0)),
                  pl.BlockSpec(csin.shape, lambda b, k: (0, 0)),
                  pl.BlockSpec((None, d, d), lambda b, k: (layer // 2, 0, 0)),
                  pl.BlockSpec(blk2, lambda b, k: (b, 0, k, 0)),
                  _mod_spec(d, mi(2))],
        out_specs=pl.BlockSpec(blk2, lambda b, k: (b, 0, k, 0)),
        out_shape=jax.ShapeDtypeStruct((n_batch, DFT_R, DFT_P, d), F32),
        scratch_shapes=[pltpu.VMEM((rows, 2 * d), BF16)] + [pltpu.VMEM((d // LANES, rows, LANES), F32)] * 2,
        compiler_params=_cparams(("parallel", "parallel")),
        name="fnet_stage2",
    )(zr, zi, tab2, ccos, csin, w_out, xo, mods)
    return out.reshape(t, d)


def _fnet_ctx_kernel(x_ref, g_ref, sh_ref, sc_ref, gt_ref, tab_ref, cc_ref, sc2_ref, wout_ref, o_ref, *, scale):
    n = x_ref.shape[0]
    h = _norm_mod(x_ref[...], g_ref[...], sh_ref[...], sc_ref[...]).astype(BF16)
    p = jnp.dot(tab_ref[...], h, preferred_element_type=F32)
    y = _mix_tail(p[:n].astype(BF16), p[n:].astype(BF16), cc_ref, sc2_ref, wout_ref, scale)
    o_ref[...] = x_ref[...] + gt_ref[...] * y


def _fnet_ctx(x, mods, g_all, w_out, layer, n_batch):
    t, d = x.shape
    n = t // n_batch
    gd = d // FNET_GROUPS
    ccos, csin = _chan_dft_tables(gd)
    tab = _ctx_dft_table(n)
    mi = lambda j: (lambda b: (layer * MOD_ROWS + CTX_ROW) * N_MOD + j)
    scale = float(1.0 / np.sqrt(float(n) * gd))
    return pl.pallas_call(
        functools.partial(_fnet_ctx_kernel, scale=scale),
        grid=(n_batch,),
        in_specs=[pl.BlockSpec((n, d), lambda b: (b, 0)),
                  pl.BlockSpec((None, 1, d), lambda b: (layer, 0, 0)),
                  _mod_spec(d, mi(0)), _mod_spec(d, mi(1)), _mod_spec(d, mi(2)),
                  pl.BlockSpec(tab.shape, lambda b: (0, 0)),
                  pl.BlockSpec(ccos.shape, lambda b: (0, 0)),
                  pl.BlockSpec(csin.shape, lambda b: (0, 0)),
                  pl.BlockSpec((None, d, d), lambda b: (layer // 2, 0, 0))],
        out_specs=pl.BlockSpec((n, d), lambda b: (b, 0)),
        out_shape=jax.ShapeDtypeStruct((t, d), F32),
        compiler_params=_cparams(("parallel",)),
        name="fnet_ctx",
    )(x, g_all, mods, mods, mods, tab, ccos, csin, w_out)


def _rope_tables(n_seq):
    rows = n_seq // GRID_W
    row = jnp.repeat(jnp.arange(rows, dtype=F32), GRID_W)
    col = jnp.tile(jnp.arange(GRID_W, dtype=F32), rows)
    inv_freq = ROPE_THETA ** (-jnp.arange(ROT_FREQS, dtype=F32) / ROT_FREQS)
    ang = jnp.stack([row[:, None] * inv_freq, col[:, None] * inv_freq], axis=1)
    cos, sin = jnp.cos(ang), jnp.sin(ang)
    zero = jnp.zeros_like(sin)
    cos_h = jnp.stack([cos, cos], axis=2).reshape(n_seq, HEAD_DIM)
    sin_lo = jnp.stack([-sin, zero], axis=2).reshape(n_seq, HEAD_DIM)
    sin_hi = jnp.stack([zero, sin], axis=2).reshape(n_seq, HEAD_DIM)
    rep = LANES // HEAD_DIM
    return jnp.tile(cos_h, (1, rep)), jnp.tile(sin_lo, (1, rep)), jnp.tile(sin_hi, (1, rep))


def _qkv_kernel(x_ref, g_ref, sh_ref, sc_ref, w_ref, *rest, rope, q_dim, kv_dim, q_scale):
    if rope:
        cos_ref, slo_ref, shi_ref, q_ref, k_ref, kt_ref, v_ref, vt_ref = rest
    else:
        q_ref, k_ref, kt_ref, v_ref, vt_ref = rest
    h = _norm_mod(x_ref[...], g_ref[...], sh_ref[...], sc_ref[...]).astype(BF16)
    qkv = jnp.dot(h, w_ref[...], preferred_element_type=F32)

    def rot(xs):
        if not rope:
            return xs
        return (xs * cos_ref[...] + pltpu.roll(xs, LANES - ROT_FREQS, axis=1) * slo_ref[...]
                + pltpu.roll(xs, ROT_FREQS, axis=1) * shi_ref[...])

    for j in range(q_dim // LANES):
        qj = rot(qkv[:, j * LANES:(j + 1) * LANES]) * q_scale
        if rope:
            q_ref[j * LANES:(j + 1) * LANES, :] = qj.T.astype(BF16)
        else:
            q_ref[:, j * LANES:(j + 1) * LANES] = qj.astype(BF16)
    k = jnp.concatenate([rot(qkv[:, q_dim + j * LANES:q_dim + (j + 1) * LANES]) for j in range(kv_dim // LANES)],
                        axis=-1)
    v = qkv[:, q_dim + kv_dim:]
    k_ref[...] = k.astype(BF16)
    kt_ref[...] = k.T.astype(BF16)
    v_ref[...] = v.astype(BF16)
    vt_ref[...] = v.T.astype(BF16)


def _qkv(x, mods, g_all, w_qkv, layer, n_batch, ctx, rope_tabs=None, tm=QKV_TM):
    t, d = x.shape
    n = t // n_batch
    tm = min(tm, n)
    tpb = n // tm
    kv_dim = N_KV_HEADS * HEAD_DIM
    q_dim = w_qkv.shape[2] - 2 * kv_dim
    mi = lambda j: _mod_index(layer, j, tpb, ctx)
    in_specs = [pl.BlockSpec((tm, d), lambda i: (i, 0)),
                pl.BlockSpec((None, 1, d), lambda i: (layer, 0, 0)),
                _mod_spec(d, mi(0)), _mod_spec(d, mi(1)),
                pl.BlockSpec((None,) + w_qkv.shape[1:], lambda i: (layer // 2, 0, 0))]
    args = [x, g_all, mods, mods, w_qkv]
    rope = rope_tabs is not None
    if rope:
        in_specs += [pl.BlockSpec((tm, LANES), lambda i: (i % tpb, 0))] * 3
        args += list(rope_tabs)
    return pl.pallas_call(
        functools.partial(_qkv_kernel, rope=rope, q_dim=q_dim, kv_dim=kv_dim, q_scale=HEAD_DIM ** -0.5),
        grid=(t // tm,),
        in_specs=in_specs,
        out_specs=[(pl.BlockSpec((None, q_dim, tm), lambda i: (i // tpb, 0, i % tpb)) if rope
                    else pl.BlockSpec((tm, q_dim), lambda i: (i, 0))),
                   pl.BlockSpec((tm, kv_dim), lambda i: (i, 0)),
                   pl.BlockSpec((None, kv_dim, tm), lambda i: (i // tpb, 0, i % tpb)),
                   pl.BlockSpec((tm, kv_dim), lambda i: (i, 0)),
                   pl.BlockSpec((None, kv_dim, tm), lambda i: (i // tpb, 0, i % tpb))],
        out_shape=[jax.ShapeDtypeStruct((n_batch, q_dim, n) if rope else (t, q_dim), BF16),
                   jax.ShapeDtypeStruct((t, kv_dim), BF16),
                   jax.ShapeDtypeStruct((n_batch, kv_dim, n), BF16),
                   jax.ShapeDtypeStruct((t, kv_dim), BF16),
                   jax.ShapeDtypeStruct((n_batch, kv_dim, n), BF16)],
        compiler_params=_cparams(("parallel",)),
        name="qkv_ctx" if ctx else "qkv_lat",
    )(*args)


ATTN_RB = 64


def _attend_ctx(q_ref, kt, v, sink_ref, o_ref):
    nq = q_ref.shape[0]
    n_heads = q_ref.shape[1] // HEAD_DIM
    group = n_heads // N_KV_HEADS
    outs = [None] * n_heads
    for g in range(N_KV_HEADS):
        heads = range(g * group, (g + 1) * group)
        qg = jnp.concatenate([q_ref[:, hd * HEAD_DIM:(hd + 1) * HEAD_DIM] for hd in heads], axis=0)
        s = jnp.dot(qg, kt[g * HEAD_DIM:(g + 1) * HEAD_DIM, :], preferred_element_type=F32)
        p_rows, den_rows = [], []
        for r0 in range(0, s.shape[0], ATTN_RB):
            rows = slice(r0, r0 + ATTN_RB)
            parts = [s[rows, k * LANES:(k + 1) * LANES] for k in range(s.shape[1] // LANES)]
            sink = sink_ref[heads[r0 // nq]]
            m = jnp.maximum(jnp.max(functools.reduce(jnp.maximum, parts), axis=-1, keepdims=True), sink)
            ps = [jnp.exp(part - m) for part in parts]
            den_rows.append(jnp.sum(functools.reduce(jnp.add, ps), axis=-1, keepdims=True) + jnp.exp(sink - m))
            p_rows.append(jnp.concatenate(ps, axis=-1).astype(BF16))
        pv = jnp.dot(jnp.concatenate(p_rows, axis=0), v, preferred_element_type=F32)
        on = pv[:, g * HEAD_DIM:(g + 1) * HEAD_DIM] / jnp.concatenate(den_rows, axis=0)
        for k, hd in enumerate(heads):
            outs[hd] = on[k * nq:(k + 1) * nq, :]
    o_ref[...] = jnp.concatenate(outs, axis=-1).astype(BF16)


def _attn_lat_kernel(sink_ref, qt_ref, kp_ref, kc_ref, kn_ref, vtp_ref, vtc_ref, vtn_ref, kx_ref, vtx_ref,
                     blo_ref, bhi_ref, ot_ref, k_scr, vt_scr):
    w = kc_ref.shape[0]
    n_ctx = kx_ref.shape[0]
    nq = qt_ref.shape[1]
    n_heads = qt_ref.shape[0] // HEAD_DIM
    group = n_heads // N_KV_HEADS
    for c, (kr, vr) in enumerate(((kp_ref, vtp_ref), (kc_ref, vtc_ref), (kn_ref, vtn_ref))):
        k_scr[c * w:(c + 1) * w, :] = kr[...]
        vt_scr[:, c * w:(c + 1) * w] = vr[...]
    k_scr[3 * w:3 * w + n_ctx, :] = kx_ref[...]
    vt_scr[:, 3 * w:3 * w + n_ctx] = vtx_ref[...]
    n_chunks = k_scr.shape[0] // w
    for g in range(N_KV_HEADS):
        heads = range(g * group, (g + 1) * group)
        qg = jnp.concatenate([qt_ref[hd * HEAD_DIM:(hd + 1) * HEAD_DIM, :] for hd in heads], axis=1)
        st = jnp.dot(k_scr[:, g * HEAD_DIM:(g + 1) * HEAD_DIM], qg, preferred_element_type=F32)
        p_cols, den_cols = [], []
        for k, hd in enumerate(heads):
            cols = slice(k * nq, (k + 1) * nq)
            parts = [st[c * w:(c + 1) * w, cols] for c in range(n_chunks)]
            parts[0] = parts[0] + blo_ref[...]
            parts[2] = parts[2] + bhi_ref[...]
            sink = sink_ref[hd]
            m = jnp.maximum(jnp.max(functools.reduce(jnp.maximum, parts), axis=0, keepdims=True), sink)
            ps = [jnp.exp(part - m) for part in parts]
            den_cols.append(jnp.sum(functools.reduce(jnp.add, ps), axis=0, keepdims=True) + jnp.exp(sink - m))
            p_cols.append(jnp.concatenate(ps, axis=0).astype(BF16))
        ot = jnp.dot(vt_scr[g * HEAD_DIM:(g + 1) * HEAD_DIM, :], jnp.concatenate(p_cols, axis=1),
                     preferred_element_type=F32)
        ot = ot / jnp.concatenate(den_cols, axis=1)
        for k, hd in enumerate(heads):
            ot_ref[hd * HEAD_DIM:(hd + 1) * HEAD_DIM, :] = ot[:, k * nq:(k + 1) * nq].astype(BF16)


def _attn_lat(qt, k, vt, kx, vtx, sinks, n_batch):
    _, qd, s = qt.shape
    n_ctx = kx.shape[0] // n_batch
    w = WINDOW
    nb = s // w
    kvd = k.shape[1]
    j_all = 3 * w + n_ctx
    ki = np.arange(w)[:, None]
    qi = np.arange(w)[None, :]
    lo = np.where(ki >= qi, 0.0, NEG).astype(np.float32)
    hi = np.where(ki <= qi, 0.0, NEG).astype(np.float32)
    off = np.full((w, w), NEG, np.float32)
    blo = jnp.asarray(np.stack([lo, off]))
    bhi = jnp.asarray(np.stack([hi, off]))
    prev = lambda b, i: (b * nb + jnp.maximum(i - 1, 0), 0)
    nxt = lambda b, i: (b * nb + jnp.minimum(i + 1, nb - 1), 0)
    tprev = lambda b, i: (b, 0, jnp.maximum(i - 1, 0))
    tnxt = lambda b, i: (b, 0, jnp.minimum(i + 1, nb - 1))
    return pl.pallas_call(
        _attn_lat_kernel,
        grid=(n_batch, nb),
        in_specs=[pl.BlockSpec(memory_space=pltpu.SMEM),
                  pl.BlockSpec((None, qd, w), lambda b, i: (b, 0, i)),
                  pl.BlockSpec((w, kvd), prev),
                  pl.BlockSpec((w, kvd), lambda b, i: (b * nb + i, 0)),
                  pl.BlockSpec((w, kvd), nxt),
                  pl.BlockSpec((None, kvd, w), tprev),
                  pl.BlockSpec((None, kvd, w), lambda b, i: (b, 0, i)),
                  pl.BlockSpec((None, kvd, w), tnxt),
                  pl.BlockSpec((n_ctx, kvd), lambda b, i: (b, 0)),
                  pl.BlockSpec((None, kvd, n_ctx), lambda b, i: (b, 0, 0)),
                  pl.BlockSpec((None, w, w), lambda b, i: (jnp.where(i == 0, 1, 0), 0, 0)),
                  pl.BlockSpec((None, w, w), lambda b, i: (jnp.where(i == nb - 1, 1, 0), 0, 0))],
        out_specs=pl.BlockSpec((None, qd, w), lambda b, i: (b, 0, i)),
        out_shape=jax.ShapeDtypeStruct((n_batch, qd, s), BF16),
        scratch_shapes=[pltpu.VMEM((j_all, kvd), BF16), pltpu.VMEM((kvd, j_all), BF16)],
        compiler_params=_cparams(("parallel", "parallel")),
        name="attn_lat",
    )(sinks, qt, k, k, k, vt, vt, vt, kx, vtx, blo, bhi)


def _attn_ctx_kernel(sink_ref, q_ref, kt_ref, v_ref, o_ref):
    _attend_ctx(q_ref, kt_ref[...], v_ref[...], sink_ref, o_ref)


def _attn_ctx(q, kt, v, sinks, n_batch):
    t, qd = q.shape
    n = t // n_batch
    kvd = v.shape[1]
    return pl.pallas_call(
        _attn_ctx_kernel,
        grid=(n_batch,),
        in_specs=[pl.BlockSpec(memory_space=pltpu.SMEM),
                  pl.BlockSpec((n, qd), lambda b: (b, 0)),
                  pl.BlockSpec((None, kvd, n), lambda b: (b, 0, 0)),
                  pl.BlockSpec((n, kvd), lambda b: (b, 0))],
        out_specs=pl.BlockSpec((n, qd), lambda b: (b, 0)),
        out_shape=jax.ShapeDtypeStruct((t, qd), BF16),
        compiler_params=_cparams(("parallel",)),
        name="attn_ctx",
    )(sinks, q, kt, v)


ROUTE_I1, ROUTE_I2, ROUTE_R1, ROUTE_R2, ROUTE_G1, ROUTE_G2 = range(6)
ROUTE_ROWS = 8
TAB_START, TAB_COUNT, TAB_USED, TAB_SIZE = 0, N_EXPERTS, 2 * N_EXPERTS, 2 * N_EXPERTS + 8
ROW_UNROLL = 8


def _router_kernel(x_ref, g_ref, sh_ref, sc_ref, wr_ref, tri_ref, route_ref, idx_ref, cnt_ref, carry_scr):
    @pl.when(pl.program_id(0) == 0)
    def _():
        carry_scr[...] = jnp.zeros_like(carry_scr)

    h = _norm_mod(x_ref[...], g_ref[...], sh_ref[...], sc_ref[...])
    hi = h.astype(BF16)
    lo = (h - hi.astype(F32)).astype(BF16)
    a = jnp.dot(hi, wr_ref[...], preferred_element_type=F32)
    logits = (a[:, :LANES] + a[:, LANES:]) + jnp.dot(lo, wr_ref[:, :LANES], preferred_element_type=F32)
    lane = lax.broadcasted_iota(jnp.int32, logits.shape, 1)
    lane_f = lane.astype(F32)
    logits = jnp.where(lane < N_EXPERTS, logits, -jnp.inf)
    m1 = jnp.max(logits, axis=-1, keepdims=True)
    i1 = jnp.min(jnp.where(logits == m1, lane_f, float(LANES)), axis=-1, keepdims=True)
    oh1 = lane_f == i1
    rest = jnp.where(oh1, -jnp.inf, logits)
    m2 = jnp.max(rest, axis=-1, keepdims=True)
    i2 = jnp.min(jnp.where(rest == m2, lane_f, float(LANES)), axis=-1, keepdims=True)
    oh2 = lane_f == i2
    e2 = jnp.exp(m2 - m1)
    g1 = 1.0 / (1.0 + e2)
    g2 = e2 / (1.0 + e2)
    sel = jnp.where(oh1, 1.0, 0.0) + jnp.where(oh2, 1.0, 0.0)
    before = jnp.dot(tri_ref[...], sel.astype(BF16), preferred_element_type=F32) + carry_scr[0:1, :]
    r1 = jnp.sum(jnp.where(oh1, before, 0.0), axis=-1, keepdims=True)
    r2 = jnp.sum(jnp.where(oh2, before, 0.0), axis=-1, keepdims=True)
    total = carry_scr[0:1, :] + jnp.sum(sel, axis=0, keepdims=True)
    carry_scr[...] = jnp.broadcast_to(total, carry_scr.shape)
    cnt_ref[...] = jnp.broadcast_to(total, cnt_ref.shape)
    rec = jnp.zeros_like(logits)
    for ln, val in ((ROUTE_I1, i1), (ROUTE_I2, i2), (ROUTE_R1, r1), (ROUTE_R2, r2), (ROUTE_G1, g1), (ROUTE_G2, g2)):
        rec = jnp.where(lane == ln, val, rec)
    route_ref[...] = rec
    idx_ref[...] = rec.T[:ROUTE_ROWS, :].astype(jnp.int32)


def _router(x, mods, g_all, w_router, layer, tiles_per_batch, ctx, tm):
    t, d = x.shape
    mi = lambda j: _mod_index(layer, j, tiles_per_batch // tm if not ctx else 1, ctx)
    wr = jnp.zeros((d, LANES), F32).at[:, :N_EXPERTS].set(w_router)
    whi = wr.astype(BF16)
    wr = jnp.concatenate([whi, (wr - whi.astype(F32)).astype(BF16)], axis=1)
    tri = jnp.asarray(np.tril(np.ones((tm, tm), np.float32), -1), BF16)
    return pl.pallas_call(
        _router_kernel,
        grid=(t // tm,),
        in_specs=[pl.BlockSpec((tm, d), lambda i: (i, 0)),
                  pl.BlockSpec((None, 1, d), lambda i: (layer, 0, 0)),
                  _mod_spec(d, mi(3)), _mod_spec(d, mi(4)),
                  pl.BlockSpec((d, 2 * LANES), lambda i: (0, 0)),
                  pl.BlockSpec((tm, tm), lambda i: (0, 0))],
        out_specs=[pl.BlockSpec((tm, LANES), lambda i: (i, 0)),
                   pl.BlockSpec((None, ROUTE_ROWS, tm), lambda i: (i, 0, 0)),
                   pl.BlockSpec((8, LANES), lambda i: (0, 0))],
        out_shape=[jax.ShapeDtypeStruct((t, LANES), F32),
                   jax.ShapeDtypeStruct((t // tm, ROUTE_ROWS, tm), jnp.int32),
                   jax.ShapeDtypeStruct((8, LANES), F32)],
        scratch_shapes=[pltpu.VMEM((8, LANES), F32)],
        compiler_params=_cparams(("arbitrary",)),
        name="router",
    )(x, g_all, mods, mods, wr, tri)


SUB = 8


def _to_row_tiles(ref, val):
    n = val.shape[0]
    for k in range(SUB):
        ref[pl.ds(k, n, stride=SUB), :] = val[:, k * LANES:(k + 1) * LANES]


def _from_row_tiles(ref):
    n = ref.shape[0] // SUB
    return jnp.concatenate([ref[pl.ds(k, n, stride=SUB), :] for k in range(SUB)], axis=-1)


def _row_copy(src, dst, src_row, dst_row, sem):
    return pltpu.make_async_copy(src.at[pl.ds(pl.multiple_of(src_row * SUB, SUB), SUB)],
                                 dst.at[pl.ds(pl.multiple_of(dst_row * SUB, SUB), SUB)], sem)


def _for_rows(tm, body):
    def blk(i, _):
        base = pl.multiple_of(i * ROW_UNROLL, ROW_UNROLL)
        for j in range(ROW_UNROLL):
            body(base + j, j)
        return 0

    lax.fori_loop(0, tm // ROW_UNROLL, blk, 0)


ZERO_SEM = 4


def _dispatch_kernel(tab_ref, slot_ref, x_ref, g_ref, sh_ref, sc_ref, xs_ref, h_scr, zero_scr, sems, *, group):
    tm = x_ref.shape[0]
    n_groups = xs_ref.shape[0] // (group * SUB)
    step = pl.program_id(0)
    n_steps = pl.num_programs(0)
    cur = lax.rem(step, 2)

    def wait_buffer(b):
        for k in range(2):
            pltpu.make_async_copy(h_scr.at[b], xs_ref.at[pl.ds(0, tm * SUB)], sems.at[2 * b + k]).wait()

    @pl.when(step >= 2)
    def _():
        wait_buffer(cur)

    _to_row_tiles(h_scr.at[cur], _norm_mod(x_ref[...], g_ref[...], sh_ref[...], sc_ref[...]))

    def issue(r, j):
        for k in range(2):
            _row_copy(h_scr.at[cur], xs_ref, r, slot_ref[0, k, r], sems.at[2 * cur + k]).start(priority=j % 2)

    _for_rows(tm, issue)

    @pl.when(jnp.logical_and(step == n_steps - 1, step >= 1))
    def _():
        wait_buffer(1 - cur)

    @pl.when(step == n_steps - 1)
    def _():
        wait_buffer(cur)
        zero_scr[...] = jnp.zeros_like(zero_scr)
        for e in range(N_EXPERTS):
            n = tab_ref[TAB_COUNT + e]
            n_pad = lax.rem(group - lax.rem(n, group), group)
            first = tab_ref[TAB_START + e] + n

            def fill(k, _, first=first):
                _row_copy(zero_scr, xs_ref, 0, first + k, sems.at[ZERO_SEM]).start()
                return 0

            def fill_done(k, _):
                _row_copy(zero_scr, xs_ref, 0, 0, sems.at[ZERO_SEM]).wait()
                return 0

            lax.fori_loop(0, n_pad, fill, 0)
            lax.fori_loop(0, n_pad, fill_done, 0)

        def clear(j, _):
            row = pl.multiple_of(j * (group * SUB), group * SUB)
            cp = pltpu.make_async_copy(zero_scr, xs_ref.at[pl.ds(row, group * SUB)], sems.at[ZERO_SEM])
            cp.start()
            cp.wait()
            return 0

        lax.fori_loop(tab_ref[TAB_USED], n_groups, clear, 0)


def _dispatch(x, tab, slots, mods, g_all, layer, tiles_per_batch, ctx, n_groups, group, tm):
    t, d = x.shape
    mi = lambda j: _mod_index(layer, j, tiles_per_batch // tm if not ctx else 1, ctx)
    return pl.pallas_call(
        functools.partial(_dispatch_kernel, group=group),
        grid=(t // tm,),
        in_specs=[pl.BlockSpec(memory_space=pltpu.SMEM),
                  pl.BlockSpec((1, 2, tm), lambda i: (i, 0, 0), memory_space=pltpu.SMEM),
                  pl.BlockSpec((tm, d), lambda i: (i, 0)),
                  pl.BlockSpec((None, 1, d), lambda i: (layer, 0, 0)),
                  _mod_spec(d, mi(3)), _mod_spec(d, mi(4))],
        out_specs=pl.BlockSpec(memory_space=pl.ANY),
        out_shape=jax.ShapeDtypeStruct((n_groups * group * SUB, LANES), F32),
        scratch_shapes=[pltpu.VMEM((2, tm * SUB, LANES), F32), pltpu.VMEM((group * SUB, LANES), F32),
                        pltpu.SemaphoreType.DMA((ZERO_SEM + 1,))],
        compiler_params=_cparams(("arbitrary",)),
        name="moe_dispatch",
    )(tab, slots, x, g_all, mods, mods)


def _moe_ffn_kernel(be_ref, bv_ref, xs_ref, wg_ref, wu_ref, wd_ref, ys_ref, h_scr, acc_scr):
    del be_ref
    i = pl.program_id(0)
    f = pl.program_id(1)
    last = f == pl.num_programs(1) - 1
    valid = bv_ref[i] > 0

    @pl.when(jnp.logical_and(valid, f == 0))
    def _():
        h_scr[...] = _from_row_tiles(xs_ref).astype(BF16)
        acc_scr[...] = jnp.zeros_like(acc_scr)

    @pl.when(valid)
    def _():
        h = h_scr[...]
        a = jnp.dot(h, wg_ref[...], preferred_element_type=F32)
        u = jnp.dot(h, wu_ref[...], preferred_element_type=F32)
        t = (a * _sigmoid(a)) * u
        y = jnp.dot(t.astype(BF16), wd_ref[...], preferred_element_type=F32)

        @pl.when(jnp.logical_not(last))
        def _():
            acc_scr[...] += y

        @pl.when(last)
        def _():
            _to_row_tiles(ys_ref, acc_scr[...] + y)

    @pl.when(jnp.logical_and(jnp.logical_not(valid), last))
    def _():
        ys_ref[...] = jnp.zeros_like(ys_ref)


def _moe_ffn(xs, block_expert, block_valid, wg, wu, wd, layer, tm, tf=MOE_TF):
    d, ff = wg.shape[2:]
    assert d == SUB * LANES and xs.shape[1] == LANES
    cap = xs.shape[0] // SUB
    tf = min(tf, ff)
    nf = ff // tf
    sj = layer // 2
    fsel = lambda i, f, bv: jnp.where(bv[i] > 0, f, nf - 1)
    grid_spec = pltpu.PrefetchScalarGridSpec(
        num_scalar_prefetch=2,
        grid=(cap // tm, nf),
        in_specs=[pl.BlockSpec((tm * SUB, LANES), lambda i, f, be, bv: (i, 0)),
                  pl.BlockSpec((None, None, d, tf), lambda i, f, be, bv: (sj, be[i], 0, fsel(i, f, bv))),
                  pl.BlockSpec((None, None, d, tf), lambda i, f, be, bv: (sj, be[i], 0, fsel(i, f, bv))),
                  pl.BlockSpec((None, None, tf, d), lambda i, f, be, bv: (sj, be[i], fsel(i, f, bv), 0))],
        out_specs=pl.BlockSpec((tm * SUB, LANES), lambda i, f, be, bv: (i, 0)),
        scratch_shapes=[pltpu.VMEM((tm, d), BF16), pltpu.VMEM((tm, d), F32)],
    )
    return pl.pallas_call(
        _moe_ffn_kernel,
        grid_spec=grid_spec,
        out_shape=jax.ShapeDtypeStruct(xs.shape, F32),
        compiler_params=_cparams(("parallel", "arbitrary")),
        name="moe_ffn",
    )(block_expert, block_valid, xs, wg, wu, wd)


def _combine_kernel(slot_ref, next_slot_ref, route_ref, x_ref, gt_ref, fg_ref, ys_ref, o_ref, buf, sems, *, final_norm):
    tm = x_ref.shape[0]
    step = pl.program_id(0)
    cur = lax.rem(step, 2)

    def gather(idx_ref, b):
        def issue(r, j):
            for k in range(2):
                _row_copy(ys_ref, buf.at[b, k], idx_ref[0, k, r], r, sems.at[2 * b + k]).start(priority=j % 2)

        _for_rows(tm, issue)

    @pl.when(step == 0)
    def _():
        gather(slot_ref, cur)

    @pl.when(step + 1 < pl.num_programs(0))
    def _():
        gather(next_slot_ref, 1 - cur)

    for k in range(2):
        pltpu.make_async_copy(ys_ref.at[pl.ds(0, tm * SUB)], buf.at[cur, k], sems.at[2 * cur + k]).wait()
    rec = route_ref[...]
    g1 = rec[:, ROUTE_G1:ROUTE_G1 + 1]
    g2 = rec[:, ROUTE_G2:ROUTE_G2 + 1]
    y = _from_row_tiles(buf.at[cur, 0]) * g1 + _from_row_tiles(buf.at[cur, 1]) * g2
    out = x_ref[...] + gt_ref[...] * y
    if final_norm:
        out = out * lax.rsqrt(jnp.mean(out * out, axis=-1, keepdims=True) + EPS) * fg_ref[...]
    o_ref[...] = out


def _combine(ys, slots, route, x, mods, final_g, layer, tiles_per_batch, ctx, final_norm, tm):
    t, d = x.shape
    mi = _mod_index(layer, 5, tiles_per_batch // tm if not ctx else 1, ctx)
    n_tiles = t // tm
    return pl.pallas_call(
        functools.partial(_combine_kernel, final_norm=final_norm),
        grid=(n_tiles,),
        in_specs=[pl.BlockSpec((1, 2, tm), lambda i: (i, 0, 0), memory_space=pltpu.SMEM),
                  pl.BlockSpec((1, 2, tm), lambda i: (jnp.minimum(i + 1, n_tiles - 1), 0, 0), memory_space=pltpu.SMEM),
                  pl.BlockSpec((tm, LANES), lambda i: (i, 0)),
                  pl.BlockSpec((tm, d), lambda i: (i, 0)),
                  _mod_spec(d, mi),
                  pl.BlockSpec((1, d), lambda i: (0, 0)),
                  pl.BlockSpec(memory_space=pl.ANY)],
        out_specs=pl.BlockSpec((tm, d), lambda i: (i, 0)),
        out_shape=jax.ShapeDtypeStruct((t, d), F32),
        scratch_shapes=[pltpu.VMEM((2, 2, tm * SUB, LANES), F32), pltpu.SemaphoreType.DMA((4,))],
        compiler_params=_cparams(("arbitrary",)),
        name="moe_combine",
    )(slots, slots, route, x, mods, final_g.reshape(1, d), ys)


def _moe(x, mods, g_all, w_router, wg, wu, wd, final_g, layer, tiles_per_batch, ctx, final_norm):
    t, d = x.shape
    group = MOE_TM if 2 * t >= 4 * N_EXPERTS * MOE_TM else MOE_TM // 2
    row_tm = min(ROW_TM, t)
    route, idx, counts = _router(x, mods, g_all, w_router, layer, tiles_per_batch, ctx, row_tm)
    counts = counts[0, :N_EXPERTS].astype(jnp.int32)
    groups = (counts + group - 1) // group
    ends = jnp.cumsum(groups)
    starts = (ends - groups) * group
    n_groups = (2 * t + group - 1) // group + N_EXPERTS
    tab = jnp.zeros((TAB_SIZE,), jnp.int32)
    tab = tab.at[TAB_START:TAB_START + N_EXPERTS].set(starts).at[TAB_COUNT:TAB_COUNT + N_EXPERTS].set(counts)
    tab = tab.at[TAB_USED].set(ends[-1])
    gi = jnp.arange(n_groups, dtype=jnp.int32)
    block_expert = jnp.minimum(jnp.sum(gi[:, None] >= ends[None, :], axis=1), N_EXPERTS - 1).astype(jnp.int32)
    block_valid = (gi < ends[-1]).astype(jnp.int32)
    def slot_rows(e, r):
        base = functools.reduce(jnp.add, [jnp.where(e == k, starts[k], 0) for k in range(N_EXPERTS)])
        return base + r
    slots = jnp.stack([slot_rows(idx[:, ROUTE_I1, :], idx[:, ROUTE_R1, :]),
                       slot_rows(idx[:, ROUTE_I2, :], idx[:, ROUTE_R2, :])], axis=1)
    xs = _dispatch(x, tab, slots, mods, g_all, layer, tiles_per_batch, ctx, n_groups, group, row_tm)
    ys = _moe_ffn(xs, block_expert, block_valid, wg, wu, wd, layer, group)
    return _combine(ys, slots, route, x, mods, final_g, layer, tiles_per_batch, ctx, final_norm, row_tm)


def kernel(x, c, ctx, c_ctx, ada_w, ada_b, norm_mix_g, norm_ffn_g, fnet_w_out, attn_w_qkv, attn_w_o, attn_sinks,
           ffn_w_gate, ffn_w_up, ffn_w_down, moe_w_router, moe_w_gate, moe_w_up, moe_w_down, final_norm_g):
    n_batch, s, d = x.shape
    n_ctx = ctx.shape[1]
    depth = ada_w.shape[0]
    mods = _ada(c, c_ctx, ada_w, ada_b)
    g_mix = norm_mix_g.reshape(depth, 1, d)
    g_ffn = norm_ffn_g.reshape(depth, 1, d)
    rope_tabs = _rope_tables(s)
    x_lat = x.reshape(n_batch * s, d)
    x_ctx = ctx.reshape(n_batch * n_ctx, d)
    bf = lambda a: a.astype(BF16)
    w_out, w_qkv, w_o = bf(fnet_w_out), bf(attn_w_qkv), bf(attn_w_o)
    wg, wu, wd = bf(ffn_w_gate), bf(ffn_w_up), bf(ffn_w_down)
    eg, eu, ed = bf(moe_w_gate), bf(moe_w_up), bf(moe_w_down)
    for i in range(depth):
        j = i // 2
        last = i == depth - 1
        if i % 2 == 0:
            x_lat = _fnet_lat(x_lat, mods, g_mix, w_out, i, n_batch)
            if not last:
                x_ctx = _fnet_ctx(x_ctx, mods, g_mix, w_out, i, n_batch)
            x_lat = _ffn_dense(x_lat, mods, g_ffn, wg, wu, wd, i, s, False)
            if not last:
                x_ctx = _ffn_dense(x_ctx, mods, g_ffn, wg, wu, wd, i, n_ctx, True)
        else:
            qt, k, _, _, vt = _qkv(x_lat, mods, g_mix, w_qkv, i, n_batch, False, rope_tabs)
            qx, kx, ktx, vx, vtx = _qkv(x_ctx, mods, g_mix, w_qkv, i, n_batch, True)
            ot = _attn_lat(qt, k, vt, kx, vtx, attn_sinks[j], n_batch)
            x_lat = _proj_res(ot, w_o, x_lat, mods, i, 2, s, False)
            if not last:
                ox = _attn_ctx(qx, ktx, vx, attn_sinks[j], n_batch)
                x_ctx = _proj_res(ox, w_o, x_ctx, mods, i, 2, n_ctx, True)
            x_lat = _moe(x_lat, mods, g_ffn, moe_w_router[j], eg, eu, ed, final_norm_g, i, s, False, last)
            if not last:
                x_ctx = _moe(x_ctx, mods, g_ffn, moe_w_router[j], eg, eu, ed, final_norm_g, i, n_ctx, True, False)
    return x_lat.reshape(n_batch, s, d)
```

```python
import functools

import numpy as np
import jax
import jax.numpy as jnp
from jax import lax
from jax.experimental import pallas as pl
from jax.experimental.pallas import tpu as pltpu

F32 = jnp.float32
BF16 = jnp.bfloat16

GRID_W = 64
N_MOD = 6
EPS = 1e-6
FNET_GROUPS = 4
HEAD_DIM = 64
N_KV_HEADS = 4
WINDOW = 128
ROPE_THETA = 10000.0
ROT_FREQS = HEAD_DIM // 4
N_EXPERTS = 8
MOD_ROWS = 8
CTX_ROW = MOD_ROWS - 1
LANES = 128
NEG = -1e30
VMEM_LIMIT = 56 * 1024 * 1024

FFN_TM = 512
FFN_TF = 1792
MOE_TM = 512
MOE_TF = 1792
ROW_TM = 512
PROJ_TM = 1024
QKV_TM = 1024
DFT_P = 64
DFT_R = 128
FNET_RB = 16
FNET_KB = 8


def _cparams(sem):
    return pltpu.CompilerParams(dimension_semantics=sem, vmem_limit_bytes=VMEM_LIMIT)


def _sigmoid(a):
    return 1.0 / (1.0 + jnp.exp(-a))


def _norm_mod(x, g, shift, scale):
    xn = x * lax.rsqrt(jnp.mean(x * x, axis=-1, keepdims=True) + EPS)
    return (xn * g) * (1.0 + scale) + shift


def _mod_spec(d, idx_fn):
    return pl.BlockSpec((None, 1, d), lambda *ids: (idx_fn(*ids), 0, 0))


def _mod_index(layer, j, tiles_per_batch, ctx):
    def fn(i, *_):
        b = CTX_ROW if ctx else i // tiles_per_batch
        return (layer * MOD_ROWS + b) * N_MOD + j
    return fn


def _ada_kernel(c_ref, w_ref, b_ref, o_ref):
    cc = c_ref[...]
    s = cc * _sigmoid(cc)
    w = w_ref[...]
    whi = w.astype(BF16)
    wlo = (w - whi.astype(F32)).astype(BF16)
    hi = s.astype(BF16)
    lo = (s - hi.astype(F32)).astype(BF16)
    n = s.shape[0]
    a = jnp.dot(jnp.concatenate([hi, lo], axis=0), whi, preferred_element_type=F32)
    o_ref[...] = (a[:n] + a[n:]) + jnp.dot(hi, wlo, preferred_element_type=F32) + b_ref[...]


def _ada(c, c_ctx, ada_w, ada_b):
    depth, d, _ = ada_w.shape
    nb = c.shape[0]
    assert nb < MOD_ROWS
    cc = jnp.concatenate([c, jnp.zeros((CTX_ROW - nb, d), F32), c_ctx[None]], axis=0)
    out = pl.pallas_call(
        _ada_kernel,
        grid=(depth, N_MOD),
        in_specs=[pl.BlockSpec((MOD_ROWS, d), lambda l, j: (0, 0)),
                  pl.BlockSpec((None, d, d), lambda l, j: (l, 0, j)),
                  pl.BlockSpec((None, 1, d), lambda l, j: (l * N_MOD + j, 0, 0))],
        out_specs=pl.BlockSpec((None, MOD_ROWS, d), lambda l, j: (l, 0, j)),
        out_shape=jax.ShapeDtypeStruct((depth, MOD_ROWS, N_MOD * d), F32),
        compiler_params=_cparams(("parallel", "parallel")),
        name="ada",
    )(cc, ada_w, ada_b.reshape(depth * N_MOD, 1, d))
    return out.reshape(depth * MOD_ROWS * N_MOD, 1, d)


def _ffn_kernel(x_ref, g_ref, sh_ref, sc_ref, gt_ref, wg_ref, wu_ref, wd_ref, o_ref, h_scr, acc_scr):
    f = pl.program_id(1)

    @pl.when(f == 0)
    def _():
        h_scr[...] = _norm_mod(x_ref[...], g_ref[...], sh_ref[...], sc_ref[...]).astype(BF16)
        acc_scr[...] = jnp.zeros_like(acc_scr)

    h = h_scr[...]
    a = jnp.dot(h, wg_ref[...], preferred_element_type=F32)
    u = jnp.dot(h, wu_ref[...], preferred_element_type=F32)
    t = (a * _sigmoid(a)) * u
    acc_scr[...] += jnp.dot(t.astype(BF16), wd_ref[...], preferred_element_type=F32)

    @pl.when(f == pl.num_programs(1) - 1)
    def _():
        o_ref[...] = x_ref[...] + gt_ref[...] * acc_scr[...]


def _ffn_dense(x, mods, g_all, wg, wu, wd, layer, tiles_per_batch, ctx, tm=FFN_TM, tf=FFN_TF):
    t, d = x.shape
    ff = wg.shape[2]
    tm = min(tm, t)
    tf = min(tf, ff)
    sj = layer // 2
    mi = functools.partial(_mod_index, layer, tiles_per_batch=tiles_per_batch // tm if not ctx else 1, ctx=ctx)
    return pl.pallas_call(
        _ffn_kernel,
        grid=(t // tm, ff // tf),
        in_specs=[pl.BlockSpec((tm, d), lambda i, f: (i, 0)),
                  pl.BlockSpec((None, 1, d), lambda i, f: (layer, 0, 0)),
                  _mod_spec(d, mi(3)), _mod_spec(d, mi(4)), _mod_spec(d, mi(5)),
                  pl.BlockSpec((None, d, tf), lambda i, f: (sj, 0, f)),
                  pl.BlockSpec((None, d, tf), lambda i, f: (sj, 0, f)),
                  pl.BlockSpec((None, tf, d), lambda i, f: (sj, f, 0))],
        out_specs=pl.BlockSpec((tm, d), lambda i, f: (i, 0)),
        out_shape=jax.ShapeDtypeStruct((t, d), F32),
        scratch_shapes=[pltpu.VMEM((tm, d), BF16), pltpu.VMEM((tm, d), F32)],
        compiler_params=_cparams(("parallel", "arbitrary")),
        name="ffn_dense",
    )(x, g_all, mods, mods, mods, wg, wu, wd)


def _proj_res_kernel(a_ref, w_ref, x_ref, gt_ref, o_ref, *, transposed):
    dims = (((0,), (0,)), ((), ())) if transposed else (((1,), (0,)), ((), ()))
    y = lax.dot_general(a_ref[...], w_ref[...], dims, preferred_element_type=F32)
    o_ref[...] = x_ref[...] + gt_ref[...] * y


def _proj_res(a, w, x, mods, layer, j, tiles_per_batch, ctx, tm=PROJ_TM):
    t, d = x.shape
    transposed = a.ndim == 3
    k = a.shape[1]
    tm = min(tm, t)
    tpb = tiles_per_batch // tm
    mi = _mod_index(layer, j, tpb if not ctx else 1, ctx)
    a_spec = (pl.BlockSpec((None, k, tm), lambda i: (i // tpb, 0, i % tpb)) if transposed
              else pl.BlockSpec((tm, k), lambda i: (i, 0)))
    return pl.pallas_call(
        functools.partial(_proj_res_kernel, transposed=transposed),
        grid=(t // tm,),
        in_specs=[a_spec,
                  pl.BlockSpec((None, k, d), lambda i: (layer // 2, 0, 0)),
                  pl.BlockSpec((tm, d), lambda i: (i, 0)),
                  _mod_spec(d, mi)],
        out_specs=pl.BlockSpec((tm, d), lambda i: (i, 0)),
        out_shape=jax.ShapeDtypeStruct((t, d), F32),
        compiler_params=_cparams(("parallel",)),
        name="proj_res",
    )(a, w, x, mods)


def _dft_angles(n):
    a = np.arange(n)
    return 2.0 * np.pi * ((a[:, None] * a[None, :]) % n) / n


def _seq_dft_tables(n):
    kp = np.arange(DFT_P)[None, :, None]
    p = np.arange(DFT_P)[None, None, :]
    r = np.arange(DFT_R)[:, None, None]
    th = 2.0 * np.pi * ((kp * (DFT_R * p + r)) % n) / n
    tab1 = np.stack([np.cos(th), -np.sin(th)], axis=1)
    tab1 = tab1.reshape(DFT_R // FNET_RB, FNET_RB, 2, DFT_P, DFT_P)
    kron = np.zeros((DFT_R // FNET_RB, 2, DFT_P, FNET_RB, DFT_P, FNET_RB), np.float32)
    for rl in range(FNET_RB):
        kron[:, :, :, rl, :, rl] = tab1[:, rl]
    tab1 = kron.reshape(DFT_R // FNET_RB, 2 * DFT_P * FNET_RB, DFT_P * FNET_RB)
    th2 = _dft_angles(DFT_R)
    c2, s2 = np.cos(th2), np.sin(th2)
    tab2 = np.block([[c2, s2], [-s2, c2]])
    return jnp.asarray(tab1, BF16), jnp.asarray(tab2, BF16)


def _chan_dft_tables(group_dim):
    th = _dft_angles(group_dim)
    return jnp.asarray(np.cos(th), BF16), jnp.asarray(np.sin(th), BF16)


def _ctx_dft_table(n_ctx):
    th = _dft_angles(n_ctx)
    return jnp.asarray(np.concatenate([np.cos(th), -np.sin(th)], axis=0), BF16)


def _cols_store(scr, val):
    for c in range(scr.shape[0]):
        scr[c] = val[:, c * LANES:(c + 1) * LANES]


def _cols_load(scr):
    return jnp.concatenate([scr[c] for c in range(scr.shape[0])], axis=-1)


def _cols_store_rows(scr, sel, val):
    for c in range(scr.shape[0]):
        scr[c, sel, :] = val[:, c * LANES:(c + 1) * LANES]


def _cols_load_rows(scr, sel):
    return jnp.concatenate([scr[c, sel, :] for c in range(scr.shape[0])], axis=-1)


def _fnet_stage1_kernel(x_ref, g_ref, sh_ref, sc_ref, tab_ref, zr_ref, zi_ref):
    d = g_ref.shape[-1]
    rows = DFT_P * FNET_RB
    h = _norm_mod(x_ref[...].reshape(rows, d), g_ref[...], sh_ref[...], sc_ref[...]).astype(BF16)
    z = jnp.dot(tab_ref[...], h, preferred_element_type=F32)
    zr_ref[...] = z[:rows].astype(BF16).reshape(DFT_P, FNET_RB, d)
    zi_ref[...] = z[rows:].astype(BF16).reshape(DFT_P, FNET_RB, d)


def _mix_tail(pr, pi, cc_ref, sc_ref, wout_ref, scale):
    gd = cc_ref.shape[0]
    ys = []
    for g in range(pr.shape[1] // gd):
        ys.append(jnp.dot(pr[:, g * gd:(g + 1) * gd], cc_ref[...], preferred_element_type=F32)
                  + jnp.dot(pi[:, g * gd:(g + 1) * gd], sc_ref[...], preferred_element_type=F32))
    mixed = (jnp.concatenate(ys, axis=-1) * scale).astype(BF16)
    return jnp.dot(mixed, wout_ref[...], preferred_element_type=F32)


def _fnet_stage2_kernel(zr_ref, zi_ref, tab2_ref, cc_ref, sc_ref, wout_ref, x_ref, gt_ref, o_ref,
                        p_scr, x_scr, o_scr, *, scale):
    d = gt_ref.shape[-1]
    rows = DFT_R * FNET_KB
    for j in range(FNET_KB):
        z = jnp.concatenate([zr_ref[j * DFT_R:(j + 1) * DFT_R, :], zi_ref[j * DFT_R:(j + 1) * DFT_R, :]], axis=0)
        p = jnp.dot(tab2_ref[...], z, preferred_element_type=F32)
        p_scr[j * DFT_R:(j + 1) * DFT_R, :d] = p[:DFT_R].astype(BF16)
        p_scr[j * DFT_R:(j + 1) * DFT_R, d:] = p[DFT_R:].astype(BF16)
    y = _mix_tail(p_scr[:, :d], p_scr[:, d:], cc_ref, sc_ref, wout_ref, scale)
    _cols_store(x_scr, x_ref[...].reshape(rows, d))
    for j in range(FNET_KB):
        sel = pl.ds(j, DFT_R, stride=FNET_KB)
        _cols_store_rows(o_scr, sel, _cols_load_rows(x_scr, sel) + gt_ref[...] * y[j * DFT_R:(j + 1) * DFT_R, :])
    o_ref[...] = _cols_load(o_scr).reshape(DFT_R, FNET_KB, d)


def _fnet_lat(x, mods, g_all, w_out, layer, n_batch):
    t, d = x.shape
    s = t // n_batch
    assert s == DFT_P * DFT_R
    tab1, tab2 = _seq_dft_tables(s)
    ccos, csin = _chan_dft_tables(d // FNET_GROUPS)
    mi = lambda j: (lambda b, *_: (layer * MOD_ROWS + b) * N_MOD + j)
    xv = x.reshape(n_batch, DFT_P, DFT_R, d)
    blk1 = (None, DFT_P, FNET_RB, d)
    rows1 = DFT_P * FNET_RB
    mi1 = lambda j: (lambda r, b: (layer * MOD_ROWS + b) * N_MOD + j)
    zr, zi = pl.pallas_call(
        _fnet_stage1_kernel,
        grid=(DFT_R // FNET_RB, n_batch),
        in_specs=[pl.BlockSpec(blk1, lambda r, b: (b, 0, r, 0)),
                  pl.BlockSpec((None, 1, d), lambda r, b: (layer, 0, 0)),
                  _mod_spec(d, mi1(0)), _mod_spec(d, mi1(1)),
                  pl.BlockSpec((None, 2 * rows1, rows1), lambda r, b: (r, 0, 0))],
        out_specs=[pl.BlockSpec(blk1, lambda r, b: (b, 0, r, 0))] * 2,
        out_shape=[jax.ShapeDtypeStruct((n_batch, DFT_P, DFT_R, d), BF16)] * 2,
        compiler_params=_cparams(("parallel", "parallel")),
        name="fnet_stage1",
    )(xv, g_all, mods, mods, tab1)
    zr = zr.reshape(n_batch, DFT_P * DFT_R, d)
    zi = zi.reshape(n_batch, DFT_P * DFT_R, d)
    xo = x.reshape(n_batch, DFT_R, DFT_P, d)
    blk2 = (None, DFT_R, FNET_KB, d)
    rows = FNET_KB * DFT_R
    scale = float(1.0 / np.sqrt(float(s) * (d // FNET_GROUPS)))
    out = pl.pallas_call(
        functools.partial(_fnet_stage2_kernel, scale=scale),
        grid=(n_batch, DFT_P // FNET_KB),
        in_specs=[pl.BlockSpec((None, rows, d), lambda b, k: (b, k, 0)),
                  pl.BlockSpec((None, rows, d), lambda b, k: (b, k, 0)),
                  pl.BlockSpec((2 * DFT_R, 2 * DFT_R), lambda b, k: (0, 0)),
                  pl.BlockSpec(ccos.shape, lambda b, k: (0, 0)),
                  pl.BlockSpec(csin.shape, lambda b, k: (0, 0)),
                  pl.BlockSpec((None, d, d), lambda b, k: (layer // 2, 0, 0)),
                  pl.BlockSpec(blk2, lambda b, k: (b, 0, k, 0)),
                  _mod_spec(d, mi(2))],
        out_specs=pl.BlockSpec(blk2, lambda b, k: (b, 0, k, 0)),
        out_shape=jax.ShapeDtypeStruct((n_batch, DFT_R, DFT_P, d), F32),
        scratch_shapes=[pltpu.VMEM((rows, 2 * d), BF16)] + [pltpu.VMEM((d // LANES, rows, LANES), F32)] * 2,
        compiler_params=_cparams(("parallel", "parallel")),
        name="fnet_stage2",
    )(zr, zi, tab2, ccos, csin, w_out, xo, mods)
    return out.reshape(t, d)


def _fnet_ctx_kernel(x_ref, g_ref, sh_ref, sc_ref, gt_ref, tab_ref, cc_ref, sc2_ref, wout_ref, o_ref, *, scale):
    n = x_ref.shape[0]
    h = _norm_mod(x_ref[...], g_ref[...], sh_ref[...], sc_ref[...]).astype(BF16)
    p = jnp.dot(tab_ref[...], h, preferred_element_type=F32)
    y = _mix_tail(p[:n].astype(BF16), p[n:].astype(BF16), cc_ref, sc2_ref, wout_ref, scale)
    o_ref[...] = x_ref[...] + gt_ref[...] * y


def _fnet_ctx(x, mods, g_all, w_out, layer, n_batch):
    t, d = x.shape
    n = t // n_batch
    gd = d // FNET_GROUPS
    ccos, csin = _chan_dft_tables(gd)
    tab = _ctx_dft_table(n)
    mi = lambda j: (lambda b: (layer * MOD_ROWS + CTX_ROW) * N_MOD + j)
    scale = float(1.0 / np.sqrt(float(n) * gd))
    return pl.pallas_call(
        functools.partial(_fnet_ctx_kernel, scale=scale),
        grid=(n_batch,),
        in_specs=[pl.BlockSpec((n, d), lambda b: (b, 0)),
                  pl.BlockSpec((None, 1, d), lambda b: (layer, 0, 0)),
                  _mod_spec(d, mi(0)), _mod_spec(d, mi(1)), _mod_spec(d, mi(2)),
                  pl.BlockSpec(tab.shape, lambda b: (0, 0)),
                  pl.BlockSpec(ccos.shape, lambda b: (0, 0)),
                  pl.BlockSpec(csin.shape, lambda b: (0, 0)),
                  pl.BlockSpec((None, d, d), lambda b: (layer // 2, 0, 0))],
        out_specs=pl.BlockSpec((n, d), lambda b: (b, 0)),
        out_shape=jax.ShapeDtypeStruct((t, d), F32),
        compiler_params=_cparams(("parallel",)),
        name="fnet_ctx",
    )(x, g_all, mods, mods, mods, tab, ccos, csin, w_out)


def _rope_tables(n_seq):
    rows = n_seq // GRID_W
    row = jnp.repeat(jnp.arange(rows, dtype=F32), GRID_W)
    col = jnp.tile(jnp.arange(GRID_W, dtype=F32), rows)
    inv_freq = ROPE_THETA ** (-jnp.arange(ROT_FREQS, dtype=F32) / ROT_FREQS)
    ang = jnp.stack([row[:, None] * inv_freq, col[:, None] * inv_freq], axis=1)
    cos, sin = jnp.cos(ang), jnp.sin(ang)
    zero = jnp.zeros_like(sin)
    cos_h = jnp.stack([cos, cos], axis=2).reshape(n_seq, HEAD_DIM)
    sin_lo = jnp.stack([-sin, zero], axis=2).reshape(n_seq, HEAD_DIM)
    sin_hi = jnp.stack([zero, sin], axis=2).reshape(n_seq, HEAD_DIM)
    rep = LANES // HEAD_DIM
    return jnp.tile(cos_h, (1, rep)), jnp.tile(sin_lo, (1, rep)), jnp.tile(sin_hi, (1, rep))


def _qkv_kernel(x_ref, g_ref, sh_ref, sc_ref, w_ref, *rest, rope, q_dim, kv_dim, q_scale):
    if rope:
        cos_ref, slo_ref, shi_ref, q_ref, k_ref, kt_ref, v_ref, vt_ref = rest
    else:
        q_ref, k_ref, kt_ref, v_ref, vt_ref = rest
    h = _norm_mod(x_ref[...], g_ref[...], sh_ref[...], sc_ref[...]).astype(BF16)
    qkv = jnp.dot(h, w_ref[...], preferred_element_type=F32)

    def rot(xs):
        if not rope:
            return xs
        return (xs * cos_ref[...] + pltpu.roll(xs, LANES - ROT_FREQS, axis=1) * slo_ref[...]
                + pltpu.roll(xs, ROT_FREQS, axis=1) * shi_ref[...])

    for j in range(q_dim // LANES):
        qj = rot(qkv[:, j * LANES:(j + 1) * LANES]) * q_scale
        if rope:
            q_ref[j * LANES:(j + 1) * LANES, :] = qj.T.astype(BF16)
        else:
            q_ref[:, j * LANES:(j + 1) * LANES] = qj.astype(BF16)
    k = jnp.concatenate([rot(qkv[:, q_dim + j * LANES:q_dim + (j + 1) * LANES]) for j in range(kv_dim // LANES)],
                        axis=-1)
    v = qkv[:, q_dim + kv_dim:]
    k_ref[...] = k.astype(BF16)
    kt_ref[...] = k.T.astype(BF16)
    v_ref[...] = v.astype(BF16)
    vt_ref[...] = v.T.astype(BF16)


def _qkv(x, mods, g_all, w_qkv, layer, n_batch, ctx, rope_tabs=None, tm=QKV_TM):
    t, d = x.shape
    n = t // n_batch
    tm = min(tm, n)
    tpb = n // tm
    kv_dim = N_KV_HEADS * HEAD_DIM
    q_dim = w_qkv.shape[2] - 2 * kv_dim
    mi = lambda j: _mod_index(layer, j, tpb, ctx)
    in_specs = [pl.BlockSpec((tm, d), lambda i: (i, 0)),
                pl.BlockSpec((None, 1, d), lambda i: (layer, 0, 0)),
                _mod_spec(d, mi(0)), _mod_spec(d, mi(1)),
                pl.BlockSpec((None,) + w_qkv.shape[1:], lambda i: (layer // 2, 0, 0))]
    args = [x, g_all, mods, mods, w_qkv]
    rope = rope_tabs is not None
    if rope:
        in_specs += [pl.BlockSpec((tm, LANES), lambda i: (i % tpb, 0))] * 3
        args += list(rope_tabs)
    return pl.pallas_call(
        functools.partial(_qkv_kernel, rope=rope, q_dim=q_dim, kv_dim=kv_dim, q_scale=HEAD_DIM ** -0.5),
        grid=(t // tm,),
        in_specs=in_specs,
        out_specs=[(pl.BlockSpec((None, q_dim, tm), lambda i: (i // tpb, 0, i % tpb)) if rope
                    else pl.BlockSpec((tm, q_dim), lambda i: (i, 0))),
                   pl.BlockSpec((tm, kv_dim), lambda i: (i, 0)),
                   pl.BlockSpec((None, kv_dim, tm), lambda i: (i // tpb, 0, i % tpb)),
                   pl.BlockSpec((tm, kv_dim), lambda i: (i, 0)),
                   pl.BlockSpec((None, kv_dim, tm), lambda i: (i // tpb, 0, i % tpb))],
        out_shape=[jax.ShapeDtypeStruct((n_batch, q_dim, n) if rope else (t, q_dim), BF16),
                   jax.ShapeDtypeStruct((t, kv_dim), BF16),
                   jax.ShapeDtypeStruct((n_batch, kv_dim, n), BF16),
                   jax.ShapeDtypeStruct((t, kv_dim), BF16),
                   jax.ShapeDtypeStruct((n_batch, kv_dim, n), BF16)],
        compiler_params=_cparams(("parallel",)),
        name="qkv_ctx" if ctx else "qkv_lat",
    )(*args)


ATTN_RB = 64


def _attend_ctx(q_ref, kt, v, sink_ref, o_ref):
    nq = q_ref.shape[0]
    n_heads = q_ref.shape[1] // HEAD_DIM
    group = n_heads // N_KV_HEADS
    outs = [None] * n_heads
    for g in range(N_KV_HEADS):
        heads = range(g * group, (g + 1) * group)
        qg = jnp.concatenate([q_ref[:, hd * HEAD_DIM:(hd + 1) * HEAD_DIM] for hd in heads], axis=0)
        s = jnp.dot(qg, kt[g * HEAD_DIM:(g + 1) * HEAD_DIM, :], preferred_element_type=F32)
        p_rows, den_rows = [], []
        for r0 in range(0, s.shape[0], ATTN_RB):
            rows = slice(r0, r0 + ATTN_RB)
            parts = [s[rows, k * LANES:(k + 1) * LANES] for k in range(s.shape[1] // LANES)]
            sink = sink_ref[heads[r0 // nq]]
            m = jnp.maximum(jnp.max(functools.reduce(jnp.maximum, parts), axis=-1, keepdims=True), sink)
            ps = [jnp.exp(part - m) for part in parts]
            den_rows.append(jnp.sum(functools.reduce(jnp.add, ps), axis=-1, keepdims=True) + jnp.exp(sink - m))
            p_rows.append(jnp.concatenate(ps, axis=-1).astype(BF16))
        pv = jnp.dot(jnp.concatenate(p_rows, axis=0), v, preferred_element_type=F32)
        on = pv[:, g * HEAD_DIM:(g + 1) * HEAD_DIM] / jnp.concatenate(den_rows, axis=0)
        for k, hd in enumerate(heads):
            outs[hd] = on[k * nq:(k + 1) * nq, :]
    o_ref[...] = jnp.concatenate(outs, axis=-1).astype(BF16)


def _attn_lat_kernel(sink_ref, qt_ref, kp_ref, kc_ref, kn_ref, vtp_ref, vtc_ref, vtn_ref, kx_ref, vtx_ref,
                     blo_ref, bhi_ref, ot_ref, k_scr, vt_scr):
    w = kc_ref.shape[0]
    n_ctx = kx_ref.shape[0]
    nq = qt_ref.shape[1]
    n_heads = qt_ref.shape[0] // HEAD_DIM
    group = n_heads // N_KV_HEADS
    for c, (kr, vr) in enumerate(((kp_ref, vtp_ref), (kc_ref, vtc_ref), (kn_ref, vtn_ref))):
        k_scr[c * w:(c + 1) * w, :] = kr[...]
        vt_scr[:, c * w:(c + 1) * w] = vr[...]
    k_scr[3 * w:3 * w + n_ctx, :] = kx_ref[...]
    vt_scr[:, 3 * w:3 * w + n_ctx] = vtx_ref[...]
    n_chunks = k_scr.shape[0] // w
    for g in range(N_KV_HEADS):
        heads = range(g * group, (g + 1) * group)
        qg = jnp.concatenate([qt_ref[hd * HEAD_DIM:(hd + 1) * HEAD_DIM, :] for hd in heads], axis=1)
        st = jnp.dot(k_scr[:, g * HEAD_DIM:(g + 1) * HEAD_DIM], qg, preferred_element_type=F32)
        p_cols, den_cols = [], []
        for k, hd in enumerate(heads):
            cols = slice(k * nq, (k + 1) * nq)
            parts = [st[c * w:(c + 1) * w, cols] for c in range(n_chunks)]
            parts[0] = parts[0] + blo_ref[...]
            parts[2] = parts[2] + bhi_ref[...]
            sink = sink_ref[hd]
            m = jnp.maximum(jnp.max(functools.reduce(jnp.maximum, parts), axis=0, keepdims=True), sink)
            ps = [jnp.exp(part - m) for part in parts]
            den_cols.append(jnp.sum(functools.reduce(jnp.add, ps), axis=0, keepdims=True) + jnp.exp(sink - m))
            p_cols.append(jnp.concatenate(ps, axis=0).astype(BF16))
        ot = jnp.dot(vt_scr[g * HEAD_DIM:(g + 1) * HEAD_DIM, :], jnp.concatenate(p_cols, axis=1),
                     preferred_element_type=F32)
        ot = ot / jnp.concatenate(den_cols, axis=1)
        for k, hd in enumerate(heads):
            ot_ref[hd * HEAD_DIM:(hd + 1) * HEAD_DIM, :] = ot[:, k * nq:(k + 1) * nq].astype(BF16)


def _attn_lat(qt, k, vt, kx, vtx, sinks, n_batch):
    _, qd, s = qt.shape
    n_ctx = kx.shape[0] // n_batch
    w = WINDOW
    nb = s // w
    kvd = k.shape[1]
    j_all = 3 * w + n_ctx
    ki = np.arange(w)[:, None]
    qi = np.arange(w)[None, :]
    lo = np.where(ki >= qi, 0.0, NEG).astype(np.float32)
    hi = np.where(ki <= qi, 0.0, NEG).astype(np.float32)
    off = np.full((w, w), NEG, np.float32)
    blo = jnp.asarray(np.stack([lo, off]))
    bhi = jnp.asarray(np.stack([hi, off]))
    prev = lambda b, i: (b * nb + jnp.maximum(i - 1, 0), 0)
    nxt = lambda b, i: (b * nb + jnp.minimum(i + 1, nb - 1), 0)
    tprev = lambda b, i: (b, 0, jnp.maximum(i - 1, 0))
    tnxt = lambda b, i: (b, 0, jnp.minimum(i + 1, nb - 1))
    return pl.pallas_call(
        _attn_lat_kernel,
        grid=(n_batch, nb),
        in_specs=[pl.BlockSpec(memory_space=pltpu.SMEM),
                  pl.BlockSpec((None, qd, w), lambda b, i: (b, 0, i)),
                  pl.BlockSpec((w, kvd), prev),
                  pl.BlockSpec((w, kvd), lambda b, i: (b * nb + i, 0)),
                  pl.BlockSpec((w, kvd), nxt),
                  pl.BlockSpec((None, kvd, w), tprev),
                  pl.BlockSpec((None, kvd, w), lambda b, i: (b, 0, i)),
                  pl.BlockSpec((None, kvd, w), tnxt),
                  pl.BlockSpec((n_ctx, kvd), lambda b, i: (b, 0)),
                  pl.BlockSpec((None, kvd, n_ctx), lambda b, i: (b, 0, 0)),
                  pl.BlockSpec((None, w, w), lambda b, i: (jnp.where(i == 0, 1, 0), 0, 0)),
                  pl.BlockSpec((None, w, w), lambda b, i: (jnp.where(i == nb - 1, 1, 0), 0, 0))],
        out_specs=pl.BlockSpec((None, qd, w), lambda b, i: (b, 0, i)),
        out_shape=jax.ShapeDtypeStruct((n_batch, qd, s), BF16),
        scratch_shapes=[pltpu.VMEM((j_all, kvd), BF16), pltpu.VMEM((kvd, j_all), BF16)],
        compiler_params=_cparams(("parallel", "parallel")),
        name="attn_lat",
    )(sinks, qt, k, k, k, vt, vt, vt, kx, vtx, blo, bhi)


def _attn_ctx_kernel(sink_ref, q_ref, kt_ref, v_ref, o_ref):
    _attend_ctx(q_ref, kt_ref[...], v_ref[...], sink_ref, o_ref)


def _attn_ctx(q, kt, v, sinks, n_batch):
    t, qd = q.shape
    n = t // n_batch
    kvd = v.shape[1]
    return pl.pallas_call(
        _attn_ctx_kernel,
        grid=(n_batch,),
        in_specs=[pl.BlockSpec(memory_space=pltpu.SMEM),
                  pl.BlockSpec((n, qd), lambda b: (b, 0)),
                  pl.BlockSpec((None, kvd, n), lambda b: (b, 0, 0)),
                  pl.BlockSpec((n, kvd), lambda b: (b, 0))],
        out_specs=pl.BlockSpec((n, qd), lambda b: (b, 0)),
        out_shape=jax.ShapeDtypeStruct((t, qd), BF16),
        compiler_params=_cparams(("parallel",)),
        name="attn_ctx",
    )(sinks, q, kt, v)


ROUTE_I1, ROUTE_I2, ROUTE_R1, ROUTE_R2, ROUTE_G1, ROUTE_G2 = range(6)
ROUTE_ROWS = 8
TAB_START, TAB_COUNT, TAB_USED, TAB_SIZE = 0, N_EXPERTS, 2 * N_EXPERTS, 2 * N_EXPERTS + 8
ROW_UNROLL = 8


def _router_kernel(x_ref, g_ref, sh_ref, sc_ref, wr_ref, tri_ref, route_ref, idx_ref, cnt_ref, carry_scr):
    @pl.when(pl.program_id(0) == 0)
    def _():
        carry_scr[...] = jnp.zeros_like(carry_scr)

    h = _norm_mod(x_ref[...], g_ref[...], sh_ref[...], sc_ref[...])
    hi = h.astype(BF16)
    lo = (h - hi.astype(F32)).astype(BF16)
    a = jnp.dot(hi, wr_ref[...], preferred_element_type=F32)
    logits = (a[:, :LANES] + a[:, LANES:]) + jnp.dot(lo, wr_ref[:, :LANES], preferred_element_type=F32)
    lane = lax.broadcasted_iota(jnp.int32, logits.shape, 1)
    lane_f = lane.astype(F32)
    logits = jnp.where(lane < N_EXPERTS, logits, -jnp.inf)
    m1 = jnp.max(logits, axis=-1, keepdims=True)
    i1 = jnp.min(jnp.where(logits == m1, lane_f, float(LANES)), axis=-1, keepdims=True)
    oh1 = lane_f == i1
    rest = jnp.where(oh1, -jnp.inf, logits)
    m2 = jnp.max(rest, axis=-1, keepdims=True)
    i2 = jnp.min(jnp.where(rest == m2, lane_f, float(LANES)), axis=-1, keepdims=True)
    oh2 = lane_f == i2
    e2 = jnp.exp(m2 - m1)
    g1 = 1.0 / (1.0 + e2)
    g2 = e2 / (1.0 + e2)
    sel = jnp.where(oh1, 1.0, 0.0) + jnp.where(oh2, 1.0, 0.0)
    before = jnp.dot(tri_ref[...], sel.astype(BF16), preferred_element_type=F32) + carry_scr[0:1, :]
    r1 = jnp.sum(jnp.where(oh1, before, 0.0), axis=-1, keepdims=True)
    r2 = jnp.sum(jnp.where(oh2, before, 0.0), axis=-1, keepdims=True)
    total = carry_scr[0:1, :] + jnp.sum(sel, axis=0, keepdims=True)
    carry_scr[...] = jnp.broadcast_to(total, carry_scr.shape)
    cnt_ref[...] = jnp.broadcast_to(total, cnt_ref.shape)
    rec = jnp.zeros_like(logits)
    for ln, val in ((ROUTE_I1, i1), (ROUTE_I2, i2), (ROUTE_R1, r1), (ROUTE_R2, r2), (ROUTE_G1, g1), (ROUTE_G2, g2)):
        rec = jnp.where(lane == ln, val, rec)
    route_ref[...] = rec
    idx_ref[...] = rec.T[:ROUTE_ROWS, :].astype(jnp.int32)


def _router(x, mods, g_all, w_router, layer, tiles_per_batch, ctx, tm):
    t, d = x.shape
    mi = lambda j: _mod_index(layer, j, tiles_per_batch // tm if not ctx else 1, ctx)
    wr = jnp.zeros((d, LANES), F32).at[:, :N_EXPERTS].set(w_router)
    whi = wr.astype(BF16)
    wr = jnp.concatenate([whi, (wr - whi.astype(F32)).astype(BF16)], axis=1)
    tri = jnp.asarray(np.tril(np.ones((tm, tm), np.float32), -1), BF16)
    return pl.pallas_call(
        _router_kernel,
        grid=(t // tm,),
        in_specs=[pl.BlockSpec((tm, d), lambda i: (i, 0)),
                  pl.BlockSpec((None, 1, d), lambda i: (layer, 0, 0)),
                  _mod_spec(d, mi(3)), _mod_spec(d, mi(4)),
                  pl.BlockSpec((d, 2 * LANES), lambda i: (0, 0)),
                  pl.BlockSpec((tm, tm), lambda i: (0, 0))],
        out_specs=[pl.BlockSpec((tm, LANES), lambda i: (i, 0)),
                   pl.BlockSpec((None, ROUTE_ROWS, tm), lambda i: (i, 0, 0)),
                   pl.BlockSpec((8, LANES), lambda i: (0, 0))],
        out_shape=[jax.ShapeDtypeStruct((t, LANES), F32),
                   jax.ShapeDtypeStruct((t // tm, ROUTE_ROWS, tm), jnp.int32),
                   jax.ShapeDtypeStruct((8, LANES), F32)],
        scratch_shapes=[pltpu.VMEM((8, LANES), F32)],
        compiler_params=_cparams(("arbitrary",)),
        name="router",
    )(x, g_all, mods, mods, wr, tri)


SUB = 8


def _to_row_tiles(ref, val):
    n = val.shape[0]
    for k in range(SUB):
        ref[pl.ds(k, n, stride=SUB), :] = val[:, k * LANES:(k + 1) * LANES]


def _from_row_tiles(ref):
    n = ref.shape[0] // SUB
    return jnp.concatenate([ref[pl.ds(k, n, stride=SUB), :] for k in range(SUB)], axis=-1)


def _row_copy(src, dst, src_row, dst_row, sem):
    return pltpu.make_async_copy(src.at[pl.ds(pl.multiple_of(src_row * SUB, SUB), SUB)],
                                 dst.at[pl.ds(pl.multiple_of(dst_row * SUB, SUB), SUB)], sem)


def _for_rows(tm, body):
    def blk(i, _):
        base = pl.multiple_of(i * ROW_UNROLL, ROW_UNROLL)
        for j in range(ROW_UNROLL):
            body(base + j, j)
        return 0

    lax.fori_loop(0, tm // ROW_UNROLL, blk, 0)


ZERO_SEM = 4


def _dispatch_kernel(tab_ref, slot_ref, x_ref, g_ref, sh_ref, sc_ref, xs_ref, h_scr, zero_scr, sems, *, group):
    tm = x_ref.shape[0]
    n_groups = xs_ref.shape[0] // (group * SUB)
    step = pl.program_id(0)
    n_steps = pl.num_programs(0)
    cur = lax.rem(step, 2)

    def wait_buffer(b):
        for k in range(2):
            pltpu.make_async_copy(h_scr.at[b], xs_ref.at[pl.ds(0, tm * SUB)], sems.at[2 * b + k]).wait()

    @pl.when(step >= 2)
    def _():
        wait_buffer(cur)

    _to_row_tiles(h_scr.at[cur], _norm_mod(x_ref[...], g_ref[...], sh_ref[...], sc_ref[...]))

    def issue(r, j):
        for k in range(2):
            _row_copy(h_scr.at[cur], xs_ref, r, slot_ref[0, k, r], sems.at[2 * cur + k]).start(priority=j % 2)

    _for_rows(tm, issue)

    @pl.when(jnp.logical_and(step == n_steps - 1, step >= 1))
    def _():
        wait_buffer(1 - cur)

    @pl.when(step == n_steps - 1)
    def _():
        wait_buffer(cur)
        zero_scr[...] = jnp.zeros_like(zero_scr)
        for e in range(N_EXPERTS):
            n = tab_ref[TAB_COUNT + e]
            n_pad = lax.rem(group - lax.rem(n, group), group)
            first = tab_ref[TAB_START + e] + n

            def fill(k, _, first=first):
                _row_copy(zero_scr, xs_ref, 0, first + k, sems.at[ZERO_SEM]).start()
                return 0

            def fill_done(k, _):
                _row_copy(zero_scr, xs_ref, 0, 0, sems.at[ZERO_SEM]).wait()
                return 0

            lax.fori_loop(0, n_pad, fill, 0)
            lax.fori_loop(0, n_pad, fill_done, 0)

        def clear(j, _):
            row = pl.multiple_of(j * (group * SUB), group * SUB)
            cp = pltpu.make_async_copy(zero_scr, xs_ref.at[pl.ds(row, group * SUB)], sems.at[ZERO_SEM])
            cp.start()
            cp.wait()
            return 0

        lax.fori_loop(tab_ref[TAB_USED], n_groups, clear, 0)


def _dispatch(x, tab, slots, mods, g_all, layer, tiles_per_batch, ctx, n_groups, group, tm):
    t, d = x.shape
    mi = lambda j: _mod_index(layer, j, tiles_per_batch // tm if not ctx else 1, ctx)
    return pl.pallas_call(
        functools.partial(_dispatch_kernel, group=group),
        grid=(t // tm,),
        in_specs=[pl.BlockSpec(memory_space=pltpu.SMEM),
                  pl.BlockSpec((1, 2, tm), lambda i: (i, 0, 0), memory_space=pltpu.SMEM),
                  pl.BlockSpec((tm, d), lambda i: (i, 0)),
                  pl.BlockSpec((None, 1, d), lambda i: (layer, 0, 0)),
                  _mod_spec(d, mi(3)), _mod_spec(d, mi(4))],
        out_specs=pl.BlockSpec(memory_space=pl.ANY),
        out_shape=jax.ShapeDtypeStruct((n_groups * group * SUB, LANES), F32),
        scratch_shapes=[pltpu.VMEM((2, tm * SUB, LANES), F32), pltpu.VMEM((group * SUB, LANES), F32),
                        pltpu.SemaphoreType.DMA((ZERO_SEM + 1,))],
        compiler_params=_cparams(("arbitrary",)),
        name="moe_dispatch",
    )(tab, slots, x, g_all, mods, mods)


def _moe_ffn_kernel(be_ref, bv_ref, xs_ref, wg_ref, wu_ref, wd_ref, ys_ref, h_scr, acc_scr):
    del be_ref
    i = pl.program_id(0)
    f = pl.program_id(1)
    last = f == pl.num_programs(1) - 1
    valid = bv_ref[i] > 0

    @pl.when(jnp.logical_and(valid, f == 0))
    def _():
        h_scr[...] = _from_row_tiles(xs_ref).astype(BF16)
        acc_scr[...] = jnp.zeros_like(acc_scr)

    @pl.when(valid)
    def _():
        h = h_scr[...]
        a = jnp.dot(h, wg_ref[...], preferred_element_type=F32)
        u = jnp.dot(h, wu_ref[...], preferred_element_type=F32)
        t = (a * _sigmoid(a)) * u
        acc_scr[...] += jnp.dot(t.astype(BF16), wd_ref[...], preferred_element_type=F32)

    @pl.when(jnp.logical_and(valid, last))
    def _():
        _to_row_tiles(ys_ref, acc_scr[...])

    @pl.when(jnp.logical_and(jnp.logical_not(valid), last))
    def _():
        ys_ref[...] = jnp.zeros_like(ys_ref)


def _moe_ffn(xs, block_expert, block_valid, wg, wu, wd, layer, tm, tf=MOE_TF):
    d, ff = wg.shape[2:]
    assert d == SUB * LANES and xs.shape[1] == LANES
    cap = xs.shape[0] // SUB
    tf = min(tf, ff)
    nf = ff // tf
    sj = layer // 2
    fsel = lambda i, f, bv: jnp.where(bv[i] > 0, f, nf - 1)
    grid_spec = pltpu.PrefetchScalarGridSpec(
        num_scalar_prefetch=2,
        grid=(cap // tm, nf),
        in_specs=[pl.BlockSpec((tm * SUB, LANES), lambda i, f, be, bv: (i, 0)),
                  pl.BlockSpec((None, None, d, tf), lambda i, f, be, bv: (sj, be[i], 0, fsel(i, f, bv))),
                  pl.BlockSpec((None, None, d, tf), lambda i, f, be, bv: (sj, be[i], 0, fsel(i, f, bv))),
                  pl.BlockSpec((None, None, tf, d), lambda i, f, be, bv: (sj, be[i], fsel(i, f, bv), 0))],
        out_specs=pl.BlockSpec((tm * SUB, LANES), lambda i, f, be, bv: (i, 0)),
        scratch_shapes=[pltpu.VMEM((tm, d), BF16), pltpu.VMEM((tm, d), F32)],
    )
    return pl.pallas_call(
        _moe_ffn_kernel,
        grid_spec=grid_spec,
        out_shape=jax.ShapeDtypeStruct(xs.shape, F32),
        compiler_params=_cparams(("parallel", "arbitrary")),
        name="moe_ffn",
    )(block_expert, block_valid, xs, wg, wu, wd)


def _combine_kernel(slot_ref, next_slot_ref, route_ref, x_ref, gt_ref, fg_ref, ys_ref, o_ref, buf, sems, *, final_norm):
    tm = x_ref.shape[0]
    step = pl.program_id(0)
    cur = lax.rem(step, 2)

    def gather(idx_ref, b):
        def issue(r, j):
            for k in range(2):
                _row_copy(ys_ref, buf.at[b, k], idx_ref[0, k, r], r, sems.at[2 * b + k]).start(priority=j % 2)

        _for_rows(tm, issue)

    @pl.when(step == 0)
    def _():
        gather(slot_ref, cur)

    @pl.when(step + 1 < pl.num_programs(0))
    def _():
        gather(next_slot_ref, 1 - cur)

    for k in range(2):
        pltpu.make_async_copy(ys_ref.at[pl.ds(0, tm * SUB)], buf.at[cur, k], sems.at[2 * cur + k]).wait()
    rec = route_ref[...]
    g1 = rec[:, ROUTE_G1:ROUTE_G1 + 1]
    g2 = rec[:, ROUTE_G2:ROUTE_G2 + 1]
    y = _from_row_tiles(buf.at[cur, 0]) * g1 + _from_row_tiles(buf.at[cur, 1]) * g2
    out = x_ref[...] + gt_ref[...] * y
    if final_norm:
        out = out * lax.rsqrt(jnp.mean(out * out, axis=-1, keepdims=True) + EPS) * fg_ref[...]
    o_ref[...] = out


def _combine(ys, slots, route, x, mods, final_g, layer, tiles_per_batch, ctx, final_norm, tm):
    t, d = x.shape
    mi = _mod_index(layer, 5, tiles_per_batch // tm if not ctx else 1, ctx)
    n_tiles = t // tm
    return pl.pallas_call(
        functools.partial(_combine_kernel, final_norm=final_norm),
        grid=(n_tiles,),
        in_specs=[pl.BlockSpec((1, 2, tm), lambda i: (i, 0, 0), memory_space=pltpu.SMEM),
                  pl.BlockSpec((1, 2, tm), lambda i: (jnp.minimum(i + 1, n_tiles - 1), 0, 0), memory_space=pltpu.SMEM),
                  pl.BlockSpec((tm, LANES), lambda i: (i, 0)),
                  pl.BlockSpec((tm, d), lambda i: (i, 0)),
                  _mod_spec(d, mi),
                  pl.BlockSpec((1, d), lambda i: (0, 0)),
                  pl.BlockSpec(memory_space=pl.ANY)],
        out_specs=pl.BlockSpec((tm, d), lambda i: (i, 0)),
        out_shape=jax.ShapeDtypeStruct((t, d), F32),
        scratch_shapes=[pltpu.VMEM((2, 2, tm * SUB, LANES), F32), pltpu.SemaphoreType.DMA((4,))],
        compiler_params=_cparams(("arbitrary",)),
        name="moe_combine",
    )(slots, slots, route, x, mods, final_g.reshape(1, d), ys)


def _moe(x, mods, g_all, w_router, wg, wu, wd, final_g, layer, tiles_per_batch, ctx, final_norm):
    t, d = x.shape
    group = MOE_TM if 2 * t >= 4 * N_EXPERTS * MOE_TM else MOE_TM // 2
    row_tm = min(ROW_TM, t)
    route, idx, counts = _router(x, mods, g_all, w_router, layer, tiles_per_batch, ctx, row_tm)
    counts = counts[0, :N_EXPERTS].astype(jnp.int32)
    groups = (counts + group - 1) // group
    ends = jnp.cumsum(groups)
    starts = (ends - groups) * group
    n_groups = (2 * t + group - 1) // group + N_EXPERTS
    tab = jnp.zeros((TAB_SIZE,), jnp.int32)
    tab = tab.at[TAB_START:TAB_START + N_EXPERTS].set(starts).at[TAB_COUNT:TAB_COUNT + N_EXPERTS].set(counts)
    tab = tab.at[TAB_USED].set(ends[-1])
    gi = jnp.arange(n_groups, dtype=jnp.int32)
    block_expert = jnp.minimum(jnp.sum(gi[:, None] >= ends[None, :], axis=1), N_EXPERTS - 1).astype(jnp.int32)
    block_valid = (gi < ends[-1]).astype(jnp.int32)
    def slot_rows(e, r):
        base = functools.reduce(jnp.add, [jnp.where(e == k, starts[k], 0) for k in range(N_EXPERTS)])
        return base + r
    slots = jnp.stack([slot_rows(idx[:, ROUTE_I1, :], idx[:, ROUTE_R1, :]),
                       slot_rows(idx[:, ROUTE_I2, :], idx[:, ROUTE_R2, :])], axis=1)
    xs = _dispatch(x, tab, slots, mods, g_all, layer, tiles_per_batch, ctx, n_groups, group, row_tm)
    ys = _moe_ffn(xs, block_expert, block_valid, wg, wu, wd, layer, group)
    return _combine(ys, slots, route, x, mods, final_g, layer, tiles_per_batch, ctx, final_norm, row_tm)


def kernel(x, c, ctx, c_ctx, ada_w, ada_b, norm_mix_g, norm_ffn_g, fnet_w_out, attn_w_qkv, attn_w_o, attn_sinks,
           ffn_w_gate, ffn_w_up, ffn_w_down, moe_w_router, moe_w_gate, moe_w_up, moe_w_down, final_norm_g):
    n_batch, s, d = x.shape
    n_ctx = ctx.shape[1]
    depth = ada_w.shape[0]
    mods = _ada(c, c_ctx, ada_w, ada_b)
    g_mix = norm_mix_g.reshape(depth, 1, d)
    g_ffn = norm_ffn_g.reshape(depth, 1, d)
    rope_tabs = _rope_tables(s)
    x_lat = x.reshape(n_batch * s, d)
    x_ctx = ctx.reshape(n_batch * n_ctx, d)
    bf = lambda a: a.astype(BF16)
    w_out, w_qkv, w_o = bf(fnet_w_out), bf(attn_w_qkv), bf(attn_w_o)
    wg, wu, wd = bf(ffn_w_gate), bf(ffn_w_up), bf(ffn_w_down)
    eg, eu, ed = bf(moe_w_gate), bf(moe_w_up), bf(moe_w_down)
    for i in range(depth):
        j = i // 2
        last = i == depth - 1
        if i % 2 == 0:
            x_lat = _fnet_lat(x_lat, mods, g_mix, w_out, i, n_batch)
            if not last:
                x_ctx = _fnet_ctx(x_ctx, mods, g_mix, w_out, i, n_batch)
            x_lat = _ffn_dense(x_lat, mods, g_ffn, wg, wu, wd, i, s, False)
            if not last:
                x_ctx = _ffn_dense(x_ctx, mods, g_ffn, wg, wu, wd, i, n_ctx, True)
        else:
            qt, k, _, _, vt = _qkv(x_lat, mods, g_mix, w_qkv, i, n_batch, False, rope_tabs)
            qx, kx, ktx, vx, vtx = _qkv(x_ctx, mods, g_mix, w_qkv, i, n_batch, True)
            ot = _attn_lat(qt, k, vt, kx, vtx, attn_sinks[j], n_batch)
            x_lat = _proj_res(ot, w_o, x_lat, mods, i, 2, s, False)
            if not last:
                ox = _attn_ctx(qx, ktx, vx, attn_sinks[j], n_batch)
                x_ctx = _proj_res(ox, w_o, x_ctx, mods, i, 2, n_ctx, True)
            x_lat = _moe(x_lat, mods, g_ffn, moe_w_router[j], eg, eu, ed, final_norm_g, i, s, False, last)
            if not last:
                x_ctx = _moe(x_ctx, mods, g_ffn, moe_w_router[j], eg, eu, ed, final_norm_g, i, n_ctx, True, False)
    return x_lat.reshape(n_batch, s, d)
```

```python
import functools

import numpy as np
import jax
import jax.numpy as jnp
from jax import lax
from jax.experimental import pallas as pl
from jax.experimental.pallas import tpu as pltpu

F32 = jnp.float32
BF16 = jnp.bfloat16

GRID_W = 64
N_MOD = 6
EPS = 1e-6
FNET_GROUPS = 4
HEAD_DIM = 64
N_KV_HEADS = 4
WINDOW = 128
ROPE_THETA = 10000.0
ROT_FREQS = HEAD_DIM // 4
N_EXPERTS = 8
MOD_ROWS = 8
CTX_ROW = MOD_ROWS - 1
LANES = 128
NEG = -1e30
VMEM_LIMIT = 56 * 1024 * 1024

FFN_TM = 512
FFN_TF = 1792
MOE_TM = 512
MOE_TF = 1792
ROW_TM = 512
PROJ_TM = 1024
QKV_TM = 1024
DFT_P = 64
DFT_R = 128
FNET_RB = 16
FNET_KB = 8


def _cparams(sem):
    return pltpu.CompilerParams(dimension_semantics=sem, vmem_limit_bytes=VMEM_LIMIT)


def _sigmoid(a):
    return 1.0 / (1.0 + jnp.exp(-a))


def _norm_mod(x, g, shift, scale):
    xn = x * lax.rsqrt(jnp.mean(x * x, axis=-1, keepdims=True) + EPS)
    return (xn * g) * (1.0 + scale) + shift


def _mod_spec(d, idx_fn):
    return pl.BlockSpec((None, 1, d), lambda *ids: (idx_fn(*ids), 0, 0))


def _mod_index(layer, j, tiles_per_batch, ctx):
    def fn(i, *_):
        b = CTX_ROW if ctx else i // tiles_per_batch
        return (layer * MOD_ROWS + b) * N_MOD + j
    return fn


def _ada_kernel(c_ref, w_ref, b_ref, o_ref):
    cc = c_ref[...]
    s = cc * _sigmoid(cc)
    o_ref[...] = jnp.dot(s, w_ref[...], preferred_element_type=F32,
                         precision=lax.Precision.HIGHEST) + b_ref[...]


def _ada(c, c_ctx, ada_w, ada_b):
    depth, d, _ = ada_w.shape
    nb = c.shape[0]
    assert nb < MOD_ROWS
    cc = jnp.concatenate([c, jnp.zeros((CTX_ROW - nb, d), F32), c_ctx[None]], axis=0)
    out = pl.pallas_call(
        _ada_kernel,
        grid=(depth, N_MOD),
        in_specs=[pl.BlockSpec((MOD_ROWS, d), lambda l, j: (0, 0)),
                  pl.BlockSpec((None, d, d), lambda l, j: (l, 0, j)),
                  pl.BlockSpec((None, 1, d), lambda l, j: (l * N_MOD + j, 0, 0))],
        out_specs=pl.BlockSpec((None, MOD_ROWS, d), lambda l, j: (l, 0, j)),
        out_shape=jax.ShapeDtypeStruct((depth, MOD_ROWS, N_MOD * d), F32),
        compiler_params=_cparams(("parallel", "parallel")),
        name="ada",
    )(cc, ada_w, ada_b.reshape(depth * N_MOD, 1, d))
    return out.reshape(depth * MOD_ROWS * N_MOD, 1, d)


def _ffn_kernel(x_ref, g_ref, sh_ref, sc_ref, gt_ref, wg_ref, wu_ref, wd_ref, o_ref, h_scr, acc_scr):
    f = pl.program_id(1)

    @pl.when(f == 0)
    def _():
        h_scr[...] = _norm_mod(x_ref[...], g_ref[...], sh_ref[...], sc_ref[...]).astype(BF16)
        acc_scr[...] = jnp.zeros_like(acc_scr)

    h = h_scr[...]
    a = jnp.dot(h, wg_ref[...], preferred_element_type=F32)
    u = jnp.dot(h, wu_ref[...], preferred_element_type=F32)
    t = (a * _sigmoid(a)) * u
    acc_scr[...] += jnp.dot(t.astype(BF16), wd_ref[...], preferred_element_type=F32)

    @pl.when(f == pl.num_programs(1) - 1)
    def _():
        o_ref[...] = x_ref[...] + gt_ref[...] * acc_scr[...]


def _ffn_dense(x, mods, g_all, wg, wu, wd, layer, tiles_per_batch, ctx, tm=FFN_TM, tf=FFN_TF):
    t, d = x.shape
    ff = wg.shape[2]
    tm = min(tm, t)
    tf = min(tf, ff)
    sj = layer // 2
    mi = functools.partial(_mod_index, layer, tiles_per_batch=tiles_per_batch // tm if not ctx else 1, ctx=ctx)
    return pl.pallas_call(
        _ffn_kernel,
        grid=(t // tm, ff // tf),
        in_specs=[pl.BlockSpec((tm, d), lambda i, f: (i, 0)),
                  pl.BlockSpec((None, 1, d), lambda i, f: (layer, 0, 0)),
                  _mod_spec(d, mi(3)), _mod_spec(d, mi(4)), _mod_spec(d, mi(5)),
                  pl.BlockSpec((None, d, tf), lambda i, f: (sj, 0, f)),
                  pl.BlockSpec((None, d, tf), lambda i, f: (sj, 0, f)),
                  pl.BlockSpec((None, tf, d), lambda i, f: (sj, f, 0))],
        out_specs=pl.BlockSpec((tm, d), lambda i, f: (i, 0)),
        out_shape=jax.ShapeDtypeStruct((t, d), F32),
        scratch_shapes=[pltpu.VMEM((tm, d), BF16), pltpu.VMEM((tm, d), F32)],
        compiler_params=_cparams(("parallel", "arbitrary")),
        name="ffn_dense",
    )(x, g_all, mods, mods, mods, wg, wu, wd)


def _proj_res_kernel(a_ref, w_ref, x_ref, gt_ref, o_ref, *, transposed):
    dims = (((0,), (0,)), ((), ())) if transposed else (((1,), (0,)), ((), ()))
    y = lax.dot_general(a_ref[...], w_ref[...], dims, preferred_element_type=F32)
    o_ref[...] = x_ref[...] + gt_ref[...] * y


def _proj_res(a, w, x, mods, layer, j, tiles_per_batch, ctx, tm=PROJ_TM):
    t, d = x.shape
    transposed = a.ndim == 3
    k = a.shape[1]
    tm = min(tm, t)
    tpb = tiles_per_batch // tm
    mi = _mod_index(layer, j, tpb if not ctx else 1, ctx)
    a_spec = (pl.BlockSpec((None, k, tm), lambda i: (i // tpb, 0, i % tpb)) if transposed
              else pl.BlockSpec((tm, k), lambda i: (i, 0)))
    return pl.pallas_call(
        functools.partial(_proj_res_kernel, transposed=transposed),
        grid=(t // tm,),
        in_specs=[a_spec,
                  pl.BlockSpec((None, k, d), lambda i: (layer // 2, 0, 0)),
                  pl.BlockSpec((tm, d), lambda i: (i, 0)),
                  _mod_spec(d, mi)],
        out_specs=pl.BlockSpec((tm, d), lambda i: (i, 0)),
        out_shape=jax.ShapeDtypeStruct((t, d), F32),
        compiler_params=_cparams(("parallel",)),
        name="proj_res",
    )(a, w, x, mods)


def _dft_angles(n):
    a = np.arange(n)
    return 2.0 * np.pi * ((a[:, None] * a[None, :]) % n) / n


def _seq_dft_tables(n):
    kp = np.arange(DFT_P)[None, :, None]
    p = np.arange(DFT_P)[None, None, :]
    r = np.arange(DFT_R)[:, None, None]
    th = 2.0 * np.pi * ((kp * (DFT_R * p + r)) % n) / n
    tab1 = np.stack([np.cos(th), -np.sin(th)], axis=1)
    tab1 = tab1.reshape(DFT_R // FNET_RB, FNET_RB, 2, DFT_P, DFT_P)
    kron = np.zeros((DFT_R // FNET_RB, 2, DFT_P, FNET_RB, DFT_P, FNET_RB), np.float32)
    for rl in range(FNET_RB):
        kron[:, :, :, rl, :, rl] = tab1[:, rl]
    tab1 = kron.reshape(DFT_R // FNET_RB, 2 * DFT_P * FNET_RB, DFT_P * FNET_RB)
    th2 = _dft_angles(DFT_R)
    c2, s2 = np.cos(th2), np.sin(th2)
    tab2 = np.block([[c2, s2], [-s2, c2]])
    return jnp.asarray(tab1, BF16), jnp.asarray(tab2, BF16)


def _chan_dft_tables(group_dim):
    th = _dft_angles(group_dim)
    return jnp.asarray(np.cos(th), BF16), jnp.asarray(np.sin(th), BF16)


def _ctx_dft_table(n_ctx):
    th = _dft_angles(n_ctx)
    return jnp.asarray(np.concatenate([np.cos(th), -np.sin(th)], axis=0), BF16)


def _cols_store(scr, val):
    for c in range(scr.shape[0]):
        scr[c] = val[:, c * LANES:(c + 1) * LANES]


def _cols_load(scr):
    return jnp.concatenate([scr[c] for c in range(scr.shape[0])], axis=-1)


def _cols_store_rows(scr, sel, val):
    for c in range(scr.shape[0]):
        scr[c, sel, :] = val[:, c * LANES:(c + 1) * LANES]


def _cols_load_rows(scr, sel):
    return jnp.concatenate([scr[c, sel, :] for c in range(scr.shape[0])], axis=-1)


def _fnet_stage1_kernel(x_ref, g_ref, sh_ref, sc_ref, tab_ref, zr_ref, zi_ref):
    d = g_ref.shape[-1]
    rows = DFT_P * FNET_RB
    h = _norm_mod(x_ref[...].reshape(rows, d), g_ref[...], sh_ref[...], sc_ref[...]).astype(BF16)
    z = jnp.dot(tab_ref[...], h, preferred_element_type=F32)
    zr_ref[...] = z[:rows].astype(BF16).reshape(DFT_P, FNET_RB, d)
    zi_ref[...] = z[rows:].astype(BF16).reshape(DFT_P, FNET_RB, d)


def _mix_tail(pr, pi, cc_ref, sc_ref, wout_ref, scale):
    gd = cc_ref.shape[0]
    ys = []
    for g in range(pr.shape[1] // gd):
        ys.append(jnp.dot(pr[:, g * gd:(g + 1) * gd], cc_ref[...], preferred_element_type=F32)
                  + jnp.dot(pi[:, g * gd:(g + 1) * gd], sc_ref[...], preferred_element_type=F32))
    mixed = (jnp.concatenate(ys, axis=-1) * scale).astype(BF16)
    return jnp.dot(mixed, wout_ref[...], preferred_element_type=F32)


def _fnet_stage2_kernel(zr_ref, zi_ref, tab2_ref, cc_ref, sc_ref, wout_ref, x_ref, gt_ref, o_ref,
                        p_scr, x_scr, o_scr, *, scale):
    d = gt_ref.shape[-1]
    rows = DFT_R * FNET_KB
    for j in range(FNET_KB):
        z = jnp.concatenate([zr_ref[j * DFT_R:(j + 1) * DFT_R, :], zi_ref[j * DFT_R:(j + 1) * DFT_R, :]], axis=0)
        p = jnp.dot(tab2_ref[...], z, preferred_element_type=F32)
        p_scr[j * DFT_R:(j + 1) * DFT_R, :d] = p[:DFT_R].astype(BF16)
        p_scr[j * DFT_R:(j + 1) * DFT_R, d:] = p[DFT_R:].astype(BF16)
    y = _mix_tail(p_scr[:, :d], p_scr[:, d:], cc_ref, sc_ref, wout_ref, scale)
    _cols_store(x_scr, x_ref[...].reshape(rows, d))
    for j in range(FNET_KB):
        sel = pl.ds(j, DFT_R, stride=FNET_KB)
        _cols_store_rows(o_scr, sel, _cols_load_rows(x_scr, sel) + gt_ref[...] * y[j * DFT_R:(j + 1) * DFT_R, :])
    o_ref[...] = _cols_load(o_scr).reshape(DFT_R, FNET_KB, d)


def _fnet_lat(x, mods, g_all, w_out, layer, n_batch):
    t, d = x.shape
    s = t // n_batch
    assert s == DFT_P * DFT_R
    tab1, tab2 = _seq_dft_tables(s)
    ccos, csin = _chan_dft_tables(d // FNET_GROUPS)
    mi = lambda j: (lambda b, *_: (layer * MOD_ROWS + b) * N_MOD + j)
    xv = x.reshape(n_batch, DFT_P, DFT_R, d)
    blk1 = (None, DFT_P, FNET_RB, d)
    rows1 = DFT_P * FNET_RB
    mi1 = lambda j: (lambda r, b: (layer * MOD_ROWS + b) * N_MOD + j)
    zr, zi = pl.pallas_call(
        _fnet_stage1_kernel,
        grid=(DFT_R // FNET_RB, n_batch),
        in_specs=[pl.BlockSpec(blk1, lambda r, b: (b, 0, r, 0)),
                  pl.BlockSpec((None, 1, d), lambda r, b: (layer, 0, 0)),
                  _mod_spec(d, mi1(0)), _mod_spec(d, mi1(1)),
                  pl.BlockSpec((None, 2 * rows1, rows1), lambda r, b: (r, 0, 0))],
        out_specs=[pl.BlockSpec(blk1, lambda r, b: (b, 0, r, 0))] * 2,
        out_shape=[jax.ShapeDtypeStruct((n_batch, DFT_P, DFT_R, d), BF16)] * 2,
        compiler_params=_cparams(("parallel", "parallel")),
        name="fnet_stage1",
    )(xv, g_all, mods, mods, tab1)
    zr = zr.reshape(n_batch, DFT_P * DFT_R, d)
    zi = zi.reshape(n_batch, DFT_P * DFT_R, d)
    xo = x.reshape(n_batch, DFT_R, DFT_P, d)
    blk2 = (None, DFT_R, FNET_KB, d)
    rows = FNET_KB * DFT_R
    scale = float(1.0 / np.sqrt(float(s) * (d // FNET_GROUPS)))
    out = pl.pallas_call(
        functools.partial(_fnet_stage2_kernel, scale=scale),
        grid=(n_batch, DFT_P // FNET_KB),
        in_specs=[pl.BlockSpec((None, rows, d), lambda b, k: (b, k, 0)),
                  pl.BlockSpec((None, rows, d), lambda b, k: (b, k, 0)),
                  pl.BlockSpec((2 * DFT_R, 2 * DFT_R), lambda b, k: (0, 0)),
                  pl.BlockSpec(ccos.shape, lambda b, k: (0, 0)),
                  pl.BlockSpec(csin.shape, lambda b, k: (0, 0)),
                  pl.BlockSpec((None, d, d), lambda b, k: (layer // 2, 0, 0)),
                  pl.BlockSpec(blk2, lambda b, k: (b, 0, k, 0)),
                  _mod_spec(d, mi(2))],
        out_specs=pl.BlockSpec(blk2, lambda b, k: (b, 0, k, 0)),
        out_shape=jax.ShapeDtypeStruct((n_batch, DFT_R, DFT_P, d), F32),
        scratch_shapes=[pltpu.VMEM((rows, 2 * d), BF16)] + [pltpu.VMEM((d // LANES, rows, LANES), F32)] * 2,
        compiler_params=_cparams(("parallel", "parallel")),
        name="fnet_stage2",
    )(zr, zi, tab2, ccos, csin, w_out, xo, mods)
    return out.reshape(t, d)


def _fnet_ctx_kernel(x_ref, g_ref, sh_ref, sc_ref, gt_ref, tab_ref, cc_ref, sc2_ref, wout_ref, o_ref, *, scale):
    n = x_ref.shape[0]
    h = _norm_mod(x_ref[...], g_ref[...], sh_ref[...], sc_ref[...]).astype(BF16)
    p = jnp.dot(tab_ref[...], h, preferred_element_type=F32)
    y = _mix_tail(p[:n].astype(BF16), p[n:].astype(BF16), cc_ref, sc2_ref, wout_ref, scale)
    o_ref[...] = x_ref[...] + gt_ref[...] * y


def _fnet_ctx(x, mods, g_all, w_out, layer, n_batch):
    t, d = x.shape
    n = t // n_batch
    gd = d // FNET_GROUPS
    ccos, csin = _chan_dft_tables(gd)
    tab = _ctx_dft_table(n)
    mi = lambda j: (lambda b: (layer * MOD_ROWS + CTX_ROW) * N_MOD + j)
    scale = float(1.0 / np.sqrt(float(n) * gd))
    return pl.pallas_call(
        functools.partial(_fnet_ctx_kernel, scale=scale),
        grid=(n_batch,),
        in_specs=[pl.BlockSpec((n, d), lambda b: (b, 0)),
                  pl.BlockSpec((None, 1, d), lambda b: (layer, 0, 0)),
                  _mod_spec(d, mi(0)), _mod_spec(d, mi(1)), _mod_spec(d, mi(2)),
                  pl.BlockSpec(tab.shape, lambda b: (0, 0)),
                  pl.BlockSpec(ccos.shape, lambda b: (0, 0)),
                  pl.BlockSpec(csin.shape, lambda b: (0, 0)),
                  pl.BlockSpec((None, d, d), lambda b: (layer // 2, 0, 0))],
        out_specs=pl.BlockSpec((n, d), lambda b: (b, 0)),
        out_shape=jax.ShapeDtypeStruct((t, d), F32),
        compiler_params=_cparams(("parallel",)),
        name="fnet_ctx",
    )(x, g_all, mods, mods, mods, tab, ccos, csin, w_out)


def _rope_tables(n_seq):
    rows = n_seq // GRID_W
    row = jnp.repeat(jnp.arange(rows, dtype=F32), GRID_W)
    col = jnp.tile(jnp.arange(GRID_W, dtype=F32), rows)
    inv_freq = ROPE_THETA ** (-jnp.arange(ROT_FREQS, dtype=F32) / ROT_FREQS)
    ang = jnp.stack([row[:, None] * inv_freq, col[:, None] * inv_freq], axis=1)
    cos, sin = jnp.cos(ang), jnp.sin(ang)
    zero = jnp.zeros_like(sin)
    cos_h = jnp.stack([cos, cos], axis=2).reshape(n_seq, HEAD_DIM)
    sin_lo = jnp.stack([-sin, zero], axis=2).reshape(n_seq, HEAD_DIM)
    sin_hi = jnp.stack([zero, sin], axis=2).reshape(n_seq, HEAD_DIM)
    rep = LANES // HEAD_DIM
    return jnp.tile(cos_h, (1, rep)), jnp.tile(sin_lo, (1, rep)), jnp.tile(sin_hi, (1, rep))


def _qkv_kernel(x_ref, g_ref, sh_ref, sc_ref, w_ref, *rest, rope, q_dim, kv_dim, q_scale):
    if rope:
        cos_ref, slo_ref, shi_ref, q_ref, k_ref, kt_ref, v_ref, vt_ref = rest
    else:
        q_ref, k_ref, kt_ref, v_ref, vt_ref = rest
    h = _norm_mod(x_ref[...], g_ref[...], sh_ref[...], sc_ref[...]).astype(BF16)
    qkv = jnp.dot(h, w_ref[...], preferred_element_type=F32)

    def rot(xs):
        if not rope:
            return xs
        return (xs * cos_ref[...] + pltpu.roll(xs, LANES - ROT_FREQS, axis=1) * slo_ref[...]
                + pltpu.roll(xs, ROT_FREQS, axis=1) * shi_ref[...])

    for j in range(q_dim // LANES):
        qj = rot(qkv[:, j * LANES:(j + 1) * LANES]) * q_scale
        if rope:
            q_ref[j * LANES:(j + 1) * LANES, :] = qj.T.astype(BF16)
        else:
            q_ref[:, j * LANES:(j + 1) * LANES] = qj.astype(BF16)
    k = jnp.concatenate([rot(qkv[:, q_dim + j * LANES:q_dim + (j + 1) * LANES]) for j in range(kv_dim // LANES)],
                        axis=-1)
    v = qkv[:, q_dim + kv_dim:]
    k_ref[...] = k.astype(BF16)
    kt_ref[...] = k.T.astype(BF16)
    v_ref[...] = v.astype(BF16)
    vt_ref[...] = v.T.astype(BF16)


def _qkv(x, mods, g_all, w_qkv, layer, n_batch, ctx, rope_tabs=None, tm=QKV_TM):
    t, d = x.shape
    n = t // n_batch
    tm = min(tm, n)
    tpb = n // tm
    kv_dim = N_KV_HEADS * HEAD_DIM
    q_dim = w_qkv.shape[2] - 2 * kv_dim
    mi = lambda j: _mod_index(layer, j, tpb, ctx)
    in_specs = [pl.BlockSpec((tm, d), lambda i: (i, 0)),
                pl.BlockSpec((None, 1, d), lambda i: (layer, 0, 0)),
                _mod_spec(d, mi(0)), _mod_spec(d, mi(1)),
                pl.BlockSpec((None,) + w_qkv.shape[1:], lambda i: (layer // 2, 0, 0))]
    args = [x, g_all, mods, mods, w_qkv]
    rope = rope_tabs is not None
    if rope:
        in_specs += [pl.BlockSpec((tm, LANES), lambda i: (i % tpb, 0))] * 3
        args += list(rope_tabs)
    return pl.pallas_call(
        functools.partial(_qkv_kernel, rope=rope, q_dim=q_dim, kv_dim=kv_dim, q_scale=HEAD_DIM ** -0.5),
        grid=(t // tm,),
        in_specs=in_specs,
        out_specs=[(pl.BlockSpec((None, q_dim, tm), lambda i: (i // tpb, 0, i % tpb)) if rope
                    else pl.BlockSpec((tm, q_dim), lambda i: (i, 0))),
                   pl.BlockSpec((tm, kv_dim), lambda i: (i, 0)),
                   pl.BlockSpec((None, kv_dim, tm), lambda i: (i // tpb, 0, i % tpb)),
                   pl.BlockSpec((tm, kv_dim), lambda i: (i, 0)),
                   pl.BlockSpec((None, kv_dim, tm), lambda i: (i // tpb, 0, i % tpb))],
        out_shape=[jax.ShapeDtypeStruct((n_batch, q_dim, n) if rope else (t, q_dim), BF16),
                   jax.ShapeDtypeStruct((t, kv_dim), BF16),
                   jax.ShapeDtypeStruct((n_batch, kv_dim, n), BF16),
                   jax.ShapeDtypeStruct((t, kv_dim), BF16),
                   jax.ShapeDtypeStruct((n_batch, kv_dim, n), BF16)],
        compiler_params=_cparams(("parallel",)),
        name="qkv_ctx" if ctx else "qkv_lat",
    )(*args)


ATTN_RB = 64


def _heads_attend(q_ref, kt, v, bias, sink_ref, o_ref):
    nq = q_ref.shape[0]
    n_heads = q_ref.shape[1] // HEAD_DIM
    group = n_heads // N_KV_HEADS
    outs = [None] * n_heads
    for g in range(N_KV_HEADS):
        heads = range(g * group, (g + 1) * group)
        qg = jnp.concatenate([q_ref[:, hd * HEAD_DIM:(hd + 1) * HEAD_DIM] for hd in heads], axis=0)
        s = jnp.dot(qg, kt[g * HEAD_DIM:(g + 1) * HEAD_DIM, :], preferred_element_type=F32)
        p_rows, den_rows = [], []
        for r0 in range(0, s.shape[0], ATTN_RB):
            rows = slice(r0, r0 + ATTN_RB)
            parts = [s[rows, k * LANES:(k + 1) * LANES] for k in range(s.shape[1] // LANES)]
            if bias is not None:
                parts[0] = parts[0] + bias[0][rows]
                parts[2] = parts[2] + bias[1][rows]
            sink = sink_ref[heads[r0 // nq]]
            m = jnp.maximum(jnp.max(functools.reduce(jnp.maximum, parts), axis=-1, keepdims=True), sink)
            ps = [jnp.exp(part - m) for part in parts]
            den_rows.append(jnp.sum(functools.reduce(jnp.add, ps), axis=-1, keepdims=True) + jnp.exp(sink - m))
            p_rows.append(jnp.concatenate(ps, axis=-1).astype(BF16))
        pv = jnp.dot(jnp.concatenate(p_rows, axis=0), v, preferred_element_type=F32)
        on = pv[:, g * HEAD_DIM:(g + 1) * HEAD_DIM] / jnp.concatenate(den_rows, axis=0)
        for k, hd in enumerate(heads):
            outs[hd] = on[k * nq:(k + 1) * nq, :]
    o_ref[...] = jnp.concatenate(outs, axis=-1).astype(BF16)


def _attn_lat_kernel(sink_ref, qt_ref, kp_ref, kc_ref, kn_ref, vtp_ref, vtc_ref, vtn_ref, kx_ref, vtx_ref,
                     blo_ref, bhi_ref, lo_ref, hi_ref, ot_ref, k_scr, vt_scr):
    w = kp_ref.shape[0]
    n_ctx = kx_ref.shape[0]
    n_heads = qt_ref.shape[0] // HEAD_DIM
    group = n_heads // N_KV_HEADS
    span = 3 * w + n_ctx
    k_scr[0:w, :] = kp_ref[...]
    k_scr[w:3 * w, :] = kc_ref[...]
    k_scr[3 * w:span, :] = kx_ref[...]
    k_scr[span:span + w, :] = kn_ref[...]
    vt_scr[:, 0:w] = vtp_ref[...]
    vt_scr[:, w:3 * w] = vtc_ref[...]
    vt_scr[:, 3 * w:span] = vtx_ref[...]
    vt_scr[:, span:span + w] = vtn_ref[...]
    n_chunks = span // w
    for blk in range(2):
        first = blk * w
        lo_mask = blo_ref if blk == 0 else lo_ref
        hi_mask = hi_ref if blk == 0 else bhi_ref
        hi_chunk = 2 if blk == 0 else n_chunks - 1
        for g in range(N_KV_HEADS):
            heads = range(g * group, (g + 1) * group)
            qg = jnp.concatenate([qt_ref[hd * HEAD_DIM:(hd + 1) * HEAD_DIM, first:first + w] for hd in heads], axis=1)
            st = jnp.dot(k_scr[first:first + span, g * HEAD_DIM:(g + 1) * HEAD_DIM], qg,
                         preferred_element_type=F32)
            p_cols, den_cols = [], []
            for k, hd in enumerate(heads):
                cols = slice(k * w, (k + 1) * w)
                parts = [st[c * w:(c + 1) * w, cols] for c in range(n_chunks)]
                parts[0] = parts[0] + lo_mask[...]
                parts[hi_chunk] = parts[hi_chunk] + hi_mask[...]
                sink = sink_ref[hd]
                m = jnp.maximum(jnp.max(functools.reduce(jnp.maximum, parts), axis=0, keepdims=True), sink)
                ps = [jnp.exp(part - m) for part in parts]
                den_cols.append(jnp.sum(functools.reduce(jnp.add, ps), axis=0, keepdims=True) + jnp.exp(sink - m))
                p_cols.append(jnp.concatenate(ps, axis=0).astype(BF16))
            ot = jnp.dot(vt_scr[g * HEAD_DIM:(g + 1) * HEAD_DIM, first:first + span], jnp.concatenate(p_cols, axis=1),
                         preferred_element_type=F32)
            ot = ot / jnp.concatenate(den_cols, axis=1)
            for k, hd in enumerate(heads):
                ot_ref[hd * HEAD_DIM:(hd + 1) * HEAD_DIM, first:first + w] = ot[:, k * w:(k + 1) * w].astype(BF16)


def _attn_lat(qt, k, vt, kx, vtx, sinks, n_batch):
    _, qd, s = qt.shape
    n_ctx = kx.shape[0] // n_batch
    w = WINDOW
    nb = s // w
    kvd = k.shape[1]
    j_all = 3 * w + n_ctx
    ki = np.arange(w)[:, None]
    qi = np.arange(w)[None, :]
    lo = np.where(ki >= qi, 0.0, NEG).astype(np.float32)
    hi = np.where(ki <= qi, 0.0, NEG).astype(np.float32)
    off = np.full((w, w), NEG, np.float32)
    blo = jnp.asarray(np.stack([lo, off]))
    bhi = jnp.asarray(np.stack([hi, off]))
    assert nb % 2 == 0
    ns = nb // 2
    pidx = lambda i: jnp.maximum(2 * i - 1, 0)
    nidx = lambda i: jnp.minimum(2 * i + 2, nb - 1)
    return pl.pallas_call(
        _attn_lat_kernel,
        grid=(n_batch, ns),
        in_specs=[pl.BlockSpec(memory_space=pltpu.SMEM),
                  pl.BlockSpec((None, qd, 2 * w), lambda b, i: (b, 0, i)),
                  pl.BlockSpec((w, kvd), lambda b, i: (b * nb + pidx(i), 0)),
                  pl.BlockSpec((2 * w, kvd), lambda b, i: (b * ns + i, 0)),
                  pl.BlockSpec((w, kvd), lambda b, i: (b * nb + nidx(i), 0)),
                  pl.BlockSpec((None, kvd, w), lambda b, i: (b, 0, pidx(i))),
                  pl.BlockSpec((None, kvd, 2 * w), lambda b, i: (b, 0, i)),
                  pl.BlockSpec((None, kvd, w), lambda b, i: (b, 0, nidx(i))),
                  pl.BlockSpec((n_ctx, kvd), lambda b, i: (b, 0)),
                  pl.BlockSpec((None, kvd, n_ctx), lambda b, i: (b, 0, 0)),
                  pl.BlockSpec((None, w, w), lambda b, i: (jnp.where(i == 0, 1, 0), 0, 0)),
                  pl.BlockSpec((None, w, w), lambda b, i: (jnp.where(i == ns - 1, 1, 0), 0, 0)),
                  pl.BlockSpec((None, w, w), lambda b, i: (0, 0, 0)),
                  pl.BlockSpec((None, w, w), lambda b, i: (0, 0, 0))],
        out_specs=pl.BlockSpec((None, qd, 2 * w), lambda b, i: (b, 0, i)),
        out_shape=jax.ShapeDtypeStruct((n_batch, qd, s), BF16),
        scratch_shapes=[pltpu.VMEM((j_all + w, kvd), BF16), pltpu.VMEM((kvd, j_all + w), BF16)],
        compiler_params=_cparams(("parallel", "parallel")),
        name="attn_lat",
    )(sinks, qt, k, k, k, vt, vt, vt, kx, vtx, blo, bhi, blo, bhi)


def _attn_ctx_kernel(sink_ref, q_ref, kt_ref, v_ref, o_ref):
    _heads_attend(q_ref, kt_ref[...], v_ref[...], None, sink_ref, o_ref)


def _attn_ctx(q, kt, v, sinks, n_batch):
    t, qd = q.shape
    n = t // n_batch
    kvd = v.shape[1]
    return pl.pallas_call(
        _attn_ctx_kernel,
        grid=(n_batch,),
        in_specs=[pl.BlockSpec(memory_space=pltpu.SMEM),
                  pl.BlockSpec((n, qd), lambda b: (b, 0)),
                  pl.BlockSpec((None, kvd, n), lambda b: (b, 0, 0)),
                  pl.BlockSpec((n, kvd), lambda b: (b, 0))],
        out_specs=pl.BlockSpec((n, qd), lambda b: (b, 0)),
        out_shape=jax.ShapeDtypeStruct((t, qd), BF16),
        compiler_params=_cparams(("parallel",)),
        name="attn_ctx",
    )(sinks, q, kt, v)


ROUTE_I1, ROUTE_I2, ROUTE_R1, ROUTE_R2, ROUTE_G1, ROUTE_G2 = range(6)
ROUTE_ROWS = 8
TAB_START, TAB_COUNT, TAB_USED, TAB_SIZE = 0, N_EXPERTS, 2 * N_EXPERTS, 2 * N_EXPERTS + 8
ROW_UNROLL = 8


def _router_kernel(x_ref, g_ref, sh_ref, sc_ref, wr_ref, tri_ref, route_ref, idx_ref, cnt_ref, carry_scr):
    @pl.when(pl.program_id(0) == 0)
    def _():
        carry_scr[...] = jnp.zeros_like(carry_scr)

    h = _norm_mod(x_ref[...], g_ref[...], sh_ref[...], sc_ref[...])
    hi = h.astype(BF16)
    lo = (h - hi.astype(F32)).astype(BF16)
    a = jnp.dot(hi, wr_ref[...], preferred_element_type=F32)
    logits = (a[:, :LANES] + a[:, LANES:]) + jnp.dot(lo, wr_ref[:, :LANES], preferred_element_type=F32)
    lane = lax.broadcasted_iota(jnp.int32, logits.shape, 1)
    lane_f = lane.astype(F32)
    logits = jnp.where(lane < N_EXPERTS, logits, -jnp.inf)
    m1 = jnp.max(logits, axis=-1, keepdims=True)
    i1 = jnp.min(jnp.where(logits == m1, lane_f, float(LANES)), axis=-1, keepdims=True)
    oh1 = lane_f == i1
    rest = jnp.where(oh1, -jnp.inf, logits)
    m2 = jnp.max(rest, axis=-1, keepdims=True)
    i2 = jnp.min(jnp.where(rest == m2, lane_f, float(LANES)), axis=-1, keepdims=True)
    oh2 = lane_f == i2
    e2 = jnp.exp(m2 - m1)
    g1 = 1.0 / (1.0 + e2)
    g2 = e2 / (1.0 + e2)
    sel = jnp.where(oh1, 1.0, 0.0) + jnp.where(oh2, 1.0, 0.0)
    before = jnp.dot(tri_ref[...], sel.astype(BF16), preferred_element_type=F32) + carry_scr[0:1, :]
    r1 = jnp.sum(jnp.where(oh1, before, 0.0), axis=-1, keepdims=True)
    r2 = jnp.sum(jnp.where(oh2, before, 0.0), axis=-1, keepdims=True)
    total = carry_scr[0:1, :] + jnp.sum(sel, axis=0, keepdims=True)
    carry_scr[...] = jnp.broadcast_to(total, carry_scr.shape)
    cnt_ref[...] = jnp.broadcast_to(total, cnt_ref.shape)
    rec = jnp.zeros_like(logits)
    for ln, val in ((ROUTE_I1, i1), (ROUTE_I2, i2), (ROUTE_R1, r1), (ROUTE_R2, r2), (ROUTE_G1, g1), (ROUTE_G2, g2)):
        rec = jnp.where(lane == ln, val, rec)
    route_ref[...] = rec
    idx_ref[...] = rec.T[:ROUTE_ROWS, :].astype(jnp.int32)


def _router(x, mods, g_all, w_router, layer, tiles_per_batch, ctx, tm):
    t, d = x.shape
    mi = lambda j: _mod_index(layer, j, tiles_per_batch // tm if not ctx else 1, ctx)
    wr = jnp.zeros((d, LANES), F32).at[:, :N_EXPERTS].set(w_router)
    whi = wr.astype(BF16)
    wr = jnp.concatenate([whi, (wr - whi.astype(F32)).astype(BF16)], axis=1)
    tri = jnp.asarray(np.tril(np.ones((tm, tm), np.float32), -1), BF16)
    return pl.pallas_call(
        _router_kernel,
        grid=(t // tm,),
        in_specs=[pl.BlockSpec((tm, d), lambda i: (i, 0)),
                  pl.BlockSpec((None, 1, d), lambda i: (layer, 0, 0)),
                  _mod_spec(d, mi(3)), _mod_spec(d, mi(4)),
                  pl.BlockSpec((d, 2 * LANES), lambda i: (0, 0)),
                  pl.BlockSpec((tm, tm), lambda i: (0, 0))],
        out_specs=[pl.BlockSpec((tm, LANES), lambda i: (i, 0)),
                   pl.BlockSpec((None, ROUTE_ROWS, tm), lambda i: (i, 0, 0)),
                   pl.BlockSpec((8, LANES), lambda i: (0, 0))],
        out_shape=[jax.ShapeDtypeStruct((t, LANES), F32),
                   jax.ShapeDtypeStruct((t // tm, ROUTE_ROWS, tm), jnp.int32),
                   jax.ShapeDtypeStruct((8, LANES), F32)],
        scratch_shapes=[pltpu.VMEM((8, LANES), F32)],
        compiler_params=_cparams(("arbitrary",)),
        name="router",
    )(x, g_all, mods, mods, wr, tri)


SUB = 8


def _to_row_tiles(ref, val):
    n = val.shape[0]
    for k in range(SUB):
        ref[pl.ds(k, n, stride=SUB), :] = val[:, k * LANES:(k + 1) * LANES]


def _from_row_tiles(ref):
    n = ref.shape[0] // SUB
    return jnp.concatenate([ref[pl.ds(k, n, stride=SUB), :] for k in range(SUB)], axis=-1)


def _row_copy(src, dst, src_row, dst_row, sem):
    return pltpu.make_async_copy(src.at[pl.ds(pl.multiple_of(src_row * SUB, SUB), SUB)],
                                 dst.at[pl.ds(pl.multiple_of(dst_row * SUB, SUB), SUB)], sem)


def _for_rows(tm, body):
    def blk(i, _):
        base = pl.multiple_of(i * ROW_UNROLL, ROW_UNROLL)
        for j in range(ROW_UNROLL):
            body(base + j, j)
        return 0

    lax.fori_loop(0, tm // ROW_UNROLL, blk, 0)


ZERO_SEM = 4


def _dispatch_kernel(tab_ref, slot_ref, x_ref, g_ref, sh_ref, sc_ref, xs_ref, h_scr, zero_scr, sems, *, group):
    tm = x_ref.shape[0]
    n_groups = xs_ref.shape[0] // (group * SUB)
    step = pl.program_id(0)
    n_steps = pl.num_programs(0)
    cur = lax.rem(step, 2)

    def wait_buffer(b):
        for k in range(2):
            pltpu.make_async_copy(h_scr.at[b], xs_ref.at[pl.ds(0, tm * SUB)], sems.at[2 * b + k]).wait()

    @pl.when(step >= 2)
    def _():
        wait_buffer(cur)

    _to_row_tiles(h_scr.at[cur], _norm_mod(x_ref[...], g_ref[...], sh_ref[...], sc_ref[...]))

    def issue(r, j):
        for k in range(2):
            _row_copy(h_scr.at[cur], xs_ref, r, slot_ref[0, k, r], sems.at[2 * cur + k]).start(priority=j % 2)

    _for_rows(tm, issue)

    @pl.when(jnp.logical_and(step == n_steps - 1, step >= 1))
    def _():
        wait_buffer(1 - cur)

    @pl.when(step == n_steps - 1)
    def _():
        wait_buffer(cur)
        zero_scr[...] = jnp.zeros_like(zero_scr)
        for e in range(N_EXPERTS):
            n = tab_ref[TAB_COUNT + e]
            n_pad = lax.rem(group - lax.rem(n, group), group)
            first = tab_ref[TAB_START + e] + n

            def fill(k, _, first=first):
                _row_copy(zero_scr, xs_ref, 0, first + k, sems.at[ZERO_SEM]).start()
                return 0

            def fill_done(k, _):
                _row_copy(zero_scr, xs_ref, 0, 0, sems.at[ZERO_SEM]).wait()
                return 0

            lax.fori_loop(0, n_pad, fill, 0)
            lax.fori_loop(0, n_pad, fill_done, 0)

        def clear(j, _):
            row = pl.multiple_of(j * (group * SUB), group * SUB)
            cp = pltpu.make_async_copy(zero_scr, xs_ref.at[pl.ds(row, group * SUB)], sems.at[ZERO_SEM])
            cp.start()
            cp.wait()
            return 0

        lax.fori_loop(tab_ref[TAB_USED], n_groups, clear, 0)


def _dispatch(x, tab, slots, mods, g_all, layer, tiles_per_batch, ctx, n_groups, group, tm):
    t, d = x.shape
    mi = lambda j: _mod_index(layer, j, tiles_per_batch // tm if not ctx else 1, ctx)
    return pl.pallas_call(
        functools.partial(_dispatch_kernel, group=group),
        grid=(t // tm,),
        in_specs=[pl.BlockSpec(memory_space=pltpu.SMEM),
                  pl.BlockSpec((1, 2, tm), lambda i: (i, 0, 0), memory_space=pltpu.SMEM),
                  pl.BlockSpec((tm, d), lambda i: (i, 0)),
                  pl.BlockSpec((None, 1, d), lambda i: (layer, 0, 0)),
                  _mod_spec(d, mi(3)), _mod_spec(d, mi(4))],
        out_specs=pl.BlockSpec(memory_space=pl.ANY),
        out_shape=jax.ShapeDtypeStruct((n_groups * group * SUB, LANES), F32),
        scratch_shapes=[pltpu.VMEM((2, tm * SUB, LANES), F32), pltpu.VMEM((group * SUB, LANES), F32),
                        pltpu.SemaphoreType.DMA((ZERO_SEM + 1,))],
        compiler_params=_cparams(("arbitrary",)),
        name="moe_dispatch",
    )(tab, slots, x, g_all, mods, mods)


def _moe_ffn_kernel(be_ref, bv_ref, xs_ref, wg_ref, wu_ref, wd_ref, ys_ref, h_scr, acc_scr):
    del be_ref
    i = pl.program_id(0)
    f = pl.program_id(1)
    last = f == pl.num_programs(1) - 1
    valid = bv_ref[i] > 0

    @pl.when(jnp.logical_and(valid, f == 0))
    def _():
        h_scr[...] = _from_row_tiles(xs_ref).astype(BF16)
        acc_scr[...] = jnp.zeros_like(acc_scr)

    @pl.when(valid)
    def _():
        h = h_scr[...]
        a = jnp.dot(h, wg_ref[...], preferred_element_type=F32)
        u = jnp.dot(h, wu_ref[...], preferred_element_type=F32)
        t = (a * _sigmoid(a)) * u
        acc_scr[...] += jnp.dot(t.astype(BF16), wd_ref[...], preferred_element_type=F32)

    @pl.when(jnp.logical_and(valid, last))
    def _():
        _to_row_tiles(ys_ref, acc_scr[...])

    @pl.when(jnp.logical_and(jnp.logical_not(valid), last))
    def _():
        ys_ref[...] = jnp.zeros_like(ys_ref)


def _moe_ffn(xs, block_expert, block_valid, wg, wu, wd, layer, tm, tf=MOE_TF):
    d, ff = wg.shape[2:]
    assert d == SUB * LANES and xs.shape[1] == LANES
    cap = xs.shape[0] // SUB
    tf = min(tf, ff)
    nf = ff // tf
    sj = layer // 2
    fsel = lambda i, f, bv: jnp.where(bv[i] > 0, f, nf - 1)
    grid_spec = pltpu.PrefetchScalarGridSpec(
        num_scalar_prefetch=2,
        grid=(cap // tm, nf),
        in_specs=[pl.BlockSpec((tm * SUB, LANES), lambda i, f, be, bv: (i, 0)),
                  pl.BlockSpec((None, None, d, tf), lambda i, f, be, bv: (sj, be[i], 0, fsel(i, f, bv))),
                  pl.BlockSpec((None, None, d, tf), lambda i, f, be, bv: (sj, be[i], 0, fsel(i, f, bv))),
                  pl.BlockSpec((None, None, tf, d), lambda i, f, be, bv: (sj, be[i], fsel(i, f, bv), 0))],
        out_specs=pl.BlockSpec((tm * SUB, LANES), lambda i, f, be, bv: (i, 0)),
        scratch_shapes=[pltpu.VMEM((tm, d), BF16), pltpu.VMEM((tm, d), F32)],
    )
    return pl.pallas_call(
        _moe_ffn_kernel,
        grid_spec=grid_spec,
        out_shape=jax.ShapeDtypeStruct(xs.shape, F32),
        compiler_params=_cparams(("parallel", "arbitrary")),
        name="moe_ffn",
    )(block_expert, block_valid, xs, wg, wu, wd)


def _combine_kernel(slot_ref, next_slot_ref, route_ref, x_ref, gt_ref, fg_ref, ys_ref, o_ref, buf, sems, *, final_norm):
    tm = x_ref.shape[0]
    step = pl.program_id(0)
    cur = lax.rem(step, 2)

    def gather(idx_ref, b):
        def issue(r, j):
            for k in range(2):
                _row_copy(ys_ref, buf.at[b, k], idx_ref[0, k, r], r, sems.at[2 * b + k]).start(priority=j % 2)

        _for_rows(tm, issue)

    @pl.when(step == 0)
    def _():
        gather(slot_ref, cur)

    @pl.when(step + 1 < pl.num_programs(0))
    def _():
        gather(next_slot_ref, 1 - cur)

    for k in range(2):
        pltpu.make_async_copy(ys_ref.at[pl.ds(0, tm * SUB)], buf.at[cur, k], sems.at[2 * cur + k]).wait()
    rec = route_ref[...]
    g1 = rec[:, ROUTE_G1:ROUTE_G1 + 1]
    g2 = rec[:, ROUTE_G2:ROUTE_G2 + 1]
    y = _from_row_tiles(buf.at[cur, 0]) * g1 + _from_row_tiles(buf.at[cur, 1]) * g2
    out = x_ref[...] + gt_ref[...] * y
    if final_norm:
        out = out * lax.rsqrt(jnp.mean(out * out, axis=-1, keepdims=True) + EPS) * fg_ref[...]
    o_ref[...] = out


def _combine(ys, slots, route, x, mods, final_g, layer, tiles_per_batch, ctx, final_norm, tm):
    t, d = x.shape
    mi = _mod_index(layer, 5, tiles_per_batch // tm if not ctx else 1, ctx)
    n_tiles = t // tm
    return pl.pallas_call(
        functools.partial(_combine_kernel, final_norm=final_norm),
        grid=(n_tiles,),
        in_specs=[pl.BlockSpec((1, 2, tm), lambda i: (i, 0, 0), memory_space=pltpu.SMEM),
                  pl.BlockSpec((1, 2, tm), lambda i: (jnp.minimum(i + 1, n_tiles - 1), 0, 0), memory_space=pltpu.SMEM),
                  pl.BlockSpec((tm, LANES), lambda i: (i, 0)),
                  pl.BlockSpec((tm, d), lambda i: (i, 0)),
                  _mod_spec(d, mi),
                  pl.BlockSpec((1, d), lambda i: (0, 0)),
                  pl.BlockSpec(memory_space=pl.ANY)],
        out_specs=pl.BlockSpec((tm, d), lambda i: (i, 0)),
        out_shape=jax.ShapeDtypeStruct((t, d), F32),
        scratch_shapes=[pltpu.VMEM((2, 2, tm * SUB, LANES), F32), pltpu.SemaphoreType.DMA((4,))],
        compiler_params=_cparams(("arbitrary",)),
        name="moe_combine",
    )(slots, slots, route, x, mods, final_g.reshape(1, d), ys)


def _moe(x, mods, g_all, w_router, wg, wu, wd, final_g, layer, tiles_per_batch, ctx, final_norm):
    t, d = x.shape
    group = MOE_TM if 2 * t >= 4 * N_EXPERTS * MOE_TM else MOE_TM // 2
    row_tm = min(ROW_TM, t)
    route, idx, counts = _router(x, mods, g_all, w_router, layer, tiles_per_batch, ctx, row_tm)
    counts = counts[0, :N_EXPERTS].astype(jnp.int32)
    groups = (counts + group - 1) // group
    ends = jnp.cumsum(groups)
    starts = (ends - groups) * group
    n_groups = (2 * t + group - 1) // group + N_EXPERTS
    tab = jnp.zeros((TAB_SIZE,), jnp.int32)
    tab = tab.at[TAB_START:TAB_START + N_EXPERTS].set(starts).at[TAB_COUNT:TAB_COUNT + N_EXPERTS].set(counts)
    tab = tab.at[TAB_USED].set(ends[-1])
    gi = jnp.arange(n_groups, dtype=jnp.int32)
    block_expert = jnp.minimum(jnp.sum(gi[:, None] >= ends[None, :], axis=1), N_EXPERTS - 1).astype(jnp.int32)
    block_valid = (gi < ends[-1]).astype(jnp.int32)
    def slot_rows(e, r):
        base = functools.reduce(jnp.add, [jnp.where(e == k, starts[k], 0) for k in range(N_EXPERTS)])
        return base + r
    slots = jnp.stack([slot_rows(idx[:, ROUTE_I1, :], idx[:, ROUTE_R1, :]),
                       slot_rows(idx[:, ROUTE_I2, :], idx[:, ROUTE_R2, :])], axis=1)
    xs = _dispatch(x, tab, slots, mods, g_all, layer, tiles_per_batch, ctx, n_groups, group, row_tm)
    ys = _moe_ffn(xs, block_expert, block_valid, wg, wu, wd, layer, group)
    return _combine(ys, slots, route, x, mods, final_g, layer, tiles_per_batch, ctx, final_norm, row_tm)


def kernel(x, c, ctx, c_ctx, ada_w, ada_b, norm_mix_g, norm_ffn_g, fnet_w_out, attn_w_qkv, attn_w_o, attn_sinks,
           ffn_w_gate, ffn_w_up, ffn_w_down, moe_w_router, moe_w_gate, moe_w_up, moe_w_down, final_norm_g):
    n_batch, s, d = x.shape
    n_ctx = ctx.shape[1]
    depth = ada_w.shape[0]
    mods = _ada(c, c_ctx, ada_w, ada_b)
    g_mix = norm_mix_g.reshape(depth, 1, d)
    g_ffn = norm_ffn_g.reshape(depth, 1, d)
    rope_tabs = _rope_tables(s)
    x_lat = x.reshape(n_batch * s, d)
    x_ctx = ctx.reshape(n_batch * n_ctx, d)
    bf = lambda a: a.astype(BF16)
    w_out, w_qkv, w_o = bf(fnet_w_out), bf(attn_w_qkv), bf(attn_w_o)
    wg, wu, wd = bf(ffn_w_gate), bf(ffn_w_up), bf(ffn_w_down)
    eg, eu, ed = bf(moe_w_gate), bf(moe_w_up), bf(moe_w_down)
    for i in range(depth):
        j = i // 2
        last = i == depth - 1
        if i % 2 == 0:
            x_lat = _fnet_lat(x_lat, mods, g_mix, w_out, i, n_batch)
            if not last:
                x_ctx = _fnet_ctx(x_ctx, mods, g_mix, w_out, i, n_batch)
            x_lat = _ffn_dense(x_lat, mods, g_ffn, wg, wu, wd, i, s, False)
            if not last:
                x_ctx = _ffn_dense(x_ctx, mods, g_ffn, wg, wu, wd, i, n_ctx, True)
        else:
            qt, k, _, _, vt = _qkv(x_lat, mods, g_mix, w_qkv, i, n_batch, False, rope_tabs)
            qx, kx, ktx, vx, vtx = _qkv(x_ctx, mods, g_mix, w_qkv, i, n_batch, True)
            ot = _attn_lat(qt, k, vt, kx, vtx, attn_sinks[j], n_batch)
            x_lat = _proj_res(ot, w_o, x_lat, mods, i, 2, s, False)
            if not last:
                ox = _attn_ctx(qx, ktx, vx, attn_sinks[j], n_batch)
                x_ctx = _proj_res(ox, w_o, x_ctx, mods, i, 2, n_ctx, True)
            x_lat = _moe(x_lat, mods, g_ffn, moe_w_router[j], eg, eu, ed, final_norm_g, i, s, False, last)
            if not last:
                x_ctx = _moe(x_ctx, mods, g_ffn, moe_w_router[j], eg, eu, ed, final_norm_g, i, n_ctx, True, False)
    return x_lat.reshape(n_batch, s, d)
```

```python
import functools

import numpy as np
import jax
import jax.numpy as jnp
from jax import lax
from jax.experimental import pallas as pl
from jax.experimental.pallas import tpu as pltpu

F32 = jnp.float32
BF16 = jnp.bfloat16

GRID_W = 64
N_MOD = 6
EPS = 1e-6
FNET_GROUPS = 4
HEAD_DIM = 64
N_KV_HEADS = 4
WINDOW = 128
ROPE_THETA = 10000.0
ROT_FREQS = HEAD_DIM // 4
N_EXPERTS = 8
MOD_ROWS = 8
CTX_ROW = MOD_ROWS - 1
LANES = 128
NEG = -1e30
VMEM_LIMIT = 56 * 1024 * 1024

FFN_TM = 512
FFN_TF = 1792
MOE_TM = 512
MOE_TF = 1792
ROW_TM = 512
PROJ_TM = 1024
QKV_TM = 1024
DFT_P = 64
DFT_R = 128
FNET_RB = 16
FNET_KB = 8


def _cparams(sem):
    return pltpu.CompilerParams(dimension_semantics=sem, vmem_limit_bytes=VMEM_LIMIT)


def _sigmoid(a):
    return 1.0 / (1.0 + jnp.exp(-a))


def _norm_mod(x, g, shift, scale):
    xn = x * lax.rsqrt(jnp.mean(x * x, axis=-1, keepdims=True) + EPS)
    return (xn * g) * (1.0 + scale) + shift


def _mod_spec(d, idx_fn):
    return pl.BlockSpec((None, 1, d), lambda *ids: (idx_fn(*ids), 0, 0))


def _mod_index(layer, j, tiles_per_batch, ctx):
    def fn(i, *_):
        b = CTX_ROW if ctx else i // tiles_per_batch
        return (layer * MOD_ROWS + b) * N_MOD + j
    return fn


def _ada_kernel(c_ref, w_ref, b_ref, o_ref):
    cc = c_ref[...]
    s = cc * _sigmoid(cc)
    w = w_ref[...]
    whi = w.astype(BF16)
    wlo = (w - whi.astype(F32)).astype(BF16)
    hi = s.astype(BF16)
    lo = (s - hi.astype(F32)).astype(BF16)
    n = s.shape[0]
    a = jnp.dot(jnp.concatenate([hi, lo], axis=0), whi, preferred_element_type=F32)
    o_ref[...] = (a[:n] + a[n:]) + jnp.dot(hi, wlo, preferred_element_type=F32) + b_ref[...]


def _ada(c, c_ctx, ada_w, ada_b):
    depth, d, _ = ada_w.shape
    nb = c.shape[0]
    assert nb < MOD_ROWS
    cc = jnp.concatenate([c, jnp.zeros((CTX_ROW - nb, d), F32), c_ctx[None]], axis=0)
    out = pl.pallas_call(
        _ada_kernel,
        grid=(depth, N_MOD),
        in_specs=[pl.BlockSpec((MOD_ROWS, d), lambda l, j: (0, 0)),
                  pl.BlockSpec((None, d, d), lambda l, j: (l, 0, j)),
                  pl.BlockSpec((None, 1, d), lambda l, j: (l * N_MOD + j, 0, 0))],
        out_specs=pl.BlockSpec((None, MOD_ROWS, d), lambda l, j: (l, 0, j)),
        out_shape=jax.ShapeDtypeStruct((depth, MOD_ROWS, N_MOD * d), F32),
        compiler_params=_cparams(("parallel", "parallel")),
        name="ada",
    )(cc, ada_w, ada_b.reshape(depth * N_MOD, 1, d))
    return out.reshape(depth * MOD_ROWS * N_MOD, 1, d)


def _ffn_kernel(x_ref, g_ref, sh_ref, sc_ref, gt_ref, wg_ref, wu_ref, wd_ref, o_ref, h_scr, acc_scr):
    f = pl.program_id(1)

    @pl.when(f == 0)
    def _():
        h_scr[...] = _norm_mod(x_ref[...], g_ref[...], sh_ref[...], sc_ref[...]).astype(BF16)
        acc_scr[...] = jnp.zeros_like(acc_scr)

    h = h_scr[...]
    a = jnp.dot(h, wg_ref[...], preferred_element_type=F32)
    u = jnp.dot(h, wu_ref[...], preferred_element_type=F32)
    t = (a * _sigmoid(a)) * u
    acc_scr[...] += jnp.dot(t.astype(BF16), wd_ref[...], preferred_element_type=F32)

    @pl.when(f == pl.num_programs(1) - 1)
    def _():
        o_ref[...] = x_ref[...] + gt_ref[...] * acc_scr[...]


def _ffn_dense(x, mods, g_all, wg, wu, wd, layer, tiles_per_batch, ctx, tm=FFN_TM, tf=FFN_TF):
    t, d = x.shape
    ff = wg.shape[2]
    tm = min(tm, t)
    tf = min(tf, ff)
    sj = layer // 2
    mi = functools.partial(_mod_index, layer, tiles_per_batch=tiles_per_batch // tm if not ctx else 1, ctx=ctx)
    return pl.pallas_call(
        _ffn_kernel,
        grid=(t // tm, ff // tf),
        in_specs=[pl.BlockSpec((tm, d), lambda i, f: (i, 0)),
                  pl.BlockSpec((None, 1, d), lambda i, f: (layer, 0, 0)),
                  _mod_spec(d, mi(3)), _mod_spec(d, mi(4)), _mod_spec(d, mi(5)),
                  pl.BlockSpec((None, d, tf), lambda i, f: (sj, 0, f)),
                  pl.BlockSpec((None, d, tf), lambda i, f: (sj, 0, f)),
                  pl.BlockSpec((None, tf, d), lambda i, f: (sj, f, 0))],
        out_specs=pl.BlockSpec((tm, d), lambda i, f: (i, 0)),
        out_shape=jax.ShapeDtypeStruct((t, d), F32),
        scratch_shapes=[pltpu.VMEM((tm, d), BF16), pltpu.VMEM((tm, d), F32)],
        compiler_params=_cparams(("parallel", "arbitrary")),
        name="ffn_dense",
    )(x, g_all, mods, mods, mods, wg, wu, wd)


def _proj_res_kernel(a_ref, w_ref, x_ref, gt_ref, o_ref, *, transposed):
    dims = (((0,), (0,)), ((), ())) if transposed else (((1,), (0,)), ((), ()))
    y = lax.dot_general(a_ref[...], w_ref[...], dims, preferred_element_type=F32)
    o_ref[...] = x_ref[...] + gt_ref[...] * y


def _proj_res(a, w, x, mods, layer, j, tiles_per_batch, ctx, tm=PROJ_TM):
    t, d = x.shape
    transposed = a.ndim == 3
    k = a.shape[1]
    tm = min(tm, t)
    tpb = tiles_per_batch // tm
    mi = _mod_index(layer, j, tpb if not ctx else 1, ctx)
    a_spec = (pl.BlockSpec((None, k, tm), lambda i: (i // tpb, 0, i % tpb)) if transposed
              else pl.BlockSpec((tm, k), lambda i: (i, 0)))
    return pl.pallas_call(
        functools.partial(_proj_res_kernel, transposed=transposed),
        grid=(t // tm,),
        in_specs=[a_spec,
                  pl.BlockSpec((None, k, d), lambda i: (layer // 2, 0, 0)),
                  pl.BlockSpec((tm, d), lambda i: (i, 0)),
                  _mod_spec(d, mi)],
        out_specs=pl.BlockSpec((tm, d), lambda i: (i, 0)),
        out_shape=jax.ShapeDtypeStruct((t, d), F32),
        compiler_params=_cparams(("parallel",)),
        name="proj_res",
    )(a, w, x, mods)


def _dft_angles(n):
    a = np.arange(n)
    return 2.0 * np.pi * ((a[:, None] * a[None, :]) % n) / n


def _seq_dft_tables(n):
    kp = np.arange(DFT_P)[None, :, None]
    p = np.arange(DFT_P)[None, None, :]
    r = np.arange(DFT_R)[:, None, None]
    th = 2.0 * np.pi * ((kp * (DFT_R * p + r)) % n) / n
    tab1 = np.stack([np.cos(th), -np.sin(th)], axis=1)
    tab1 = tab1.reshape(DFT_R // FNET_RB, FNET_RB, 2, DFT_P, DFT_P)
    kron = np.zeros((DFT_R // FNET_RB, 2, DFT_P, FNET_RB, DFT_P, FNET_RB), np.float32)
    for rl in range(FNET_RB):
        kron[:, :, :, rl, :, rl] = tab1[:, rl]
    tab1 = kron.reshape(DFT_R // FNET_RB, 2 * DFT_P * FNET_RB, DFT_P * FNET_RB)
    th2 = _dft_angles(DFT_R)
    c2, s2 = np.cos(th2), np.sin(th2)
    tab2 = np.block([[c2, s2], [-s2, c2]])
    return jnp.asarray(tab1, BF16), jnp.asarray(tab2, BF16)


def _chan_dft_tables(group_dim):
    th = _dft_angles(group_dim)
    return jnp.asarray(np.cos(th), BF16), jnp.asarray(np.sin(th), BF16)


def _ctx_dft_table(n_ctx):
    th = _dft_angles(n_ctx)
    return jnp.asarray(np.concatenate([np.cos(th), -np.sin(th)], axis=0), BF16)


def _cols_store(scr, val):
    for c in range(scr.shape[0]):
        scr[c] = val[:, c * LANES:(c + 1) * LANES]


def _cols_load(scr):
    return jnp.concatenate([scr[c] for c in range(scr.shape[0])], axis=-1)


def _cols_store_rows(scr, sel, val):
    for c in range(scr.shape[0]):
        scr[c, sel, :] = val[:, c * LANES:(c + 1) * LANES]


def _cols_load_rows(scr, sel):
    return jnp.concatenate([scr[c, sel, :] for c in range(scr.shape[0])], axis=-1)


def _fnet_stage1_kernel(x_ref, g_ref, sh_ref, sc_ref, tab_ref, zr_ref, zi_ref):
    d = g_ref.shape[-1]
    rows = DFT_P * FNET_RB
    h = _norm_mod(x_ref[...].reshape(rows, d), g_ref[...], sh_ref[...], sc_ref[...]).astype(BF16)
    z = jnp.dot(tab_ref[...], h, preferred_element_type=F32)
    zr_ref[...] = z[:rows].astype(BF16).reshape(DFT_P, FNET_RB, d)
    zi_ref[...] = z[rows:].astype(BF16).reshape(DFT_P, FNET_RB, d)


def _mix_tail(pr, pi, cc_ref, sc_ref, wout_ref, scale):
    gd = cc_ref.shape[0]
    ys = []
    for g in range(pr.shape[1] // gd):
        ys.append(jnp.dot(pr[:, g * gd:(g + 1) * gd], cc_ref[...], preferred_element_type=F32)
                  + jnp.dot(pi[:, g * gd:(g + 1) * gd], sc_ref[...], preferred_element_type=F32))
    mixed = (jnp.concatenate(ys, axis=-1) * scale).astype(BF16)
    return jnp.dot(mixed, wout_ref[...], preferred_element_type=F32)


def _fnet_stage2_kernel(zr_ref, zi_ref, tab2_ref, cc_ref, sc_ref, wout_ref, x_ref, gt_ref, o_ref,
                        p_scr, x_scr, o_scr, *, scale):
    d = gt_ref.shape[-1]
    rows = DFT_R * FNET_KB
    for j in range(FNET_KB):
        z = jnp.concatenate([zr_ref[j * DFT_R:(j + 1) * DFT_R, :], zi_ref[j * DFT_R:(j + 1) * DFT_R, :]], axis=0)
        p = jnp.dot(tab2_ref[...], z, preferred_element_type=F32)
        p_scr[j * DFT_R:(j + 1) * DFT_R, :d] = p[:DFT_R].astype(BF16)
        p_scr[j * DFT_R:(j + 1) * DFT_R, d:] = p[DFT_R:].astype(BF16)
    y = _mix_tail(p_scr[:, :d], p_scr[:, d:], cc_ref, sc_ref, wout_ref, scale)
    _cols_store(x_scr, x_ref[...].reshape(rows, d))
    for j in range(FNET_KB):
        sel = pl.ds(j, DFT_R, stride=FNET_KB)
        _cols_store_rows(o_scr, sel, _cols_load_rows(x_scr, sel) + gt_ref[...] * y[j * DFT_R:(j + 1) * DFT_R, :])
    o_ref[...] = _cols_load(o_scr).reshape(DFT_R, FNET_KB, d)


def _fnet_lat(x, mods, g_all, w_out, layer, n_batch):
    t, d = x.shape
    s = t // n_batch
    assert s == DFT_P * DFT_R
    tab1, tab2 = _seq_dft_tables(s)
    ccos, csin = _chan_dft_tables(d // FNET_GROUPS)
    mi = lambda j: (lambda b, *_: (layer * MOD_ROWS + b) * N_MOD + j)
    xv = x.reshape(n_batch, DFT_P, DFT_R, d)
    blk1 = (None, DFT_P, FNET_RB, d)
    rows1 = DFT_P * FNET_RB
    mi1 = lambda j: (lambda r, b: (layer * MOD_ROWS + b) * N_MOD + j)
    zr, zi = pl.pallas_call(
        _fnet_stage1_kernel,
        grid=(DFT_R // FNET_RB, n_batch),
        in_specs=[pl.BlockSpec(blk1, lambda r, b: (b, 0, r, 0)),
                  pl.BlockSpec((None, 1, d), lambda r, b: (layer, 0, 0)),
                  _mod_spec(d, mi1(0)), _mod_spec(d, mi1(1)),
                  pl.BlockSpec((None, 2 * rows1, rows1), lambda r, b: (r, 0, 0))],
        out_specs=[pl.BlockSpec(blk1, lambda r, b: (b, 0, r, 0))] * 2,
        out_shape=[jax.ShapeDtypeStruct((n_batch, DFT_P, DFT_R, d), BF16)] * 2,
        compiler_params=_cparams(("parallel", "parallel")),
        name="fnet_stage1",
    )(xv, g_all, mods, mods, tab1)
    zr = zr.reshape(n_batch, DFT_P * DFT_R, d)
    zi = zi.reshape(n_batch, DFT_P * DFT_R, d)
    xo = x.reshape(n_batch, DFT_R, DFT_P, d)
    blk2 = (None, DFT_R, FNET_KB, d)
    rows = FNET_KB * DFT_R
    scale = float(1.0 / np.sqrt(float(s) * (d // FNET_GROUPS)))
    out = pl.pallas_call(
        functools.partial(_fnet_stage2_kernel, scale=scale),
        grid=(n_batch, DFT_P // FNET_KB),
        in_specs=[pl.BlockSpec((None, rows, d), lambda b, k: (b, k, 0)),
                  pl.BlockSpec((None, rows, d), lambda b, k: (b, k, 0)),
                  pl.BlockSpec((2 * DFT_R, 2 * DFT_R), lambda b, k: (0, 0)),
                  pl.BlockSpec(ccos.shape, lambda b, k: (0, 0)),
                  pl.BlockSpec(csin.shape, lambda b, k: (0, 0)),
                  pl.BlockSpec((None, d, d), lambda b, k: (layer // 2, 0, 0)),
                  pl.BlockSpec(blk2, lambda b, k: (b, 0, k, 0)),
                  _mod_spec(d, mi(2))],
        out_specs=pl.BlockSpec(blk2, lambda b, k: (b, 0, k, 0)),
        out_shape=jax.ShapeDtypeStruct((n_batch, DFT_R, DFT_P, d), F32),
        scratch_shapes=[pltpu.VMEM((rows, 2 * d), BF16)] + [pltpu.VMEM((d // LANES, rows, LANES), F32)] * 2,
        compiler_params=_cparams(("parallel", "parallel")),
        name="fnet_stage2",
    )(zr, zi, tab2, ccos, csin, w_out, xo, mods)
    return out.reshape(t, d)


def _fnet_ctx_kernel(x_ref, g_ref, sh_ref, sc_ref, gt_ref, tab_ref, cc_ref, sc2_ref, wout_ref, o_ref, *, scale):
    n = x_ref.shape[0]
    h = _norm_mod(x_ref[...], g_ref[...], sh_ref[...], sc_ref[...]).astype(BF16)
    p = jnp.dot(tab_ref[...], h, preferred_element_type=F32)
    y = _mix_tail(p[:n].astype(BF16), p[n:].astype(BF16), cc_ref, sc2_ref, wout_ref, scale)
    o_ref[...] = x_ref[...] + gt_ref[...] * y


def _fnet_ctx(x, mods, g_all, w_out, layer, n_batch):
    t, d = x.shape
    n = t // n_batch
    gd = d // FNET_GROUPS
    ccos, csin = _chan_dft_tables(gd)
    tab = _ctx_dft_table(n)
    mi = lambda j: (lambda b: (layer * MOD_ROWS + CTX_ROW) * N_MOD + j)
    scale = float(1.0 / np.sqrt(float(n) * gd))
    return pl.pallas_call(
        functools.partial(_fnet_ctx_kernel, scale=scale),
        grid=(n_batch,),
        in_specs=[pl.BlockSpec((n, d), lambda b: (b, 0)),
                  pl.BlockSpec((None, 1, d), lambda b: (layer, 0, 0)),
                  _mod_spec(d, mi(0)), _mod_spec(d, mi(1)), _mod_spec(d, mi(2)),
                  pl.BlockSpec(tab.shape, lambda b: (0, 0)),
                  pl.BlockSpec(ccos.shape, lambda b: (0, 0)),
                  pl.BlockSpec(csin.shape, lambda b: (0, 0)),
                  pl.BlockSpec((None, d, d), lambda b: (layer // 2, 0, 0))],
        out_specs=pl.BlockSpec((n, d), lambda b: (b, 0)),
        out_shape=jax.ShapeDtypeStruct((t, d), F32),
        compiler_params=_cparams(("parallel",)),
        name="fnet_ctx",
    )(x, g_all, mods, mods, mods, tab, ccos, csin, w_out)


def _rope_tables(n_seq):
    rows = n_seq // GRID_W
    row = jnp.repeat(jnp.arange(rows, dtype=F32), GRID_W)
    col = jnp.tile(jnp.arange(GRID_W, dtype=F32), rows)
    inv_freq = ROPE_THETA ** (-jnp.arange(ROT_FREQS, dtype=F32) / ROT_FREQS)
    ang = jnp.stack([row[:, None] * inv_freq, col[:, None] * inv_freq], axis=1)
    cos, sin = jnp.cos(ang), jnp.sin(ang)
    zero = jnp.zeros_like(sin)
    cos_h = jnp.stack([cos, cos], axis=2).reshape(n_seq, HEAD_DIM)
    sin_lo = jnp.stack([-sin, zero], axis=2).reshape(n_seq, HEAD_DIM)
    sin_hi = jnp.stack([zero, sin], axis=2).reshape(n_seq, HEAD_DIM)
    rep = LANES // HEAD_DIM
    return jnp.tile(cos_h, (1, rep)), jnp.tile(sin_lo, (1, rep)), jnp.tile(sin_hi, (1, rep))


def _qkv_kernel(x_ref, g_ref, sh_ref, sc_ref, w_ref, *rest, rope, q_dim, kv_dim, q_scale):
    if rope:
        cos_ref, slo_ref, shi_ref, q_ref, k_ref, kt_ref, v_ref, vt_ref = rest
    else:
        q_ref, k_ref, kt_ref, v_ref, vt_ref = rest
    h = _norm_mod(x_ref[...], g_ref[...], sh_ref[...], sc_ref[...]).astype(BF16)
    qkv = jnp.dot(h, w_ref[...], preferred_element_type=F32)

    def rot(xs):
        if not rope:
            return xs
        return (xs * cos_ref[...] + pltpu.roll(xs, LANES - ROT_FREQS, axis=1) * slo_ref[...]
                + pltpu.roll(xs, ROT_FREQS, axis=1) * shi_ref[...])

    for j in range(q_dim // LANES):
        qj = rot(qkv[:, j * LANES:(j + 1) * LANES]) * q_scale
        if rope:
            q_ref[j * LANES:(j + 1) * LANES, :] = qj.T.astype(BF16)
        else:
            q_ref[:, j * LANES:(j + 1) * LANES] = qj.astype(BF16)
    k = jnp.concatenate([rot(qkv[:, q_dim + j * LANES:q_dim + (j + 1) * LANES]) for j in range(kv_dim // LANES)],
                        axis=-1)
    v = qkv[:, q_dim + kv_dim:]
    k_ref[...] = k.astype(BF16)
    kt_ref[...] = k.T.astype(BF16)
    v_ref[...] = v.astype(BF16)
    vt_ref[...] = v.T.astype(BF16)


def _qkv(x, mods, g_all, w_qkv, layer, n_batch, ctx, rope_tabs=None, tm=QKV_TM):
    t, d = x.shape
    n = t // n_batch
    tm = min(tm, n)
    tpb = n // tm
    kv_dim = N_KV_HEADS * HEAD_DIM
    q_dim = w_qkv.shape[2] - 2 * kv_dim
    mi = lambda j: _mod_index(layer, j, tpb, ctx)
    in_specs = [pl.BlockSpec((tm, d), lambda i: (i, 0)),
                pl.BlockSpec((None, 1, d), lambda i: (layer, 0, 0)),
                _mod_spec(d, mi(0)), _mod_spec(d, mi(1)),
                pl.BlockSpec((None,) + w_qkv.shape[1:], lambda i: (layer // 2, 0, 0))]
    args = [x, g_all, mods, mods, w_qkv]
    rope = rope_tabs is not None
    if rope:
        in_specs += [pl.BlockSpec((tm, LANES), lambda i: (i % tpb, 0))] * 3
        args += list(rope_tabs)
    return pl.pallas_call(
        functools.partial(_qkv_kernel, rope=rope, q_dim=q_dim, kv_dim=kv_dim, q_scale=HEAD_DIM ** -0.5),
        grid=(t // tm,),
        in_specs=in_specs,
        out_specs=[(pl.BlockSpec((None, q_dim, tm), lambda i: (i // tpb, 0, i % tpb)) if rope
                    else pl.BlockSpec((tm, q_dim), lambda i: (i, 0))),
                   pl.BlockSpec((tm, kv_dim), lambda i: (i, 0)),
                   pl.BlockSpec((None, kv_dim, tm), lambda i: (i // tpb, 0, i % tpb)),
                   pl.BlockSpec((tm, kv_dim), lambda i: (i, 0)),
                   pl.BlockSpec((None, kv_dim, tm), lambda i: (i // tpb, 0, i % tpb))],
        out_shape=[jax.ShapeDtypeStruct((n_batch, q_dim, n) if rope else (t, q_dim), BF16),
                   jax.ShapeDtypeStruct((t, kv_dim), BF16),
                   jax.ShapeDtypeStruct((n_batch, kv_dim, n), BF16),
                   jax.ShapeDtypeStruct((t, kv_dim), BF16),
                   jax.ShapeDtypeStruct((n_batch, kv_dim, n), BF16)],
        compiler_params=_cparams(("parallel",)),
        name="qkv_ctx" if ctx else "qkv_lat",
    )(*args)


ATTN_RB = 64


def _attend_ctx(q_ref, kt, v, sink_ref, o_ref):
    nq = q_ref.shape[0]
    n_heads = q_ref.shape[1] // HEAD_DIM
    group = n_heads // N_KV_HEADS
    outs = [None] * n_heads
    for g in range(N_KV_HEADS):
        heads = range(g * group, (g + 1) * group)
        qg = jnp.concatenate([q_ref[:, hd * HEAD_DIM:(hd + 1) * HEAD_DIM] for hd in heads], axis=0)
        s = jnp.dot(qg, kt[g * HEAD_DIM:(g + 1) * HEAD_DIM, :], preferred_element_type=F32)
        p_rows, den_rows = [], []
        for r0 in range(0, s.shape[0], ATTN_RB):
            rows = slice(r0, r0 + ATTN_RB)
            parts = [s[rows, k * LANES:(k + 1) * LANES] for k in range(s.shape[1] // LANES)]
            sink = sink_ref[heads[r0 // nq]]
            m = jnp.maximum(jnp.max(functools.reduce(jnp.maximum, parts), axis=-1, keepdims=True), sink)
            ps = [jnp.exp(part - m) for part in parts]
            den_rows.append(jnp.sum(functools.reduce(jnp.add, ps), axis=-1, keepdims=True) + jnp.exp(sink - m))
            p_rows.append(jnp.concatenate(ps, axis=-1).astype(BF16))
        pv = jnp.dot(jnp.concatenate(p_rows, axis=0), v, preferred_element_type=F32)
        on = pv[:, g * HEAD_DIM:(g + 1) * HEAD_DIM] / jnp.concatenate(den_rows, axis=0)
        for k, hd in enumerate(heads):
            outs[hd] = on[k * nq:(k + 1) * nq, :]
    o_ref[...] = jnp.concatenate(outs, axis=-1).astype(BF16)


def _attn_lat_kernel(sink_ref, qt_ref, kp_ref, kc_ref, kn_ref, vtp_ref, vtc_ref, vtn_ref, kx_ref, vtx_ref,
                     blo_ref, bhi_ref, lo_ref, hi_ref, ot_ref, k_scr, vt_scr):
    w = kp_ref.shape[0]
    n_ctx = kx_ref.shape[0]
    n_heads = qt_ref.shape[0] // HEAD_DIM
    group = n_heads // N_KV_HEADS
    span = 3 * w + n_ctx
    k_scr[0:w, :] = kp_ref[...]
    k_scr[w:3 * w, :] = kc_ref[...]
    k_scr[3 * w:span, :] = kx_ref[...]
    k_scr[span:span + w, :] = kn_ref[...]
    vt_scr[:, 0:w] = vtp_ref[...]
    vt_scr[:, w:3 * w] = vtc_ref[...]
    vt_scr[:, 3 * w:span] = vtx_ref[...]
    vt_scr[:, span:span + w] = vtn_ref[...]
    n_chunks = span // w
    for blk in range(2):
        first = blk * w
        lo_mask = blo_ref if blk == 0 else lo_ref
        hi_mask = hi_ref if blk == 0 else bhi_ref
        hi_chunk = 2 if blk == 0 else n_chunks - 1
        for g in range(N_KV_HEADS):
            heads = range(g * group, (g + 1) * group)
            qg = jnp.concatenate([qt_ref[hd * HEAD_DIM:(hd + 1) * HEAD_DIM, first:first + w] for hd in heads], axis=1)
            st = jnp.dot(k_scr[first:first + span, g * HEAD_DIM:(g + 1) * HEAD_DIM], qg,
                         preferred_element_type=F32)
            p_cols, den_cols = [], []
            for k, hd in enumerate(heads):
                cols = slice(k * w, (k + 1) * w)
                parts = [st[c * w:(c + 1) * w, cols] for c in range(n_chunks)]
                parts[0] = parts[0] + lo_mask[...]
                parts[hi_chunk] = parts[hi_chunk] + hi_mask[...]
                sink = sink_ref[hd]
                m = jnp.maximum(jnp.max(functools.reduce(jnp.maximum, parts), axis=0, keepdims=True), sink)
                ps = [jnp.exp(part - m) for part in parts]
                den_cols.append(jnp.sum(functools.reduce(jnp.add, ps), axis=0, keepdims=True) + jnp.exp(sink - m))
                p_cols.append(jnp.concatenate(ps, axis=0).astype(BF16))
            ot = jnp.dot(vt_scr[g * HEAD_DIM:(g + 1) * HEAD_DIM, first:first + span], jnp.concatenate(p_cols, axis=1),
                         preferred_element_type=F32)
            ot = ot / jnp.concatenate(den_cols, axis=1)
            for k, hd in enumerate(heads):
                ot_ref[hd * HEAD_DIM:(hd + 1) * HEAD_DIM, first:first + w] = ot[:, k * w:(k + 1) * w].astype(BF16)


def _attn_lat(qt, k, vt, kx, vtx, sinks, n_batch):
    _, qd, s = qt.shape
    n_ctx = kx.shape[0] // n_batch
    w = WINDOW
    nb = s // w
    kvd = k.shape[1]
    j_all = 3 * w + n_ctx
    ki = np.arange(w)[:, None]
    qi = np.arange(w)[None, :]
    lo = np.where(ki >= qi, 0.0, NEG).astype(np.float32)
    hi = np.where(ki <= qi, 0.0, NEG).astype(np.float32)
    off = np.full((w, w), NEG, np.float32)
    blo = jnp.asarray(np.stack([lo, off]))
    bhi = jnp.asarray(np.stack([hi, off]))
    assert nb % 2 == 0
    ns = nb // 2
    pidx = lambda i: jnp.maximum(2 * i - 1, 0)
    nidx = lambda i: jnp.minimum(2 * i + 2, nb - 1)
    return pl.pallas_call(
        _attn_lat_kernel,
        grid=(n_batch, ns),
        in_specs=[pl.BlockSpec(memory_space=pltpu.SMEM),
                  pl.BlockSpec((None, qd, 2 * w), lambda b, i: (b, 0, i)),
                  pl.BlockSpec((w, kvd), lambda b, i: (b * nb + pidx(i), 0)),
                  pl.BlockSpec((2 * w, kvd), lambda b, i: (b * ns + i, 0)),
                  pl.BlockSpec((w, kvd), lambda b, i: (b * nb + nidx(i), 0)),
                  pl.BlockSpec((None, kvd, w), lambda b, i: (b, 0, pidx(i))),
                  pl.BlockSpec((None, kvd, 2 * w), lambda b, i: (b, 0, i)),
                  pl.BlockSpec((None, kvd, w), lambda b, i: (b, 0, nidx(i))),
                  pl.BlockSpec((n_ctx, kvd), lambda b, i: (b, 0)),
                  pl.BlockSpec((None, kvd, n_ctx), lambda b, i: (b, 0, 0)),
                  pl.BlockSpec((None, w, w), lambda b, i: (jnp.where(i == 0, 1, 0), 0, 0)),
                  pl.BlockSpec((None, w, w), lambda b, i: (jnp.where(i == ns - 1, 1, 0), 0, 0)),
                  pl.BlockSpec((None, w, w), lambda b, i: (0, 0, 0)),
                  pl.BlockSpec((None, w, w), lambda b, i: (0, 0, 0))],
        out_specs=pl.BlockSpec((None, qd, 2 * w), lambda b, i: (b, 0, i)),
        out_shape=jax.ShapeDtypeStruct((n_batch, qd, s), BF16),
        scratch_shapes=[pltpu.VMEM((j_all + w, kvd), BF16), pltpu.VMEM((kvd, j_all + w), BF16)],
        compiler_params=_cparams(("parallel", "parallel")),
        name="attn_lat",
    )(sinks, qt, k, k, k, vt, vt, vt, kx, vtx, blo, bhi, blo, bhi)


def _attn_ctx_kernel(sink_ref, q_ref, kt_ref, v_ref, o_ref):
    _attend_ctx(q_ref, kt_ref[...], v_ref[...], sink_ref, o_ref)


def _attn_ctx(q, kt, v, sinks, n_batch):
    t, qd = q.shape
    n = t // n_batch
    kvd = v.shape[1]
    return pl.pallas_call(
        _attn_ctx_kernel,
        grid=(n_batch,),
        in_specs=[pl.BlockSpec(memory_space=pltpu.SMEM),
                  pl.BlockSpec((n, qd), lambda b: (b, 0)),
                  pl.BlockSpec((None, kvd, n), lambda b: (b, 0, 0)),
                  pl.BlockSpec((n, kvd), lambda b: (b, 0))],
        out_specs=pl.BlockSpec((n, qd), lambda b: (b, 0)),
        out_shape=jax.ShapeDtypeStruct((t, qd), BF16),
        compiler_params=_cparams(("parallel",)),
        name="attn_ctx",
    )(sinks, q, kt, v)


ROUTE_I1, ROUTE_I2, ROUTE_R1, ROUTE_R2, ROUTE_G1, ROUTE_G2 = range(6)
ROUTE_ROWS = 8
TAB_START, TAB_COUNT, TAB_USED, TAB_SIZE = 0, N_EXPERTS, 2 * N_EXPERTS, 2 * N_EXPERTS + 8
ROW_UNROLL = 8


def _router_kernel(x_ref, g_ref, sh_ref, sc_ref, wr_ref, tri_ref, route_ref, idx_ref, cnt_ref, carry_scr):
    @pl.when(pl.program_id(0) == 0)
    def _():
        carry_scr[...] = jnp.zeros_like(carry_scr)

    h = _norm_mod(x_ref[...], g_ref[...], sh_ref[...], sc_ref[...])
    hi = h.astype(BF16)
    lo = (h - hi.astype(F32)).astype(BF16)
    a = jnp.dot(hi, wr_ref[...], preferred_element_type=F32)
    logits = (a[:, :LANES] + a[:, LANES:]) + jnp.dot(lo, wr_ref[:, :LANES], preferred_element_type=F32)
    lane = lax.broadcasted_iota(jnp.int32, logits.shape, 1)
    lane_f = lane.astype(F32)
    logits = jnp.where(lane < N_EXPERTS, logits, -jnp.inf)
    m1 = jnp.max(logits, axis=-1, keepdims=True)
    i1 = jnp.min(jnp.where(logits == m1, lane_f, float(LANES)), axis=-1, keepdims=True)
    oh1 = lane_f == i1
    rest = jnp.where(oh1, -jnp.inf, logits)
    m2 = jnp.max(rest, axis=-1, keepdims=True)
    i2 = jnp.min(jnp.where(rest == m2, lane_f, float(LANES)), axis=-1, keepdims=True)
    oh2 = lane_f == i2
    e2 = jnp.exp(m2 - m1)
    g1 = 1.0 / (1.0 + e2)
    g2 = e2 / (1.0 + e2)
    sel = jnp.where(oh1, 1.0, 0.0) + jnp.where(oh2, 1.0, 0.0)
    before = jnp.dot(tri_ref[...], sel.astype(BF16), preferred_element_type=F32) + carry_scr[0:1, :]
    r1 = jnp.sum(jnp.where(oh1, before, 0.0), axis=-1, keepdims=True)
    r2 = jnp.sum(jnp.where(oh2, before, 0.0), axis=-1, keepdims=True)
    total = carry_scr[0:1, :] + jnp.sum(sel, axis=0, keepdims=True)
    carry_scr[...] = jnp.broadcast_to(total, carry_scr.shape)
    cnt_ref[...] = jnp.broadcast_to(total, cnt_ref.shape)
    rec = jnp.zeros_like(logits)
    for ln, val in ((ROUTE_I1, i1), (ROUTE_I2, i2), (ROUTE_R1, r1), (ROUTE_R2, r2), (ROUTE_G1, g1), (ROUTE_G2, g2)):
        rec = jnp.where(lane == ln, val, rec)
    route_ref[...] = rec
    idx_ref[...] = rec.T[:ROUTE_ROWS, :].astype(jnp.int32)


def _router(x, mods, g_all, w_router, layer, tiles_per_batch, ctx, tm):
    t, d = x.shape
    mi = lambda j: _mod_index(layer, j, tiles_per_batch // tm if not ctx else 1, ctx)
    wr = jnp.zeros((d, LANES), F32).at[:, :N_EXPERTS].set(w_router)
    whi = wr.astype(BF16)
    wr = jnp.concatenate([whi, (wr - whi.astype(F32)).astype(BF16)], axis=1)
    tri = jnp.asarray(np.tril(np.ones((tm, tm), np.float32), -1), BF16)
    return pl.pallas_call(
        _router_kernel,
        grid=(t // tm,),
        in_specs=[pl.BlockSpec((tm, d), lambda i: (i, 0)),
                  pl.BlockSpec((None, 1, d), lambda i: (layer, 0, 0)),
                  _mod_spec(d, mi(3)), _mod_spec(d, mi(4)),
                  pl.BlockSpec((d, 2 * LANES), lambda i: (0, 0)),
                  pl.BlockSpec((tm, tm), lambda i: (0, 0))],
        out_specs=[pl.BlockSpec((tm, LANES), lambda i: (i, 0)),
                   pl.BlockSpec((None, ROUTE_ROWS, tm), lambda i: (i, 0, 0)),
                   pl.BlockSpec((8, LANES), lambda i: (0, 0))],
        out_shape=[jax.ShapeDtypeStruct((t, LANES), F32),
                   jax.ShapeDtypeStruct((t // tm, ROUTE_ROWS, tm), jnp.int32),
                   jax.ShapeDtypeStruct((8, LANES), F32)],
        scratch_shapes=[pltpu.VMEM((8, LANES), F32)],
        compiler_params=_cparams(("arbitrary",)),
        name="router",
    )(x, g_all, mods, mods, wr, tri)


SUB = 8


def _to_row_tiles(ref, val):
    n = val.shape[0]
    for k in range(SUB):
        ref[pl.ds(k, n, stride=SUB), :] = val[:, k * LANES:(k + 1) * LANES]


def _from_row_tiles(ref):
    n = ref.shape[0] // SUB
    return jnp.concatenate([ref[pl.ds(k, n, stride=SUB), :] for k in range(SUB)], axis=-1)


def _row_copy(src, dst, src_row, dst_row, sem):
    return pltpu.make_async_copy(src.at[pl.ds(pl.multiple_of(src_row * SUB, SUB), SUB)],
                                 dst.at[pl.ds(pl.multiple_of(dst_row * SUB, SUB), SUB)], sem)


def _for_rows(tm, body):
    def blk(i, _):
        base = pl.multiple_of(i * ROW_UNROLL, ROW_UNROLL)
        for j in range(ROW_UNROLL):
            body(base + j, j)
        return 0

    lax.fori_loop(0, tm // ROW_UNROLL, blk, 0)


ZERO_SEM = 4


def _dispatch_kernel(tab_ref, slot_ref, x_ref, g_ref, sh_ref, sc_ref, xs_ref, h_scr, zero_scr, sems, *, group):
    tm = x_ref.shape[0]
    n_groups = xs_ref.shape[0] // (group * SUB)
    step = pl.program_id(0)
    n_steps = pl.num_programs(0)
    cur = lax.rem(step, 2)

    def wait_buffer(b):
        for k in range(2):
            pltpu.make_async_copy(h_scr.at[b], xs_ref.at[pl.ds(0, tm * SUB)], sems.at[2 * b + k]).wait()

    @pl.when(step >= 2)
    def _():
        wait_buffer(cur)

    _to_row_tiles(h_scr.at[cur], _norm_mod(x_ref[...], g_ref[...], sh_ref[...], sc_ref[...]))

    def issue(r, j):
        for k in range(2):
            _row_copy(h_scr.at[cur], xs_ref, r, slot_ref[0, k, r], sems.at[2 * cur + k]).start(priority=j % 2)

    _for_rows(tm, issue)

    @pl.when(jnp.logical_and(step == n_steps - 1, step >= 1))
    def _():
        wait_buffer(1 - cur)

    @pl.when(step == n_steps - 1)
    def _():
        wait_buffer(cur)
        zero_scr[...] = jnp.zeros_like(zero_scr)
        for e in range(N_EXPERTS):
            n = tab_ref[TAB_COUNT + e]
            n_pad = lax.rem(group - lax.rem(n, group), group)
            first = tab_ref[TAB_START + e] + n

            def fill(k, _, first=first):
                _row_copy(zero_scr, xs_ref, 0, first + k, sems.at[ZERO_SEM]).start()
                return 0

            def fill_done(k, _):
                _row_copy(zero_scr, xs_ref, 0, 0, sems.at[ZERO_SEM]).wait()
                return 0

            lax.fori_loop(0, n_pad, fill, 0)
            lax.fori_loop(0, n_pad, fill_done, 0)

        def clear(j, _):
            row = pl.multiple_of(j * (group * SUB), group * SUB)
            cp = pltpu.make_async_copy(zero_scr, xs_ref.at[pl.ds(row, group * SUB)], sems.at[ZERO_SEM])
            cp.start()
            cp.wait()
            return 0

        lax.fori_loop(tab_ref[TAB_USED], n_groups, clear, 0)


def _dispatch(x, tab, slots, mods, g_all, layer, tiles_per_batch, ctx, n_groups, group, tm):
    t, d = x.shape
    mi = lambda j: _mod_index(layer, j, tiles_per_batch // tm if not ctx else 1, ctx)
    return pl.pallas_call(
        functools.partial(_dispatch_kernel, group=group),
        grid=(t // tm,),
        in_specs=[pl.BlockSpec(memory_space=pltpu.SMEM),
                  pl.BlockSpec((1, 2, tm), lambda i: (i, 0, 0), memory_space=pltpu.SMEM),
                  pl.BlockSpec((tm, d), lambda i: (i, 0)),
                  pl.BlockSpec((None, 1, d), lambda i: (layer, 0, 0)),
                  _mod_spec(d, mi(3)), _mod_spec(d, mi(4))],
        out_specs=pl.BlockSpec(memory_space=pl.ANY),
        out_shape=jax.ShapeDtypeStruct((n_groups * group * SUB, LANES), F32),
        scratch_shapes=[pltpu.VMEM((2, tm * SUB, LANES), F32), pltpu.VMEM((group * SUB, LANES), F32),
                        pltpu.SemaphoreType.DMA((ZERO_SEM + 1,))],
        compiler_params=_cparams(("arbitrary",)),
        name="moe_dispatch",
    )(tab, slots, x, g_all, mods, mods)


def _moe_ffn_kernel(be_ref, bv_ref, xs_ref, wg_ref, wu_ref, wd_ref, ys_ref, h_scr, acc_scr):
    del be_ref
    i = pl.program_id(0)
    f = pl.program_id(1)
    last = f == pl.num_programs(1) - 1
    valid = bv_ref[i] > 0

    @pl.when(jnp.logical_and(valid, f == 0))
    def _():
        h_scr[...] = _from_row_tiles(xs_ref).astype(BF16)
        acc_scr[...] = jnp.zeros_like(acc_scr)

    @pl.when(valid)
    def _():
        h = h_scr[...]
        a = jnp.dot(h, wg_ref[...], preferred_element_type=F32)
        u = jnp.dot(h, wu_ref[...], preferred_element_type=F32)
        t = (a * _sigmoid(a)) * u
        acc_scr[...] += jnp.dot(t.astype(BF16), wd_ref[...], preferred_element_type=F32)

    @pl.when(jnp.logical_and(valid, last))
    def _():
        _to_row_tiles(ys_ref, acc_scr[...])

    @pl.when(jnp.logical_and(jnp.logical_not(valid), last))
    def _():
        ys_ref[...] = jnp.zeros_like(ys_ref)


def _moe_ffn(xs, block_expert, block_valid, wg, wu, wd, layer, tm, tf=MOE_TF):
    d, ff = wg.shape[2:]
    assert d == SUB * LANES and xs.shape[1] == LANES
    cap = xs.shape[0] // SUB
    tf = min(tf, ff)
    nf = ff // tf
    sj = layer // 2
    fsel = lambda i, f, bv: jnp.where(bv[i] > 0, f, nf - 1)
    grid_spec = pltpu.PrefetchScalarGridSpec(
        num_scalar_prefetch=2,
        grid=(cap // tm, nf),
        in_specs=[pl.BlockSpec((tm * SUB, LANES), lambda i, f, be, bv: (i, 0)),
                  pl.BlockSpec((None, None, d, tf), lambda i, f, be, bv: (sj, be[i], 0, fsel(i, f, bv))),
                  pl.BlockSpec((None, None, d, tf), lambda i, f, be, bv: (sj, be[i], 0, fsel(i, f, bv))),
                  pl.BlockSpec((None, None, tf, d), lambda i, f, be, bv: (sj, be[i], fsel(i, f, bv), 0))],
        out_specs=pl.BlockSpec((tm * SUB, LANES), lambda i, f, be, bv: (i, 0)),
        scratch_shapes=[pltpu.VMEM((tm, d), BF16), pltpu.VMEM((tm, d), F32)],
    )
    return pl.pallas_call(
        _moe_ffn_kernel,
        grid_spec=grid_spec,
        out_shape=jax.ShapeDtypeStruct(xs.shape, F32),
        compiler_params=_cparams(("parallel", "arbitrary")),
        name="moe_ffn",
    )(block_expert, block_valid, xs, wg, wu, wd)


def _combine_kernel(slot_ref, next_slot_ref, route_ref, x_ref, gt_ref, fg_ref, ys_ref, o_ref, buf, sems, *, final_norm):
    tm = x_ref.shape[0]
    step = pl.program_id(0)
    cur = lax.rem(step, 2)

    def gather(idx_ref, b):
        def issue(r, j):
            for k in range(2):
                _row_copy(ys_ref, buf.at[b, k], idx_ref[0, k, r], r, sems.at[2 * b + k]).start(priority=j % 2)

        _for_rows(tm, issue)

    @pl.when(step == 0)
    def _():
        gather(slot_ref, cur)

    @pl.when(step + 1 < pl.num_programs(0))
    def _():
        gather(next_slot_ref, 1 - cur)

    for k in range(2):
        pltpu.make_async_copy(ys_ref.at[pl.ds(0, tm * SUB)], buf.at[cur, k], sems.at[2 * cur + k]).wait()
    rec = route_ref[...]
    g1 = rec[:, ROUTE_G1:ROUTE_G1 + 1]
    g2 = rec[:, ROUTE_G2:ROUTE_G2 + 1]
    y = _from_row_tiles(buf.at[cur, 0]) * g1 + _from_row_tiles(buf.at[cur, 1]) * g2
    out = x_ref[...] + gt_ref[...] * y
    if final_norm:
        out = out * lax.rsqrt(jnp.mean(out * out, axis=-1, keepdims=True) + EPS) * fg_ref[...]
    o_ref[...] = out


def _combine(ys, slots, route, x, mods, final_g, layer, tiles_per_batch, ctx, final_norm, tm):
    t, d = x.shape
    mi = _mod_index(layer, 5, tiles_per_batch // tm if not ctx else 1, ctx)
    n_tiles = t // tm
    return pl.pallas_call(
        functools.partial(_combine_kernel, final_norm=final_norm),
        grid=(n_tiles,),
        in_specs=[pl.BlockSpec((1, 2, tm), lambda i: (i, 0, 0), memory_space=pltpu.SMEM),
                  pl.BlockSpec((1, 2, tm), lambda i: (jnp.minimum(i + 1, n_tiles - 1), 0, 0), memory_space=pltpu.SMEM),
                  pl.BlockSpec((tm, LANES), lambda i: (i, 0)),
                  pl.BlockSpec((tm, d), lambda i: (i, 0)),
                  _mod_spec(d, mi),
                  pl.BlockSpec((1, d), lambda i: (0, 0)),
                  pl.BlockSpec(memory_space=pl.ANY)],
        out_specs=pl.BlockSpec((tm, d), lambda i: (i, 0)),
        out_shape=jax.ShapeDtypeStruct((t, d), F32),
        scratch_shapes=[pltpu.VMEM((2, 2, tm * SUB, LANES), F32), pltpu.SemaphoreType.DMA((4,))],
        compiler_params=_cparams(("arbitrary",)),
        name="moe_combine",
    )(slots, slots, route, x, mods, final_g.reshape(1, d), ys)


def _moe(x, mods, g_all, w_router, wg, wu, wd, final_g, layer, tiles_per_batch, ctx, final_norm):
    t, d = x.shape
    group = MOE_TM if 2 * t >= 4 * N_EXPERTS * MOE_TM else MOE_TM // 2
    row_tm = min(ROW_TM, t)
    route, idx, counts = _router(x, mods, g_all, w_router, layer, tiles_per_batch, ctx, row_tm)
    counts = counts[0, :N_EXPERTS].astype(jnp.int32)
    groups = (counts + group - 1) // group
    ends = jnp.cumsum(groups)
    starts = (ends - groups) * group
    n_groups = (2 * t + group - 1) // group + N_EXPERTS
    tab = jnp.zeros((TAB_SIZE,), jnp.int32)
    tab = tab.at[TAB_START:TAB_START + N_EXPERTS].set(starts).at[TAB_COUNT:TAB_COUNT + N_EXPERTS].set(counts)
    tab = tab.at[TAB_USED].set(ends[-1])
    gi = jnp.arange(n_groups, dtype=jnp.int32)
    block_expert = jnp.minimum(jnp.sum(gi[:, None] >= ends[None, :], axis=1), N_EXPERTS - 1).astype(jnp.int32)
    block_valid = (gi < ends[-1]).astype(jnp.int32)
    def slot_rows(e, r):
        base = functools.reduce(jnp.add, [jnp.where(e == k, starts[k], 0) for k in range(N_EXPERTS)])
        return base + r
    slots = jnp.stack([slot_rows(idx[:, ROUTE_I1, :], idx[:, ROUTE_R1, :]),
                       slot_rows(idx[:, ROUTE_I2, :], idx[:, ROUTE_R2, :])], axis=1)
    xs = _dispatch(x, tab, slots, mods, g_all, layer, tiles_per_batch, ctx, n_groups, group, row_tm)
    ys = _moe_ffn(xs, block_expert, block_valid, wg, wu, wd, layer, group)
    return _combine(ys, slots, route, x, mods, final_g, layer, tiles_per_batch, ctx, final_norm, row_tm)


def kernel(x, c, ctx, c_ctx, ada_w, ada_b, norm_mix_g, norm_ffn_g, fnet_w_out, attn_w_qkv, attn_w_o, attn_sinks,
           ffn_w_gate, ffn_w_up, ffn_w_down, moe_w_router, moe_w_gate, moe_w_up, moe_w_down, final_norm_g):
    n_batch, s, d = x.shape
    n_ctx = ctx.shape[1]
    depth = ada_w.shape[0]
    mods = _ada(c, c_ctx, ada_w, ada_b)
    g_mix = norm_mix_g.reshape(depth, 1, d)
    g_ffn = norm_ffn_g.reshape(depth, 1, d)
    rope_tabs = _rope_tables(s)
    x_lat = x.reshape(n_batch * s, d)
    x_ctx = ctx.reshape(n_batch * n_ctx, d)
    bf = lambda a: a.astype(BF16)
    w_out, w_qkv, w_o = bf(fnet_w_out), bf(attn_w_qkv), bf(attn_w_o)
    wg, wu, wd = bf(ffn_w_gate), bf(ffn_w_up), bf(ffn_w_down)
    eg, eu, ed = bf(moe_w_gate), bf(moe_w_up), bf(moe_w_down)
    for i in range(depth):
        j = i // 2
        last = i == depth - 1
        if i % 2 == 0:
            x_lat = _fnet_lat(x_lat, mods, g_mix, w_out, i, n_batch)
            if not last:
                x_ctx = _fnet_ctx(x_ctx, mods, g_mix, w_out, i, n_batch)
            x_lat = _ffn_dense(x_lat, mods, g_ffn, wg, wu, wd, i, s, False)
            if not last:
                x_ctx = _ffn_dense(x_ctx, mods, g_ffn, wg, wu, wd, i, n_ctx, True)
        else:
            qt, k, _, _, vt = _qkv(x_lat, mods, g_mix, w_qkv, i, n_batch, False, rope_tabs)
            qx, kx, ktx, vx, vtx = _qkv(x_ctx, mods, g_mix, w_qkv, i, n_batch, True)
            ot = _attn_lat(qt, k, vt, kx, vtx, attn_sinks[j], n_batch)
            x_lat = _proj_res(ot, w_o, x_lat, mods, i, 2, s, False)
            if not last:
                ox = _attn_ctx(qx, ktx, vx, attn_sinks[j], n_batch)
                x_ctx = _proj_res(ox, w_o, x_ctx, mods, i, 2, n_ctx, True)
            x_lat = _moe(x_lat, mods, g_ffn, moe_w_router[j], eg, eu, ed, final_norm_g, i, s, False, last)
            if not last:
                x_ctx = _moe(x_ctx, mods, g_ffn, moe_w_router[j], eg, eu, ed, final_norm_g, i, n_ctx, True, False)
    return x_lat.reshape(n_batch, s, d)
```
